```python
import math
import jax, jax.numpy as jnp
from jax import lax
import numpy as np

D_MODEL = 2048
BATCH = 8
SEQ = 8192
DEPTH = 2

N_MIXERS = 2
N_FOX_LAYERS = (DEPTH + 1) // 2
N_GDN_LAYERS = DEPTH // 2
EPS = 1e-6

FOX_HEAD_DIM = 128
FOX_HEADS = D_MODEL // FOX_HEAD_DIM
FOX_Q = FOX_HEADS * FOX_HEAD_DIM
FOX_IN = 3 * FOX_Q + FOX_HEADS + FOX_Q
FOX_F_BIAS_INIT = 3.0
Q_BLOCK = 128

GDN_HEAD_DIM_K = 128
GDN_HEAD_DIM_V = 128
GDN_QK_HEADS = D_MODEL // 128
GDN_V_HEADS = 2 * GDN_QK_HEADS
GDN_KD = GDN_QK_HEADS * GDN_HEAD_DIM_K
GDN_VD = GDN_V_HEADS * GDN_HEAD_DIM_V
GDN_CONV_DIM = 2 * GDN_KD + GDN_VD
GDN_IN = 2 * GDN_KD + 2 * GDN_VD + 2 * GDN_V_HEADS
GDN_CONV = 4
GDN_CHUNK = 64

D_FF = 11 * D_MODEL // 4
FFN_CONV = 3

kernel_name = "fox_gdn_interleaved_convglu_adaln_sandwich"


def rmsnorm(x, g):
    xf = x.astype(jnp.float32)
    y = xf * lax.rsqrt(jnp.mean(xf * xf, axis=-1, keepdims=True) + EPS)
    return (y * g.astype(jnp.float32)).astype(x.dtype)


def l2norm(x):
    return x * lax.rsqrt(jnp.sum(x * x, axis=-1, keepdims=True) + EPS)


def causal_dwconv(x, w):
    K = w.shape[0]
    S = x.shape[1]
    xp = jnp.pad(x, ((0, 0), (K - 1, 0), (0, 0)))
    return sum(xp[:, k:k + S] * w[k] for k in range(K))


def fox_attention(h, w_in, f_bias, q_norm_g, k_norm_g, w_o):
    B, S, _ = h.shape
    H, Dh = FOX_HEADS, FOX_HEAD_DIM
    proj = h @ w_in
    q, k, v, f_logit, o_gate = jnp.split(proj, [FOX_Q, 2 * FOX_Q, 3 * FOX_Q, 3 * FOX_Q + H], axis=-1)
    qh = rmsnorm(q.reshape(B, S, H, Dh), q_norm_g).transpose(0, 2, 1, 3)
    kh = rmsnorm(k.reshape(B, S, H, Dh), k_norm_g).transpose(0, 2, 1, 3)
    vh = v.reshape(B, S, H, Dh).transpose(0, 2, 1, 3)
    log_f = jax.nn.log_sigmoid(f_logit.astype(jnp.float32) + f_bias.astype(jnp.float32))
    f_cum = jnp.cumsum(log_f, axis=1).transpose(0, 2, 1)
    scale = Dh ** -0.5
    key_pos = jnp.arange(S)

    def block(i):
        start = i * Q_BLOCK
        qb = lax.dynamic_slice_in_dim(qh, start, Q_BLOCK, axis=2)
        fq = lax.dynamic_slice_in_dim(f_cum, start, Q_BLOCK, axis=2)
        s = jnp.einsum('bhqd,bhkd->bhqk', qb, kh, preferred_element_type=jnp.float32) * scale
        s = s + fq[..., :, None] - f_cum[..., None, :]
        q_pos = start + jnp.arange(Q_BLOCK)
        s = jnp.where(key_pos[None, :] <= q_pos[:, None], s, -jnp.inf)
        p = jax.nn.softmax(s, axis=-1)
        return jnp.einsum('bhqk,bhkd->bhqd', p.astype(vh.dtype), vh)

    out = lax.map(block, jnp.arange(S // Q_BLOCK))
    out = out.transpose(1, 0, 3, 2, 4).reshape(B, S, H * Dh)
    out = out * jax.nn.sigmoid(o_gate)
    return out @ w_o


def chunk_gated_delta_rule(q, k, v, g, beta):
    B, S, H, Dk = q.shape
    Dv = v.shape[-1]
    C = GDN_CHUNK
    N = S // C

    def to_chunks(t):
        return t.reshape(B, N, C, H, -1).transpose(0, 3, 1, 2, 4)

    q, k, v = to_chunks(q), to_chunks(k), to_chunks(v)
    beta = beta.reshape(B, N, C, H).transpose(0, 3, 1, 2)
    g = jnp.cumsum(g.reshape(B, N, C, H).transpose(0, 3, 1, 2), axis=-1)
    idx = jnp.arange(C)
    lower_incl = idx[:, None] >= idx[None, :]
    strict = idx[:, None] > idx[None, :]
    decay_mat = jnp.exp(jnp.where(lower_incl, g[..., :, None] - g[..., None, :], -jnp.inf))
    kk = jnp.einsum('bhncd,bhnjd->bhncj', k, k)
    A = jnp.where(strict, kk * decay_mat * beta[..., :, None], 0.0)
    eye = jnp.eye(C, dtype=jnp.float32)
    rhs = jnp.concatenate([v * beta[..., None], k * (beta * jnp.exp(g))[..., None]], axis=-1)
    sol = lax.linalg.triangular_solve(eye + A, rhs, left_side=True, lower=True)
    u, w = jnp.split(sol, [Dv], axis=-1)
    qk = jnp.where(lower_incl, jnp.einsum('bhncd,bhnjd->bhncj', q, k) * decay_mat, 0.0)
    q_dec = q * jnp.exp(g)[..., None]
    g_last = g[..., -1]
    k_dec = k * jnp.exp(g_last[..., None] - g)[..., None]

    def step(state, inp):
        q_d, k_d, w_c, u_c, qk_c, gl = inp
        v_new = u_c - jnp.einsum('bhcd,bhde->bhce', w_c, state)
        o = jnp.einsum('bhcd,bhde->bhce', q_d, state) + jnp.einsum('bhcj,bhje->bhce', qk_c, v_new)
        state = state * jnp.exp(gl)[..., None, None] + jnp.einsum('bhcd,bhce->bhde', k_d, v_new)
        return state, o

    xs = (jnp.moveaxis(q_dec, 2, 0), jnp.moveaxis(k_dec, 2, 0), jnp.moveaxis(w, 2, 0),
          jnp.moveaxis(u, 2, 0), jnp.moveaxis(qk, 2, 0), jnp.moveaxis(g_last, 2, 0))
    state0 = jnp.zeros((B, H, Dk, Dv), jnp.float32)
    _, o = lax.scan(step, state0, xs)
    return o.transpose(1, 0, 3, 2, 4).reshape(B, S, H, Dv)


def gated_deltanet(h, w_in, conv_w, a_log, dt_bias, out_norm_g, w_o):
    B, S, _ = h.shape
    Hk, Hv, Dk, Dv = GDN_QK_HEADS, GDN_V_HEADS, GDN_HEAD_DIM_K, GDN_HEAD_DIM_V
    proj = h @ w_in
    qkv, z, a, b = jnp.split(proj, [GDN_CONV_DIM, GDN_CONV_DIM + GDN_VD, GDN_CONV_DIM + GDN_VD + Hv], axis=-1)
    qkv = jax.nn.silu(causal_dwconv(qkv, conv_w))
    q, k, v = jnp.split(qkv, [GDN_KD, 2 * GDN_KD], axis=-1)
    q = l2norm(q.reshape(B, S, Hk, Dk).astype(jnp.float32)) * (Dk ** -0.5)
    k = l2norm(k.reshape(B, S, Hk, Dk).astype(jnp.float32))
    q = jnp.repeat(q, Hv // Hk, axis=2)
    k = jnp.repeat(k, Hv // Hk, axis=2)
    v = v.reshape(B, S, Hv, Dv).astype(jnp.float32)
    beta = jax.nn.sigmoid(b.astype(jnp.float32))
    g = -jnp.exp(a_log.astype(jnp.float32)) * jax.nn.softplus(a.astype(jnp.float32) + dt_bias.astype(jnp.float32))
    o = chunk_gated_delta_rule(q, k, v, g, beta)
    o = rmsnorm(o, out_norm_g) * jax.nn.silu(z.reshape(B, S, Hv, Dv).astype(jnp.float32))
    return o.reshape(B, S, GDN_VD).astype(h.dtype) @ w_o


def conv_glu(h, w_up, conv_w, conv_b, w_down):
    gate, val = jnp.split(h @ w_up, 2, axis=-1)
    gate = causal_dwconv(gate, conv_w) + conv_b
    return (jax.nn.gelu(gate, approximate=False) * val) @ w_down


def modulate(x, g_pre, mod):
    shift, scale, gate = jnp.split(mod[:, None, :], 3, axis=-1)
    return rmsnorm(x, g_pre) * (1 + scale) + shift, gate


def _fwd_setup_inputs(seed: int = 0) -> dict:
    key = jax.random.key(seed)
    ks = iter(jax.random.split(key, 32))
    NF, NG = N_FOX_LAYERS, N_GDN_LAYERS

    def nrm(shape, scale):
        return jax.random.normal(next(ks), shape, jnp.float32) * scale

    def gain(shape):
        return 1.0 + 0.02 * jax.random.normal(next(ks), shape, jnp.float32)

    x = nrm((BATCH, SEQ, D_MODEL), 1.0)
    c = nrm((BATCH, D_MODEL), 1.0)
    ada_w = nrm((DEPTH, 2, D_MODEL, 3 * D_MODEL), 0.5 * D_MODEL ** -0.5)
    ada_b = nrm((DEPTH, 2, 3 * D_MODEL), 0.02)
    norm_g = gain((DEPTH, 4, D_MODEL))
    fox_w_in = nrm((NF, D_MODEL, FOX_IN), D_MODEL ** -0.5)
    fox_f_bias = FOX_F_BIAS_INIT + nrm((NF, FOX_HEADS), 0.1)
    fox_q_norm = gain((NF, FOX_HEAD_DIM))
    fox_k_norm = gain((NF, FOX_HEAD_DIM))
    fox_w_o = nrm((NF, FOX_Q, D_MODEL), FOX_Q ** -0.5)
    gdn_w_in = nrm((NG, D_MODEL, GDN_IN), D_MODEL ** -0.5)
    gdn_conv_w = nrm((NG, GDN_CONV, GDN_CONV_DIM), GDN_CONV ** -0.5)
    gdn_a_log = jnp.log(jax.random.uniform(next(ks), (NG, GDN_V_HEADS), jnp.float32, 1.0, 16.0))
    dt = jnp.exp(jax.random.uniform(next(ks), (NG, GDN_V_HEADS), jnp.float32, math.log(1e-3), math.log(1e-1)))
    gdn_dt_bias = dt + jnp.log(-jnp.expm1(-dt))
    gdn_out_norm = gain((NG, GDN_HEAD_DIM_V))
    gdn_w_o = nrm((NG, GDN_VD, D_MODEL), GDN_VD ** -0.5)
    ffn_w_up = nrm((DEPTH, D_MODEL, 2 * D_FF), D_MODEL ** -0.5)
    ffn_conv_w = nrm((DEPTH, FFN_CONV, D_FF), FFN_CONV ** -0.5)
    ffn_conv_b = nrm((DEPTH, D_FF), 0.02)
    ffn_w_down = nrm((DEPTH, D_FF, D_MODEL), D_FF ** -0.5)
    return {"x": x, "c": c, "ada_w": ada_w, "ada_b": ada_b, "norm_g": norm_g,
            "fox_w_in": fox_w_in, "fox_f_bias": fox_f_bias, "fox_q_norm": fox_q_norm,
            "fox_k_norm": fox_k_norm, "fox_w_o": fox_w_o,
            "gdn_w_in": gdn_w_in, "gdn_conv_w": gdn_conv_w, "gdn_a_log": gdn_a_log,
            "gdn_dt_bias": gdn_dt_bias, "gdn_out_norm": gdn_out_norm, "gdn_w_o": gdn_w_o,
            "ffn_w_up": ffn_w_up, "ffn_conv_w": ffn_conv_w, "ffn_conv_b": ffn_conv_b,
            "ffn_w_down": ffn_w_down}


def _fwd_reference(x, c, ada_w, ada_b, norm_g, fox_w_in, fox_f_bias, fox_q_norm, fox_k_norm, fox_w_o,
              gdn_w_in, gdn_conv_w, gdn_a_log, gdn_dt_bias, gdn_out_norm, gdn_w_o,
              ffn_w_up, ffn_conv_w, ffn_conv_b, ffn_w_down):
    c_act = jax.nn.silu(c)
    for i in range(DEPTH):
        mods = jnp.einsum('bd,sde->sbe', c_act, ada_w[i]) + ada_b[i][:, None, :]
        h, gate = modulate(x, norm_g[i, 0], mods[0])
        j = i // N_MIXERS
        if i % N_MIXERS == 0:
            y = fox_attention(h, fox_w_in[j], fox_f_bias[j], fox_q_norm[j], fox_k_norm[j], fox_w_o[j])
        else:
            y = gated_deltanet(h, gdn_w_in[j], gdn_conv_w[j], gdn_a_log[j], gdn_dt_bias[j],
                               gdn_out_norm[j], gdn_w_o[j])
        x = x + gate * rmsnorm(y, norm_g[i, 1])
        h, gate = modulate(x, norm_g[i, 2], mods[1])
        y = conv_glu(h, ffn_w_up[i], ffn_conv_w[i], ffn_conv_b[i], ffn_w_down[i])
        x = x + gate * rmsnorm(y, norm_g[i, 3])
    return x


import jax as _jax
import jax.numpy as _jnp

TWIN_FORMAT = 'train_step'
FWD_PARAMS = ['x', 'c', 'ada_w', 'ada_b', 'norm_g', 'fox_w_in', 'fox_f_bias', 'fox_q_norm', 'fox_k_norm', 'fox_w_o', 'gdn_w_in', 'gdn_conv_w', 'gdn_a_log', 'gdn_dt_bias', 'gdn_out_norm', 'gdn_w_o', 'ffn_w_up', 'ffn_conv_w', 'ffn_conv_b', 'ffn_w_down']
TWIN_WEIGHTS = ['ada_w', 'ada_b', 'norm_g', 'fox_w_in', 'fox_f_bias', 'fox_q_norm', 'fox_k_norm', 'fox_w_o', 'gdn_w_in', 'gdn_conv_w', 'gdn_a_log', 'gdn_dt_bias', 'gdn_out_norm', 'gdn_w_o', 'ffn_w_up', 'ffn_conv_w', 'ffn_conv_b', 'ffn_w_down']
TWIN_DIFF_INPUT = 'x'
TWIN_INPUTS = ['x', 'c', 'ada_w', 'ada_b', 'norm_g', 'fox_w_in', 'fox_f_bias', 'fox_q_norm', 'fox_k_norm', 'fox_w_o', 'gdn_w_in', 'gdn_conv_w', 'gdn_a_log', 'gdn_dt_bias', 'gdn_out_norm', 'gdn_w_o', 'ffn_w_up', 'ffn_conv_w', 'ffn_conv_b', 'ffn_w_down', 'loss_target', 'm_ada_w', 'm_ada_b', 'm_norm_g', 'm_fox_w_in', 'm_fox_f_bias', 'm_fox_q_norm', 'm_fox_k_norm', 'm_fox_w_o', 'm_gdn_w_in', 'm_gdn_conv_w', 'm_gdn_a_log', 'm_gdn_dt_bias', 'm_gdn_out_norm', 'm_gdn_w_o', 'm_ffn_w_up', 'm_ffn_conv_w', 'm_ffn_conv_b', 'm_ffn_w_down', 'v_ada_w', 'v_ada_b', 'v_norm_g', 'v_fox_w_in', 'v_fox_f_bias', 'v_fox_q_norm', 'v_fox_k_norm', 'v_fox_w_o', 'v_gdn_w_in', 'v_gdn_conv_w', 'v_gdn_a_log', 'v_gdn_dt_bias', 'v_gdn_out_norm', 'v_gdn_w_o', 'v_ffn_w_up', 'v_ffn_conv_w', 'v_ffn_conv_b', 'v_ffn_w_down']
TWIN_OUTPUTS = ['loss', 'grad_x', 'grad_ada_w', 'grad_ada_b', 'grad_norm_g', 'grad_fox_w_in', 'grad_fox_f_bias', 'grad_fox_q_norm', 'grad_fox_k_norm', 'grad_fox_w_o', 'grad_gdn_w_in', 'grad_gdn_conv_w', 'grad_gdn_a_log', 'grad_gdn_dt_bias', 'grad_gdn_out_norm', 'grad_gdn_w_o', 'grad_ffn_w_up', 'grad_ffn_conv_w', 'grad_ffn_conv_b', 'grad_ffn_w_down', 'delta_ada_w', 'delta_ada_b', 'delta_norm_g', 'delta_fox_w_in', 'delta_fox_f_bias', 'delta_fox_q_norm', 'delta_fox_k_norm', 'delta_fox_w_o', 'delta_gdn_w_in', 'delta_gdn_conv_w', 'delta_gdn_a_log', 'delta_gdn_dt_bias', 'delta_gdn_out_norm', 'delta_gdn_w_o', 'delta_ffn_w_up', 'delta_ffn_conv_w', 'delta_ffn_conv_b', 'delta_ffn_w_down', 'new_m_ada_w', 'new_m_ada_b', 'new_m_norm_g', 'new_m_fox_w_in', 'new_m_fox_f_bias', 'new_m_fox_q_norm', 'new_m_fox_k_norm', 'new_m_fox_w_o', 'new_m_gdn_w_in', 'new_m_gdn_conv_w', 'new_m_gdn_a_log', 'new_m_gdn_dt_bias', 'new_m_gdn_out_norm', 'new_m_gdn_w_o', 'new_m_ffn_w_up', 'new_m_ffn_conv_w', 'new_m_ffn_conv_b', 'new_m_ffn_w_down', 'new_v_ada_w', 'new_v_ada_b', 'new_v_norm_g', 'new_v_fox_w_in', 'new_v_fox_f_bias', 'new_v_fox_q_norm', 'new_v_fox_k_norm', 'new_v_fox_w_o', 'new_v_gdn_w_in', 'new_v_gdn_conv_w', 'new_v_gdn_a_log', 'new_v_gdn_dt_bias', 'new_v_gdn_out_norm', 'new_v_gdn_w_o', 'new_v_ffn_w_up', 'new_v_ffn_conv_w', 'new_v_ffn_conv_b', 'new_v_ffn_w_down']
TWIN_LEAF_KINDS = {'loss': 'loss', 'grad_x': 'grad_x', 'grad_ada_w': 'grad_w', 'grad_ada_b': 'grad_w', 'grad_norm_g': 'grad_w', 'grad_fox_w_in': 'grad_w', 'grad_fox_f_bias': 'grad_w', 'grad_fox_q_norm': 'grad_w', 'grad_fox_k_norm': 'grad_w', 'grad_fox_w_o': 'grad_w', 'grad_gdn_w_in': 'grad_w', 'grad_gdn_conv_w': 'grad_w', 'grad_gdn_a_log': 'grad_w', 'grad_gdn_dt_bias': 'grad_w', 'grad_gdn_out_norm': 'grad_w', 'grad_gdn_w_o': 'grad_w', 'grad_ffn_w_up': 'grad_w', 'grad_ffn_conv_w': 'grad_w', 'grad_ffn_conv_b': 'grad_w', 'grad_ffn_w_down': 'grad_w', 'delta_ada_w': 'delta_w', 'delta_ada_b': 'delta_w', 'delta_norm_g': 'delta_w', 'delta_fox_w_in': 'delta_w', 'delta_fox_f_bias': 'delta_w', 'delta_fox_q_norm': 'delta_w', 'delta_fox_k_norm': 'delta_w', 'delta_fox_w_o': 'delta_w', 'delta_gdn_w_in': 'delta_w', 'delta_gdn_conv_w': 'delta_w', 'delta_gdn_a_log': 'delta_w', 'delta_gdn_dt_bias': 'delta_w', 'delta_gdn_out_norm': 'delta_w', 'delta_gdn_w_o': 'delta_w', 'delta_ffn_w_up': 'delta_w', 'delta_ffn_conv_w': 'delta_w', 'delta_ffn_conv_b': 'delta_w', 'delta_ffn_w_down': 'delta_w', 'new_m_ada_w': 'new_m', 'new_m_ada_b': 'new_m', 'new_m_norm_g': 'new_m', 'new_m_fox_w_in': 'new_m', 'new_m_fox_f_bias': 'new_m', 'new_m_fox_q_norm': 'new_m', 'new_m_fox_k_norm': 'new_m', 'new_m_fox_w_o': 'new_m', 'new_m_gdn_w_in': 'new_m', 'new_m_gdn_conv_w': 'new_m', 'new_m_gdn_a_log': 'new_m', 'new_m_gdn_dt_bias': 'new_m', 'new_m_gdn_out_norm': 'new_m', 'new_m_gdn_w_o': 'new_m', 'new_m_ffn_w_up': 'new_m', 'new_m_ffn_conv_w': 'new_m', 'new_m_ffn_conv_b': 'new_m', 'new_m_ffn_w_down': 'new_m', 'new_v_ada_w': 'new_v', 'new_v_ada_b': 'new_v', 'new_v_norm_g': 'new_v', 'new_v_fox_w_in': 'new_v', 'new_v_fox_f_bias': 'new_v', 'new_v_fox_q_norm': 'new_v', 'new_v_fox_k_norm': 'new_v', 'new_v_fox_w_o': 'new_v', 'new_v_gdn_w_in': 'new_v', 'new_v_gdn_conv_w': 'new_v', 'new_v_gdn_a_log': 'new_v', 'new_v_gdn_dt_bias': 'new_v', 'new_v_gdn_out_norm': 'new_v', 'new_v_gdn_w_o': 'new_v', 'new_v_ffn_w_up': 'new_v', 'new_v_ffn_conv_w': 'new_v', 'new_v_ffn_conv_b': 'new_v', 'new_v_ffn_w_down': 'new_v'}


def _forward(args):
    return _fwd_reference(*[args[k] for k in FWD_PARAMS])


def _output_shape():
    def fwd():
        inp = _fwd_setup_inputs(0)
        return _fwd_reference(*[inp[k] for k in FWD_PARAMS])
    out = _jax.eval_shape(fwd)
    return out.shape, out.dtype

N_MICROBATCH = 1
ADAM_LR = 0.001
ADAM_B1 = 0.9
ADAM_B2 = 0.999
ADAM_EPS = 1e-08
ADAM_WD = 0.01
ADAM_STEP = 10
PER_EXAMPLE_BATCH_AXIS = {'x': 0, 'c': 0, 'loss_target': 0}
SHARED_INPUTS = []
_WEIGHT_DTYPES = {'ada_w': _jnp.float32, 'ada_b': _jnp.float32, 'norm_g': _jnp.float32, 'fox_w_in': _jnp.float32, 'fox_f_bias': _jnp.float32, 'fox_q_norm': _jnp.float32, 'fox_k_norm': _jnp.float32, 'fox_w_o': _jnp.float32, 'gdn_w_in': _jnp.float32, 'gdn_conv_w': _jnp.float32, 'gdn_a_log': _jnp.float32, 'gdn_dt_bias': _jnp.float32, 'gdn_out_norm': _jnp.float32, 'gdn_w_o': _jnp.float32, 'ffn_w_up': _jnp.float32, 'ffn_conv_w': _jnp.float32, 'ffn_conv_b': _jnp.float32, 'ffn_w_down': _jnp.float32}
MOMENT_SCALE = {'ada_w': 1.304742e+00, 'ada_b': 2.785307e+00, 'norm_g': 2.287849e+00, 'fox_w_in': 2.877110e-01, 'fox_f_bias': 7.108854e-01, 'fox_q_norm': 1.726432e-01, 'fox_k_norm': 1.706854e-01, 'fox_w_o': 6.106937e-01, 'gdn_w_in': 6.813839e-02, 'gdn_conv_w': 7.585621e-02, 'gdn_a_log': 3.017358e-01, 'gdn_dt_bias': 2.887414e-01, 'gdn_out_norm': 5.910979e-01, 'gdn_w_o': 2.001705e-01, 'ffn_w_up': 5.714368e-02, 'ffn_conv_w': 6.007183e-02, 'ffn_conv_b': 8.937607e-02, 'ffn_w_down': 1.148367e-01}


def _to_microbatches(a, axis):
    t = _jnp.moveaxis(a, axis, 0)
    t = t.reshape((N_MICROBATCH, t.shape[0] // N_MICROBATCH) + t.shape[1:])
    return _jnp.moveaxis(t, 1, axis + 1)


def setup_inputs(seed: int = 0) -> dict:
    inp = _fwd_setup_inputs(seed)
    key = _jax.random.fold_in(_jax.random.key(seed), 7919)
    shape, _ = _output_shape()
    out = dict(inp)
    out["loss_target"] = _jax.random.normal(_jax.random.fold_in(key, 0), shape, _jnp.float32)
    for i, name in enumerate(TWIN_WEIGHTS):
        w = inp[name].astype(_jnp.float32)
        if MOMENT_SCALE is None:
            s = _jnp.sqrt(_jnp.mean(_jnp.square(w)) + 1e-30)
        else:
            s = MOMENT_SCALE[name]
        km, kv = _jax.random.split(_jax.random.fold_in(key, i + 1))
        out[name] = w
        out["m_" + name] = s * _jax.random.normal(km, w.shape, _jnp.float32)
        out["v_" + name] = (s * s) * _jax.random.uniform(kv, w.shape, _jnp.float32, 0.5, 1.5)
    if N_MICROBATCH > 1:
        for name, axis in PER_EXAMPLE_BATCH_AXIS.items():
            out[name] = _to_microbatches(out[name], axis)
    return {'x': out['x'], 'c': out['c'], 'ada_w': out['ada_w'], 'ada_b': out['ada_b'], 'norm_g': out['norm_g'], 'fox_w_in': out['fox_w_in'], 'fox_f_bias': out['fox_f_bias'], 'fox_q_norm': out['fox_q_norm'], 'fox_k_norm': out['fox_k_norm'], 'fox_w_o': out['fox_w_o'], 'gdn_w_in': out['gdn_w_in'], 'gdn_conv_w': out['gdn_conv_w'], 'gdn_a_log': out['gdn_a_log'], 'gdn_dt_bias': out['gdn_dt_bias'], 'gdn_out_norm': out['gdn_out_norm'], 'gdn_w_o': out['gdn_w_o'], 'ffn_w_up': out['ffn_w_up'], 'ffn_conv_w': out['ffn_conv_w'], 'ffn_conv_b': out['ffn_conv_b'], 'ffn_w_down': out['ffn_w_down'], 'loss_target': out['loss_target'], 'm_ada_w': out['m_ada_w'], 'm_ada_b': out['m_ada_b'], 'm_norm_g': out['m_norm_g'], 'm_fox_w_in': out['m_fox_w_in'], 'm_fox_f_bias': out['m_fox_f_bias'], 'm_fox_q_norm': out['m_fox_q_norm'], 'm_fox_k_norm': out['m_fox_k_norm'], 'm_fox_w_o': out['m_fox_w_o'], 'm_gdn_w_in': out['m_gdn_w_in'], 'm_gdn_conv_w': out['m_gdn_conv_w'], 'm_gdn_a_log': out['m_gdn_a_log'], 'm_gdn_dt_bias': out['m_gdn_dt_bias'], 'm_gdn_out_norm': out['m_gdn_out_norm'], 'm_gdn_w_o': out['m_gdn_w_o'], 'm_ffn_w_up': out['m_ffn_w_up'], 'm_ffn_conv_w': out['m_ffn_conv_w'], 'm_ffn_conv_b': out['m_ffn_conv_b'], 'm_ffn_w_down': out['m_ffn_w_down'], 'v_ada_w': out['v_ada_w'], 'v_ada_b': out['v_ada_b'], 'v_norm_g': out['v_norm_g'], 'v_fox_w_in': out['v_fox_w_in'], 'v_fox_f_bias': out['v_fox_f_bias'], 'v_fox_q_norm': out['v_fox_q_norm'], 'v_fox_k_norm': out['v_fox_k_norm'], 'v_fox_w_o': out['v_fox_w_o'], 'v_gdn_w_in': out['v_gdn_w_in'], 'v_gdn_conv_w': out['v_gdn_conv_w'], 'v_gdn_a_log': out['v_gdn_a_log'], 'v_gdn_dt_bias': out['v_gdn_dt_bias'], 'v_gdn_out_norm': out['v_gdn_out_norm'], 'v_gdn_w_o': out['v_gdn_w_o'], 'v_ffn_w_up': out['v_ffn_w_up'], 'v_ffn_conv_w': out['v_ffn_conv_w'], 'v_ffn_conv_b': out['v_ffn_conv_b'], 'v_ffn_w_down': out['v_ffn_w_down']}


def _loss(weights, diff, rest, loss_target):
    with _jax.named_scope("forward"):
        args = {**rest, TWIN_DIFF_INPUT: diff, **{k: w.astype(_WEIGHT_DTYPES[k]) for k, w in weights.items()}}
        y = _forward(args)
    with _jax.named_scope("loss_head"):
        err = _jnp.square(y.astype(_jnp.float32) - loss_target)
        return 0.5 * _jnp.sum(_jnp.mean(err, axis=-1)) if err.ndim else 0.5 * err


def _adamw(w, g, m, v):
    m = ADAM_B1 * m + (1.0 - ADAM_B1) * g
    v = ADAM_B2 * v + (1.0 - ADAM_B2) * _jnp.square(g)
    m_hat = m / (1.0 - ADAM_B1 ** ADAM_STEP)
    v_hat = v / (1.0 - ADAM_B2 ** ADAM_STEP)
    delta = -ADAM_LR * (m_hat / (_jnp.sqrt(v_hat) + ADAM_EPS) + ADAM_WD * w)
    return delta, m, v


def reference(x, c, ada_w, ada_b, norm_g, fox_w_in, fox_f_bias, fox_q_norm, fox_k_norm, fox_w_o, gdn_w_in, gdn_conv_w, gdn_a_log, gdn_dt_bias, gdn_out_norm, gdn_w_o, ffn_w_up, ffn_conv_w, ffn_conv_b, ffn_w_down, loss_target, m_ada_w, m_ada_b, m_norm_g, m_fox_w_in, m_fox_f_bias, m_fox_q_norm, m_fox_k_norm, m_fox_w_o, m_gdn_w_in, m_gdn_conv_w, m_gdn_a_log, m_gdn_dt_bias, m_gdn_out_norm, m_gdn_w_o, m_ffn_w_up, m_ffn_conv_w, m_ffn_conv_b, m_ffn_w_down, v_ada_w, v_ada_b, v_norm_g, v_fox_w_in, v_fox_f_bias, v_fox_q_norm, v_fox_k_norm, v_fox_w_o, v_gdn_w_in, v_gdn_conv_w, v_gdn_a_log, v_gdn_dt_bias, v_gdn_out_norm, v_gdn_w_o, v_ffn_w_up, v_ffn_conv_w, v_ffn_conv_b, v_ffn_w_down):
    given = dict(x=x, c=c, ada_w=ada_w, ada_b=ada_b, norm_g=norm_g, fox_w_in=fox_w_in, fox_f_bias=fox_f_bias, fox_q_norm=fox_q_norm, fox_k_norm=fox_k_norm, fox_w_o=fox_w_o, gdn_w_in=gdn_w_in, gdn_conv_w=gdn_conv_w, gdn_a_log=gdn_a_log, gdn_dt_bias=gdn_dt_bias, gdn_out_norm=gdn_out_norm, gdn_w_o=gdn_w_o, ffn_w_up=ffn_w_up, ffn_conv_w=ffn_conv_w, ffn_conv_b=ffn_conv_b, ffn_w_down=ffn_w_down, loss_target=loss_target, m_ada_w=m_ada_w, m_ada_b=m_ada_b, m_norm_g=m_norm_g, m_fox_w_in=m_fox_w_in, m_fox_f_bias=m_fox_f_bias, m_fox_q_norm=m_fox_q_norm, m_fox_k_norm=m_fox_k_norm, m_fox_w_o=m_fox_w_o, m_gdn_w_in=m_gdn_w_in, m_gdn_conv_w=m_gdn_conv_w, m_gdn_a_log=m_gdn_a_log, m_gdn_dt_bias=m_gdn_dt_bias, m_gdn_out_norm=m_gdn_out_norm, m_gdn_w_o=m_gdn_w_o, m_ffn_w_up=m_ffn_w_up, m_ffn_conv_w=m_ffn_conv_w, m_ffn_conv_b=m_ffn_conv_b, m_ffn_w_down=m_ffn_w_down, v_ada_w=v_ada_w, v_ada_b=v_ada_b, v_norm_g=v_norm_g, v_fox_w_in=v_fox_w_in, v_fox_f_bias=v_fox_f_bias, v_fox_q_norm=v_fox_q_norm, v_fox_k_norm=v_fox_k_norm, v_fox_w_o=v_fox_w_o, v_gdn_w_in=v_gdn_w_in, v_gdn_conv_w=v_gdn_conv_w, v_gdn_a_log=v_gdn_a_log, v_gdn_dt_bias=v_gdn_dt_bias, v_gdn_out_norm=v_gdn_out_norm, v_gdn_w_o=v_gdn_w_o, v_ffn_w_up=v_ffn_w_up, v_ffn_conv_w=v_ffn_conv_w, v_ffn_conv_b=v_ffn_conv_b, v_ffn_w_down=v_ffn_w_down)
    weights = {n: given[n] for n in TWIN_WEIGHTS}
    shared = {n: given[n] for n in SHARED_INPUTS}
    per_example = {n: given[n] for n in ['x', 'c']}
    grad_fn = _jax.value_and_grad(_loss, argnums=(0, 1))

    def one_microbatch(ex, loss_target):
        ex = dict(ex)
        diff = ex.pop(TWIN_DIFF_INPUT)
        return grad_fn(weights, diff, {**shared, **ex}, loss_target)

    if N_MICROBATCH == 1:
        loss, (grad_w, grad_x) = one_microbatch(per_example, given["loss_target"])
    else:
        def body(carry, xs):
            loss_sum, grad_sum = carry
            l_k, (gw_k, gx_k) = one_microbatch(xs[0], xs[1])
            with _jax.named_scope("update"):
                return (loss_sum + l_k, _jax.tree.map(_jnp.add, grad_sum, gw_k)), gx_k

        init = (_jnp.zeros((), _jnp.float32), _jax.tree.map(_jnp.zeros_like, weights))
        (loss, grad_w), grad_x = _jax.lax.scan(body, init, (per_example, given["loss_target"]))
    with _jax.named_scope("update"):
        delta_w, new_m, new_v = {}, {}, {}
        for n in TWIN_WEIGHTS:
            delta_w[n], new_m[n], new_v[n] = _adamw(weights[n], grad_w[n], given["m_" + n], given["v_" + n])
    return (loss, grad_x, *[grad_w[n] for n in TWIN_WEIGHTS], *[delta_w[n] for n in TWIN_WEIGHTS],
            *[new_m[n] for n in TWIN_WEIGHTS], *[new_v[n] for n in TWIN_WEIGHTS])
```

```python
import functools
import math

import jax
import jax.numpy as jnp
from jax import lax
from jax.experimental import pallas as pl
from jax.experimental.pallas import tpu as pltpu

F32 = jnp.float32
BF16 = jnp.bfloat16
EPS = 1e-6
HD = 128
GDN_CHUNK = 64
GDN_CONV = 4
FFN_CONV = 3
LANES = 128
SUBLANES = 8
PACK_COLS = 1024
VMEM_LIMIT = 56 * 1024 * 1024
HIGHEST = lax.Precision.HIGHEST
NEG = -1e30

ADAM_LR = 0.001
ADAM_B1 = 0.9
ADAM_B2 = 0.999
ADAM_EPS = 1e-08
ADAM_WD = 0.01
ADAM_STEP = 10

WEIGHTS = ['ada_w', 'ada_b', 'norm_g', 'fox_w_in', 'fox_f_bias', 'fox_q_norm', 'fox_k_norm', 'fox_w_o',
           'gdn_w_in', 'gdn_conv_w', 'gdn_a_log', 'gdn_dt_bias', 'gdn_out_norm', 'gdn_w_o',
           'ffn_w_up', 'ffn_conv_w', 'ffn_conv_b', 'ffn_w_down']
BIG = ['fox_w_in', 'fox_w_o', 'gdn_w_in', 'gdn_w_o', 'ffn_w_up', 'ffn_w_down']
BIG_SHARD_AXIS = {'fox_w_in': 2, 'fox_w_o': 1, 'gdn_w_in': 2, 'gdn_w_o': 1, 'ffn_w_up': 2, 'ffn_w_down': 1}
SMALL = ['ada_b', 'norm_g', 'fox_f_bias', 'fox_q_norm', 'fox_k_norm', 'gdn_conv_w', 'gdn_a_log',
         'gdn_dt_bias', 'gdn_out_norm', 'ffn_conv_w', 'ffn_conv_b']
SMALL_SHARDED = ['ada_b', 'norm_g', 'gdn_conv_w', 'ffn_conv_w']
N_CHIPS = 4
N_DEV = 8
MESH = pl.DeviceIdType.MESH


def _tile(n, cands):
    for c in cands:
        if n % c == 0:
            return c
    return n


def _cp(*sem):
    return pltpu.CompilerParams(dimension_semantics=sem, vmem_limit_bytes=VMEM_LIMIT)


def _dot(a, b, mode='nn', precision=None):
    dims = {'nn': (((1,), (0,)), ((), ())), 'nt': (((1,), (1,)), ((), ())), 'tn': (((0,), (0,)), ((), ()))}[mode]
    return lax.dot_general(a, b, dims, precision=precision, preferred_element_type=F32)


def _bdot(a, b, mode='nn'):
    return _dot(a.astype(BF16), b.astype(BF16), mode)


def _hdot(a, b, mode='nn'):
    return _dot(a, b, mode, precision=HIGHEST)


def _sigmoid(x):
    return 1.0 / (1.0 + jnp.exp(-x))


def _silu(x):
    return x * _sigmoid(x)


def _softplus(x):
    return jnp.maximum(x, 0.0) + jnp.log(1.0 + jnp.exp(-jnp.abs(x)))


def _erf(x):
    return lax.erf(x)


def _gelu(x):
    return 0.5 * x * (1.0 + _erf(x * (2.0 ** -0.5)))


def _gelu_grad(x):
    cdf = 0.5 * (1.0 + _erf(x * (2.0 ** -0.5)))
    pdf = jnp.exp(-0.5 * x * x) * (1.0 / math.sqrt(2.0 * math.pi))
    return cdf + x * pdf


def _mm(a, b, mode, out_dtype, name, a_act=None):
    if mode == 'nn':
        (M, K), (_, N) = a.shape, b.shape
    elif mode == 'nt':
        (M, K), (N, _) = a.shape, b.shape
    else:
        (K, M), (_, N) = a.shape, b.shape
    big = (1024, 512, 256, 128)
    tm, tn, tk = _tile(M, big), _tile(N, big), _tile(K, (512, 256, 128))
    nk = K // tk

    def body(a_ref, b_ref, o_ref, acc_ref):
        k = pl.program_id(2)

        @pl.when(k == 0)
        def _():
            acc_ref[...] = jnp.zeros_like(acc_ref)

        av = a_ref[...]
        if a_act == 'silu':
            av = _silu(av.astype(F32))
        acc_ref[...] += _bdot(av, b_ref[...], mode)

        @pl.when(k == nk - 1)
        def _():
            o_ref[...] = acc_ref[...].astype(o_ref.dtype)

    if mode == 'nn':
        a_spec = pl.BlockSpec((tm, tk), lambda i, j, k: (i, k))
        b_spec = pl.BlockSpec((tk, tn), lambda i, j, k: (k, j))
    elif mode == 'nt':
        a_spec = pl.BlockSpec((tm, tk), lambda i, j, k: (i, k))
        b_spec = pl.BlockSpec((tn, tk), lambda i, j, k: (j, k))
    else:
        a_spec = pl.BlockSpec((tk, tm), lambda i, j, k: (k, i))
        b_spec = pl.BlockSpec((tk, tn), lambda i, j, k: (k, j))
    return pl.pallas_call(
        body, name=name, grid=(M // tm, N // tn, nk),
        in_specs=[a_spec, b_spec],
        out_specs=pl.BlockSpec((tm, tn), lambda i, j, k: (i, j)),
        out_shape=jax.ShapeDtypeStruct((M, N), out_dtype),
        scratch_shapes=[pltpu.VMEM((tm, tn), F32)],
        compiler_params=_cp("parallel", "parallel", "arbitrary"),
    )(a, b)


ROW_TILES = (256, 128, 64, 32, 16, 8)


def _row_spec(tT, D):
    return pl.BlockSpec((tT, D), lambda i: (i, 0))


def _vec_spec(D):
    return pl.BlockSpec((1, D), lambda i: (0, 0))


def _pre_norm(x, g, scale, shift, name):
    T, D = x.shape
    tT = _tile(T, ROW_TILES)

    def body(x_ref, g_ref, sc_ref, sh_ref, h_ref):
        xv = x_ref[...]
        r = lax.rsqrt(jnp.mean(xv * xv, axis=-1, keepdims=True) + EPS)
        h_ref[...] = ((xv * r) * g_ref[...] * (1.0 + sc_ref[...]) + sh_ref[...]).astype(h_ref.dtype)

    return pl.pallas_call(
        body, name=name, grid=(T // tT,),
        in_specs=[_row_spec(tT, D), _vec_spec(D), _vec_spec(D), _vec_spec(D)],
        out_specs=_row_spec(tT, D), out_shape=jax.ShapeDtypeStruct((T, D), BF16),
        compiler_params=_cp("parallel"),
    )(x, g, scale, shift)


def _post_res(x, y, gate, g, name):
    T, D = x.shape
    tT = _tile(T, ROW_TILES)

    def body(x_ref, y_ref, gate_ref, g_ref, o_ref):
        yv = y_ref[...]
        r = lax.rsqrt(jnp.mean(yv * yv, axis=-1, keepdims=True) + EPS)
        o_ref[...] = x_ref[...] + gate_ref[...] * ((yv * r) * g_ref[...])

    return pl.pallas_call(
        body, name=name, grid=(T // tT,),
        in_specs=[_row_spec(tT, D), _row_spec(tT, D), _vec_spec(D), _vec_spec(D)],
        out_specs=_row_spec(tT, D), out_shape=jax.ShapeDtypeStruct((T, D), F32),
        compiler_params=_cp("parallel"),
    )(x, y, gate, g)


def _post_res_bwd(dout, y, gate, g, name):
    T, D = y.shape
    tT = _tile(T, ROW_TILES)

    def body(do_ref, y_ref, gate_ref, g_ref, dy_ref, dgate_ref, dg_ref):
        @pl.when(pl.program_id(0) == 0)
        def _():
            dgate_ref[...] = jnp.zeros_like(dgate_ref)
            dg_ref[...] = jnp.zeros_like(dg_ref)

        yv, dov, gatev, gv = y_ref[...], do_ref[...], gate_ref[...], g_ref[...]
        r = lax.rsqrt(jnp.mean(yv * yv, axis=-1, keepdims=True) + EPS)
        yn = yv * r
        t = dov * yn
        dgate_ref[...] += jnp.sum(t * gv, axis=0, keepdims=True)
        dg_ref[...] += jnp.sum(t * gatev, axis=0, keepdims=True)
        dyn = dov * (gatev * gv)
        dy_ref[...] = (r * (dyn - yn * jnp.mean(dyn * yn, axis=-1, keepdims=True))).astype(dy_ref.dtype)

    return pl.pallas_call(
        body, name=name, grid=(T // tT,),
        in_specs=[_row_spec(tT, D), _row_spec(tT, D), _vec_spec(D), _vec_spec(D)],
        out_specs=[_row_spec(tT, D), _vec_spec(D), _vec_spec(D)],
        out_shape=[jax.ShapeDtypeStruct((T, D), BF16), jax.ShapeDtypeStruct((1, D), F32),
                   jax.ShapeDtypeStruct((1, D), F32)],
        compiler_params=_cp("arbitrary"),
    )(dout, y, gate, g)


def _pre_norm_bwd(dhs, x, g, scale, dres, name):
    T, D = x.shape
    tT = _tile(T, ROW_TILES)
    n = len(dhs)

    def body(*refs):
        dh_refs = refs[:n]
        x_ref, g_ref, sc_ref, dres_ref, dx_ref, dsh_ref, dsc_ref, dg_ref = refs[n:]

        @pl.when(pl.program_id(0) == 0)
        def _():
            dsh_ref[...] = jnp.zeros_like(dsh_ref)
            dsc_ref[...] = jnp.zeros_like(dsc_ref)
            dg_ref[...] = jnp.zeros_like(dg_ref)

        dh = dh_refs[0][...]
        for r_ in dh_refs[1:]:
            dh = dh + r_[...]
        xv, gv, scv = x_ref[...], g_ref[...], sc_ref[...]
        r = lax.rsqrt(jnp.mean(xv * xv, axis=-1, keepdims=True) + EPS)
        xn = xv * r
        t = dh * xn
        dsh_ref[...] += jnp.sum(dh, axis=0, keepdims=True)
        dsc_ref[...] += jnp.sum(t * gv, axis=0, keepdims=True)
        dg_ref[...] += jnp.sum(t * (1.0 + scv), axis=0, keepdims=True)
        dxn = dh * (gv * (1.0 + scv))
        dx_ref[...] = dres_ref[...] + r * (dxn - xn * jnp.mean(dxn * xn, axis=-1, keepdims=True))

    return pl.pallas_call(
        body, name=name, grid=(T // tT,),
        in_specs=[_row_spec(tT, D)] * n + [_row_spec(tT, D), _vec_spec(D), _vec_spec(D), _row_spec(tT, D)],
        out_specs=[_row_spec(tT, D), _vec_spec(D), _vec_spec(D), _vec_spec(D)],
        out_shape=[jax.ShapeDtypeStruct((T, D), F32)] + [jax.ShapeDtypeStruct((1, D), F32)] * 3,
        compiler_params=_cp("arbitrary"),
    )(*dhs, x, g, scale, dres)


def _loss_head(y, target, name):
    T, D = y.shape
    tT = _tile(T, ROW_TILES)

    def body(y_ref, t_ref, dy_ref, l_ref):
        @pl.when(pl.program_id(0) == 0)
        def _():
            l_ref[...] = jnp.zeros_like(l_ref)

        e = y_ref[...] - t_ref[...]
        dy_ref[...] = e * (1.0 / D)
        s = jnp.sum(jnp.mean(e * e, axis=-1, keepdims=True), axis=0, keepdims=True)
        l_ref[...] += 0.5 * s

    return pl.pallas_call(
        body, name=name, grid=(T // tT,),
        in_specs=[_row_spec(tT, D), _row_spec(tT, D)],
        out_specs=[_row_spec(tT, D), pl.BlockSpec((SUBLANES, LANES), lambda i: (0, 0))],
        out_shape=[jax.ShapeDtypeStruct((T, D), F32), jax.ShapeDtypeStruct((SUBLANES, LANES), F32)],
        compiler_params=_cp("arbitrary"),
    )(y, target)


HEAD_ROW_TILES = (1024, 512, 256, 128, 64)


def _hb(tT, off=0):
    return pl.BlockSpec((tT, HD), lambda i, h: (i, off + h))


def _hvec():
    return pl.BlockSpec((1, HD), lambda i, h: (0, 0))


def _headnorm(x, off, H, g, c1, post, out_dtype, name):
    T = x.shape[0]
    tT = _tile(T, HEAD_ROW_TILES)
    has_g = g is not None

    def body(*refs):
        x_ref = refs[0]
        o_ref = refs[-1]
        xv = x_ref[...]
        yv = xv * lax.rsqrt(c1 * jnp.sum(xv * xv, axis=-1, keepdims=True) + EPS)
        if has_g:
            yv = yv * refs[1][...]
        if post != 1.0:
            yv = yv * post
        o_ref[...] = yv.astype(o_ref.dtype)

    return pl.pallas_call(
        body, name=name, grid=(T // tT, H),
        in_specs=[_hb(tT, off)] + ([_hvec()] if has_g else []),
        out_specs=_hb(tT), out_shape=jax.ShapeDtypeStruct((T, H * HD), out_dtype),
        compiler_params=_cp("parallel", "parallel"),
    )(*([x, g] if has_g else [x]))


def _headnorm_bwd(dys, x, off, H, g, c1, post, out_dtype, name):
    T = x.shape[0]
    tT = _tile(T, HEAD_ROW_TILES)
    n = len(dys)
    has_g = g is not None

    def body(*refs):
        dy_refs = refs[:n]
        x_ref = refs[n]
        g_ref = refs[n + 1] if has_g else None
        dx_ref, dg_ref = refs[-2], refs[-1]

        @pl.when((pl.program_id(0) == 0) & (pl.program_id(1) == 0))
        def _():
            dg_ref[...] = jnp.zeros_like(dg_ref)

        dy = dy_refs[0][...].astype(F32)
        for r_ in dy_refs[1:]:
            dy = dy + r_[...].astype(F32)
        if post != 1.0:
            dy = dy * post
        xv = x_ref[...]
        r = lax.rsqrt(c1 * jnp.sum(xv * xv, axis=-1, keepdims=True) + EPS)
        xn = xv * r
        if has_g:
            dg_ref[...] += jnp.sum(dy * xn, axis=0, keepdims=True)
            dy = dy * g_ref[...]
        dx_ref[...] = (r * (dy - xn * (c1 * jnp.sum(dy * xn, axis=-1, keepdims=True)))).astype(dx_ref.dtype)

    dy_specs = [pl.BlockSpec((tT, HD), lambda i, h, st=st, of=of: (i, st * h + of)) for (_, st, of) in dys]
    return pl.pallas_call(
        body, name=name, grid=(T // tT, H),
        in_specs=dy_specs + [_hb(tT, off)] + ([_hvec()] if has_g else []),
        out_specs=[_hb(tT), _hvec()],
        out_shape=[jax.ShapeDtypeStruct((T, H * HD), out_dtype), jax.ShapeDtypeStruct((1, HD), F32)],
        compiler_params=_cp("arbitrary", "arbitrary"),
    )(*[d[0] for d in dys], x, *([g] if has_g else []))


def _fox_gate(ao, proj, og_off, H, name):
    T = ao.shape[0]
    tT = _tile(T, HEAD_ROW_TILES)

    def body(ao_ref, og_ref, o_ref):
        o_ref[...] = (ao_ref[...] * _sigmoid(og_ref[...])).astype(o_ref.dtype)

    return pl.pallas_call(
        body, name=name, grid=(T // tT, H),
        in_specs=[_hb(tT), _hb(tT, og_off)], out_specs=_hb(tT),
        out_shape=jax.ShapeDtypeStruct((T, H * HD), BF16), compiler_params=_cp("parallel", "parallel"),
    )(ao, proj)


def _fox_gate_bwd(dgated, ao, proj, og_off, H, name):
    T = ao.shape[0]
    tT = _tile(T, HEAD_ROW_TILES)

    def body(dg_ref, ao_ref, og_ref, dao_ref, dog_ref, delta_ref):
        dg, aov = dg_ref[...], ao_ref[...]
        sg = _sigmoid(og_ref[...])
        dao = dg * sg
        dao_ref[...] = dao.astype(dao_ref.dtype)
        dog_ref[...] = (dg * aov * sg * (1.0 - sg)).astype(dog_ref.dtype)
        delta_ref[...] = jnp.broadcast_to(jnp.sum(dao * aov, axis=-1, keepdims=True), delta_ref.shape)

    return pl.pallas_call(
        body, name=name, grid=(T // tT, H),
        in_specs=[_hb(tT), _hb(tT), _hb(tT, og_off)], out_specs=[_hb(tT)] * 3,
        out_shape=[jax.ShapeDtypeStruct((T, H * HD), BF16), jax.ShapeDtypeStruct((T, H * HD), BF16),
                   jax.ShapeDtypeStruct((T, H * HD), F32)],
        compiler_params=_cp("parallel", "parallel"),
    )(dgated, ao, proj)


def _gdn_out(o, proj, z_off, Hv, g, name):
    T = o.shape[0]
    tT = _tile(T, HEAD_ROW_TILES)

    def body(o_ref, z_ref, g_ref, y_ref):
        ov, zv = o_ref[...], z_ref[...]
        r = lax.rsqrt(jnp.mean(ov * ov, axis=-1, keepdims=True) + EPS)
        y_ref[...] = (((ov * r) * g_ref[...]) * _silu(zv)).astype(y_ref.dtype)

    return pl.pallas_call(
        body, name=name, grid=(T // tT, Hv),
        in_specs=[_hb(tT), _hb(tT, z_off), _hvec()], out_specs=_hb(tT),
        out_shape=jax.ShapeDtypeStruct((T, Hv * HD), BF16), compiler_params=_cp("parallel", "parallel"),
    )(o, proj, g)


def _gdn_out_bwd(dy, o, proj, z_off, Hv, g, name):
    T = o.shape[0]
    tT = _tile(T, HEAD_ROW_TILES)

    def body(dy_ref, o_ref, z_ref, g_ref, do_ref, dz_ref, dg_ref):
        @pl.when((pl.program_id(0) == 0) & (pl.program_id(1) == 0))
        def _():
            dg_ref[...] = jnp.zeros_like(dg_ref)

        dyv, ov, zv, gv = dy_ref[...], o_ref[...], z_ref[...], g_ref[...]
        r = lax.rsqrt(jnp.mean(ov * ov, axis=-1, keepdims=True) + EPS)
        on = ov * r
        sg = _sigmoid(zv)
        sz = zv * sg
        dz_ref[...] = (dyv * (on * gv) * (sg * (1.0 + zv * (1.0 - sg)))).astype(dz_ref.dtype)
        t = dyv * sz
        dg_ref[...] += jnp.sum(t * on, axis=0, keepdims=True)
        don = t * gv
        do_ref[...] = r * (don - on * jnp.mean(don * on, axis=-1, keepdims=True))

    return pl.pallas_call(
        body, name=name, grid=(T // tT, Hv),
        in_specs=[_hb(tT), _hb(tT), _hb(tT, z_off), _hvec()], out_specs=[_hb(tT), _hb(tT), _hvec()],
        out_shape=[jax.ShapeDtypeStruct((T, Hv * HD), F32), jax.ShapeDtypeStruct((T, Hv * HD), BF16),
                   jax.ShapeDtypeStruct((1, HD), F32)],
        compiler_params=_cp("arbitrary", "arbitrary"),
    )(dy, o, proj, g)


def _lrow(tT):
    return pl.BlockSpec((tT, LANES), lambda i: (i, 0))


def _lvec():
    return pl.BlockSpec((1, LANES), lambda i: (0, 0))


def _logsig(x, b, name):
    T = x.shape[0]
    tT = _tile(T, HEAD_ROW_TILES)

    def body(x_ref, b_ref, o_ref):
        o_ref[...] = -_softplus(-(x_ref[...] + b_ref[...]))

    return pl.pallas_call(body, name=name, grid=(T // tT,), in_specs=[_lrow(tT), _lvec()], out_specs=_lrow(tT),
                          out_shape=jax.ShapeDtypeStruct((T, LANES), F32), compiler_params=_cp("parallel"))(x, b)


def _logsig_bwd(dy, x, b, name):
    T = x.shape[0]
    tT = _tile(T, HEAD_ROW_TILES)

    def body(dy_ref, x_ref, b_ref, dx_ref, db_ref):
        @pl.when(pl.program_id(0) == 0)
        def _():
            db_ref[...] = jnp.zeros_like(db_ref)

        dx = dy_ref[...] * _sigmoid(-(x_ref[...] + b_ref[...]))
        dx_ref[...] = dx.astype(dx_ref.dtype)
        db_ref[...] += jnp.sum(dx, axis=0, keepdims=True)

    return pl.pallas_call(
        body, name=name, grid=(T // tT,), in_specs=[_lrow(tT), _lrow(tT), _lvec()], out_specs=[_lrow(tT), _lvec()],
        out_shape=[jax.ShapeDtypeStruct((T, LANES), BF16), jax.ShapeDtypeStruct((1, LANES), F32)],
        compiler_params=_cp("arbitrary"))(dy, x, b)


def _gdn_gates(ab, alog, dtb, name):
    T = ab.shape[0]
    tT = _tile(T, HEAD_ROW_TILES)

    def body(ab_ref, al_ref, dt_ref, g_ref, be_ref):
        v = ab_ref[...]
        g_ref[...] = -jnp.exp(al_ref[...]) * _softplus(v + dt_ref[...])
        be_ref[...] = _sigmoid(v)

    return pl.pallas_call(
        body, name=name, grid=(T // tT,), in_specs=[_lrow(tT), _lvec(), _lvec()], out_specs=[_lrow(tT)] * 2,
        out_shape=[jax.ShapeDtypeStruct((T, LANES), F32)] * 2, compiler_params=_cp("parallel"))(ab, alog, dtb)


def _gdn_gates_bwd(dg, dbeta, ab, alog, dtb, name):
    T = ab.shape[0]
    tT = _tile(T, HEAD_ROW_TILES)

    def body(dg_ref, dbe_ref, ab_ref, al_ref, dt_ref, dab_ref, dal_ref, ddt_ref):
        @pl.when(pl.program_id(0) == 0)
        def _():
            dal_ref[...] = jnp.zeros_like(dal_ref)
            ddt_ref[...] = jnp.zeros_like(ddt_ref)

        v, dgv = ab_ref[...], dg_ref[...]
        ea = jnp.exp(al_ref[...])
        z = v + dt_ref[...]
        da = dgv * (-ea * _sigmoid(z))
        sb = _sigmoid(v)
        dab_ref[...] = (da + dbe_ref[...] * sb * (1.0 - sb)).astype(dab_ref.dtype)
        dal_ref[...] += jnp.sum(dgv * (-ea * _softplus(z)), axis=0, keepdims=True)
        ddt_ref[...] += jnp.sum(da, axis=0, keepdims=True)

    return pl.pallas_call(
        body, name=name, grid=(T // tT,), in_specs=[_lrow(tT), _lrow(tT), _lrow(tT), _lvec(), _lvec()],
        out_specs=[_lrow(tT), _lvec(), _lvec()],
        out_shape=[jax.ShapeDtypeStruct((T, LANES), BF16), jax.ShapeDtypeStruct((1, LANES), F32),
                   jax.ShapeDtypeStruct((1, LANES), F32)],
        compiler_params=_cp("arbitrary"))(dg, dbeta, ab, alog, dtb)


def _cumsum(x, seg, reverse, name):
    T = x.shape[0]
    tb = _tile(T, (256, 128, 64))
    nb = T // tb
    carry = seg is None

    def body(x_ref, o_ref, c_ref):
        @pl.when(pl.program_id(0) == 0)
        def _():
            c_ref[...] = jnp.zeros_like(c_ref)

        ri = lax.broadcasted_iota(jnp.int32, (tb, tb), 0)
        ci = lax.broadcasted_iota(jnp.int32, (tb, tb), 1)
        keep = (ci >= ri) if reverse else (ci <= ri)
        if seg is not None:
            keep = keep & ((ri // seg) == (ci // seg))
        y = _hdot(keep.astype(F32), x_ref[...])
        if carry:
            y = y + c_ref[...]
            c_ref[...] = y[0:1, :] if reverse else y[tb - 1:tb, :]
        o_ref[...] = y

    imap = (lambda i: (nb - 1 - i, 0)) if reverse else (lambda i: (i, 0))
    return pl.pallas_call(
        body, name=name, grid=(nb,), in_specs=[pl.BlockSpec((tb, LANES), imap)],
        out_specs=pl.BlockSpec((tb, LANES), imap), out_shape=jax.ShapeDtypeStruct((T, LANES), F32),
        scratch_shapes=[pltpu.VMEM((1, LANES), F32)], compiler_params=_cp("arbitrary"))(x)


ATT_TILES = (512, 256, 128)


def _att_scores(q, k, fq, fk, qi, ki, tq):
    s = _dot(q, k, 'nt') * (HD ** -0.5) + fq[:, :1] - fk
    row = qi * tq + lax.broadcasted_iota(jnp.int32, (tq, tq), 0)
    col = ki * tq + lax.broadcasted_iota(jnp.int32, (tq, tq), 1)
    return s, col <= row


def _flash_fwd(qn, kn, vb, fqb, fkr, H, name):
    T = qn.shape[0]
    tq = _tile(T, ATT_TILES)
    nq = T // tq

    def body(q_ref, k_ref, v_ref, fq_ref, fk_ref, o_ref, lse_ref, m_s, l_s, acc_s):
        qi, ki = pl.program_id(1), pl.program_id(2)

        @pl.when(ki == 0)
        def _():
            m_s[...] = jnp.full_like(m_s, NEG)
            l_s[...] = jnp.zeros_like(l_s)
            acc_s[...] = jnp.zeros_like(acc_s)

        @pl.when(ki <= qi)
        def _():
            s, keep = _att_scores(q_ref[...], k_ref[...], fq_ref[...], fk_ref[...], qi, ki, tq)
            s = jnp.where(keep, s, NEG)
            m_prev = m_s[...]
            m_new = jnp.maximum(m_prev, jnp.max(s, axis=1, keepdims=True))
            alpha = jnp.exp(m_prev - m_new)
            p = jnp.exp(s - m_new[:, :1])
            l_s[...] = alpha * l_s[...] + jnp.sum(p, axis=1, keepdims=True)
            acc_s[...] = acc_s[...] * alpha + _bdot(p, v_ref[...])
            m_s[...] = m_new

        @pl.when(ki == qi)
        def _():
            o_ref[...] = acc_s[...] / l_s[...]
            lse_ref[...] = m_s[...] + jnp.log(l_s[...])

    qspec = pl.BlockSpec((tq, HD), lambda h, qi, ki: (qi, h))
    kspec = pl.BlockSpec((tq, HD), lambda h, qi, ki: (jnp.minimum(ki, qi), h))
    fkspec = pl.BlockSpec((None, 1, tq), lambda h, qi, ki: (h, 0, jnp.minimum(ki, qi)))
    return pl.pallas_call(
        body, name=name, grid=(H, nq, nq),
        in_specs=[qspec, kspec, kspec, qspec, fkspec], out_specs=[qspec, qspec],
        out_shape=[jax.ShapeDtypeStruct((T, H * HD), F32)] * 2,
        scratch_shapes=[pltpu.VMEM((tq, HD), F32)] * 3,
        compiler_params=_cp("parallel", "parallel", "arbitrary"),
    )(qn, kn, vb, fqb, fkr)


def _flash_bwd_dq(qn, kn, vb, fqb, fkr, dao, lse, delta, H, name):
    T = qn.shape[0]
    tq = _tile(T, ATT_TILES)
    nq = T // tq

    def body(q_ref, k_ref, v_ref, fq_ref, fk_ref, do_ref, lse_ref, dl_ref, dq_ref, dfq_ref, dq_s, dfq_s):
        qi, ki = pl.program_id(1), pl.program_id(2)

        @pl.when(ki == 0)
        def _():
            dq_s[...] = jnp.zeros_like(dq_s)
            dfq_s[...] = jnp.zeros_like(dfq_s)

        @pl.when(ki <= qi)
        def _():
            s, keep = _att_scores(q_ref[...], k_ref[...], fq_ref[...], fk_ref[...], qi, ki, tq)
            p = jnp.where(keep, jnp.exp(s - lse_ref[...][:, :1]), 0.0)
            dp = _dot(do_ref[...], v_ref[...], 'nt')
            ds = p * (dp - dl_ref[...][:, :1])
            dq_s[...] += _bdot(ds, k_ref[...]) * (HD ** -0.5)
            dfq_s[...] += jnp.sum(ds, axis=1, keepdims=True)

        @pl.when(ki == qi)
        def _():
            dq_ref[...] = dq_s[...]
            dfq_ref[...] = dfq_s[...]

    qspec = pl.BlockSpec((tq, HD), lambda h, qi, ki: (qi, h))
    kspec = pl.BlockSpec((tq, HD), lambda h, qi, ki: (jnp.minimum(ki, qi), h))
    fkspec = pl.BlockSpec((None, 1, tq), lambda h, qi, ki: (h, 0, jnp.minimum(ki, qi)))
    return pl.pallas_call(
        body, name=name, grid=(H, nq, nq),
        in_specs=[qspec, kspec, kspec, qspec, fkspec, qspec, qspec, qspec], out_specs=[qspec, qspec],
        out_shape=[jax.ShapeDtypeStruct((T, H * HD), F32)] * 2,
        scratch_shapes=[pltpu.VMEM((tq, HD), F32)] * 2,
        compiler_params=_cp("parallel", "parallel", "arbitrary"),
    )(qn, kn, vb, fqb, fkr, dao, lse, delta)


def _flash_bwd_dkv(qn, kn, vb, fqb, fkr, dao, lse, delta, H, name):
    T = qn.shape[0]
    tq = _tile(T, ATT_TILES)
    nq = T // tq

    def body(q_ref, k_ref, v_ref, fq_ref, fk_ref, do_ref, lse_ref, dl_ref, dk_ref, dv_ref, dfk_ref, dk_s, dv_s, dfk_s):
        ki, qi = pl.program_id(1), pl.program_id(2)

        @pl.when(qi == 0)
        def _():
            dk_s[...] = jnp.zeros_like(dk_s)
            dv_s[...] = jnp.zeros_like(dv_s)
            dfk_s[...] = jnp.zeros_like(dfk_s)

        @pl.when(qi >= ki)
        def _():
            s, keep = _att_scores(q_ref[...], k_ref[...], fq_ref[...], fk_ref[...], qi, ki, tq)
            p = jnp.where(keep, jnp.exp(s - lse_ref[...][:, :1]), 0.0)
            dp = _dot(do_ref[...], v_ref[...], 'nt')
            ds = p * (dp - dl_ref[...][:, :1])
            dv_s[...] += _bdot(p, do_ref[...], 'tn')
            dk_s[...] += _bdot(ds, q_ref[...], 'tn') * (HD ** -0.5)
            dfk_s[...] -= jnp.sum(ds, axis=0, keepdims=True)

        @pl.when(qi == nq - 1)
        def _():
            dk_ref[...] = dk_s[...]
            dv_ref[...] = dv_s[...].astype(dv_ref.dtype)
            dfk_ref[...] = dfk_s[...]

    qspec = pl.BlockSpec((tq, HD), lambda h, ki, qi: (jnp.maximum(qi, ki), h))
    kspec = pl.BlockSpec((tq, HD), lambda h, ki, qi: (ki, h))
    fkspec = pl.BlockSpec((None, 1, tq), lambda h, ki, qi: (h, 0, ki))
    return pl.pallas_call(
        body, name=name, grid=(H, nq, nq),
        in_specs=[qspec, kspec, kspec, qspec, fkspec, qspec, qspec, qspec], out_specs=[kspec, kspec, fkspec],
        out_shape=[jax.ShapeDtypeStruct((T, H * HD), F32), jax.ShapeDtypeStruct((T, H * HD), BF16),
                   jax.ShapeDtypeStruct((H, 1, T), F32)],
        scratch_shapes=[pltpu.VMEM((tq, HD), F32), pltpu.VMEM((tq, HD), F32), pltpu.VMEM((1, tq), F32)],
        compiler_params=_cp("parallel", "parallel", "arbitrary"),
    )(qn, kn, vb, fqb, fkr, dao, lse, delta)


CONV_TILES = (512, 256, 128, 64)
HALO = SUBLANES


def _dwconv(x, xoff, W, w, b, act, voff, out_dtype, name):
    T = x.shape[0]
    K = w.shape[0]
    tT, tC = _tile(T, CONV_TILES), _tile(W, CONV_TILES)
    xb, hb = xoff // tC, tT // HALO
    glu = act == 'glu'

    def body(*refs):
        if glu:
            x_ref, xp_ref, w_ref, b_ref, v_ref, o_ref, buf = refs
        else:
            x_ref, xp_ref, w_ref, o_ref, buf = refs
        i = pl.program_id(0)
        buf[0:HALO, :] = jnp.where(i > 0, xp_ref[...], 0.0)
        buf[HALO:, :] = x_ref[...]
        conv = w_ref[0:1, :] * buf[pl.ds(HALO - (K - 1), tT), :]
        for k in range(1, K):
            conv = conv + w_ref[k:k + 1, :] * buf[pl.ds(HALO - (K - 1) + k, tT), :]
        if glu:
            o_ref[...] = (_gelu(conv + b_ref[...]) * v_ref[...]).astype(o_ref.dtype)
        else:
            o_ref[...] = _silu(conv).astype(o_ref.dtype)

    cur = pl.BlockSpec((tT, tC), lambda i, j: (i, xb + j))
    prev = pl.BlockSpec((HALO, tC), lambda i, j: (jnp.maximum(i * hb - 1, 0), xb + j))
    wspec = pl.BlockSpec((K, tC), lambda i, j: (0, j))
    in_specs, args = [cur, prev, wspec], [x, x, w]
    if glu:
        vb = voff // tC
        in_specs += [pl.BlockSpec((1, tC), lambda i, j: (0, j)), pl.BlockSpec((tT, tC), lambda i, j: (i, vb + j))]
        args += [b, x]
    return pl.pallas_call(
        body, name=name, grid=(T // tT, W // tC), in_specs=in_specs,
        out_specs=pl.BlockSpec((tT, tC), lambda i, j: (i, j)), out_shape=jax.ShapeDtypeStruct((T, W), out_dtype),
        scratch_shapes=[pltpu.VMEM((tT + HALO, tC), F32)], compiler_params=_cp("parallel", "parallel"),
    )(*args)


def _dwconv_bwd(x, xoff, W, w, woff, b, act, voff, dy, name):
    T = x.shape[0]
    K = w.shape[0]
    tT, tC = _tile(T, CONV_TILES), _tile(W, CONV_TILES)
    xb, wb, hb, nT = xoff // tC, woff // tC, tT // HALO, T // tT
    last_halo = T // HALO - 1
    glu = act == 'glu'

    def body(*refs):
        if glu:
            (x_ref, xp_ref, xn_ref, dy_ref, dyn_ref, w_ref, b_ref, v_ref, vn_ref,
             dx_ref, dv_ref, dw_ref, db_ref, xbuf, dybuf, dbuf, vbuf) = refs
        else:
            x_ref, xp_ref, xn_ref, dy_ref, dyn_ref, w_ref, dx_ref, dw_ref, xbuf, dybuf, dbuf = refs
        i = pl.program_id(1)

        @pl.when(i == 0)
        def _():
            dw_ref[...] = jnp.zeros_like(dw_ref)
            if glu:
                db_ref[...] = jnp.zeros_like(db_ref)

        ext = tT + HALO
        xbuf[0:HALO, :] = jnp.where(i > 0, xp_ref[...], 0.0)
        xbuf[HALO:HALO + tT, :] = x_ref[...]
        xbuf[HALO + tT:, :] = xn_ref[...]
        dybuf[0:tT, :] = dy_ref[...].astype(F32)
        dybuf[tT:, :] = jnp.where(i < nT - 1, dyn_ref[...].astype(F32), 0.0)
        conv = w_ref[0:1, :] * xbuf[pl.ds(HALO - (K - 1), ext), :]
        for k in range(1, K):
            conv = conv + w_ref[k:k + 1, :] * xbuf[pl.ds(HALO - (K - 1) + k, ext), :]
        dyv = dybuf[...]
        if glu:
            vbuf[0:tT, :] = v_ref[...]
            vbuf[tT:, :] = vn_ref[...]
            z = conv + b_ref[...]
            dconv = dyv * vbuf[...] * _gelu_grad(z)
            dv_ref[...] = (dyv[0:tT, :] * _gelu(z[0:tT, :])).astype(dv_ref.dtype)
        else:
            sg = _sigmoid(conv)
            dconv = dyv * (sg * (1.0 + conv * (1.0 - sg)))
        dbuf[...] = dconv
        dx = w_ref[0:1, :] * dbuf[pl.ds(K - 1, tT), :]
        for k in range(1, K):
            dx = dx + w_ref[k:k + 1, :] * dbuf[pl.ds(K - 1 - k, tT), :]
        dx_ref[...] = dx.astype(dx_ref.dtype)
        dc = dconv[0:tT, :]
        for k in range(K):
            dw_ref[k:k + 1, :] += jnp.sum(dc * xbuf[pl.ds(HALO - (K - 1) + k, tT), :], axis=0, keepdims=True)
        if glu:
            db_ref[...] += jnp.sum(dc, axis=0, keepdims=True)

    def cur(off):
        return pl.BlockSpec((tT, tC), lambda j, i: (i, off + j))

    def nxt(off):
        return pl.BlockSpec((HALO, tC), lambda j, i: (jnp.minimum((i + 1) * hb, last_halo), off + j))

    prev = pl.BlockSpec((HALO, tC), lambda j, i: (jnp.maximum(i * hb - 1, 0), xb + j))
    wspec = pl.BlockSpec((K, tC), lambda j, i: (0, wb + j))
    acc_w = pl.BlockSpec((K, tC), lambda j, i: (0, j))
    acc_b = pl.BlockSpec((1, tC), lambda j, i: (0, j))
    in_specs = [cur(xb), prev, nxt(xb), cur(0), nxt(0), wspec]
    args = [x, x, x, dy, dy, w]
    out_specs = [cur(0)]
    out_shape = [jax.ShapeDtypeStruct((T, W), BF16)]
    scratch = [pltpu.VMEM((tT + 2 * HALO, tC), F32), pltpu.VMEM((tT + HALO, tC), F32), pltpu.VMEM((tT + HALO, tC), F32)]
    if glu:
        vb = voff // tC
        in_specs += [pl.BlockSpec((1, tC), lambda j, i: (0, wb + j)), cur(vb), nxt(vb)]
        args += [b, x, x]
        out_specs += [cur(0), acc_w, acc_b]
        out_shape += [jax.ShapeDtypeStruct((T, W), BF16), jax.ShapeDtypeStruct((K, W), F32),
                      jax.ShapeDtypeStruct((1, W), F32)]
        scratch += [pltpu.VMEM((tT + HALO, tC), F32)]
    else:
        out_specs += [acc_w]
        out_shape += [jax.ShapeDtypeStruct((K, W), F32)]
    return pl.pallas_call(
        body, name=name, grid=(W // tC, nT), in_specs=in_specs, out_specs=out_specs, out_shape=out_shape,
        scratch_shapes=scratch, compiler_params=_cp("parallel", "arbitrary"),
    )(*args)


V_PER_K = 2


def _gdn_intra(q, k, v, gb, bb):
    C = q.shape[0]
    ri = lax.broadcasted_iota(jnp.int32, (C, C), 0)
    ci = lax.broadcasted_iota(jnp.int32, (C, C), 1)
    lower, strict = ri >= ci, ri > ci
    pick0 = (lax.broadcasted_iota(jnp.int32, (C, HD), 1) == 0).astype(F32)
    g_cols = _hdot(pick0, gb, 'nt')
    dm = jnp.exp(jnp.where(lower, gb[:, :C] - g_cols, NEG))
    kk = _bdot(k, k, 'nt')
    a = jnp.where(strict, kk * dm * bb[:, :C], 0.0)
    eye = (ri == ci).astype(F32)
    p = -a
    tm = eye + p
    for _ in range(5):
        p = _hdot(p, p)
        tm = tm + _hdot(tm, p)
    eg = jnp.exp(gb)
    u = _hdot(tm, v * bb)
    w = _hdot(tm, k * (bb * eg))
    qkr = _bdot(q, k, 'nt')
    qkm = jnp.where(lower, qkr * dm, 0.0)
    gl = gb[C - 1:C, :]
    qd = q * eg
    kd = k * jnp.exp(gl - gb)
    return dict(lower=lower, strict=strict, dm=dm, kk=kk, a=a, tm=tm, eg=eg, u=u, w=w, qkr=qkr, qkm=qkm,
                gl=gl, qd=qd, kd=kd)


def _gdn_chunk_fwd(qn, kn, qkvc, voff, gcb, betab, Hk, name):
    T = qn.shape[0]
    C = GDN_CHUNK
    N = T // C
    Hv = Hk * V_PER_K
    vb = voff // (V_PER_K * HD)

    def body(q_ref, k_ref, v_ref, g_ref, b_ref, o_ref, sp_ref, s_s):
        @pl.when(pl.program_id(1) == 0)
        def _():
            s_s[...] = jnp.zeros_like(s_s)

        q, k = q_ref[...], k_ref[...]
        for j in range(V_PER_K):
            sl = slice(j * HD, (j + 1) * HD)
            it = _gdn_intra(q, k, v_ref[:, sl], g_ref[:, sl], b_ref[:, sl])
            s = s_s[j]
            sp_ref[j] = s
            vn = it['u'] - _bdot(it['w'], s)
            o_ref[:, sl] = _bdot(it['qd'], s) + _bdot(it['qkm'], vn)
            s_s[j] = s * jnp.exp(it['gl']) + _bdot(it['kd'], vn, 'tn')

    kspec = pl.BlockSpec((C, HD), lambda h, n: (n, h))
    pair = pl.BlockSpec((C, V_PER_K * HD), lambda h, n: (n, h))
    vspec = pl.BlockSpec((C, V_PER_K * HD), lambda h, n: (n, vb + h))
    sspec = pl.BlockSpec((V_PER_K, None, HD, HD), lambda h, n: (h, n, 0, 0))
    return pl.pallas_call(
        body, name=name, grid=(Hk, N), in_specs=[kspec, kspec, vspec, pair, pair], out_specs=[pair, sspec],
        out_shape=[jax.ShapeDtypeStruct((T, Hv * HD), F32), jax.ShapeDtypeStruct((Hv, N, HD, HD), F32)],
        scratch_shapes=[pltpu.VMEM((V_PER_K, HD, HD), F32)], compiler_params=_cp("parallel", "arbitrary"),
    )(qn, kn, qkvc, gcb, betab)


def _gdn_chunk_bwd(qn, kn, qkvc, voff, gcb, betab, sprev, do, Hk, name):
    T = qn.shape[0]
    C = GDN_CHUNK
    N = T // C
    Hv = Hk * V_PER_K
    vb = voff // (V_PER_K * HD)

    def body(q_ref, k_ref, v_ref, g_ref, b_ref, sp_ref, do_ref, dq_ref, dk_ref, dv_ref, dg_ref, dbe_ref, ds_s):
        @pl.when(pl.program_id(1) == 0)
        def _():
            ds_s[...] = jnp.zeros_like(ds_s)

        q, k = q_ref[...], k_ref[...]
        ri = lax.broadcasted_iota(jnp.int32, (C, C), 0)
        ci = lax.broadcasted_iota(jnp.int32, (C, C), 1)
        suffix = (ci >= ri).astype(F32)
        ones = jnp.ones((C, HD), F32)
        last = (lax.broadcasted_iota(jnp.int32, (C, HD), 0) == C - 1).astype(F32)
        for j in range(V_PER_K):
            sl = slice(j * HD, (j + 1) * HD)
            v, gb, bb = v_ref[:, sl], g_ref[:, sl], b_ref[:, sl]
            it = _gdn_intra(q, k, v, gb, bb)
            s, dsn, dov = sp_ref[j], ds_s[j], do_ref[:, sl]
            eg, gl, dm, tm, u, w = it['eg'], it['gl'], it['dm'], it['tm'], it['u'], it['w']
            egl = jnp.exp(gl)
            vn = u - _bdot(w, s)
            dqd = _bdot(dov, s, 'nt')
            dqkm = jnp.where(it['lower'], _bdot(dov, vn, 'nt'), 0.0)
            dvn = _bdot(it['qkm'], dov, 'tn')
            dkd = _bdot(vn, dsn, 'nt')
            dvn = dvn + _bdot(it['kd'], dsn)
            dgl = jnp.sum(jnp.sum(dsn * s, axis=0, keepdims=True) * egl, axis=1, keepdims=True)
            dw = -_bdot(dvn, s, 'nt')
            ds_s[j] = dsn * egl + _bdot(it['qd'], dov, 'tn') - _bdot(w, dvn, 'tn')
            dbv = _hdot(tm, dvn, 'tn')
            dbk = _hdot(tm, dw, 'tn')
            da = jnp.where(it['strict'], -(_bdot(dbv, u, 'nt') + _bdot(dbk, w, 'nt')), 0.0)
            dv_ref[:, sl] = dbv * bb
            rk = jnp.sum(dbk * k, axis=-1, keepdims=True)
            dbeta = jnp.sum(dbv * v, axis=-1, keepdims=True) + rk * eg[:, :1] \
                + jnp.sum(da * it['kk'] * dm, axis=-1, keepdims=True)
            dk = dbk * (bb * eg)
            dg = rk * (bb[:, :1] * eg[:, :1])
            dkk = da * dm * bb[:, :C]
            dk = dk + _bdot(dkk, k) + _bdot(dkk, k, 'tn')
            dqkr = dqkm * dm
            dq = _bdot(dqkr, k)
            dk = dk + _bdot(dqkr, q, 'tn')
            de = da * it['a'] + dqkm * it['qkm']
            dg = dg + jnp.sum(de, axis=-1, keepdims=True) - _hdot(de, ones, 'tn')[:, :1]
            dq = dq + dqd * eg
            dg = dg + jnp.sum(dqd * it['qd'], axis=-1, keepdims=True)
            dk = dk + dkd * jnp.exp(gl - gb)
            sk = jnp.sum(dkd * it['kd'], axis=-1, keepdims=True)
            dg = dg - sk
            dgl = dgl + jnp.sum(sk, axis=0, keepdims=True)
            dgb = jnp.broadcast_to(dg, (C, HD)) + last * dgl
            dq_ref[:, sl] = dq
            dk_ref[:, sl] = dk
            dg_ref[:, sl] = _hdot(suffix, dgb)
            dbe_ref[:, sl] = jnp.broadcast_to(dbeta, (C, HD))

    kspec = pl.BlockSpec((C, HD), lambda h, n: (N - 1 - n, h))
    pair = pl.BlockSpec((C, V_PER_K * HD), lambda h, n: (N - 1 - n, h))
    vspec = pl.BlockSpec((C, V_PER_K * HD), lambda h, n: (N - 1 - n, vb + h))
    sspec = pl.BlockSpec((V_PER_K, None, HD, HD), lambda h, n: (h, N - 1 - n, 0, 0))
    return pl.pallas_call(
        body, name=name, grid=(Hk, N), in_specs=[kspec, kspec, vspec, pair, pair, sspec, pair],
        out_specs=[pair] * 5, out_shape=[jax.ShapeDtypeStruct((T, Hv * HD), F32)] * 5,
        scratch_shapes=[pltpu.VMEM((V_PER_K, HD, HD), F32)], compiler_params=_cp("parallel", "arbitrary"),
    )(qn, kn, qkvc, gcb, betab, sprev, do)


def _adamw_math(w, g, m, v):
    m = ADAM_B1 * m + (1.0 - ADAM_B1) * g
    v = ADAM_B2 * v + (1.0 - ADAM_B2) * jnp.square(g)
    m_hat = m / (1.0 - ADAM_B1 ** ADAM_STEP)
    v_hat = v / (1.0 - ADAM_B2 ** ADAM_STEP)
    delta = -ADAM_LR * (m_hat / (jnp.sqrt(v_hat) + ADAM_EPS) + ADAM_WD * w)
    return delta, m, v


def _adamw(w, m, v, gs, name):
    R, C = w.shape
    tr = _tile(R, (512, 256, 128, 64, 32, 16, 8))
    n = len(gs)

    def body(*refs):
        w_ref, m_ref, v_ref = refs[:3]
        g_refs = refs[3:3 + n]
        g_out, d_out, m_out, v_out = refs[3 + n:]
        g = g_refs[0][...]
        for r_ in g_refs[1:]:
            g = g + r_[...]
        g_out[...] = g
        d_out[...], m_out[...], v_out[...] = _adamw_math(w_ref[...], g, m_ref[...], v_ref[...])

    spec = pl.BlockSpec((tr, C), lambda i: (i, 0))
    return pl.pallas_call(
        body, name=name, grid=(R // tr,), in_specs=[spec] * (3 + n), out_specs=[spec] * 4,
        out_shape=[jax.ShapeDtypeStruct((R, C), F32)] * 4, compiler_params=_cp("parallel"),
    )(w, m, v, *gs)


def _sum_devices(g8, name):
    _, M, C = g8.shape
    tr = _tile(M, (512, 256, 128, 64, 32, 16, 8))

    def body(g_ref, o_ref):
        acc = g_ref[0]
        for d in range(1, N_DEV):
            acc = acc + g_ref[d]
        o_ref[...] = acc

    return pl.pallas_call(
        body, name=name, grid=(M // tr,), in_specs=[pl.BlockSpec((N_DEV, tr, C), lambda i: (0, i, 0))],
        out_specs=pl.BlockSpec((tr, C), lambda i: (i, 0)), out_shape=jax.ShapeDtypeStruct((M, C), F32),
        compiler_params=_cp("parallel"),
    )(g8)


def _ada_w_update(c_all, dm, w, m, v, name):
    n_mod, D, Ns = w.shape
    tr = _tile(D, (256, 128))

    def body(c_ref, dm_ref, w_ref, m_ref, v_ref, g_out, d_out, m_out, v_out):
        g = _hdot(_silu(c_ref[...]), dm_ref[...], 'tn')
        g_out[...] = g
        d_out[...], m_out[...], v_out[...] = _adamw_math(w_ref[...], g, m_ref[...], v_ref[...])

    wspec = pl.BlockSpec((None, tr, Ns), lambda i, r: (i, r, 0))
    return pl.pallas_call(
        body, name=name, grid=(n_mod, D // tr),
        in_specs=[pl.BlockSpec((N_DEV, tr), lambda i, r: (0, r)), pl.BlockSpec((None, N_DEV, Ns), lambda i, r: (i, 0, 0)),
                  wspec, wspec, wspec],
        out_specs=[wspec] * 4, out_shape=[jax.ShapeDtypeStruct((n_mod, D, Ns), F32)] * 4,
        compiler_params=_cp("parallel", "parallel"),
    )(c_all, dm, w, m, v)


def _place():
    return lax.axis_index("x"), lax.axis_index("y"), lax.axis_index("c")


def _allgather8(x_shard, name):
    m_per, n = x_shard.shape

    def body(x_ref, out_ref, send_sems, recv_sems, local_sem):
        x, y, c = _place()
        me, sibling = (x, y, c), (x, y, 1 - c)
        chips = [(1 - x, y), (x, 1 - y), (1 - x, 1 - y)]

        def rows(px, py, pc):
            return out_ref.at[pl.ds((4 * px + 2 * py + pc) * m_per, m_per), :]

        def copy(k, block, to, src=None):
            return pltpu.make_async_remote_copy(
                src_ref=rows(*block) if src is None else src, dst_ref=rows(*block),
                send_sem=send_sems.at[k], recv_sem=recv_sems.at[k], device_id=to, device_id_type=MESH)

        mine = pltpu.make_async_copy(x_ref, rows(*me), local_sem)
        mine.start()
        first = [copy(0, me, sibling, src=x_ref)]
        first += [copy(1 + j, me, (*chip, c), src=x_ref) for j, chip in enumerate(chips)]
        for cp in first:
            cp.start()
        passed = [copy(4 + j, (*chip, c), sibling) for j, chip in enumerate(chips)]
        for j, chip in enumerate(chips):
            copy(1 + j, (*chip, c), me).wait_recv()
            passed[j].start()
        copy(0, sibling, me).wait_recv()
        for j, chip in enumerate(chips):
            copy(4 + j, (*chip, 1 - c), me).wait_recv()
        for cp in first + passed:
            cp.wait_send()
        mine.wait()

    return pl.pallas_call(
        body, name=name, out_shape=jax.ShapeDtypeStruct((N_DEV * m_per, n), x_shard.dtype),
        in_specs=[pl.BlockSpec(memory_space=pltpu.VMEM)], out_specs=pl.BlockSpec(memory_space=pltpu.VMEM),
        scratch_shapes=[pltpu.SemaphoreType.DMA((7,)), pltpu.SemaphoreType.DMA((7,)), pltpu.SemaphoreType.DMA],
        compiler_params=pltpu.CompilerParams(vmem_limit_bytes=VMEM_LIMIT),
    )(x_shard)


HBM_SPEC = pl.BlockSpec(memory_space=pltpu.HBM)


def _gather_weights(w_flat, name):
    R, C = w_flat.shape
    half = R // 2

    def body(w_ref, out_ref, send_sems, recv_sems, local_sem):
        x, y, c = _place()
        sibling = (x, y, 1 - c)
        chips = [(1 - x, y), (x, 1 - y), (1 - x, 1 - y)]

        def part(cx, cy, hc):
            return out_ref.at[2 * cx + cy, pl.ds(hc * half, half), :]

        def copy(k, block, to, src=None):
            return pltpu.make_async_remote_copy(
                src_ref=part(*block) if src is None else src, dst_ref=part(*block),
                send_sem=send_sems.at[k], recv_sem=recv_sems.at[k], device_id=to, device_id_type=MESH)

        mine = pltpu.make_async_copy(w_ref, out_ref.at[2 * x + y], local_sem)
        mine.start()
        first = [copy(j, (x, y, c), (*chip, c), src=w_ref.at[pl.ds(c * half, half), :]) for j, chip in enumerate(chips)]
        for cp in first:
            cp.start()
        passed = [copy(3 + j, (*chip, c), sibling) for j, chip in enumerate(chips)]
        for j, chip in enumerate(chips):
            copy(j, (*chip, c), (x, y, c)).wait_recv()
            passed[j].start()
        for j, chip in enumerate(chips):
            copy(3 + j, (*chip, 1 - c), (x, y, c)).wait_recv()
        for cp in first + passed:
            cp.wait_send()
        mine.wait()

    return pl.pallas_call(
        body, name=name, out_shape=jax.ShapeDtypeStruct((N_CHIPS, R, C), w_flat.dtype),
        in_specs=[HBM_SPEC], out_specs=HBM_SPEC,
        scratch_shapes=[pltpu.SemaphoreType.DMA((6,)), pltpu.SemaphoreType.DMA((6,)), pltpu.SemaphoreType.DMA],
    )(w_flat)


def _swap_halves(g, name):
    n, R, C = g.shape
    half = R // 2

    def body(g_ref, got_ref, send_sem, recv_sem):
        x, y, c = _place()
        cp = pltpu.make_async_remote_copy(
            src_ref=g_ref.at[:, pl.ds((1 - c) * half, half), :], dst_ref=got_ref,
            send_sem=send_sem, recv_sem=recv_sem, device_id=(x, y, 1 - c), device_id_type=MESH)
        cp.start()
        cp.wait()

    return pl.pallas_call(
        body, name=name, out_shape=jax.ShapeDtypeStruct((n, half, C), g.dtype),
        in_specs=[HBM_SPEC], out_specs=HBM_SPEC,
        scratch_shapes=[pltpu.SemaphoreType.DMA, pltpu.SemaphoreType.DMA],
    )(g)


def _scatter_chips(q, name):
    n, R2, C = q.shape

    def body(q_ref, got_ref, send_sems, recv_sems):
        x, y, c = _place()
        chips = [(1 - x, y), (x, 1 - y), (1 - x, 1 - y)]
        cps = [pltpu.make_async_remote_copy(
            src_ref=q_ref.at[2 * cx + cy], dst_ref=got_ref.at[j], send_sem=send_sems.at[j], recv_sem=recv_sems.at[j],
            device_id=(cx, cy, c), device_id_type=MESH) for j, (cx, cy) in enumerate(chips)]
        for cp in cps:
            cp.start()
        for cp in cps:
            cp.wait()

    return pl.pallas_call(
        body, name=name, out_shape=jax.ShapeDtypeStruct((3, R2, C), q.dtype),
        in_specs=[HBM_SPEC], out_specs=HBM_SPEC,
        scratch_shapes=[pltpu.SemaphoreType.DMA((3,)), pltpu.SemaphoreType.DMA((3,))],
    )(q)


def _join_halves(h, name):
    R2, C = h.shape

    def body(h_ref, out_ref, send_sem, recv_sem, local_sem):
        x, y, c = _place()
        mine = pltpu.make_async_copy(h_ref, out_ref.at[pl.ds(c * R2, R2), :], local_sem)
        mine.start()
        cp = pltpu.make_async_remote_copy(
            src_ref=h_ref, dst_ref=out_ref.at[pl.ds(c * R2, R2), :],
            send_sem=send_sem, recv_sem=recv_sem, device_id=(x, y, 1 - c), device_id_type=MESH)
        cp.start()
        cp.wait()
        mine.wait()

    return pl.pallas_call(
        body, name=name, out_shape=jax.ShapeDtypeStruct((2 * R2, C), h.dtype),
        in_specs=[HBM_SPEC], out_specs=HBM_SPEC,
        scratch_shapes=[pltpu.SemaphoreType.DMA, pltpu.SemaphoreType.DMA, pltpu.SemaphoreType.DMA],
    )(h)


def _add_halves(g, got, c_idx, name):
    n, R, C = g.shape
    half = R // 2
    tr = _tile(half, (512, 256, 128, 64, 32, 16))
    nb = half // tr

    def body(c_ref, g_ref, got_ref, o_ref):
        o_ref[...] = (g_ref[...] + got_ref[...]).astype(o_ref.dtype)

    grid_spec = pltpu.PrefetchScalarGridSpec(
        num_scalar_prefetch=1, grid=(n, nb),
        in_specs=[pl.BlockSpec((None, tr, C), lambda s, i, c_ref: (s, c_ref[0] * nb + i, 0)),
                  pl.BlockSpec((None, tr, C), lambda s, i, c_ref: (s, i, 0))],
        out_specs=pl.BlockSpec((None, tr, C), lambda s, i, c_ref: (s, i, 0)))
    return pl.pallas_call(
        body, name=name, grid_spec=grid_spec, out_shape=jax.ShapeDtypeStruct((n, half, C), BF16),
        compiler_params=_cp("parallel", "parallel"),
    )(c_idx, g, got)


def _add_chips(q, got, s_idx, name):
    n, R2, C = q.shape
    tr = _tile(R2, (512, 256, 128, 64, 32, 16))

    def body(s_ref, q_ref, g0_ref, g1_ref, g2_ref, o_ref):
        o_ref[...] = ((q_ref[...].astype(F32) + g0_ref[...].astype(F32)) + g1_ref[...].astype(F32)) \
            + g2_ref[...].astype(F32)

    def got_spec(j):
        return pl.BlockSpec((None, tr, C), lambda i, s_ref: (j, i, 0))

    grid_spec = pltpu.PrefetchScalarGridSpec(
        num_scalar_prefetch=1, grid=(R2 // tr,),
        in_specs=[pl.BlockSpec((None, tr, C), lambda i, s_ref: (s_ref[0], i, 0)), got_spec(0), got_spec(1), got_spec(2)],
        out_specs=pl.BlockSpec((tr, C), lambda i, s_ref: (i, 0)))
    return pl.pallas_call(
        body, name=name, grid_spec=grid_spec, out_shape=jax.ShapeDtypeStruct((R2, C), F32),
        compiler_params=_cp("parallel"),
    )(s_idx, q, got, got, got)


def _pack(arrs, width, row_mult, lead=()):
    nl = len(lead)
    flat = [a.reshape(lead + (-1,)) for a in arrs]
    total = sum(f.shape[-1] for f in flat)
    chunk = width * row_mult
    padded = -(-total // chunk) * chunk
    if padded > total:
        flat.append(jnp.zeros(lead + (padded - total,), flat[0].dtype))
    return jnp.concatenate(flat, axis=nl).reshape(lead + (padded // width, width))


def _unpack(packed, shapes, lead=()):
    nl = len(lead)
    flat = packed.reshape(lead + (-1,))
    out, off = [], 0
    for shp in shapes:
        n = math.prod(shp)
        out.append(lax.slice_in_dim(flat, off, off + n, axis=nl).reshape(lead + tuple(shp)))
        off += n
    return out


def _pack_lanes(arrs):
    rows = []
    for a in arrs:
        f = a.reshape(-1)
        n = -(-f.shape[0] // LANES) * LANES
        rows.append(jnp.pad(f, (0, n - f.shape[0])).reshape(-1, LANES))
    out = jnp.concatenate(rows, axis=0)
    pad = -out.shape[0] % SUBLANES
    return jnp.pad(out, ((0, pad), (0, 0)))


def _unpack_lanes(packed, shapes):
    out, r = [], 0
    for shp in shapes:
        n = math.prod(shp)
        nr = -(-n // LANES)
        out.append(packed[r:r + nr].reshape(-1)[:n].reshape(shp))
        r += nr
    return out


def _shards_to_full(sh, axis):
    return jnp.concatenate([sh[i] for i in range(N_CHIPS)], axis=axis)


def _full_to_shards(full, axis):
    return jnp.stack(jnp.split(full, N_CHIPS, axis=axis), axis=0)


def _pad_cols(a, n):
    return jnp.pad(a, ((0, 0), (0, n - a.shape[1])))


def _lane_bcast(a):
    return jnp.repeat(a, HD, axis=1)


def _split_mod(mod):
    D = mod.shape[0] // 3
    return mod[None, :D], mod[None, D:2 * D], mod[None, 2 * D:]


def _fox_fwd(h, w, tag):
    T, D = h.shape
    H = D // HD
    proj = _mm(h, w['cat'], 'nn', F32, tag + '_proj')
    flog = _mm(h, w['f'], 'nn', F32, tag + '_flog')
    qn = _headnorm(proj, 0, H, w['q_norm'], 1.0 / HD, 1.0, BF16, tag + '_qnorm')
    kn = _headnorm(proj, H, H, w['k_norm'], 1.0 / HD, 1.0, BF16, tag + '_knorm')
    vb = proj[:, 2 * D:3 * D].astype(BF16)
    fcum = _cumsum(_logsig(flog, w['f_bias'], tag + '_logf'), None, False, tag + '_fcum')
    fqb = _lane_bcast(fcum[:, :H])
    fkr = fcum[:, :H].T.reshape(H, 1, T)
    ao, lse = _flash_fwd(qn, kn, vb, fqb, fkr, H, tag + '_att')
    gated = _fox_gate(ao, proj, 3 * H, H, tag + '_ogate')
    y = _mm(gated, w['o'], 'nn', F32, tag + '_out')
    return y, dict(h=h, proj=proj, flog=flog, qn=qn, kn=kn, vb=vb, fqb=fqb, fkr=fkr, ao=ao, lse=lse, gated=gated)


def _fox_bwd(dy, w, sv, tag):
    h, proj = sv['h'], sv['proj']
    T, D = h.shape
    H = D // HD
    g = {}
    g['o'] = _mm(sv['gated'], dy, 'tn', F32, tag + '_dwo')
    dgated = _mm(dy, w['o'], 'nt', F32, tag + '_dgated')
    dao, dog, delta = _fox_gate_bwd(dgated, sv['ao'], proj, 3 * H, H, tag + '_ogate_bwd')
    att = (sv['qn'], sv['kn'], sv['vb'], sv['fqb'], sv['fkr'], dao, sv['lse'], delta, H)
    dq, dfq = _flash_bwd_dq(*att, tag + '_att_dq')
    dk, dv, dfk = _flash_bwd_dkv(*att, tag + '_att_dkv')
    dfcum = _pad_cols(dfq[:, ::HD] + dfk.reshape(H, T).T, LANES)
    dlogf = _cumsum(dfcum, None, True, tag + '_fcum_bwd')
    dflog, g['f_bias'] = _logsig_bwd(dlogf, sv['flog'], w['f_bias'], tag + '_logf_bwd')
    dqr, g['q_norm'] = _headnorm_bwd([(dq, 1, 0)], proj, 0, H, w['q_norm'], 1.0 / HD, 1.0, BF16, tag + '_qnorm_bwd')
    dkr, g['k_norm'] = _headnorm_bwd([(dk, 1, 0)], proj, H, H, w['k_norm'], 1.0 / HD, 1.0, BF16, tag + '_knorm_bwd')
    dproj = jnp.concatenate([dqr, dkr, dv, dog], axis=1)
    g['cat'] = _mm(h, dproj, 'tn', F32, tag + '_dwcat')
    g['f'] = _mm(h, dflog, 'tn', F32, tag + '_dwf')
    dh = [_mm(dproj, w['cat'], 'nt', F32, tag + '_dh'), _mm(dflog, w['f'], 'nt', F32, tag + '_dh_f')]
    return dh, g


def _gdn_fwd(h, w, tag):
    T, D = h.shape
    Hk = D // HD
    Hv = V_PER_K * Hk
    proj = _mm(h, w['cat'], 'nn', F32, tag + '_proj')
    ab = _mm(h, w['ab'], 'nn', F32, tag + '_ab')
    qkvc = _dwconv(proj, 0, 4 * D, w['conv'], None, 'silu', 0, F32, tag + '_conv')
    qn = _headnorm(qkvc, 0, Hk, None, 1.0, HD ** -0.5, F32, tag + '_qnorm')
    kn = _headnorm(qkvc, Hk, Hk, None, 1.0, 1.0, F32, tag + '_knorm')
    graw, beta = _gdn_gates(ab, w['a_log'], w['dt_bias'], tag + '_gates')
    gc = _cumsum(graw, GDN_CHUNK, False, tag + '_gcum')
    gcb = _lane_bcast(gc[:, :Hv])
    betab = _lane_bcast(beta[:, Hv:2 * Hv])
    o, sprev = _gdn_chunk_fwd(qn, kn, qkvc, 2 * D, gcb, betab, Hk, tag + '_chunk')
    go = _gdn_out(o, proj, 4 * Hk, Hv, w['out_norm'], tag + '_onorm')
    y = _mm(go, w['o'], 'nn', F32, tag + '_out')
    return y, dict(h=h, proj=proj, ab=ab, qkvc=qkvc, qn=qn, kn=kn, gcb=gcb, betab=betab, o=o, sprev=sprev, go=go)


def _gdn_bwd(dy, w, sv, tag):
    h, proj, qkvc = sv['h'], sv['proj'], sv['qkvc']
    T, D = h.shape
    Hk = D // HD
    Hv = V_PER_K * Hk
    g = {}
    g['o'] = _mm(sv['go'], dy, 'tn', F32, tag + '_dwo')
    dgo = _mm(dy, w['o'], 'nt', F32, tag + '_dgo')
    do, dz, g['out_norm'] = _gdn_out_bwd(dgo, sv['o'], proj, 4 * Hk, Hv, w['out_norm'], tag + '_onorm_bwd')
    dqp, dkp, dv, dgb, dbetab = _gdn_chunk_bwd(sv['qn'], sv['kn'], qkvc, 2 * D, sv['gcb'], sv['betab'], sv['sprev'],
                                               do, Hk, tag + '_chunk_bwd')
    pairs = lambda a: [(a, V_PER_K, j) for j in range(V_PER_K)]
    dqc, _ = _headnorm_bwd(pairs(dqp), qkvc, 0, Hk, None, 1.0, HD ** -0.5, F32, tag + '_qnorm_bwd')
    dkc, _ = _headnorm_bwd(pairs(dkp), qkvc, Hk, Hk, None, 1.0, 1.0, F32, tag + '_knorm_bwd')
    zeros = jnp.zeros((T, Hv), F32)
    dg_pad = _pad_cols(dgb[:, ::HD], LANES)
    dbeta_pad = _pad_cols(jnp.concatenate([zeros, dbetab[:, ::HD]], axis=1), LANES)
    dab, g['a_log'], g['dt_bias'] = _gdn_gates_bwd(dg_pad, dbeta_pad, sv['ab'], w['a_log'], w['dt_bias'], tag + '_gates_bwd')
    dpq, dwq = _dwconv_bwd(proj, 0, D, w['conv'], 0, None, 'silu', 0, dqc, tag + '_conv_bwd_q')
    dpk, dwk = _dwconv_bwd(proj, D, D, w['conv'], D, None, 'silu', 0, dkc, tag + '_conv_bwd_k')
    dpv, dwv = _dwconv_bwd(proj, 2 * D, 2 * D, w['conv'], 2 * D, None, 'silu', 0, dv, tag + '_conv_bwd_v')
    g['conv'] = jnp.concatenate([dwq, dwk, dwv], axis=1)
    dproj = jnp.concatenate([dpq, dpk, dpv, dz], axis=1)
    g['cat'] = _mm(h, dproj, 'tn', F32, tag + '_dwcat')
    g['ab'] = _mm(h, dab, 'tn', F32, tag + '_dwab')
    dh = [_mm(dproj, w['cat'], 'nt', F32, tag + '_dh'), _mm(dab, w['ab'], 'nt', F32, tag + '_dh_ab')]
    return dh, g


def _ffn_fwd(h, w, tag):
    dff = w['down'].shape[0]
    up = _mm(h, w['up'], 'nn', F32, tag + '_up')
    act = _dwconv(up, 0, dff, w['conv'], w['conv_b'], 'glu', dff, BF16, tag + '_conv')
    y = _mm(act, w['down'], 'nn', F32, tag + '_down')
    return y, dict(h=h, up=up, act=act)


def _ffn_bwd(dy, w, sv, tag):
    h, up = sv['h'], sv['up']
    dff = w['down'].shape[0]
    g = {}
    g['down'] = _mm(sv['act'], dy, 'tn', F32, tag + '_dwdown')
    dact = _mm(dy, w['down'], 'nt', F32, tag + '_dact')
    dgate, dval, g['conv'], g['conv_b'] = _dwconv_bwd(up, 0, dff, w['conv'], 0, w['conv_b'], 'glu', dff, dact,
                                                      tag + '_conv_bwd')
    dup = jnp.concatenate([dgate, dval], axis=1)
    g['up'] = _mm(h, dup, 'tn', F32, tag + '_dwup')
    dh = [_mm(dup, w['up'], 'nt', F32, tag + '_dh')]
    return dh, g


def _local_step(x, target, mods, norm_g, wf, wg, wffn):
    mixers = [(_fox_fwd, _fox_bwd, wf, 'fox'), (_gdn_fwd, _gdn_bwd, wg, 'gdn')]
    tape = []
    for i in range(2):
        for sub in range(2):
            if sub == 0:
                fwd, bwd, w, tag = mixers[i]
            else:
                fwd, bwd, w, tag = _ffn_fwd, _ffn_bwd, wffn[i], 'ffn%d' % i
            shift, scale, gate = _split_mod(mods[i, sub])
            g_pre, g_post = norm_g[i, 2 * sub][None], norm_g[i, 2 * sub + 1][None]
            h = _pre_norm(x, g_pre, scale, shift, tag + '_prenorm')
            y, sv = fwd(h, w, tag)
            x_out = _post_res(x, y, gate, g_post, tag + '_postnorm')
            tape.append((bwd, w, tag, sv, x, y, g_pre, g_post, scale, gate))
            x = x_out
    dx, lsum = _loss_head(x, target, 'loss_head')
    loss = lsum[0, 0]
    dmods = [[None, None], [None, None]]
    dnorm = [[None] * 4, [None] * 4]
    wgrads = {}
    for idx in reversed(range(4)):
        i, sub = divmod(idx, 2)
        bwd, w, tag, sv, x_in, y, g_pre, g_post, scale, gate = tape[idx]
        dy, dgate, dgpost = _post_res_bwd(dx, y, gate, g_post, tag + '_postnorm_bwd')
        dh, wgrads[tag] = bwd(dy, w, sv, tag)
        dx, dshift, dscale, dgpre = _pre_norm_bwd(dh, x_in, g_pre, scale, dx, tag + '_prenorm_bwd')
        dmods[i][sub] = jnp.concatenate([dshift[0], dscale[0], dgate[0]])
        dnorm[i][2 * sub], dnorm[i][2 * sub + 1] = dgpre[0], dgpost[0]
    dmods = jnp.stack([jnp.stack(r) for r in dmods])
    dnorm = jnp.stack([jnp.stack(r) for r in dnorm])
    return loss, dx, dmods, dnorm, wgrads


PACK_ROW_MULT = 1024


def _unpack_lanes_dev(packed, shapes):
    n = packed.shape[0]
    out, r = [], 0
    for shp in shapes:
        k = math.prod(shp)
        nr = -(-k // LANES)
        out.append(packed[:, r:r + nr].reshape(n, -1)[:, :k].reshape((n,) + tuple(shp)))
        r += nr
    return out


def _gather_lanes(arrs, name):
    packed = _pack_lanes(arrs)
    got = _allgather8(packed, name).reshape(N_DEV, packed.shape[0], LANES)
    return _unpack_lanes_dev(got, [a.shape for a in arrs]), got


def kernel(x, c, ada_w, ada_b, norm_g, fox_w_in, fox_f_bias, fox_q_norm, fox_k_norm, fox_w_o, gdn_w_in, gdn_conv_w, gdn_a_log, gdn_dt_bias, gdn_out_norm, gdn_w_o, ffn_w_up, ffn_conv_w, ffn_conv_b, ffn_w_down, loss_target, m_ada_w, m_ada_b, m_norm_g, m_fox_w_in, m_fox_f_bias, m_fox_q_norm, m_fox_k_norm, m_fox_w_o, m_gdn_w_in, m_gdn_conv_w, m_gdn_a_log, m_gdn_dt_bias, m_gdn_out_norm, m_gdn_w_o, m_ffn_w_up, m_ffn_conv_w, m_ffn_conv_b, m_ffn_w_down, v_ada_w, v_ada_b, v_norm_g, v_fox_w_in, v_fox_f_bias, v_fox_q_norm, v_fox_k_norm, v_fox_w_o, v_gdn_w_in, v_gdn_conv_w, v_gdn_a_log, v_gdn_dt_bias, v_gdn_out_norm, v_gdn_w_o, v_ffn_w_up, v_ffn_conv_w, v_ffn_conv_b, v_ffn_w_down):
    args = locals()
    w = {n: args[n] for n in WEIGHTS}
    mom = {n: args['m_' + n] for n in WEIGHTS}
    var = {n: args['v_' + n] for n in WEIGHTS}
    _, T, D = x.shape
    H = D // HD
    Hv = V_PER_K * H
    xi, yi, ci = _place()
    s_idx = 2 * xi + yi
    b_idx = 4 * xi + 2 * yi + ci
    s_arr = jnp.reshape(s_idx, (1,)).astype(jnp.int32)
    c_arr = jnp.reshape(ci, (1,)).astype(jnp.int32)

    (c_all, ab_all, ng_all, gcw_all, fcw_all), _ = _gather_lanes(
        [jnp.tile(c, (SUBLANES, 1)), ada_b, norm_g, gdn_conv_w, ffn_conv_w], 'gather_small')
    c_all = c_all[:, 0, :]
    chips = lambda a: jnp.concatenate([a[2 * s] for s in range(N_CHIPS)], axis=-1)
    norm_g_full, gdn_conv_full, ffn_conv_full = chips(ng_all), chips(gcw_all), chips(fcw_all)

    Ns = ada_w.shape[-1]
    ada_w4 = ada_w.reshape(4, D, Ns)
    part = jnp.stack([_mm(c_all, ada_w4[i], 'nn', F32, 'ada_proj%d' % i, a_act='silu') for i in range(4)])
    part = part + ada_b.reshape(4, 1, Ns)
    (part_all,), _ = _gather_lanes([part], 'gather_mods')
    mine = lax.dynamic_index_in_dim(part_all[0::2], b_idx, axis=2, keepdims=False)
    mods = mine.transpose(1, 0, 2).reshape(2, 2, N_CHIPS * Ns)

    big_shapes = [w[n].shape for n in BIG]
    w_flat = _pack([w[n].astype(BF16) for n in BIG], PACK_COLS, PACK_ROW_MULT)
    w_all = _gather_weights(w_flat, 'gather_weights')
    full = {n: _shards_to_full(a, BIG_SHARD_AXIS[n])
            for n, a in zip(BIG, _unpack(w_all, big_shapes, lead=(N_CHIPS,)))}
    fw, gw = full['fox_w_in'][0], full['gdn_w_in'][0]
    wf = dict(cat=jnp.concatenate([fw[:, :3 * D], fw[:, 3 * D + H:]], axis=1), f=_pad_cols(fw[:, 3 * D:3 * D + H], LANES),
              f_bias=_pad_cols(fox_f_bias, LANES), q_norm=fox_q_norm, k_norm=fox_k_norm, o=full['fox_w_o'][0])
    wg = dict(cat=gw[:, :6 * D], ab=_pad_cols(gw[:, 6 * D:], LANES), conv=gdn_conv_full[0],
              a_log=_pad_cols(gdn_a_log, LANES), dt_bias=_pad_cols(gdn_dt_bias, LANES), out_norm=gdn_out_norm,
              o=full['gdn_w_o'][0])
    wffn = [dict(up=full['ffn_w_up'][i], conv=ffn_conv_full[i], conv_b=ffn_conv_b[i][None], down=full['ffn_w_down'][i])
            for i in range(2)]

    loss, dx, dmods, dnorm, g = _local_step(x[0], loss_target[0], mods, norm_g_full, wf, wg, wffn)
    loss = lax.psum(loss, ('x', 'y', 'c'))

    gf, gg = g['fox'], g['gdn']
    big_grads = {
        'fox_w_in': jnp.concatenate([gf['cat'][:, :3 * D], gf['f'][:, :H], gf['cat'][:, 3 * D:]], axis=1)[None],
        'fox_w_o': gf['o'][None],
        'gdn_w_in': jnp.concatenate([gg['cat'], gg['ab'][:, :2 * Hv]], axis=1)[None],
        'gdn_w_o': gg['o'][None],
        'ffn_w_up': jnp.stack([g['ffn0']['up'], g['ffn1']['up']]),
        'ffn_w_down': jnp.stack([g['ffn0']['down'], g['ffn1']['down']]),
    }
    g_pack = _pack([_full_to_shards(big_grads[n], BIG_SHARD_AXIS[n]) for n in BIG], PACK_COLS, PACK_ROW_MULT,
                   lead=(N_CHIPS,))
    pair_sum = _add_halves(g_pack, _swap_halves(g_pack, 'grads_to_sibling'), c_arr, 'grads_add_sibling')
    half_sum = _add_chips(pair_sum, _scatter_chips(pair_sum, 'grads_to_chips'), s_arr, 'grads_add_chips')
    g_big = _join_halves(half_sum, 'grads_join')
    packf = lambda d: _pack([d[n] for n in BIG], PACK_COLS, PACK_ROW_MULT)
    big_out = [_unpack(o, big_shapes) for o in _adamw(packf(w), packf(mom), packf(var), [g_big], 'adamw_big')]

    small_part = [dmods, dnorm, gf['f_bias'][:, :H], gf['q_norm'], gf['k_norm'], gg['conv'][None],
                  gg['a_log'][:, :Hv], gg['dt_bias'][:, :Hv], gg['out_norm'],
                  jnp.stack([g['ffn0']['conv'], g['ffn1']['conv']]),
                  jnp.concatenate([g['ffn0']['conv_b'], g['ffn1']['conv_b']], axis=0)]
    (dmods_all, *_), got = _gather_lanes(small_part, 'gather_small_grads')
    tot = _unpack_lanes(_sum_devices(got, 'sum_small_grads'), [a.shape for a in small_part])
    small_full = dict(zip(SMALL, tot))
    small_g = {n: (lax.dynamic_slice_in_dim(small_full[n], s_idx * w[n].shape[-1], w[n].shape[-1], axis=-1)
                   if n in SMALL_SHARDED else small_full[n]) for n in SMALL}
    packs = lambda d: _pack_lanes([d[n] for n in SMALL])
    small_shapes = [w[n].shape for n in SMALL]
    small_out = [_unpack_lanes(o, small_shapes)
                 for o in _adamw(packs(w), packs(mom), packs(var), [packs(small_g)], 'adamw_small')]

    dm = lax.dynamic_slice_in_dim(dmods_all.reshape(N_DEV, 4, N_CHIPS * Ns), s_idx * Ns, Ns, axis=-1).transpose(1, 0, 2)
    ada_out = [o.reshape(ada_w.shape) for o in
               _ada_w_update(c_all, dm, ada_w4, m_ada_w.reshape(4, D, Ns), v_ada_w.reshape(4, D, Ns), 'adamw_ada_w')]

    outs = []
    for k in range(4):
        by_name = {'ada_w': ada_out[k]}
        by_name.update(zip(BIG, big_out[k]))
        by_name.update(zip(SMALL, small_out[k]))
        outs += [by_name[n] for n in WEIGHTS]
    return (loss, dx[None], *outs)
```

```python
import functools
import math

import jax
import jax.numpy as jnp
from jax import lax
from jax.experimental import pallas as pl
from jax.experimental.pallas import tpu as pltpu

F32 = jnp.float32
BF16 = jnp.bfloat16
EPS = 1e-6
HD = 128
GDN_CHUNK = 64
GDN_CONV = 4
FFN_CONV = 3
LANES = 128
SUBLANES = 8
VMEM_LIMIT = 56 * 1024 * 1024
HIGHEST = lax.Precision.HIGHEST
NEG = -1e30

ADAM_LR = 0.001
ADAM_B1 = 0.9
ADAM_B2 = 0.999
ADAM_EPS = 1e-08
ADAM_WD = 0.01
ADAM_STEP = 10

WEIGHTS = ['ada_w', 'ada_b', 'norm_g', 'fox_w_in', 'fox_f_bias', 'fox_q_norm', 'fox_k_norm', 'fox_w_o',
           'gdn_w_in', 'gdn_conv_w', 'gdn_a_log', 'gdn_dt_bias', 'gdn_out_norm', 'gdn_w_o',
           'ffn_w_up', 'ffn_conv_w', 'ffn_conv_b', 'ffn_w_down']
BIG = ['fox_w_in', 'fox_w_o', 'gdn_w_in', 'gdn_w_o', 'ffn_w_up', 'ffn_w_down']
BIG_SHARD_AXIS = {'fox_w_in': 2, 'fox_w_o': 1, 'gdn_w_in': 2, 'gdn_w_o': 1, 'ffn_w_up': 2, 'ffn_w_down': 1}
SMALL = ['ada_b', 'norm_g', 'fox_f_bias', 'fox_q_norm', 'fox_k_norm', 'gdn_conv_w', 'gdn_a_log',
         'gdn_dt_bias', 'gdn_out_norm', 'ffn_conv_w', 'ffn_conv_b']
SMALL_SHARDED = ['ada_b', 'norm_g', 'gdn_conv_w', 'ffn_conv_w']
N_CHIPS = 4
N_DEV = 8
MESH = pl.DeviceIdType.MESH


def _tile(n, cands):
    for c in cands:
        if n % c == 0:
            return c
    return n


def _cp(*sem):
    return pltpu.CompilerParams(dimension_semantics=sem, vmem_limit_bytes=VMEM_LIMIT)


def _dot(a, b, mode='nn', precision=None):
    dims = {'nn': (((1,), (0,)), ((), ())), 'nt': (((1,), (1,)), ((), ())), 'tn': (((0,), (0,)), ((), ()))}[mode]
    return lax.dot_general(a, b, dims, precision=precision, preferred_element_type=F32)


def _bdot(a, b, mode='nn'):
    return _dot(a.astype(BF16), b.astype(BF16), mode)


def _hdot(a, b, mode='nn'):
    return _dot(a, b, mode, precision=HIGHEST)


def _sigmoid(x):
    return 1.0 / (1.0 + jnp.exp(-x))


def _silu(x):
    return x * _sigmoid(x)


def _softplus(x):
    return jnp.maximum(x, 0.0) + jnp.log(1.0 + jnp.exp(-jnp.abs(x)))


def _erf(x):
    return lax.erf(x)


def _gelu(x):
    return 0.5 * x * (1.0 + _erf(x * (2.0 ** -0.5)))


def _gelu_grad(x):
    cdf = 0.5 * (1.0 + _erf(x * (2.0 ** -0.5)))
    pdf = jnp.exp(-0.5 * x * x) * (1.0 / math.sqrt(2.0 * math.pi))
    return cdf + x * pdf


def _mm(a, b, mode, out_dtype, name, a_act=None):
    if mode == 'nn':
        (M, K), (_, N) = a.shape, b.shape
    elif mode == 'nt':
        (M, K), (N, _) = a.shape, b.shape
    else:
        (K, M), (_, N) = a.shape, b.shape
    big = (1024, 512, 256, 128)
    tm, tn, tk = _tile(M, big), _tile(N, big), _tile(K, (512, 256, 128))
    nk = K // tk

    def body(a_ref, b_ref, o_ref, acc_ref):
        k = pl.program_id(2)

        @pl.when(k == 0)
        def _():
            acc_ref[...] = jnp.zeros_like(acc_ref)

        av = a_ref[...]
        if a_act == 'silu':
            av = _silu(av.astype(F32))
        acc_ref[...] += _bdot(av, b_ref[...], mode)

        @pl.when(k == nk - 1)
        def _():
            o_ref[...] = acc_ref[...].astype(o_ref.dtype)

    if mode == 'nn':
        a_spec = pl.BlockSpec((tm, tk), lambda i, j, k: (i, k))
        b_spec = pl.BlockSpec((tk, tn), lambda i, j, k: (k, j))
    elif mode == 'nt':
        a_spec = pl.BlockSpec((tm, tk), lambda i, j, k: (i, k))
        b_spec = pl.BlockSpec((tn, tk), lambda i, j, k: (j, k))
    else:
        a_spec = pl.BlockSpec((tk, tm), lambda i, j, k: (k, i))
        b_spec = pl.BlockSpec((tk, tn), lambda i, j, k: (k, j))
    return pl.pallas_call(
        body, name=name, grid=(M // tm, N // tn, nk),
        in_specs=[a_spec, b_spec],
        out_specs=pl.BlockSpec((tm, tn), lambda i, j, k: (i, j)),
        out_shape=jax.ShapeDtypeStruct((M, N), out_dtype),
        scratch_shapes=[pltpu.VMEM((tm, tn), F32)],
        compiler_params=_cp("parallel", "parallel", "arbitrary"),
    )(a, b)


ROW_TILES = (256, 128, 64, 32, 16, 8)


def _row_spec(tT, D):
    return pl.BlockSpec((tT, D), lambda i: (i, 0))


def _vec_spec(D):
    return pl.BlockSpec((1, D), lambda i: (0, 0))


def _pre_norm(x, g, scale, shift, name):
    T, D = x.shape
    tT = _tile(T, ROW_TILES)

    def body(x_ref, g_ref, sc_ref, sh_ref, h_ref):
        xv = x_ref[...]
        r = lax.rsqrt(jnp.mean(xv * xv, axis=-1, keepdims=True) + EPS)
        h_ref[...] = ((xv * r) * g_ref[...] * (1.0 + sc_ref[...]) + sh_ref[...]).astype(h_ref.dtype)

    return pl.pallas_call(
        body, name=name, grid=(T // tT,),
        in_specs=[_row_spec(tT, D), _vec_spec(D), _vec_spec(D), _vec_spec(D)],
        out_specs=_row_spec(tT, D), out_shape=jax.ShapeDtypeStruct((T, D), BF16),
        compiler_params=_cp("parallel"),
    )(x, g, scale, shift)


def _post_res(x, y, gate, g, name):
    T, D = x.shape
    tT = _tile(T, ROW_TILES)

    def body(x_ref, y_ref, gate_ref, g_ref, o_ref):
        yv = y_ref[...]
        r = lax.rsqrt(jnp.mean(yv * yv, axis=-1, keepdims=True) + EPS)
        o_ref[...] = x_ref[...] + gate_ref[...] * ((yv * r) * g_ref[...])

    return pl.pallas_call(
        body, name=name, grid=(T // tT,),
        in_specs=[_row_spec(tT, D), _row_spec(tT, D), _vec_spec(D), _vec_spec(D)],
        out_specs=_row_spec(tT, D), out_shape=jax.ShapeDtypeStruct((T, D), F32),
        compiler_params=_cp("parallel"),
    )(x, y, gate, g)


def _post_res_bwd(dout, y, gate, g, name):
    T, D = y.shape
    tT = _tile(T, ROW_TILES)

    def body(do_ref, y_ref, gate_ref, g_ref, dy_ref, dgate_ref, dg_ref):
        @pl.when(pl.program_id(0) == 0)
        def _():
            dgate_ref[...] = jnp.zeros_like(dgate_ref)
            dg_ref[...] = jnp.zeros_like(dg_ref)

        yv, dov, gatev, gv = y_ref[...], do_ref[...], gate_ref[...], g_ref[...]
        r = lax.rsqrt(jnp.mean(yv * yv, axis=-1, keepdims=True) + EPS)
        yn = yv * r
        t = dov * yn
        dgate_ref[...] += jnp.sum(t * gv, axis=0, keepdims=True)
        dg_ref[...] += jnp.sum(t * gatev, axis=0, keepdims=True)
        dyn = dov * (gatev * gv)
        dy_ref[...] = (r * (dyn - yn * jnp.mean(dyn * yn, axis=-1, keepdims=True))).astype(dy_ref.dtype)

    return pl.pallas_call(
        body, name=name, grid=(T // tT,),
        in_specs=[_row_spec(tT, D), _row_spec(tT, D), _vec_spec(D), _vec_spec(D)],
        out_specs=[_row_spec(tT, D), _vec_spec(D), _vec_spec(D)],
        out_shape=[jax.ShapeDtypeStruct((T, D), BF16), jax.ShapeDtypeStruct((1, D), F32),
                   jax.ShapeDtypeStruct((1, D), F32)],
        compiler_params=_cp("arbitrary"),
    )(dout, y, gate, g)


def _pre_norm_bwd(dhs, x, g, scale, dres, name):
    T, D = x.shape
    tT = _tile(T, ROW_TILES)
    n = len(dhs)

    def body(*refs):
        dh_refs = refs[:n]
        x_ref, g_ref, sc_ref, dres_ref, dx_ref, dsh_ref, dsc_ref, dg_ref = refs[n:]

        @pl.when(pl.program_id(0) == 0)
        def _():
            dsh_ref[...] = jnp.zeros_like(dsh_ref)
            dsc_ref[...] = jnp.zeros_like(dsc_ref)
            dg_ref[...] = jnp.zeros_like(dg_ref)

        dh = dh_refs[0][...]
        for r_ in dh_refs[1:]:
            dh = dh + r_[...]
        xv, gv, scv = x_ref[...], g_ref[...], sc_ref[...]
        r = lax.rsqrt(jnp.mean(xv * xv, axis=-1, keepdims=True) + EPS)
        xn = xv * r
        t = dh * xn
        dsh_ref[...] += jnp.sum(dh, axis=0, keepdims=True)
        dsc_ref[...] += jnp.sum(t * gv, axis=0, keepdims=True)
        dg_ref[...] += jnp.sum(t * (1.0 + scv), axis=0, keepdims=True)
        dxn = dh * (gv * (1.0 + scv))
        dx_ref[...] = dres_ref[...] + r * (dxn - xn * jnp.mean(dxn * xn, axis=-1, keepdims=True))

    return pl.pallas_call(
        body, name=name, grid=(T // tT,),
        in_specs=[_row_spec(tT, D)] * n + [_row_spec(tT, D), _vec_spec(D), _vec_spec(D), _row_spec(tT, D)],
        out_specs=[_row_spec(tT, D), _vec_spec(D), _vec_spec(D), _vec_spec(D)],
        out_shape=[jax.ShapeDtypeStruct((T, D), F32)] + [jax.ShapeDtypeStruct((1, D), F32)] * 3,
        compiler_params=_cp("arbitrary"),
    )(*dhs, x, g, scale, dres)


def _loss_head(y, target, name):
    T, D = y.shape
    tT = _tile(T, ROW_TILES)

    def body(y_ref, t_ref, dy_ref, l_ref):
        @pl.when(pl.program_id(0) == 0)
        def _():
            l_ref[...] = jnp.zeros_like(l_ref)

        e = y_ref[...] - t_ref[...]
        dy_ref[...] = e * (1.0 / D)
        s = jnp.sum(jnp.mean(e * e, axis=-1, keepdims=True), axis=0, keepdims=True)
        l_ref[...] += 0.5 * s

    return pl.pallas_call(
        body, name=name, grid=(T // tT,),
        in_specs=[_row_spec(tT, D), _row_spec(tT, D)],
        out_specs=[_row_spec(tT, D), pl.BlockSpec((SUBLANES, LANES), lambda i: (0, 0))],
        out_shape=[jax.ShapeDtypeStruct((T, D), F32), jax.ShapeDtypeStruct((SUBLANES, LANES), F32)],
        compiler_params=_cp("arbitrary"),
    )(y, target)


HEAD_ROW_TILES = (1024, 512, 256, 128, 64)


def _hb(tT, off=0):
    return pl.BlockSpec((tT, HD), lambda i, h: (i, off + h))


def _hvec():
    return pl.BlockSpec((1, HD), lambda i, h: (0, 0))


def _headnorm(x, off, H, g, c1, post, out_dtype, name):
    T = x.shape[0]
    tT = _tile(T, HEAD_ROW_TILES)
    has_g = g is not None

    def body(*refs):
        x_ref = refs[0]
        o_ref = refs[-1]
        xv = x_ref[...]
        yv = xv * lax.rsqrt(c1 * jnp.sum(xv * xv, axis=-1, keepdims=True) + EPS)
        if has_g:
            yv = yv * refs[1][...]
        if post != 1.0:
            yv = yv * post
        o_ref[...] = yv.astype(o_ref.dtype)

    return pl.pallas_call(
        body, name=name, grid=(T // tT, H),
        in_specs=[_hb(tT, off)] + ([_hvec()] if has_g else []),
        out_specs=_hb(tT), out_shape=jax.ShapeDtypeStruct((T, H * HD), out_dtype),
        compiler_params=_cp("parallel", "parallel"),
    )(*([x, g] if has_g else [x]))


def _headnorm_bwd(dys, x, off, H, g, c1, post, out_dtype, name):
    T = x.shape[0]
    tT = _tile(T, HEAD_ROW_TILES)
    n = len(dys)
    has_g = g is not None

    def body(*refs):
        dy_refs = refs[:n]
        x_ref = refs[n]
        g_ref = refs[n + 1] if has_g else None
        dx_ref, dg_ref = refs[-2], refs[-1]

        @pl.when((pl.program_id(0) == 0) & (pl.program_id(1) == 0))
        def _():
            dg_ref[...] = jnp.zeros_like(dg_ref)

        dy = dy_refs[0][...].astype(F32)
        for r_ in dy_refs[1:]:
            dy = dy + r_[...].astype(F32)
        if post != 1.0:
            dy = dy * post
        xv = x_ref[...]
        r = lax.rsqrt(c1 * jnp.sum(xv * xv, axis=-1, keepdims=True) + EPS)
        xn = xv * r
        if has_g:
            dg_ref[...] += jnp.sum(dy * xn, axis=0, keepdims=True)
            dy = dy * g_ref[...]
        dx_ref[...] = (r * (dy - xn * (c1 * jnp.sum(dy * xn, axis=-1, keepdims=True)))).astype(dx_ref.dtype)

    dy_specs = [pl.BlockSpec((tT, HD), lambda i, h, st=st, of=of: (i, st * h + of)) for (_, st, of) in dys]
    return pl.pallas_call(
        body, name=name, grid=(T // tT, H),
        in_specs=dy_specs + [_hb(tT, off)] + ([_hvec()] if has_g else []),
        out_specs=[_hb(tT), _hvec()],
        out_shape=[jax.ShapeDtypeStruct((T, H * HD), out_dtype), jax.ShapeDtypeStruct((1, HD), F32)],
        compiler_params=_cp("arbitrary", "arbitrary"),
    )(*[d[0] for d in dys], x, *([g] if has_g else []))


def _fox_gate(ao, proj, og_off, H, name):
    T = ao.shape[0]
    tT = _tile(T, HEAD_ROW_TILES)

    def body(ao_ref, og_ref, o_ref):
        o_ref[...] = (ao_ref[...] * _sigmoid(og_ref[...])).astype(o_ref.dtype)

    return pl.pallas_call(
        body, name=name, grid=(T // tT, H),
        in_specs=[_hb(tT), _hb(tT, og_off)], out_specs=_hb(tT),
        out_shape=jax.ShapeDtypeStruct((T, H * HD), BF16), compiler_params=_cp("parallel", "parallel"),
    )(ao, proj)


def _fox_gate_bwd(dgated, ao, proj, og_off, H, name):
    T = ao.shape[0]
    tT = _tile(T, HEAD_ROW_TILES)

    def body(dg_ref, ao_ref, og_ref, dao_ref, dog_ref, delta_ref):
        dg, aov = dg_ref[...], ao_ref[...]
        sg = _sigmoid(og_ref[...])
        dao = dg * sg
        dao_ref[...] = dao.astype(dao_ref.dtype)
        dog_ref[...] = (dg * aov * sg * (1.0 - sg)).astype(dog_ref.dtype)
        delta_ref[...] = jnp.broadcast_to(jnp.sum(dao * aov, axis=-1, keepdims=True), delta_ref.shape)

    return pl.pallas_call(
        body, name=name, grid=(T // tT, H),
        in_specs=[_hb(tT), _hb(tT), _hb(tT, og_off)], out_specs=[_hb(tT)] * 3,
        out_shape=[jax.ShapeDtypeStruct((T, H * HD), BF16), jax.ShapeDtypeStruct((T, H * HD), BF16),
                   jax.ShapeDtypeStruct((T, H * HD), F32)],
        compiler_params=_cp("parallel", "parallel"),
    )(dgated, ao, proj)


def _gdn_out(o, proj, z_off, Hv, g, name):
    T = o.shape[0]
    tT = _tile(T, HEAD_ROW_TILES)

    def body(o_ref, z_ref, g_ref, y_ref):
        ov, zv = o_ref[...], z_ref[...]
        r = lax.rsqrt(jnp.mean(ov * ov, axis=-1, keepdims=True) + EPS)
        y_ref[...] = (((ov * r) * g_ref[...]) * _silu(zv)).astype(y_ref.dtype)

    return pl.pallas_call(
        body, name=name, grid=(T // tT, Hv),
        in_specs=[_hb(tT), _hb(tT, z_off), _hvec()], out_specs=_hb(tT),
        out_shape=jax.ShapeDtypeStruct((T, Hv * HD), BF16), compiler_params=_cp("parallel", "parallel"),
    )(o, proj, g)


def _gdn_out_bwd(dy, o, proj, z_off, Hv, g, name):
    T = o.shape[0]
    tT = _tile(T, HEAD_ROW_TILES)

    def body(dy_ref, o_ref, z_ref, g_ref, do_ref, dz_ref, dg_ref):
        @pl.when((pl.program_id(0) == 0) & (pl.program_id(1) == 0))
        def _():
            dg_ref[...] = jnp.zeros_like(dg_ref)

        dyv, ov, zv, gv = dy_ref[...], o_ref[...], z_ref[...], g_ref[...]
        r = lax.rsqrt(jnp.mean(ov * ov, axis=-1, keepdims=True) + EPS)
        on = ov * r
        sg = _sigmoid(zv)
        sz = zv * sg
        dz_ref[...] = (dyv * (on * gv) * (sg * (1.0 + zv * (1.0 - sg)))).astype(dz_ref.dtype)
        t = dyv * sz
        dg_ref[...] += jnp.sum(t * on, axis=0, keepdims=True)
        don = t * gv
        do_ref[...] = r * (don - on * jnp.mean(don * on, axis=-1, keepdims=True))

    return pl.pallas_call(
        body, name=name, grid=(T // tT, Hv),
        in_specs=[_hb(tT), _hb(tT), _hb(tT, z_off), _hvec()], out_specs=[_hb(tT), _hb(tT), _hvec()],
        out_shape=[jax.ShapeDtypeStruct((T, Hv * HD), F32), jax.ShapeDtypeStruct((T, Hv * HD), BF16),
                   jax.ShapeDtypeStruct((1, HD), F32)],
        compiler_params=_cp("arbitrary", "arbitrary"),
    )(dy, o, proj, g)


def _lrow(tT):
    return pl.BlockSpec((tT, LANES), lambda i: (i, 0))


def _lvec():
    return pl.BlockSpec((1, LANES), lambda i: (0, 0))


def _logsig(x, b, name):
    T = x.shape[0]
    tT = _tile(T, HEAD_ROW_TILES)

    def body(x_ref, b_ref, o_ref):
        o_ref[...] = -_softplus(-(x_ref[...] + b_ref[...]))

    return pl.pallas_call(body, name=name, grid=(T // tT,), in_specs=[_lrow(tT), _lvec()], out_specs=_lrow(tT),
                          out_shape=jax.ShapeDtypeStruct((T, LANES), F32), compiler_params=_cp("parallel"))(x, b)


def _logsig_bwd(dy, x, b, name):
    T = x.shape[0]
    tT = _tile(T, HEAD_ROW_TILES)

    def body(dy_ref, x_ref, b_ref, dx_ref, db_ref):
        @pl.when(pl.program_id(0) == 0)
        def _():
            db_ref[...] = jnp.zeros_like(db_ref)

        dx = dy_ref[...] * _sigmoid(-(x_ref[...] + b_ref[...]))
        dx_ref[...] = dx.astype(dx_ref.dtype)
        db_ref[...] += jnp.sum(dx, axis=0, keepdims=True)

    return pl.pallas_call(
        body, name=name, grid=(T // tT,), in_specs=[_lrow(tT), _lrow(tT), _lvec()], out_specs=[_lrow(tT), _lvec()],
        out_shape=[jax.ShapeDtypeStruct((T, LANES), BF16), jax.ShapeDtypeStruct((1, LANES), F32)],
        compiler_params=_cp("arbitrary"))(dy, x, b)


def _gdn_gates(ab, alog, dtb, name):
    T = ab.shape[0]
    tT = _tile(T, HEAD_ROW_TILES)

    def body(ab_ref, al_ref, dt_ref, g_ref, be_ref):
        v = ab_ref[...]
        g_ref[...] = -jnp.exp(al_ref[...]) * _softplus(v + dt_ref[...])
        be_ref[...] = _sigmoid(v)

    return pl.pallas_call(
        body, name=name, grid=(T // tT,), in_specs=[_lrow(tT), _lvec(), _lvec()], out_specs=[_lrow(tT)] * 2,
        out_shape=[jax.ShapeDtypeStruct((T, LANES), F32)] * 2, compiler_params=_cp("parallel"))(ab, alog, dtb)


def _gdn_gates_bwd(dg, dbeta, ab, alog, dtb, name):
    T = ab.shape[0]
    tT = _tile(T, HEAD_ROW_TILES)

    def body(dg_ref, dbe_ref, ab_ref, al_ref, dt_ref, dab_ref, dal_ref, ddt_ref):
        @pl.when(pl.program_id(0) == 0)
        def _():
            dal_ref[...] = jnp.zeros_like(dal_ref)
            ddt_ref[...] = jnp.zeros_like(ddt_ref)

        v, dgv = ab_ref[...], dg_ref[...]
        ea = jnp.exp(al_ref[...])
        z = v + dt_ref[...]
        da = dgv * (-ea * _sigmoid(z))
        sb = _sigmoid(v)
        dab_ref[...] = (da + dbe_ref[...] * sb * (1.0 - sb)).astype(dab_ref.dtype)
        dal_ref[...] += jnp.sum(dgv * (-ea * _softplus(z)), axis=0, keepdims=True)
        ddt_ref[...] += jnp.sum(da, axis=0, keepdims=True)

    return pl.pallas_call(
        body, name=name, grid=(T // tT,), in_specs=[_lrow(tT), _lrow(tT), _lrow(tT), _lvec(), _lvec()],
        out_specs=[_lrow(tT), _lvec(), _lvec()],
        out_shape=[jax.ShapeDtypeStruct((T, LANES), BF16), jax.ShapeDtypeStruct((1, LANES), F32),
                   jax.ShapeDtypeStruct((1, LANES), F32)],
        compiler_params=_cp("arbitrary"))(dg, dbeta, ab, alog, dtb)


def _cumsum(x, seg, reverse, name):
    T = x.shape[0]
    tb = _tile(T, (256, 128, 64))
    nb = T // tb
    carry = seg is None

    def body(x_ref, o_ref, c_ref):
        @pl.when(pl.program_id(0) == 0)
        def _():
            c_ref[...] = jnp.zeros_like(c_ref)

        ri = lax.broadcasted_iota(jnp.int32, (tb, tb), 0)
        ci = lax.broadcasted_iota(jnp.int32, (tb, tb), 1)
        keep = (ci >= ri) if reverse else (ci <= ri)
        if seg is not None:
            keep = keep & ((ri // seg) == (ci // seg))
        y = _hdot(keep.astype(F32), x_ref[...])
        if carry:
            y = y + c_ref[...]
            c_ref[...] = y[0:1, :] if reverse else y[tb - 1:tb, :]
        o_ref[...] = y

    imap = (lambda i: (nb - 1 - i, 0)) if reverse else (lambda i: (i, 0))
    return pl.pallas_call(
        body, name=name, grid=(nb,), in_specs=[pl.BlockSpec((tb, LANES), imap)],
        out_specs=pl.BlockSpec((tb, LANES), imap), out_shape=jax.ShapeDtypeStruct((T, LANES), F32),
        scratch_shapes=[pltpu.VMEM((1, LANES), F32)], compiler_params=_cp("arbitrary"))(x)


ATT_TILES = (512, 256, 128)


def _att_scores(q, k, fq, fk):
    return _dot(q, k, 'nt') * (HD ** -0.5) + fq[:, :1] - fk


def _diag_keep(tq):
    return lax.broadcasted_iota(jnp.int32, (tq, tq), 1) <= lax.broadcasted_iota(jnp.int32, (tq, tq), 0)


def _flash_fwd(qn, kn, vb, fqb, fkr, H, name):
    T = qn.shape[0]
    tq = _tile(T, ATT_TILES)
    nq = T // tq

    def body(q_ref, k_ref, v_ref, fq_ref, fk_ref, o_ref, lse_ref, m_s, l_s, acc_s):
        qi, ki = pl.program_id(1), pl.program_id(2)

        @pl.when(ki == 0)
        def _():
            m_s[...] = jnp.full_like(m_s, NEG)
            l_s[...] = jnp.zeros_like(l_s)
            acc_s[...] = jnp.zeros_like(acc_s)

        def step(diagonal):
            s = _att_scores(q_ref[...], k_ref[...], fq_ref[...], fk_ref[...])
            if diagonal:
                s = jnp.where(_diag_keep(tq), s, NEG)
            m_prev = m_s[...]
            m_new = jnp.maximum(m_prev, jnp.max(s, axis=1, keepdims=True))
            alpha = jnp.exp(m_prev - m_new)
            p = jnp.exp(s - m_new[:, :1])
            l_s[...] = alpha * l_s[...] + jnp.sum(p, axis=1, keepdims=True)
            acc_s[...] = acc_s[...] * alpha + _bdot(p, v_ref[...])
            m_s[...] = m_new

        @pl.when(ki < qi)
        def _():
            step(False)

        @pl.when(ki == qi)
        def _():
            step(True)
            o_ref[...] = acc_s[...] / l_s[...]
            lse_ref[...] = m_s[...] + jnp.log(l_s[...])

    qspec = pl.BlockSpec((tq, HD), lambda h, qi, ki: (qi, h))
    kspec = pl.BlockSpec((tq, HD), lambda h, qi, ki: (jnp.minimum(ki, qi), h))
    fkspec = pl.BlockSpec((None, 1, tq), lambda h, qi, ki: (h, 0, jnp.minimum(ki, qi)))
    return pl.pallas_call(
        body, name=name, grid=(H, nq, nq),
        in_specs=[qspec, kspec, kspec, qspec, fkspec], out_specs=[qspec, qspec],
        out_shape=[jax.ShapeDtypeStruct((T, H * HD), F32)] * 2,
        scratch_shapes=[pltpu.VMEM((tq, HD), F32)] * 3,
        compiler_params=_cp("parallel", "parallel", "arbitrary"),
    )(qn, kn, vb, fqb, fkr)


def _flash_bwd(qn, kn, vb, fqb, fkr, dao, lse, delta, H, name):
    T = qn.shape[0]
    tq = _tile(T, ATT_TILES)
    nq = T // tq

    def body(q_ref, k_ref, v_ref, fq_ref, fk_ref, do_ref, lse_ref, dl_ref, dq_ref, dfq_ref, dk_ref, dv_ref, dfk_ref,
             dk_s, dv_s, dfk_s):
        ki, qi = pl.program_id(1), pl.program_id(2)

        @pl.when((ki == 0) & (qi == 0))
        def _():
            dq_ref[...] = jnp.zeros_like(dq_ref)
            dfq_ref[...] = jnp.zeros_like(dfq_ref)

        @pl.when(qi == 0)
        def _():
            dk_s[...] = jnp.zeros_like(dk_s)
            dv_s[...] = jnp.zeros_like(dv_s)
            dfk_s[...] = jnp.zeros_like(dfk_s)

        def step(diagonal):
            s = _att_scores(q_ref[...], k_ref[...], fq_ref[...], fk_ref[...])
            p = jnp.exp(s - lse_ref[...][:, :1])
            if diagonal:
                p = jnp.where(_diag_keep(tq), p, 0.0)
            dp = _dot(do_ref[...], v_ref[...], 'nt')
            ds = p * (dp - dl_ref[...][:, :1])
            dv_s[...] += _bdot(p, do_ref[...], 'tn')
            dk_s[...] += _bdot(ds, q_ref[...], 'tn') * (HD ** -0.5)
            dfk_s[...] -= jnp.sum(ds, axis=0, keepdims=True)
            rows = pl.ds(pl.multiple_of(qi * tq, tq), tq)
            dq_ref[rows, :] += _bdot(ds, k_ref[...]) * (HD ** -0.5)
            dfq_ref[rows, :] += jnp.broadcast_to(jnp.sum(ds, axis=1, keepdims=True), (tq, HD))

        @pl.when(qi > ki)
        def _():
            step(False)

        @pl.when(qi == ki)
        def _():
            step(True)

        @pl.when(qi == nq - 1)
        def _():
            dk_ref[...] = dk_s[...]
            dv_ref[...] = dv_s[...].astype(dv_ref.dtype)
            dfk_ref[...] = dfk_s[...]

    qspec = pl.BlockSpec((tq, HD), lambda h, ki, qi: (jnp.maximum(qi, ki), h))
    kspec = pl.BlockSpec((tq, HD), lambda h, ki, qi: (ki, h))
    fkspec = pl.BlockSpec((None, 1, tq), lambda h, ki, qi: (h, 0, ki))
    head = pl.BlockSpec((T, HD), lambda h, ki, qi: (0, h))
    return pl.pallas_call(
        body, name=name, grid=(H, nq, nq),
        in_specs=[qspec, kspec, kspec, qspec, fkspec, qspec, qspec, qspec],
        out_specs=[head, head, kspec, kspec, fkspec],
        out_shape=[jax.ShapeDtypeStruct((T, H * HD), F32)] * 3
        + [jax.ShapeDtypeStruct((T, H * HD), BF16), jax.ShapeDtypeStruct((H, 1, T), F32)],
        scratch_shapes=[pltpu.VMEM((tq, HD), F32), pltpu.VMEM((tq, HD), F32), pltpu.VMEM((1, tq), F32)],
        compiler_params=_cp("parallel", "arbitrary", "arbitrary"),
    )(qn, kn, vb, fqb, fkr, dao, lse, delta)


CONV_TILES = (512, 256, 128, 64)
HALO = SUBLANES


def _dwconv(x, xoff, W, w, b, act, voff, out_dtype, name):
    T = x.shape[0]
    K = w.shape[0]
    tT, tC = _tile(T, CONV_TILES), _tile(W, CONV_TILES)
    xb, hb = xoff // tC, tT // HALO
    glu = act == 'glu'

    def body(*refs):
        if glu:
            x_ref, xp_ref, w_ref, b_ref, v_ref, o_ref, buf = refs
        else:
            x_ref, xp_ref, w_ref, o_ref, buf = refs
        i = pl.program_id(0)
        buf[0:HALO, :] = jnp.where(i > 0, xp_ref[...], 0.0)
        buf[HALO:, :] = x_ref[...]
        conv = w_ref[0:1, :] * buf[pl.ds(HALO - (K - 1), tT), :]
        for k in range(1, K):
            conv = conv + w_ref[k:k + 1, :] * buf[pl.ds(HALO - (K - 1) + k, tT), :]
        if glu:
            o_ref[...] = (_gelu(conv + b_ref[...]) * v_ref[...]).astype(o_ref.dtype)
        else:
            o_ref[...] = _silu(conv).astype(o_ref.dtype)

    cur = pl.BlockSpec((tT, tC), lambda i, j: (i, xb + j))
    prev = pl.BlockSpec((HALO, tC), lambda i, j: (jnp.maximum(i * hb - 1, 0), xb + j))
    wspec = pl.BlockSpec((K, tC), lambda i, j: (0, j))
    in_specs, args = [cur, prev, wspec], [x, x, w]
    if glu:
        vb = voff // tC
        in_specs += [pl.BlockSpec((1, tC), lambda i, j: (0, j)), pl.BlockSpec((tT, tC), lambda i, j: (i, vb + j))]
        args += [b, x]
    return pl.pallas_call(
        body, name=name, grid=(T // tT, W // tC), in_specs=in_specs,
        out_specs=pl.BlockSpec((tT, tC), lambda i, j: (i, j)), out_shape=jax.ShapeDtypeStruct((T, W), out_dtype),
        scratch_shapes=[pltpu.VMEM((tT + HALO, tC), F32)], compiler_params=_cp("parallel", "parallel"),
    )(*args)


def _dwconv_bwd(x, xoff, W, w, woff, b, act, voff, dy, name):
    T = x.shape[0]
    K = w.shape[0]
    tT, tC = _tile(T, CONV_TILES), _tile(W, CONV_TILES)
    xb, wb, hb, nT = xoff // tC, woff // tC, tT // HALO, T // tT
    last_halo = T // HALO - 1
    glu = act == 'glu'

    def body(*refs):
        if glu:
            (x_ref, xp_ref, xn_ref, dy_ref, dyn_ref, w_ref, b_ref, v_ref, vn_ref,
             dx_ref, dv_ref, dw_ref, db_ref, xbuf, dybuf, dbuf, vbuf) = refs
        else:
            x_ref, xp_ref, xn_ref, dy_ref, dyn_ref, w_ref, dx_ref, dw_ref, xbuf, dybuf, dbuf = refs
        i = pl.program_id(1)

        @pl.when(i == 0)
        def _():
            dw_ref[...] = jnp.zeros_like(dw_ref)
            if glu:
                db_ref[...] = jnp.zeros_like(db_ref)

        ext = tT + HALO
        xbuf[0:HALO, :] = jnp.where(i > 0, xp_ref[...], 0.0)
        xbuf[HALO:HALO + tT, :] = x_ref[...]
        xbuf[HALO + tT:, :] = xn_ref[...]
        dybuf[0:tT, :] = dy_ref[...].astype(F32)
        dybuf[tT:, :] = jnp.where(i < nT - 1, dyn_ref[...].astype(F32), 0.0)
        conv = w_ref[0:1, :] * xbuf[pl.ds(HALO - (K - 1), ext), :]
        for k in range(1, K):
            conv = conv + w_ref[k:k + 1, :] * xbuf[pl.ds(HALO - (K - 1) + k, ext), :]
        dyv = dybuf[...]
        if glu:
            vbuf[0:tT, :] = v_ref[...]
            vbuf[tT:, :] = vn_ref[...]
            z = conv + b_ref[...]
            dconv = dyv * vbuf[...] * _gelu_grad(z)
            dv_ref[...] = (dyv[0:tT, :] * _gelu(z[0:tT, :])).astype(dv_ref.dtype)
        else:
            sg = _sigmoid(conv)
            dconv = dyv * (sg * (1.0 + conv * (1.0 - sg)))
        dbuf[...] = dconv
        dx = w_ref[0:1, :] * dbuf[pl.ds(K - 1, tT), :]
        for k in range(1, K):
            dx = dx + w_ref[k:k + 1, :] * dbuf[pl.ds(K - 1 - k, tT), :]
        dx_ref[...] = dx.astype(dx_ref.dtype)
        dc = dconv[0:tT, :]
        for k in range(K):
            dw_ref[k:k + 1, :] += jnp.sum(dc * xbuf[pl.ds(HALO - (K - 1) + k, tT), :], axis=0, keepdims=True)
        if glu:
            db_ref[...] += jnp.sum(dc, axis=0, keepdims=True)

    def cur(off):
        return pl.BlockSpec((tT, tC), lambda j, i: (i, off + j))

    def nxt(off):
        return pl.BlockSpec((HALO, tC), lambda j, i: (jnp.minimum((i + 1) * hb, last_halo), off + j))

    prev = pl.BlockSpec((HALO, tC), lambda j, i: (jnp.maximum(i * hb - 1, 0), xb + j))
    wspec = pl.BlockSpec((K, tC), lambda j, i: (0, wb + j))
    acc_w = pl.BlockSpec((K, tC), lambda j, i: (0, j))
    acc_b = pl.BlockSpec((1, tC), lambda j, i: (0, j))
    in_specs = [cur(xb), prev, nxt(xb), cur(0), nxt(0), wspec]
    args = [x, x, x, dy, dy, w]
    out_specs = [cur(0)]
    out_shape = [jax.ShapeDtypeStruct((T, W), BF16)]
    scratch = [pltpu.VMEM((tT + 2 * HALO, tC), F32), pltpu.VMEM((tT + HALO, tC), F32), pltpu.VMEM((tT + HALO, tC), F32)]
    if glu:
        vb = voff // tC
        in_specs += [pl.BlockSpec((1, tC), lambda j, i: (0, wb + j)), cur(vb), nxt(vb)]
        args += [b, x, x]
        out_specs += [cur(0), acc_w, acc_b]
        out_shape += [jax.ShapeDtypeStruct((T, W), BF16), jax.ShapeDtypeStruct((K, W), F32),
                      jax.ShapeDtypeStruct((1, W), F32)]
        scratch += [pltpu.VMEM((tT + HALO, tC), F32)]
    else:
        out_specs += [acc_w]
        out_shape += [jax.ShapeDtypeStruct((K, W), F32)]
    return pl.pallas_call(
        body, name=name, grid=(W // tC, nT), in_specs=in_specs, out_specs=out_specs, out_shape=out_shape,
        scratch_shapes=scratch, compiler_params=_cp("parallel", "arbitrary"),
    )(*args)


V_PER_K = 2


GDN_PREP_CHUNKS = 4
GDN_SCAN_CHUNKS = 4


def _b3(a, b, mode='nn', precision=None):
    c = {'nn': ((2,), (1,)), 'nt': ((2,), (2,)), 'tn': ((1,), (1,))}[mode]
    return lax.dot_general(a, b, (c, ((0,), (0,))), precision=precision, preferred_element_type=F32)


def _bb3(a, b, mode='nn'):
    return _b3(a.astype(BF16), b.astype(BF16), mode)


def _hb3(a, b, mode='nn'):
    return _b3(a, b, mode, precision=HIGHEST)


def _split_bf16(a):
    hi = a.astype(BF16)
    return hi, (a - hi.astype(F32)).astype(BF16)


def _nb3(a, b, mode='nn'):
    ah, al = _split_bf16(a)
    bh, bl = _split_bf16(b)
    return _b3(ah, bh, mode) + _b3(ah, bl, mode) + _b3(al, bh, mode)


def _to_batch(x, nc):
    C = GDN_CHUNK
    return jnp.concatenate([x[:, j * HD:(j + 1) * HD].reshape(nc, C, HD) for j in range(V_PER_K)], axis=0)


def _from_batch(x, nc):
    C = GDN_CHUNK
    return jnp.concatenate([x[j * nc:(j + 1) * nc].reshape(nc * C, HD) for j in range(V_PER_K)], axis=1)


def _both_heads(x, nc):
    xc = x.reshape(nc, GDN_CHUNK, HD)
    return jnp.concatenate([xc] * V_PER_K, axis=0)


def _gdn_local(q2, k2, gb, bb):
    B, C, _ = k2.shape
    ri = lax.broadcasted_iota(jnp.int32, (B, C, C), 1)
    ci = lax.broadcasted_iota(jnp.int32, (B, C, C), 2)
    lower, strict = ri >= ci, ri > ci
    pick0 = (lax.broadcasted_iota(jnp.int32, (B, C, HD), 2) == 0).astype(F32)
    g_cols = _hb3(pick0, gb, 'nt')
    dm = jnp.exp(jnp.where(lower, gb[:, :, :C] - g_cols, NEG))
    kk = _bb3(k2, k2, 'nt')
    a = jnp.where(strict, kk * dm * bb[:, :, :C], 0.0)
    eg = jnp.exp(gb)
    gl = gb[:, C - 1:C, :]
    return dict(lower=lower, strict=strict, eye=(ri == ci).astype(F32), dm=dm, kk=kk, a=a, eg=eg, gl=gl,
                qd=q2 * eg, kd=k2 * jnp.exp(gl - gb))


def _gdn_specs(T, Hk, voff, nc, rev=False):
    C = GDN_CHUNK
    nb = T // (nc * C)
    vb = voff // (V_PER_K * HD)
    ix = (lambda i: nb - 1 - i) if rev else (lambda i: i)
    kspec = pl.BlockSpec((nc * C, HD), lambda h, i: (ix(i), h))
    pair = pl.BlockSpec((nc * C, V_PER_K * HD), lambda h, i: (ix(i), h))
    vspec = pl.BlockSpec((nc * C, V_PER_K * HD), lambda h, i: (ix(i), vb + h))
    cc = pl.BlockSpec((V_PER_K, nc, C, C), lambda h, i: (h, ix(i), 0, 0))
    state = pl.BlockSpec((V_PER_K, nc, HD, HD), lambda h, i: (h, ix(i), 0, 0))
    scal = pl.BlockSpec((V_PER_K, nc, SUBLANES, HD), lambda h, i: (h, ix(i), 0, 0))
    return nb, kspec, pair, vspec, cc, state, scal


def _gdn_prep(qn, kn, qkvc, voff, gcb, betab, Hk, name):
    T = qn.shape[0]
    C, nc = GDN_CHUNK, GDN_PREP_CHUNKS
    Hv, N = Hk * V_PER_K, T // C
    nb, kspec, pair, vspec, cc, _, _ = _gdn_specs(T, Hk, voff, nc)

    def body(q_ref, k_ref, v_ref, g_ref, b_ref, u_ref, w_ref, tm_ref, qkm_ref):
        q2, k2 = _both_heads(q_ref[...], nc), _both_heads(k_ref[...], nc)
        v2, gb, bb = _to_batch(v_ref[...], nc), _to_batch(g_ref[...], nc), _to_batch(b_ref[...], nc)
        lc = _gdn_local(q2, k2, gb, bb)
        p = -lc['a']
        tm = lc['eye'] + p
        for _ in range(5):
            p = _nb3(p, p)
            tm = tm + _nb3(tm, p)
        u_ref[...] = _from_batch(_nb3(tm, v2 * bb), nc)
        w_ref[...] = _from_batch(_nb3(tm, k2 * (bb * lc['eg'])), nc)
        tm_ref[...] = tm.reshape(V_PER_K, nc, C, C)
        qkm_ref[...] = jnp.where(lc['lower'], _bb3(q2, k2, 'nt') * lc['dm'], 0.0).reshape(V_PER_K, nc, C, C)

    return pl.pallas_call(
        body, name=name, grid=(Hk, nb), in_specs=[kspec, kspec, vspec, pair, pair], out_specs=[pair, pair, cc, cc],
        out_shape=[jax.ShapeDtypeStruct((T, Hv * HD), F32)] * 2 + [jax.ShapeDtypeStruct((Hv, N, C, C), F32)] * 2,
        compiler_params=_cp("parallel", "parallel"),
    )(qn, kn, qkvc, gcb, betab)


def _scan_chunk(q_ref, k_ref, g_ref, rows):
    C = GDN_CHUNK
    gb = jnp.stack([g_ref[rows, j * HD:(j + 1) * HD] for j in range(V_PER_K)])
    gl = gb[:, C - 1:C, :]
    return q_ref[rows, :][None] * jnp.exp(gb), k_ref[rows, :][None] * jnp.exp(gl - gb), jnp.exp(gl)


def _heads(ref, rows):
    return jnp.stack([ref[rows, j * HD:(j + 1) * HD] for j in range(V_PER_K)])


def _put_heads(ref, rows, x):
    for j in range(V_PER_K):
        ref[rows, j * HD:(j + 1) * HD] = x[j]


def _gdn_scan(qn, kn, gcb, u, w, qkm, Hk, name):
    T = qn.shape[0]
    C, ns = GDN_CHUNK, GDN_SCAN_CHUNKS
    Hv, N = Hk * V_PER_K, T // C
    nb, kspec, pair, _, cc, state, _ = _gdn_specs(T, Hk, 0, ns)

    def body(q_ref, k_ref, g_ref, u_ref, w_ref, qkm_ref, o_ref, sp_ref, s_s):
        @pl.when(pl.program_id(1) == 0)
        def _():
            s_s[...] = jnp.zeros_like(s_s)

        s = s_s[...]
        for t in range(ns):
            rows = slice(t * C, (t + 1) * C)
            qd, kd, egl = _scan_chunk(q_ref, k_ref, g_ref, rows)
            sp_ref[:, t] = s
            vn = _heads(u_ref, rows) - _bb3(_heads(w_ref, rows), s)
            _put_heads(o_ref, rows, _bb3(qd, s) + _bb3(qkm_ref[:, t], vn))
            s = s * egl + _bb3(kd, vn, 'tn')
        s_s[...] = s

    return pl.pallas_call(
        body, name=name, grid=(Hk, nb), in_specs=[kspec, kspec, pair, pair, pair, cc], out_specs=[pair, state],
        out_shape=[jax.ShapeDtypeStruct((T, Hv * HD), F32), jax.ShapeDtypeStruct((Hv, N, HD, HD), F32)],
        scratch_shapes=[pltpu.VMEM((V_PER_K, HD, HD), F32)], compiler_params=_cp("parallel", "arbitrary"),
    )(qn, kn, gcb, u, w, qkm)


def _gdn_scan_bwd(qn, kn, gcb, u, w, qkm, sprev, do, Hk, name):
    T = qn.shape[0]
    C, ns = GDN_CHUNK, GDN_SCAN_CHUNKS
    Hv, N = Hk * V_PER_K, T // C
    nb, kspec, pair, _, cc, state, scal = _gdn_specs(T, Hk, 0, ns, rev=True)

    def body(q_ref, k_ref, g_ref, u_ref, w_ref, qkm_ref, sp_ref, do_ref,
             dqd_ref, dkd_ref, du_ref, dw_ref, dqkm_ref, dgl_ref, ds_s):
        @pl.when(pl.program_id(1) == 0)
        def _():
            ds_s[...] = jnp.zeros_like(ds_s)

        lower = lax.broadcasted_iota(jnp.int32, (V_PER_K, C, C), 1) >= lax.broadcasted_iota(jnp.int32, (V_PER_K, C, C), 2)
        ds = ds_s[...]
        for t in reversed(range(ns)):
            rows = slice(t * C, (t + 1) * C)
            qd, kd, egl = _scan_chunk(q_ref, k_ref, g_ref, rows)
            s, w_, qkm_, dov = sp_ref[:, t], _heads(w_ref, rows), qkm_ref[:, t], _heads(do_ref, rows)
            vn = _heads(u_ref, rows) - _bb3(w_, s)
            _put_heads(dqd_ref, rows, _bb3(dov, s, 'nt'))
            dqkm_ref[:, t] = jnp.where(lower, _bb3(dov, vn, 'nt'), 0.0)
            dvn = _bb3(qkm_, dov, 'tn') + _bb3(kd, ds)
            _put_heads(dkd_ref, rows, _bb3(vn, ds, 'nt'))
            dgl = jnp.sum(jnp.sum(ds * s, axis=1, keepdims=True) * egl, axis=2, keepdims=True)
            dgl_ref[:, t] = jnp.broadcast_to(dgl, (V_PER_K, SUBLANES, HD))
            _put_heads(du_ref, rows, dvn)
            _put_heads(dw_ref, rows, -_bb3(dvn, s, 'nt'))
            ds = ds * egl + _bb3(qd, dov, 'tn') - _bb3(w_, dvn, 'tn')
        ds_s[...] = ds

    return pl.pallas_call(
        body, name=name, grid=(Hk, nb), in_specs=[kspec, kspec, pair, pair, pair, cc, state, pair],
        out_specs=[pair] * 4 + [cc, scal],
        out_shape=[jax.ShapeDtypeStruct((T, Hv * HD), F32)] * 4
        + [jax.ShapeDtypeStruct((Hv, N, C, C), F32), jax.ShapeDtypeStruct((Hv, N, SUBLANES, HD), F32)],
        scratch_shapes=[pltpu.VMEM((V_PER_K, HD, HD), F32)], compiler_params=_cp("parallel", "arbitrary"),
    )(qn, kn, gcb, u, w, qkm, sprev, do)


def _gdn_prep_bwd(qn, kn, qkvc, voff, gcb, betab, tm, u, w, qkm, dqd, dkd, du, dw, dqkm, dgl, Hk, name):
    T = qn.shape[0]
    C, nc = GDN_CHUNK, GDN_PREP_CHUNKS
    Hv = Hk * V_PER_K
    B = V_PER_K * nc
    nb, kspec, pair, vspec, cc, _, scal = _gdn_specs(T, Hk, voff, nc)

    def body(q_ref, k_ref, v_ref, g_ref, b_ref, tm_ref, u_ref, w_ref, qkm_ref, dqd_ref, dkd_ref, du_ref, dw_ref,
             dqkm_ref, dgl_ref, dq_ref, dk_ref, dv_ref, dg_ref, dbe_ref):
        q2, k2 = _both_heads(q_ref[...], nc), _both_heads(k_ref[...], nc)
        v2, gb, bb = _to_batch(v_ref[...], nc), _to_batch(g_ref[...], nc), _to_batch(b_ref[...], nc)
        lc = _gdn_local(q2, k2, gb, bb)
        dm, eg, gl = lc['dm'], lc['eg'], lc['gl']
        tm_, qkm_, dqkm_ = (r[...].reshape(B, C, C) for r in (tm_ref, qkm_ref, dqkm_ref))
        u_, w_, dqd_, dkd_, du_, dw_ = (_to_batch(r[...], nc) for r in (u_ref, w_ref, dqd_ref, dkd_ref, du_ref, dw_ref))
        dgl_ = dgl_ref[...].reshape(B, SUBLANES, HD)[:, :1, :1]
        rowsum = lambda x: jnp.sum(x, axis=-1, keepdims=True)
        dbv = _nb3(tm_, du_, 'tn')
        dbk = _nb3(tm_, dw_, 'tn')
        da = jnp.where(lc['strict'], -(_bb3(dbv, u_, 'nt') + _bb3(dbk, w_, 'nt')), 0.0)
        rk = rowsum(dbk * k2)
        dbeta = rowsum(dbv * v2) + rk * eg[:, :, :1] + rowsum(da * lc['kk'] * dm)
        dkk = da * dm * bb[:, :, :C]
        dqkr = dqkm_ * dm
        dk = dbk * (bb * eg) + _bb3(dkk, k2) + _bb3(dkk, k2, 'tn') + _bb3(dqkr, q2, 'tn') + dkd_ * jnp.exp(gl - gb)
        dq = _bb3(dqkr, k2) + dqd_ * eg
        de = da * lc['a'] + dqkm_ * qkm_
        sk = rowsum(dkd_ * lc['kd'])
        dg = rk * (bb[:, :, :1] * eg[:, :, :1]) + rowsum(de) - _hb3(de, jnp.ones((B, C, HD), F32), 'tn')[:, :, :1] \
            + rowsum(dqd_ * lc['qd']) - sk
        last = (lax.broadcasted_iota(jnp.int32, (B, C, HD), 1) == C - 1).astype(F32)
        dgb = jnp.broadcast_to(dg, (B, C, HD)) + last * (dgl_ + jnp.sum(sk, axis=1, keepdims=True))
        suffix = (lax.broadcasted_iota(jnp.int32, (B, C, C), 2) >= lax.broadcasted_iota(jnp.int32, (B, C, C), 1)).astype(F32)
        dq_ref[...] = _from_batch(dq, nc)
        dk_ref[...] = _from_batch(dk, nc)
        dv_ref[...] = _from_batch(dbv * bb, nc)
        dg_ref[...] = _from_batch(_hb3(suffix, dgb), nc)
        dbe_ref[...] = _from_batch(jnp.broadcast_to(dbeta, (B, C, HD)), nc)

    return pl.pallas_call(
        body, name=name, grid=(Hk, nb),
        in_specs=[kspec, kspec, vspec, pair, pair, cc, pair, pair, cc, pair, pair, pair, pair, cc, scal],
        out_specs=[pair] * 5, out_shape=[jax.ShapeDtypeStruct((T, Hv * HD), F32)] * 5,
        compiler_params=_cp("parallel", "parallel"),
    )(qn, kn, qkvc, gcb, betab, tm, u, w, qkm, dqd, dkd, du, dw, dqkm, dgl)


def _adamw_math(w, g, m, v):
    m = ADAM_B1 * m + (1.0 - ADAM_B1) * g
    v = ADAM_B2 * v + (1.0 - ADAM_B2) * jnp.square(g)
    m_hat = m / (1.0 - ADAM_B1 ** ADAM_STEP)
    v_hat = v / (1.0 - ADAM_B2 ** ADAM_STEP)
    delta = -ADAM_LR * (m_hat / (jnp.sqrt(v_hat) + ADAM_EPS) + ADAM_WD * w)
    return delta, m, v


STREAM_BLOCK_BYTES = 1 << 20


def _stream_rows(R, C, mult=SUBLANES):
    for tr in (512, 256, 128, 64, 32, 16, 8):
        if R % tr == 0 and tr % mult == 0 and tr * C * 4 <= STREAM_BLOCK_BYTES:
            return tr
    return R


def _adamw(w, m, v, gs, name):
    R, C = w.shape
    tr = _stream_rows(R, C)
    n = len(gs)

    def body(*refs):
        w_ref, m_ref, v_ref = refs[:3]
        g_refs = refs[3:3 + n]
        g_out, d_out, m_out, v_out = refs[3 + n:]
        g = g_refs[0][...]
        for r_ in g_refs[1:]:
            g = g + r_[...]
        g_out[...] = g
        d_out[...], m_out[...], v_out[...] = _adamw_math(w_ref[...], g, m_ref[...], v_ref[...])

    spec = pl.BlockSpec((tr, C), lambda i: (i, 0))
    return pl.pallas_call(
        body, name=name, grid=(R // tr,), in_specs=[spec] * (3 + n), out_specs=[spec] * 4,
        out_shape=[jax.ShapeDtypeStruct((R, C), F32)] * 4, compiler_params=_cp("parallel"),
    )(w, m, v, *gs)


def _sum_devices(g8, name):
    _, M, C = g8.shape
    tr = _tile(M, (512, 256, 128, 64, 32, 16, 8))

    def body(g_ref, o_ref):
        acc = g_ref[0]
        for d in range(1, N_DEV):
            acc = acc + g_ref[d]
        o_ref[...] = acc

    return pl.pallas_call(
        body, name=name, grid=(M // tr,), in_specs=[pl.BlockSpec((N_DEV, tr, C), lambda i: (0, i, 0))],
        out_specs=pl.BlockSpec((tr, C), lambda i: (i, 0)), out_shape=jax.ShapeDtypeStruct((M, C), F32),
        compiler_params=_cp("parallel"),
    )(g8)


def _ada_w_update(c_all, dm, w, m, v, name):
    n_mod, D, Ns = w.shape
    tr = _tile(D, (256, 128))

    def body(c_ref, dm_ref, w_ref, m_ref, v_ref, g_out, d_out, m_out, v_out):
        g = _hdot(_silu(c_ref[...]), dm_ref[...], 'tn')
        g_out[...] = g
        d_out[...], m_out[...], v_out[...] = _adamw_math(w_ref[...], g, m_ref[...], v_ref[...])

    wspec = pl.BlockSpec((None, tr, Ns), lambda i, r: (i, r, 0))
    return pl.pallas_call(
        body, name=name, grid=(n_mod, D // tr),
        in_specs=[pl.BlockSpec((N_DEV, tr), lambda i, r: (0, r)), pl.BlockSpec((None, N_DEV, Ns), lambda i, r: (i, 0, 0)),
                  wspec, wspec, wspec],
        out_specs=[wspec] * 4, out_shape=[jax.ShapeDtypeStruct((n_mod, D, Ns), F32)] * 4,
        compiler_params=_cp("parallel", "parallel"),
    )(c_all, dm, w, m, v)


def _place():
    return lax.axis_index("x"), lax.axis_index("y"), lax.axis_index("c")


def _allgather8(x_shard, name):
    m_per, n = x_shard.shape

    def body(x_ref, out_ref, send_sems, recv_sems, local_sem):
        x, y, c = _place()
        me, sibling = (x, y, c), (x, y, 1 - c)
        chips = [(1 - x, y), (x, 1 - y), (1 - x, 1 - y)]

        def rows(px, py, pc):
            return out_ref.at[pl.ds((4 * px + 2 * py + pc) * m_per, m_per), :]

        def copy(k, block, to, src=None):
            return pltpu.make_async_remote_copy(
                src_ref=rows(*block) if src is None else src, dst_ref=rows(*block),
                send_sem=send_sems.at[k], recv_sem=recv_sems.at[k], device_id=to, device_id_type=MESH)

        mine = pltpu.make_async_copy(x_ref, rows(*me), local_sem)
        mine.start()
        first = [copy(0, me, sibling, src=x_ref)]
        first += [copy(1 + j, me, (*chip, c), src=x_ref) for j, chip in enumerate(chips)]
        for cp in first:
            cp.start()
        passed = [copy(4 + j, (*chip, c), sibling) for j, chip in enumerate(chips)]
        for j, chip in enumerate(chips):
            copy(1 + j, (*chip, c), me).wait_recv()
            passed[j].start()
        copy(0, sibling, me).wait_recv()
        for j, chip in enumerate(chips):
            copy(4 + j, (*chip, 1 - c), me).wait_recv()
        for cp in first + passed:
            cp.wait_send()
        mine.wait()

    return pl.pallas_call(
        body, name=name, out_shape=jax.ShapeDtypeStruct((N_DEV * m_per, n), x_shard.dtype),
        in_specs=[pl.BlockSpec(memory_space=pltpu.VMEM)], out_specs=pl.BlockSpec(memory_space=pltpu.VMEM),
        scratch_shapes=[pltpu.SemaphoreType.DMA((7,)), pltpu.SemaphoreType.DMA((7,)), pltpu.SemaphoreType.DMA],
        compiler_params=pltpu.CompilerParams(vmem_limit_bytes=VMEM_LIMIT),
    )(x_shard)


HBM_SPEC = pl.BlockSpec(memory_space=pltpu.HBM)


def _gather_weights(w_flat, name):
    R, C = w_flat.shape
    half = R // 2

    def body(w_ref, out_ref, send_sems, recv_sems, local_sem):
        x, y, c = _place()
        sibling = (x, y, 1 - c)
        chips = [(1 - x, y), (x, 1 - y), (1 - x, 1 - y)]

        def part(cx, cy, hc):
            return out_ref.at[2 * cx + cy, pl.ds(hc * half, half), :]

        def copy(k, block, to, src=None):
            return pltpu.make_async_remote_copy(
                src_ref=part(*block) if src is None else src, dst_ref=part(*block),
                send_sem=send_sems.at[k], recv_sem=recv_sems.at[k], device_id=to, device_id_type=MESH)

        mine = pltpu.make_async_copy(w_ref, out_ref.at[2 * x + y], local_sem)
        mine.start()
        first = [copy(j, (x, y, c), (*chip, c), src=w_ref.at[pl.ds(c * half, half), :]) for j, chip in enumerate(chips)]
        for cp in first:
            cp.start()
        passed = [copy(3 + j, (*chip, c), sibling) for j, chip in enumerate(chips)]
        for j, chip in enumerate(chips):
            copy(j, (*chip, c), (x, y, c)).wait_recv()
            passed[j].start()
        for j, chip in enumerate(chips):
            copy(3 + j, (*chip, 1 - c), (x, y, c)).wait_recv()
        for cp in first + passed:
            cp.wait_send()
        mine.wait()

    return pl.pallas_call(
        body, name=name, out_shape=jax.ShapeDtypeStruct((N_CHIPS, R, C), w_flat.dtype),
        in_specs=[HBM_SPEC], out_specs=HBM_SPEC,
        scratch_shapes=[pltpu.SemaphoreType.DMA((6,)), pltpu.SemaphoreType.DMA((6,)), pltpu.SemaphoreType.DMA],
    )(w_flat)


def _swap_halves(g, name):
    n, R, C = g.shape
    half = R // 2

    def body(g_ref, got_ref, send_sem, recv_sem):
        x, y, c = _place()
        cp = pltpu.make_async_remote_copy(
            src_ref=g_ref.at[:, pl.ds((1 - c) * half, half), :], dst_ref=got_ref,
            send_sem=send_sem, recv_sem=recv_sem, device_id=(x, y, 1 - c), device_id_type=MESH)
        cp.start()
        cp.wait()

    return pl.pallas_call(
        body, name=name, out_shape=jax.ShapeDtypeStruct((n, half, C), g.dtype),
        in_specs=[HBM_SPEC], out_specs=HBM_SPEC,
        scratch_shapes=[pltpu.SemaphoreType.DMA, pltpu.SemaphoreType.DMA],
    )(g)


def _scatter_chips(q, name):
    n, R2, C = q.shape

    def body(q_ref, got_ref, send_sems, recv_sems):
        x, y, c = _place()
        chips = [(1 - x, y), (x, 1 - y), (1 - x, 1 - y)]
        cps = [pltpu.make_async_remote_copy(
            src_ref=q_ref.at[2 * cx + cy], dst_ref=got_ref.at[j], send_sem=send_sems.at[j], recv_sem=recv_sems.at[j],
            device_id=(cx, cy, c), device_id_type=MESH) for j, (cx, cy) in enumerate(chips)]
        for cp in cps:
            cp.start()
        for cp in cps:
            cp.wait()

    return pl.pallas_call(
        body, name=name, out_shape=jax.ShapeDtypeStruct((3, R2, C), q.dtype),
        in_specs=[HBM_SPEC], out_specs=HBM_SPEC,
        scratch_shapes=[pltpu.SemaphoreType.DMA((3,)), pltpu.SemaphoreType.DMA((3,))],
    )(q)


def _join_halves(h, name):
    R2, C = h.shape

    def body(h_ref, out_ref, send_sem, recv_sem, local_sem):
        x, y, c = _place()
        mine = pltpu.make_async_copy(h_ref, out_ref.at[pl.ds(c * R2, R2), :], local_sem)
        mine.start()
        cp = pltpu.make_async_remote_copy(
            src_ref=h_ref, dst_ref=out_ref.at[pl.ds(c * R2, R2), :],
            send_sem=send_sem, recv_sem=recv_sem, device_id=(x, y, 1 - c), device_id_type=MESH)
        cp.start()
        cp.wait()
        mine.wait()

    return pl.pallas_call(
        body, name=name, out_shape=jax.ShapeDtypeStruct((2 * R2, C), h.dtype),
        in_specs=[HBM_SPEC], out_specs=HBM_SPEC,
        scratch_shapes=[pltpu.SemaphoreType.DMA, pltpu.SemaphoreType.DMA, pltpu.SemaphoreType.DMA],
    )(h)


def _add_halves(g, got, c_idx, name):
    n, R, C = g.shape
    half = R // 2
    tr = _stream_rows(half, C, 2 * SUBLANES)
    nb = half // tr

    def body(c_ref, g_ref, got_ref, o_ref):
        o_ref[...] = (g_ref[...] + got_ref[...]).astype(o_ref.dtype)

    grid_spec = pltpu.PrefetchScalarGridSpec(
        num_scalar_prefetch=1, grid=(n, nb),
        in_specs=[pl.BlockSpec((None, tr, C), lambda s, i, c_ref: (s, c_ref[0] * nb + i, 0)),
                  pl.BlockSpec((None, tr, C), lambda s, i, c_ref: (s, i, 0))],
        out_specs=pl.BlockSpec((None, tr, C), lambda s, i, c_ref: (s, i, 0)))
    return pl.pallas_call(
        body, name=name, grid_spec=grid_spec, out_shape=jax.ShapeDtypeStruct((n, half, C), BF16),
        compiler_params=_cp("parallel", "parallel"),
    )(c_idx, g, got)


def _add_chips(q, got, s_idx, name):
    n, R2, C = q.shape
    tr = _stream_rows(R2, C, 2 * SUBLANES)

    def body(s_ref, q_ref, g0_ref, g1_ref, g2_ref, o_ref):
        o_ref[...] = ((q_ref[...].astype(F32) + g0_ref[...].astype(F32)) + g1_ref[...].astype(F32)) \
            + g2_ref[...].astype(F32)

    def got_spec(j):
        return pl.BlockSpec((None, tr, C), lambda i, s_ref: (j, i, 0))

    grid_spec = pltpu.PrefetchScalarGridSpec(
        num_scalar_prefetch=1, grid=(R2 // tr,),
        in_specs=[pl.BlockSpec((None, tr, C), lambda i, s_ref: (s_ref[0], i, 0)), got_spec(0), got_spec(1), got_spec(2)],
        out_specs=pl.BlockSpec((tr, C), lambda i, s_ref: (i, 0)))
    return pl.pallas_call(
        body, name=name, grid_spec=grid_spec, out_shape=jax.ShapeDtypeStruct((R2, C), F32),
        compiler_params=_cp("parallel"),
    )(s_idx, q, got, got, got)


def _pack_lanes(arrs):
    rows = []
    for a in arrs:
        f = a.reshape(-1)
        n = -(-f.shape[0] // LANES) * LANES
        rows.append(jnp.pad(f, (0, n - f.shape[0])).reshape(-1, LANES))
    out = jnp.concatenate(rows, axis=0)
    pad = -out.shape[0] % SUBLANES
    return jnp.pad(out, ((0, pad), (0, 0)))


def _unpack_lanes(packed, shapes):
    out, r = [], 0
    for shp in shapes:
        n = math.prod(shp)
        nr = -(-n // LANES)
        out.append(packed[r:r + nr].reshape(-1)[:n].reshape(shp))
        r += nr
    return out


def _shards_to_full(sh, axis):
    return jnp.concatenate([sh[i] for i in range(N_CHIPS)], axis=axis)


def _full_to_shards(full, axis):
    return jnp.stack(jnp.split(full, N_CHIPS, axis=axis), axis=0)


def _pad_cols(a, n):
    return jnp.pad(a, ((0, 0), (0, n - a.shape[1])))


def _lane_bcast(a):
    return jnp.repeat(a, HD, axis=1)


def _split_mod(mod):
    D = mod.shape[0] // 3
    return mod[None, :D], mod[None, D:2 * D], mod[None, 2 * D:]


def _fox_fwd(h, w, tag):
    T, D = h.shape
    H = D // HD
    proj = _mm(h, w['cat'], 'nn', F32, tag + '_proj')
    flog = _mm(h, w['f'], 'nn', F32, tag + '_flog')
    qn = _headnorm(proj, 0, H, w['q_norm'], 1.0 / HD, 1.0, BF16, tag + '_qnorm')
    kn = _headnorm(proj, H, H, w['k_norm'], 1.0 / HD, 1.0, BF16, tag + '_knorm')
    vb = proj[:, 2 * D:3 * D].astype(BF16)
    fcum = _cumsum(_logsig(flog, w['f_bias'], tag + '_logf'), None, False, tag + '_fcum')
    fqb = _lane_bcast(fcum[:, :H])
    fkr = fcum[:, :H].T.reshape(H, 1, T)
    ao, lse = _flash_fwd(qn, kn, vb, fqb, fkr, H, tag + '_att')
    gated = _fox_gate(ao, proj, 3 * H, H, tag + '_ogate')
    y = _mm(gated, w['o'], 'nn', F32, tag + '_out')
    return y, dict(h=h, proj=proj, flog=flog, qn=qn, kn=kn, vb=vb, fqb=fqb, fkr=fkr, ao=ao, lse=lse, gated=gated)


def _fox_bwd(dy, w, sv, tag):
    h, proj = sv['h'], sv['proj']
    T, D = h.shape
    H = D // HD
    g = {}
    g['o'] = _mm(sv['gated'], dy, 'tn', F32, tag + '_dwo')
    dgated = _mm(dy, w['o'], 'nt', F32, tag + '_dgated')
    dao, dog, delta = _fox_gate_bwd(dgated, sv['ao'], proj, 3 * H, H, tag + '_ogate_bwd')
    att = (sv['qn'], sv['kn'], sv['vb'], sv['fqb'], sv['fkr'], dao, sv['lse'], delta, H)
    dq, dfq, dk, dv, dfk = _flash_bwd(*att, tag + '_att_bwd')
    dfcum = _pad_cols(dfq[:, ::HD] + dfk.reshape(H, T).T, LANES)
    dlogf = _cumsum(dfcum, None, True, tag + '_fcum_bwd')
    dflog, g['f_bias'] = _logsig_bwd(dlogf, sv['flog'], w['f_bias'], tag + '_logf_bwd')
    dqr, g['q_norm'] = _headnorm_bwd([(dq, 1, 0)], proj, 0, H, w['q_norm'], 1.0 / HD, 1.0, BF16, tag + '_qnorm_bwd')
    dkr, g['k_norm'] = _headnorm_bwd([(dk, 1, 0)], proj, H, H, w['k_norm'], 1.0 / HD, 1.0, BF16, tag + '_knorm_bwd')
    dproj = jnp.concatenate([dqr, dkr, dv, dog], axis=1)
    g['cat'] = _mm(h, dproj, 'tn', F32, tag + '_dwcat')
    g['f'] = _mm(h, dflog, 'tn', F32, tag + '_dwf')
    dh = [_mm(dproj, w['cat'], 'nt', F32, tag + '_dh'), _mm(dflog, w['f'], 'nt', F32, tag + '_dh_f')]
    return dh, g


def _gdn_fwd(h, w, tag):
    T, D = h.shape
    Hk = D // HD
    Hv = V_PER_K * Hk
    proj = _mm(h, w['cat'], 'nn', F32, tag + '_proj')
    ab = _mm(h, w['ab'], 'nn', F32, tag + '_ab')
    qkvc = _dwconv(proj, 0, 4 * D, w['conv'], None, 'silu', 0, F32, tag + '_conv')
    qn = _headnorm(qkvc, 0, Hk, None, 1.0, HD ** -0.5, F32, tag + '_qnorm')
    kn = _headnorm(qkvc, Hk, Hk, None, 1.0, 1.0, F32, tag + '_knorm')
    graw, beta = _gdn_gates(ab, w['a_log'], w['dt_bias'], tag + '_gates')
    gc = _cumsum(graw, GDN_CHUNK, False, tag + '_gcum')
    gcb = _lane_bcast(gc[:, :Hv])
    betab = _lane_bcast(beta[:, Hv:2 * Hv])
    u, wk, tm, qkm = _gdn_prep(qn, kn, qkvc, 2 * D, gcb, betab, Hk, tag + '_prep')
    o, sprev = _gdn_scan(qn, kn, gcb, u, wk, qkm, Hk, tag + '_scan')
    go = _gdn_out(o, proj, 4 * Hk, Hv, w['out_norm'], tag + '_onorm')
    y = _mm(go, w['o'], 'nn', F32, tag + '_out')
    return y, dict(h=h, proj=proj, ab=ab, qkvc=qkvc, qn=qn, kn=kn, gcb=gcb, betab=betab, o=o, sprev=sprev, go=go,
                   u=u, wk=wk, tm=tm, qkm=qkm)


def _gdn_bwd(dy, w, sv, tag):
    h, proj, qkvc = sv['h'], sv['proj'], sv['qkvc']
    T, D = h.shape
    Hk = D // HD
    Hv = V_PER_K * Hk
    g = {}
    g['o'] = _mm(sv['go'], dy, 'tn', F32, tag + '_dwo')
    dgo = _mm(dy, w['o'], 'nt', F32, tag + '_dgo')
    do, dz, g['out_norm'] = _gdn_out_bwd(dgo, sv['o'], proj, 4 * Hk, Hv, w['out_norm'], tag + '_onorm_bwd')
    local = (sv['u'], sv['wk'], sv['qkm'])
    dqd, dkd, du, dwk, dqkm, dgl = _gdn_scan_bwd(sv['qn'], sv['kn'], sv['gcb'], *local, sv['sprev'], do, Hk,
                                                 tag + '_scan_bwd')
    dqp, dkp, dv, dgb, dbetab = _gdn_prep_bwd(sv['qn'], sv['kn'], qkvc, 2 * D, sv['gcb'], sv['betab'], sv['tm'], *local,
                                              dqd, dkd, du, dwk, dqkm, dgl, Hk, tag + '_prep_bwd')
    pairs = lambda a: [(a, V_PER_K, j) for j in range(V_PER_K)]
    dqc, _ = _headnorm_bwd(pairs(dqp), qkvc, 0, Hk, None, 1.0, HD ** -0.5, F32, tag + '_qnorm_bwd')
    dkc, _ = _headnorm_bwd(pairs(dkp), qkvc, Hk, Hk, None, 1.0, 1.0, F32, tag + '_knorm_bwd')
    zeros = jnp.zeros((T, Hv), F32)
    dg_pad = _pad_cols(dgb[:, ::HD], LANES)
    dbeta_pad = _pad_cols(jnp.concatenate([zeros, dbetab[:, ::HD]], axis=1), LANES)
    dab, g['a_log'], g['dt_bias'] = _gdn_gates_bwd(dg_pad, dbeta_pad, sv['ab'], w['a_log'], w['dt_bias'], tag + '_gates_bwd')
    dpq, dwq = _dwconv_bwd(proj, 0, D, w['conv'], 0, None, 'silu', 0, dqc, tag + '_conv_bwd_q')
    dpk, dwk = _dwconv_bwd(proj, D, D, w['conv'], D, None, 'silu', 0, dkc, tag + '_conv_bwd_k')
    dpv, dwv = _dwconv_bwd(proj, 2 * D, 2 * D, w['conv'], 2 * D, None, 'silu', 0, dv, tag + '_conv_bwd_v')
    g['conv'] = jnp.concatenate([dwq, dwk, dwv], axis=1)
    dproj = jnp.concatenate([dpq, dpk, dpv, dz], axis=1)
    g['cat'] = _mm(h, dproj, 'tn', F32, tag + '_dwcat')
    g['ab'] = _mm(h, dab, 'tn', F32, tag + '_dwab')
    dh = [_mm(dproj, w['cat'], 'nt', F32, tag + '_dh'), _mm(dab, w['ab'], 'nt', F32, tag + '_dh_ab')]
    return dh, g


def _ffn_fwd(h, w, tag):
    dff = w['down'].shape[0]
    up = _mm(h, w['up'], 'nn', F32, tag + '_up')
    act = _dwconv(up, 0, dff, w['conv'], w['conv_b'], 'glu', dff, BF16, tag + '_conv')
    y = _mm(act, w['down'], 'nn', F32, tag + '_down')
    return y, dict(h=h, up=up, act=act)


def _ffn_bwd(dy, w, sv, tag):
    h, up = sv['h'], sv['up']
    dff = w['down'].shape[0]
    g = {}
    g['down'] = _mm(sv['act'], dy, 'tn', F32, tag + '_dwdown')
    dact = _mm(dy, w['down'], 'nt', F32, tag + '_dact')
    dgate, dval, g['conv'], g['conv_b'] = _dwconv_bwd(up, 0, dff, w['conv'], 0, w['conv_b'], 'glu', dff, dact,
                                                      tag + '_conv_bwd')
    dup = jnp.concatenate([dgate, dval], axis=1)
    g['up'] = _mm(h, dup, 'tn', F32, tag + '_dwup')
    dh = [_mm(dup, w['up'], 'nt', F32, tag + '_dh')]
    return dh, g


def _local_step(x, target, mods, norm_g, wf, wg, wffn):
    mixers = [(_fox_fwd, _fox_bwd, wf, 'fox'), (_gdn_fwd, _gdn_bwd, wg, 'gdn')]
    tape = []
    for i in range(2):
        for sub in range(2):
            if sub == 0:
                fwd, bwd, w, tag = mixers[i]
            else:
                fwd, bwd, w, tag = _ffn_fwd, _ffn_bwd, wffn[i], 'ffn%d' % i
            shift, scale, gate = _split_mod(mods[i, sub])
            g_pre, g_post = norm_g[i, 2 * sub][None], norm_g[i, 2 * sub + 1][None]
            h = _pre_norm(x, g_pre, scale, shift, tag + '_prenorm')
            y, sv = fwd(h, w, tag)
            x_out = _post_res(x, y, gate, g_post, tag + '_postnorm')
            tape.append((bwd, w, tag, sv, x, y, g_pre, g_post, scale, gate))
            x = x_out
    dx, lsum = _loss_head(x, target, 'loss_head')
    loss = lsum[0, 0]
    dmods = [[None, None], [None, None]]
    dnorm = [[None] * 4, [None] * 4]
    wgrads = {}
    for idx in reversed(range(4)):
        i, sub = divmod(idx, 2)
        bwd, w, tag, sv, x_in, y, g_pre, g_post, scale, gate = tape[idx]
        dy, dgate, dgpost = _post_res_bwd(dx, y, gate, g_post, tag + '_postnorm_bwd')
        dh, wgrads[tag] = bwd(dy, w, sv, tag)
        dx, dshift, dscale, dgpre = _pre_norm_bwd(dh, x_in, g_pre, scale, dx, tag + '_prenorm_bwd')
        dmods[i][sub] = jnp.concatenate([dshift[0], dscale[0], dgate[0]])
        dnorm[i][2 * sub], dnorm[i][2 * sub + 1] = dgpre[0], dgpost[0]
    dmods = jnp.stack([jnp.stack(r) for r in dmods])
    dnorm = jnp.stack([jnp.stack(r) for r in dnorm])
    return loss, dx, dmods, dnorm, wgrads


def _unpack_lanes_dev(packed, shapes):
    n = packed.shape[0]
    out, r = [], 0
    for shp in shapes:
        k = math.prod(shp)
        nr = -(-k // LANES)
        out.append(packed[:, r:r + nr].reshape(n, -1)[:, :k].reshape((n,) + tuple(shp)))
        r += nr
    return out


def _gather_lanes(arrs, name):
    packed = _pack_lanes(arrs)
    got = _allgather8(packed, name).reshape(N_DEV, packed.shape[0], LANES)
    return _unpack_lanes_dev(got, [a.shape for a in arrs]), got


def kernel(x, c, ada_w, ada_b, norm_g, fox_w_in, fox_f_bias, fox_q_norm, fox_k_norm, fox_w_o, gdn_w_in, gdn_conv_w, gdn_a_log, gdn_dt_bias, gdn_out_norm, gdn_w_o, ffn_w_up, ffn_conv_w, ffn_conv_b, ffn_w_down, loss_target, m_ada_w, m_ada_b, m_norm_g, m_fox_w_in, m_fox_f_bias, m_fox_q_norm, m_fox_k_norm, m_fox_w_o, m_gdn_w_in, m_gdn_conv_w, m_gdn_a_log, m_gdn_dt_bias, m_gdn_out_norm, m_gdn_w_o, m_ffn_w_up, m_ffn_conv_w, m_ffn_conv_b, m_ffn_w_down, v_ada_w, v_ada_b, v_norm_g, v_fox_w_in, v_fox_f_bias, v_fox_q_norm, v_fox_k_norm, v_fox_w_o, v_gdn_w_in, v_gdn_conv_w, v_gdn_a_log, v_gdn_dt_bias, v_gdn_out_norm, v_gdn_w_o, v_ffn_w_up, v_ffn_conv_w, v_ffn_conv_b, v_ffn_w_down):
    args = locals()
    w = {n: args[n] for n in WEIGHTS}
    mom = {n: args['m_' + n] for n in WEIGHTS}
    var = {n: args['v_' + n] for n in WEIGHTS}
    _, T, D = x.shape
    H = D // HD
    Hv = V_PER_K * H
    xi, yi, ci = _place()
    s_idx = 2 * xi + yi
    b_idx = 4 * xi + 2 * yi + ci
    s_arr = jnp.reshape(s_idx, (1,)).astype(jnp.int32)
    c_arr = jnp.reshape(ci, (1,)).astype(jnp.int32)

    (c_all, ab_all, ng_all, gcw_all, fcw_all), _ = _gather_lanes(
        [jnp.tile(c, (SUBLANES, 1)), ada_b, norm_g, gdn_conv_w, ffn_conv_w], 'gather_small')
    c_all = c_all[:, 0, :]
    chips = lambda a: jnp.concatenate([a[2 * s] for s in range(N_CHIPS)], axis=-1)
    norm_g_full, gdn_conv_full, ffn_conv_full = chips(ng_all), chips(gcw_all), chips(fcw_all)

    Ns = ada_w.shape[-1]
    ada_w4 = ada_w.reshape(4, D, Ns)
    part = jnp.stack([_mm(c_all, ada_w4[i], 'nn', F32, 'ada_proj%d' % i, a_act='silu') for i in range(4)])
    part = part + ada_b.reshape(4, 1, Ns)
    (part_all,), _ = _gather_lanes([part], 'gather_mods')
    mine = lax.dynamic_index_in_dim(part_all[0::2], b_idx, axis=2, keepdims=False)
    mods = mine.transpose(1, 0, 2).reshape(2, 2, N_CHIPS * Ns)

    as2d = lambda a: a.reshape(-1, a.shape[-1])
    gathered = {n: _gather_weights(as2d(w[n]).astype(BF16), 'gather_' + n) for n in BIG}
    full = {n: _shards_to_full(gathered[n].reshape((N_CHIPS,) + w[n].shape), BIG_SHARD_AXIS[n]) for n in BIG}
    fw, gw = full['fox_w_in'][0], full['gdn_w_in'][0]
    wf = dict(cat=jnp.concatenate([fw[:, :3 * D], fw[:, 3 * D + H:]], axis=1), f=_pad_cols(fw[:, 3 * D:3 * D + H], LANES),
              f_bias=_pad_cols(fox_f_bias, LANES), q_norm=fox_q_norm, k_norm=fox_k_norm, o=full['fox_w_o'][0])
    wg = dict(cat=gw[:, :6 * D], ab=_pad_cols(gw[:, 6 * D:], LANES), conv=gdn_conv_full[0],
              a_log=_pad_cols(gdn_a_log, LANES), dt_bias=_pad_cols(gdn_dt_bias, LANES), out_norm=gdn_out_norm,
              o=full['gdn_w_o'][0])
    wffn = [dict(up=full['ffn_w_up'][i], conv=ffn_conv_full[i], conv_b=ffn_conv_b[i][None], down=full['ffn_w_down'][i])
            for i in range(2)]

    loss, dx, dmods, dnorm, g = _local_step(x[0], loss_target[0], mods, norm_g_full, wf, wg, wffn)
    loss = lax.psum(loss, ('x', 'y', 'c'))

    gf, gg = g['fox'], g['gdn']
    big_grads = {
        'fox_w_in': jnp.concatenate([gf['cat'][:, :3 * D], gf['f'][:, :H], gf['cat'][:, 3 * D:]], axis=1)[None],
        'fox_w_o': gf['o'][None],
        'gdn_w_in': jnp.concatenate([gg['cat'], gg['ab'][:, :2 * Hv]], axis=1)[None],
        'gdn_w_o': gg['o'][None],
        'ffn_w_up': jnp.stack([g['ffn0']['up'], g['ffn1']['up']]),
        'ffn_w_down': jnp.stack([g['ffn0']['down'], g['ffn1']['down']]),
    }
    big_out = {}
    for n in BIG:
        shards = _full_to_shards(big_grads[n], BIG_SHARD_AXIS[n])
        shards = shards.reshape(N_CHIPS, -1, shards.shape[-1])
        pair_sum = _add_halves(shards, _swap_halves(shards, n + '_to_sibling'), c_arr, n + '_add_sibling')
        half_sum = _add_chips(pair_sum, _scatter_chips(pair_sum, n + '_to_chips'), s_arr, n + '_add_chips')
        g_shard = _join_halves(half_sum, n + '_join')
        big_out[n] = [o.reshape(w[n].shape)
                      for o in _adamw(as2d(w[n]), as2d(mom[n]), as2d(var[n]), [g_shard], 'adamw_' + n)]

    small_part = [dmods, dnorm, gf['f_bias'][:, :H], gf['q_norm'], gf['k_norm'], gg['conv'][None],
                  gg['a_log'][:, :Hv], gg['dt_bias'][:, :Hv], gg['out_norm'],
                  jnp.stack([g['ffn0']['conv'], g['ffn1']['conv']]),
                  jnp.concatenate([g['ffn0']['conv_b'], g['ffn1']['conv_b']], axis=0)]
    (dmods_all, *_), got = _gather_lanes(small_part, 'gather_small_grads')
    tot = _unpack_lanes(_sum_devices(got, 'sum_small_grads'), [a.shape for a in small_part])
    small_full = dict(zip(SMALL, tot))
    small_g = {n: (lax.dynamic_slice_in_dim(small_full[n], s_idx * w[n].shape[-1], w[n].shape[-1], axis=-1)
                   if n in SMALL_SHARDED else small_full[n]) for n in SMALL}
    packs = lambda d: _pack_lanes([d[n] for n in SMALL])
    small_shapes = [w[n].shape for n in SMALL]
    small_out = [_unpack_lanes(o, small_shapes)
                 for o in _adamw(packs(w), packs(mom), packs(var), [packs(small_g)], 'adamw_small')]

    dm = lax.dynamic_slice_in_dim(dmods_all.reshape(N_DEV, 4, N_CHIPS * Ns), s_idx * Ns, Ns, axis=-1).transpose(1, 0, 2)
    ada_out = [o.reshape(ada_w.shape) for o in
               _ada_w_update(c_all, dm, ada_w4, m_ada_w.reshape(4, D, Ns), v_ada_w.reshape(4, D, Ns), 'adamw_ada_w')]

    outs = []
    for k in range(4):
        by_name = {'ada_w': ada_out[k]}
        by_name.update({n: big_out[n][k] for n in BIG})
        by_name.update(zip(SMALL, small_out[k]))
        outs += [by_name[n] for n in WEIGHTS]
    return (loss, dx[None], *outs)
```

```python
import functools
import math

import jax
import jax.numpy as jnp
from jax import lax
from jax.experimental import pallas as pl
from jax.experimental.pallas import tpu as pltpu

F32 = jnp.float32
BF16 = jnp.bfloat16
EPS = 1e-6
HD = 128
GDN_CHUNK = 64
GDN_CONV = 4
FFN_CONV = 3
LANES = 128
SUBLANES = 8
VMEM_LIMIT = 56 * 1024 * 1024
HIGHEST = lax.Precision.HIGHEST
NEG = -1e30

ADAM_LR = 0.001
ADAM_B1 = 0.9
ADAM_B2 = 0.999
ADAM_EPS = 1e-08
ADAM_WD = 0.01
ADAM_STEP = 10

WEIGHTS = ['ada_w', 'ada_b', 'norm_g', 'fox_w_in', 'fox_f_bias', 'fox_q_norm', 'fox_k_norm', 'fox_w_o',
           'gdn_w_in', 'gdn_conv_w', 'gdn_a_log', 'gdn_dt_bias', 'gdn_out_norm', 'gdn_w_o',
           'ffn_w_up', 'ffn_conv_w', 'ffn_conv_b', 'ffn_w_down']
BIG = ['fox_w_in', 'fox_w_o', 'gdn_w_in', 'gdn_w_o', 'ffn_w_up', 'ffn_w_down']
BIG_SHARD_AXIS = {'fox_w_in': 2, 'fox_w_o': 1, 'gdn_w_in': 2, 'gdn_w_o': 1, 'ffn_w_up': 2, 'ffn_w_down': 1}
SMALL = ['ada_b', 'norm_g', 'fox_f_bias', 'fox_q_norm', 'fox_k_norm', 'gdn_conv_w', 'gdn_a_log',
         'gdn_dt_bias', 'gdn_out_norm', 'ffn_conv_w', 'ffn_conv_b']
SMALL_SHARDED = ['ada_b', 'norm_g', 'gdn_conv_w', 'ffn_conv_w']
N_CHIPS = 4
N_DEV = 8
MESH = pl.DeviceIdType.MESH


def _tile(n, cands):
    for c in cands:
        if n % c == 0:
            return c
    return n


def _cp(*sem):
    return pltpu.CompilerParams(dimension_semantics=sem, vmem_limit_bytes=VMEM_LIMIT)


def _dot(a, b, mode='nn', precision=None):
    dims = {'nn': (((1,), (0,)), ((), ())), 'nt': (((1,), (1,)), ((), ())), 'tn': (((0,), (0,)), ((), ()))}[mode]
    return lax.dot_general(a, b, dims, precision=precision, preferred_element_type=F32)


def _bdot(a, b, mode='nn'):
    return _dot(a.astype(BF16), b.astype(BF16), mode)


def _hdot(a, b, mode='nn'):
    return _dot(a, b, mode, precision=HIGHEST)


def _sigmoid(x):
    return 1.0 / (1.0 + jnp.exp(-x))


def _silu(x):
    return x * _sigmoid(x)


def _softplus(x):
    return jnp.maximum(x, 0.0) + jnp.log(1.0 + jnp.exp(-jnp.abs(x)))


def _erf(x):
    return lax.erf(x)


def _gelu(x):
    return 0.5 * x * (1.0 + _erf(x * (2.0 ** -0.5)))


def _gelu_grad(x):
    cdf = 0.5 * (1.0 + _erf(x * (2.0 ** -0.5)))
    pdf = jnp.exp(-0.5 * x * x) * (1.0 / math.sqrt(2.0 * math.pi))
    return cdf + x * pdf


MM_K_CAP = 2816


def _k_tile(K, cap):
    for t in range(cap - cap % LANES, 0, -LANES):
        if K % t == 0:
            return t
    return K


def _mm(a, b, mode, out_dtype, name, a_act=None):
    if mode == 'nn':
        (M, K), (_, N) = a.shape, b.shape
    elif mode == 'nt':
        (M, K), (N, _) = a.shape, b.shape
    else:
        (K, M), (_, N) = a.shape, b.shape
    big = (1024, 512, 256, 128)
    narrow = a.dtype.itemsize == 2 and b.dtype.itemsize == 2
    tm, tn, tk = _tile(M, big), _tile(N, big), _k_tile(K, MM_K_CAP if narrow else MM_K_CAP // 2)
    nk = K // tk

    def body(a_ref, b_ref, o_ref, *acc):
        av = a_ref[...]
        if a_act == 'silu':
            av = _silu(av.astype(F32))
        part = _bdot(av, b_ref[...], mode)
        if nk == 1:
            o_ref[...] = part.astype(o_ref.dtype)
            return
        acc_ref, = acc
        k = pl.program_id(2)

        @pl.when(k == 0)
        def _():
            acc_ref[...] = part

        @pl.when(k > 0)
        def _():
            acc_ref[...] += part

        @pl.when(k == nk - 1)
        def _():
            o_ref[...] = acc_ref[...].astype(o_ref.dtype)

    if mode == 'nn':
        a_spec = pl.BlockSpec((tm, tk), lambda i, j, k: (i, k))
        b_spec = pl.BlockSpec((tk, tn), lambda i, j, k: (k, j))
    elif mode == 'nt':
        a_spec = pl.BlockSpec((tm, tk), lambda i, j, k: (i, k))
        b_spec = pl.BlockSpec((tn, tk), lambda i, j, k: (j, k))
    else:
        a_spec = pl.BlockSpec((tk, tm), lambda i, j, k: (k, i))
        b_spec = pl.BlockSpec((tk, tn), lambda i, j, k: (k, j))
    return pl.pallas_call(
        body, name=name, grid=(M // tm, N // tn, nk),
        in_specs=[a_spec, b_spec],
        out_specs=pl.BlockSpec((tm, tn), lambda i, j, k: (i, j)),
        out_shape=jax.ShapeDtypeStruct((M, N), out_dtype),
        scratch_shapes=[pltpu.VMEM((tm, tn), F32)] if nk > 1 else [],
        compiler_params=_cp("parallel", "parallel", "arbitrary"),
    )(a, b)


ROW_TILES = (256, 128, 64, 32, 16, 8)


def _row_spec(tT, D):
    return pl.BlockSpec((tT, D), lambda i: (i, 0))


def _vec_spec(D):
    return pl.BlockSpec((1, D), lambda i: (0, 0))


def _pre_norm(x, g, scale, shift, name):
    T, D = x.shape
    tT = _tile(T, ROW_TILES)

    def body(x_ref, g_ref, sc_ref, sh_ref, h_ref):
        xv = x_ref[...]
        r = lax.rsqrt(jnp.mean(xv * xv, axis=-1, keepdims=True) + EPS)
        h_ref[...] = ((xv * r) * g_ref[...] * (1.0 + sc_ref[...]) + sh_ref[...]).astype(h_ref.dtype)

    return pl.pallas_call(
        body, name=name, grid=(T // tT,),
        in_specs=[_row_spec(tT, D), _vec_spec(D), _vec_spec(D), _vec_spec(D)],
        out_specs=_row_spec(tT, D), out_shape=jax.ShapeDtypeStruct((T, D), BF16),
        compiler_params=_cp("parallel"),
    )(x, g, scale, shift)


def _post_res(x, y, gate, g, name):
    T, D = x.shape
    tT = _tile(T, ROW_TILES)

    def body(x_ref, y_ref, gate_ref, g_ref, o_ref):
        yv = y_ref[...]
        r = lax.rsqrt(jnp.mean(yv * yv, axis=-1, keepdims=True) + EPS)
        o_ref[...] = x_ref[...] + gate_ref[...] * ((yv * r) * g_ref[...])

    return pl.pallas_call(
        body, name=name, grid=(T // tT,),
        in_specs=[_row_spec(tT, D), _row_spec(tT, D), _vec_spec(D), _vec_spec(D)],
        out_specs=_row_spec(tT, D), out_shape=jax.ShapeDtypeStruct((T, D), F32),
        compiler_params=_cp("parallel"),
    )(x, y, gate, g)


def _post_res_bwd(dout, y, gate, g, name):
    T, D = y.shape
    tT = _tile(T, ROW_TILES)

    def body(do_ref, y_ref, gate_ref, g_ref, dy_ref, dgate_ref, dg_ref):
        @pl.when(pl.program_id(0) == 0)
        def _():
            dgate_ref[...] = jnp.zeros_like(dgate_ref)
            dg_ref[...] = jnp.zeros_like(dg_ref)

        yv, dov, gatev, gv = y_ref[...], do_ref[...], gate_ref[...], g_ref[...]
        r = lax.rsqrt(jnp.mean(yv * yv, axis=-1, keepdims=True) + EPS)
        yn = yv * r
        t = dov * yn
        dgate_ref[...] += jnp.sum(t * gv, axis=0, keepdims=True)
        dg_ref[...] += jnp.sum(t * gatev, axis=0, keepdims=True)
        dyn = dov * (gatev * gv)
        dy_ref[...] = (r * (dyn - yn * jnp.mean(dyn * yn, axis=-1, keepdims=True))).astype(dy_ref.dtype)

    return pl.pallas_call(
        body, name=name, grid=(T // tT,),
        in_specs=[_row_spec(tT, D), _row_spec(tT, D), _vec_spec(D), _vec_spec(D)],
        out_specs=[_row_spec(tT, D), _vec_spec(D), _vec_spec(D)],
        out_shape=[jax.ShapeDtypeStruct((T, D), BF16), jax.ShapeDtypeStruct((1, D), F32),
                   jax.ShapeDtypeStruct((1, D), F32)],
        compiler_params=_cp("arbitrary"),
    )(dout, y, gate, g)


def _pre_norm_bwd(dhs, x, g, scale, dres, name):
    T, D = x.shape
    tT = _tile(T, ROW_TILES)
    n = len(dhs)

    def body(*refs):
        dh_refs = refs[:n]
        x_ref, g_ref, sc_ref, dres_ref, dx_ref, dsh_ref, dsc_ref, dg_ref = refs[n:]

        @pl.when(pl.program_id(0) == 0)
        def _():
            dsh_ref[...] = jnp.zeros_like(dsh_ref)
            dsc_ref[...] = jnp.zeros_like(dsc_ref)
            dg_ref[...] = jnp.zeros_like(dg_ref)

        dh = dh_refs[0][...]
        for r_ in dh_refs[1:]:
            dh = dh + r_[...]
        xv, gv, scv = x_ref[...], g_ref[...], sc_ref[...]
        r = lax.rsqrt(jnp.mean(xv * xv, axis=-1, keepdims=True) + EPS)
        xn = xv * r
        t = dh * xn
        dsh_ref[...] += jnp.sum(dh, axis=0, keepdims=True)
        dsc_ref[...] += jnp.sum(t * gv, axis=0, keepdims=True)
        dg_ref[...] += jnp.sum(t * (1.0 + scv), axis=0, keepdims=True)
        dxn = dh * (gv * (1.0 + scv))
        dx_ref[...] = dres_ref[...] + r * (dxn - xn * jnp.mean(dxn * xn, axis=-1, keepdims=True))

    return pl.pallas_call(
        body, name=name, grid=(T // tT,),
        in_specs=[_row_spec(tT, D)] * n + [_row_spec(tT, D), _vec_spec(D), _vec_spec(D), _row_spec(tT, D)],
        out_specs=[_row_spec(tT, D), _vec_spec(D), _vec_spec(D), _vec_spec(D)],
        out_shape=[jax.ShapeDtypeStruct((T, D), F32)] + [jax.ShapeDtypeStruct((1, D), F32)] * 3,
        compiler_params=_cp("arbitrary"),
    )(*dhs, x, g, scale, dres)


def _loss_head(y, target, name):
    T, D = y.shape
    tT = _tile(T, ROW_TILES)

    def body(y_ref, t_ref, dy_ref, l_ref):
        @pl.when(pl.program_id(0) == 0)
        def _():
            l_ref[...] = jnp.zeros_like(l_ref)

        e = y_ref[...] - t_ref[...]
        dy_ref[...] = e * (1.0 / D)
        s = jnp.sum(jnp.mean(e * e, axis=-1, keepdims=True), axis=0, keepdims=True)
        l_ref[...] += 0.5 * s

    return pl.pallas_call(
        body, name=name, grid=(T // tT,),
        in_specs=[_row_spec(tT, D), _row_spec(tT, D)],
        out_specs=[_row_spec(tT, D), pl.BlockSpec((SUBLANES, LANES), lambda i: (0, 0))],
        out_shape=[jax.ShapeDtypeStruct((T, D), F32), jax.ShapeDtypeStruct((SUBLANES, LANES), F32)],
        compiler_params=_cp("arbitrary"),
    )(y, target)


HEAD_ROW_TILES = (1024, 512, 256, 128, 64)


def _hb(tT, off=0):
    return pl.BlockSpec((tT, HD), lambda i, h: (i, off + h))


def _hvec():
    return pl.BlockSpec((1, HD), lambda i, h: (0, 0))


def _headnorm(x, off, H, g, c1, post, out_dtype, name):
    T = x.shape[0]
    tT = _tile(T, HEAD_ROW_TILES)
    has_g = g is not None

    def body(*refs):
        x_ref = refs[0]
        o_ref = refs[-1]
        xv = x_ref[...]
        yv = xv * lax.rsqrt(c1 * jnp.sum(xv * xv, axis=-1, keepdims=True) + EPS)
        if has_g:
            yv = yv * refs[1][...]
        if post != 1.0:
            yv = yv * post
        o_ref[...] = yv.astype(o_ref.dtype)

    return pl.pallas_call(
        body, name=name, grid=(T // tT, H),
        in_specs=[_hb(tT, off)] + ([_hvec()] if has_g else []),
        out_specs=_hb(tT), out_shape=jax.ShapeDtypeStruct((T, H * HD), out_dtype),
        compiler_params=_cp("parallel", "parallel"),
    )(*([x, g] if has_g else [x]))


def _headnorm_bwd(dys, x, off, H, g, c1, post, out_dtype, name):
    T = x.shape[0]
    tT = _tile(T, HEAD_ROW_TILES)
    n = len(dys)
    has_g = g is not None

    def body(*refs):
        dy_refs = refs[:n]
        x_ref = refs[n]
        g_ref = refs[n + 1] if has_g else None
        dx_ref, dg_ref = refs[-2], refs[-1]

        @pl.when((pl.program_id(0) == 0) & (pl.program_id(1) == 0))
        def _():
            dg_ref[...] = jnp.zeros_like(dg_ref)

        dy = dy_refs[0][...].astype(F32)
        for r_ in dy_refs[1:]:
            dy = dy + r_[...].astype(F32)
        if post != 1.0:
            dy = dy * post
        xv = x_ref[...]
        r = lax.rsqrt(c1 * jnp.sum(xv * xv, axis=-1, keepdims=True) + EPS)
        xn = xv * r
        if has_g:
            dg_ref[...] += jnp.sum(dy * xn, axis=0, keepdims=True)
            dy = dy * g_ref[...]
        dx_ref[...] = (r * (dy - xn * (c1 * jnp.sum(dy * xn, axis=-1, keepdims=True)))).astype(dx_ref.dtype)

    dy_specs = [pl.BlockSpec((tT, HD), lambda i, h, st=st, of=of: (i, st * h + of)) for (_, st, of) in dys]
    return pl.pallas_call(
        body, name=name, grid=(T // tT, H),
        in_specs=dy_specs + [_hb(tT, off)] + ([_hvec()] if has_g else []),
        out_specs=[_hb(tT), _hvec()],
        out_shape=[jax.ShapeDtypeStruct((T, H * HD), out_dtype), jax.ShapeDtypeStruct((1, HD), F32)],
        compiler_params=_cp("arbitrary", "arbitrary"),
    )(*[d[0] for d in dys], x, *([g] if has_g else []))


def _fox_gate(ao, proj, og_off, H, name):
    T = ao.shape[0]
    tT = _tile(T, HEAD_ROW_TILES)

    def body(ao_ref, og_ref, o_ref):
        o_ref[...] = (ao_ref[...] * _sigmoid(og_ref[...])).astype(o_ref.dtype)

    return pl.pallas_call(
        body, name=name, grid=(T // tT, H),
        in_specs=[_hb(tT), _hb(tT, og_off)], out_specs=_hb(tT),
        out_shape=jax.ShapeDtypeStruct((T, H * HD), BF16), compiler_params=_cp("parallel", "parallel"),
    )(ao, proj)


def _fox_gate_bwd(dgated, ao, proj, og_off, H, name):
    T = ao.shape[0]
    tT = _tile(T, HEAD_ROW_TILES)

    def body(dg_ref, ao_ref, og_ref, dao_ref, dog_ref, delta_ref):
        dg, aov = dg_ref[...], ao_ref[...]
        sg = _sigmoid(og_ref[...])
        dao = dg * sg
        dao_ref[...] = dao.astype(dao_ref.dtype)
        dog_ref[...] = (dg * aov * sg * (1.0 - sg)).astype(dog_ref.dtype)
        delta_ref[...] = jnp.broadcast_to(jnp.sum(dao * aov, axis=-1, keepdims=True), delta_ref.shape)

    return pl.pallas_call(
        body, name=name, grid=(T // tT, H),
        in_specs=[_hb(tT), _hb(tT), _hb(tT, og_off)], out_specs=[_hb(tT)] * 3,
        out_shape=[jax.ShapeDtypeStruct((T, H * HD), BF16), jax.ShapeDtypeStruct((T, H * HD), BF16),
                   jax.ShapeDtypeStruct((T, H * HD), F32)],
        compiler_params=_cp("parallel", "parallel"),
    )(dgated, ao, proj)


def _gdn_out(o, proj, z_off, Hv, g, name):
    T = o.shape[0]
    tT = _tile(T, HEAD_ROW_TILES)

    def body(o_ref, z_ref, g_ref, y_ref):
        ov, zv = o_ref[...], z_ref[...]
        r = lax.rsqrt(jnp.mean(ov * ov, axis=-1, keepdims=True) + EPS)
        y_ref[...] = (((ov * r) * g_ref[...]) * _silu(zv)).astype(y_ref.dtype)

    return pl.pallas_call(
        body, name=name, grid=(T // tT, Hv),
        in_specs=[_hb(tT), _hb(tT, z_off), _hvec()], out_specs=_hb(tT),
        out_shape=jax.ShapeDtypeStruct((T, Hv * HD), BF16), compiler_params=_cp("parallel", "parallel"),
    )(o, proj, g)


def _gdn_out_bwd(dy, o, proj, z_off, Hv, g, name):
    T = o.shape[0]
    tT = _tile(T, HEAD_ROW_TILES)

    def body(dy_ref, o_ref, z_ref, g_ref, do_ref, dz_ref, dg_ref):
        @pl.when((pl.program_id(0) == 0) & (pl.program_id(1) == 0))
        def _():
            dg_ref[...] = jnp.zeros_like(dg_ref)

        dyv, ov, zv, gv = dy_ref[...], o_ref[...], z_ref[...], g_ref[...]
        r = lax.rsqrt(jnp.mean(ov * ov, axis=-1, keepdims=True) + EPS)
        on = ov * r
        sg = _sigmoid(zv)
        sz = zv * sg
        dz_ref[...] = (dyv * (on * gv) * (sg * (1.0 + zv * (1.0 - sg)))).astype(dz_ref.dtype)
        t = dyv * sz
        dg_ref[...] += jnp.sum(t * on, axis=0, keepdims=True)
        don = t * gv
        do_ref[...] = r * (don - on * jnp.mean(don * on, axis=-1, keepdims=True))

    return pl.pallas_call(
        body, name=name, grid=(T // tT, Hv),
        in_specs=[_hb(tT), _hb(tT), _hb(tT, z_off), _hvec()], out_specs=[_hb(tT), _hb(tT), _hvec()],
        out_shape=[jax.ShapeDtypeStruct((T, Hv * HD), F32), jax.ShapeDtypeStruct((T, Hv * HD), BF16),
                   jax.ShapeDtypeStruct((1, HD), F32)],
        compiler_params=_cp("arbitrary", "arbitrary"),
    )(dy, o, proj, g)


def _lrow(tT):
    return pl.BlockSpec((tT, LANES), lambda i: (i, 0))


def _lvec():
    return pl.BlockSpec((1, LANES), lambda i: (0, 0))


def _logsig(x, b, name):
    T = x.shape[0]
    tT = _tile(T, HEAD_ROW_TILES)

    def body(x_ref, b_ref, o_ref):
        o_ref[...] = -_softplus(-(x_ref[...] + b_ref[...]))

    return pl.pallas_call(body, name=name, grid=(T // tT,), in_specs=[_lrow(tT), _lvec()], out_specs=_lrow(tT),
                          out_shape=jax.ShapeDtypeStruct((T, LANES), F32), compiler_params=_cp("parallel"))(x, b)


def _logsig_bwd(dy, x, b, name):
    T = x.shape[0]
    tT = _tile(T, HEAD_ROW_TILES)

    def body(dy_ref, x_ref, b_ref, dx_ref, db_ref):
        @pl.when(pl.program_id(0) == 0)
        def _():
            db_ref[...] = jnp.zeros_like(db_ref)

        dx = dy_ref[...] * _sigmoid(-(x_ref[...] + b_ref[...]))
        dx_ref[...] = dx.astype(dx_ref.dtype)
        db_ref[...] += jnp.sum(dx, axis=0, keepdims=True)

    return pl.pallas_call(
        body, name=name, grid=(T // tT,), in_specs=[_lrow(tT), _lrow(tT), _lvec()], out_specs=[_lrow(tT), _lvec()],
        out_shape=[jax.ShapeDtypeStruct((T, LANES), BF16), jax.ShapeDtypeStruct((1, LANES), F32)],
        compiler_params=_cp("arbitrary"))(dy, x, b)


def _gdn_gates(ab, alog, dtb, name):
    T = ab.shape[0]
    tT = _tile(T, HEAD_ROW_TILES)

    def body(ab_ref, al_ref, dt_ref, g_ref, be_ref):
        v = ab_ref[...]
        g_ref[...] = -jnp.exp(al_ref[...]) * _softplus(v + dt_ref[...])
        be_ref[...] = _sigmoid(v)

    return pl.pallas_call(
        body, name=name, grid=(T // tT,), in_specs=[_lrow(tT), _lvec(), _lvec()], out_specs=[_lrow(tT)] * 2,
        out_shape=[jax.ShapeDtypeStruct((T, LANES), F32)] * 2, compiler_params=_cp("parallel"))(ab, alog, dtb)


def _gdn_gates_bwd(dg, dbeta, ab, alog, dtb, name):
    T = ab.shape[0]
    tT = _tile(T, HEAD_ROW_TILES)

    def body(dg_ref, dbe_ref, ab_ref, al_ref, dt_ref, dab_ref, dal_ref, ddt_ref):
        @pl.when(pl.program_id(0) == 0)
        def _():
            dal_ref[...] = jnp.zeros_like(dal_ref)
            ddt_ref[...] = jnp.zeros_like(ddt_ref)

        v, dgv = ab_ref[...], dg_ref[...]
        ea = jnp.exp(al_ref[...])
        z = v + dt_ref[...]
        da = dgv * (-ea * _sigmoid(z))
        sb = _sigmoid(v)
        dab_ref[...] = (da + dbe_ref[...] * sb * (1.0 - sb)).astype(dab_ref.dtype)
        dal_ref[...] += jnp.sum(dgv * (-ea * _softplus(z)), axis=0, keepdims=True)
        ddt_ref[...] += jnp.sum(da, axis=0, keepdims=True)

    return pl.pallas_call(
        body, name=name, grid=(T // tT,), in_specs=[_lrow(tT), _lrow(tT), _lrow(tT), _lvec(), _lvec()],
        out_specs=[_lrow(tT), _lvec(), _lvec()],
        out_shape=[jax.ShapeDtypeStruct((T, LANES), BF16), jax.ShapeDtypeStruct((1, LANES), F32),
                   jax.ShapeDtypeStruct((1, LANES), F32)],
        compiler_params=_cp("arbitrary"))(dg, dbeta, ab, alog, dtb)


def _cumsum(x, seg, reverse, name):
    T = x.shape[0]
    tb = _tile(T, (256, 128, 64))
    nb = T // tb
    carry = seg is None

    def body(x_ref, o_ref, c_ref):
        @pl.when(pl.program_id(0) == 0)
        def _():
            c_ref[...] = jnp.zeros_like(c_ref)

        ri = lax.broadcasted_iota(jnp.int32, (tb, tb), 0)
        ci = lax.broadcasted_iota(jnp.int32, (tb, tb), 1)
        keep = (ci >= ri) if reverse else (ci <= ri)
        if seg is not None:
            keep = keep & ((ri // seg) == (ci // seg))
        y = _hdot(keep.astype(F32), x_ref[...])
        if carry:
            y = y + c_ref[...]
            c_ref[...] = y[0:1, :] if reverse else y[tb - 1:tb, :]
        o_ref[...] = y

    imap = (lambda i: (nb - 1 - i, 0)) if reverse else (lambda i: (i, 0))
    return pl.pallas_call(
        body, name=name, grid=(nb,), in_specs=[pl.BlockSpec((tb, LANES), imap)],
        out_specs=pl.BlockSpec((tb, LANES), imap), out_shape=jax.ShapeDtypeStruct((T, LANES), F32),
        scratch_shapes=[pltpu.VMEM((1, LANES), F32)], compiler_params=_cp("arbitrary"))(x)


ATT_TILES = (512, 256, 128)


def _att_scores(q, k, f0, fk):
    return _dot(q, k, 'nt') - (fk - f0)


def _diag_keep(tq):
    return lax.broadcasted_iota(jnp.int32, (tq, tq), 1) <= lax.broadcasted_iota(jnp.int32, (tq, tq), 0)


def _tri_pairs(nq, by_key):
    if by_key:
        pairs = [(qi, ki) for ki in range(nq) for qi in range(ki, nq)]
    else:
        pairs = [(qi, ki) for qi in range(nq) for ki in range(qi + 1)]
    return jnp.asarray([p[0] for p in pairs], jnp.int32), jnp.asarray([p[1] for p in pairs], jnp.int32)


def _att_specs(tq):
    qspec = pl.BlockSpec((tq, HD), lambda h, p, qt, kt: (qt[p], h))
    kspec = pl.BlockSpec((tq, HD), lambda h, p, qt, kt: (kt[p], h))
    f0spec = pl.BlockSpec((None, None, 1, 1), lambda h, p, qt, kt: (h, qt[p], 0, 0))
    fkspec = pl.BlockSpec((None, 1, tq), lambda h, p, qt, kt: (h, 0, kt[p]))
    return qspec, kspec, f0spec, fkspec


def _flash_fwd(qn, kn, vb, f0, fkr, H, name):
    T = qn.shape[0]
    tq = _tile(T, ATT_TILES)
    nq = T // tq
    qt, kt = _tri_pairs(nq, by_key=False)

    def body(qt_ref, kt_ref, q_ref, k_ref, v_ref, f0_ref, fk_ref, o_ref, lse_ref, m_s, l_s, acc_s):
        qi, ki = qt_ref[pl.program_id(1)], kt_ref[pl.program_id(1)]

        @pl.when(ki == 0)
        def _():
            m_s[...] = jnp.full_like(m_s, NEG)
            l_s[...] = jnp.zeros_like(l_s)
            acc_s[...] = jnp.zeros_like(acc_s)

        def step(diagonal):
            s = _att_scores(q_ref[...], k_ref[...], f0_ref[...], fk_ref[...])
            if diagonal:
                s = jnp.where(_diag_keep(tq), s, NEG)
            m_prev = m_s[...]
            m_new = jnp.maximum(m_prev, jnp.max(s, axis=1, keepdims=True))
            alpha = jnp.exp(m_prev - m_new)
            p = jnp.exp(s - m_new[:, :1])
            l_s[...] = alpha * l_s[...] + jnp.sum(p, axis=1, keepdims=True)
            acc_s[...] = acc_s[...] * alpha + _bdot(p, v_ref[...])
            m_s[...] = m_new

        @pl.when(ki < qi)
        def _():
            step(False)

        @pl.when(ki == qi)
        def _():
            step(True)
            o_ref[...] = acc_s[...] / l_s[...]
            lse_ref[...] = m_s[...] + jnp.log(l_s[...])

    qspec, kspec, f0spec, fkspec = _att_specs(tq)
    grid_spec = pltpu.PrefetchScalarGridSpec(
        num_scalar_prefetch=2, grid=(H, qt.shape[0]),
        in_specs=[qspec, kspec, kspec, f0spec, fkspec], out_specs=[qspec, qspec],
        scratch_shapes=[pltpu.VMEM((tq, HD), F32)] * 3)
    return pl.pallas_call(
        body, name=name, grid_spec=grid_spec, out_shape=[jax.ShapeDtypeStruct((T, H * HD), F32)] * 2,
        compiler_params=_cp("parallel", "arbitrary"),
    )(qt, kt, qn, kn, vb, f0, fkr)


def _flash_bwd(qn, kn, vb, f0, fkr, dao, lse, delta, H, name):
    T = qn.shape[0]
    tq = _tile(T, ATT_TILES)
    nq = T // tq
    qt, kt = _tri_pairs(nq, by_key=True)

    def body(qt_ref, kt_ref, q_ref, k_ref, v_ref, f0_ref, fk_ref, do_ref, lse_ref, dl_ref,
             dq_ref, dfq_ref, dk_ref, dv_ref, dfk_ref, dk_s, dv_s, dfk_s):
        qi, ki = qt_ref[pl.program_id(1)], kt_ref[pl.program_id(1)]

        @pl.when(pl.program_id(1) == 0)
        def _():
            dq_ref[...] = jnp.zeros_like(dq_ref)
            dfq_ref[...] = jnp.zeros_like(dfq_ref)

        @pl.when(qi == ki)
        def _():
            dk_s[...] = jnp.zeros_like(dk_s)
            dv_s[...] = jnp.zeros_like(dv_s)
            dfk_s[...] = jnp.zeros_like(dfk_s)

        def step(diagonal):
            s = _att_scores(q_ref[...], k_ref[...], f0_ref[...], fk_ref[...])
            p = jnp.exp(s - lse_ref[...][:, :1])
            if diagonal:
                p = jnp.where(_diag_keep(tq), p, 0.0)
            dp = _dot(do_ref[...], v_ref[...], 'nt')
            ds = p * (dp - dl_ref[...][:, :1])
            dv_s[...] += _bdot(p, do_ref[...], 'tn')
            dk_s[...] += _bdot(ds, q_ref[...], 'tn')
            dfk_s[...] -= jnp.sum(ds, axis=0, keepdims=True)
            rows = pl.ds(pl.multiple_of(qi * tq, tq), tq)
            dq_ref[rows, :] += _bdot(ds, k_ref[...])
            dfq_ref[rows, :] += jnp.broadcast_to(jnp.sum(ds, axis=1, keepdims=True), (tq, HD))

        @pl.when(qi > ki)
        def _():
            step(False)

        @pl.when(qi == ki)
        def _():
            step(True)

        @pl.when(qi == nq - 1)
        def _():
            dk_ref[...] = dk_s[...]
            dv_ref[...] = dv_s[...].astype(dv_ref.dtype)
            dfk_ref[...] = dfk_s[...]

    qspec, kspec, f0spec, fkspec = _att_specs(tq)
    head = pl.BlockSpec((T, HD), lambda h, p, qt, kt: (0, h))
    grid_spec = pltpu.PrefetchScalarGridSpec(
        num_scalar_prefetch=2, grid=(H, qt.shape[0]),
        in_specs=[qspec, kspec, kspec, f0spec, fkspec, qspec, qspec, qspec],
        out_specs=[head, head, kspec, kspec, fkspec],
        scratch_shapes=[pltpu.VMEM((tq, HD), F32), pltpu.VMEM((tq, HD), F32), pltpu.VMEM((1, tq), F32)])
    return pl.pallas_call(
        body, name=name, grid_spec=grid_spec,
        out_shape=[jax.ShapeDtypeStruct((T, H * HD), F32)] * 3
        + [jax.ShapeDtypeStruct((T, H * HD), BF16), jax.ShapeDtypeStruct((H, 1, T), F32)],
        compiler_params=_cp("parallel", "arbitrary"),
    )(qt, kt, qn, kn, vb, f0, fkr, dao, lse, delta)


CONV_TILES = (512, 256, 128, 64)
HALO = SUBLANES


def _dwconv(x, xoff, W, w, b, act, voff, out_dtype, name):
    T = x.shape[0]
    K = w.shape[0]
    tT, tC = _tile(T, CONV_TILES), _tile(W, CONV_TILES)
    xb, hb = xoff // tC, tT // HALO
    glu = act == 'glu'

    def body(*refs):
        if glu:
            x_ref, xp_ref, w_ref, b_ref, v_ref, o_ref, buf = refs
        else:
            x_ref, xp_ref, w_ref, o_ref, buf = refs
        i = pl.program_id(0)
        buf[0:HALO, :] = jnp.where(i > 0, xp_ref[...], 0.0)
        buf[HALO:, :] = x_ref[...]
        conv = w_ref[0:1, :] * buf[pl.ds(HALO - (K - 1), tT), :]
        for k in range(1, K):
            conv = conv + w_ref[k:k + 1, :] * buf[pl.ds(HALO - (K - 1) + k, tT), :]
        if glu:
            o_ref[...] = (_gelu(conv + b_ref[...]) * v_ref[...]).astype(o_ref.dtype)
        else:
            o_ref[...] = _silu(conv).astype(o_ref.dtype)

    cur = pl.BlockSpec((tT, tC), lambda i, j: (i, xb + j))
    prev = pl.BlockSpec((HALO, tC), lambda i, j: (jnp.maximum(i * hb - 1, 0), xb + j))
    wspec = pl.BlockSpec((K, tC), lambda i, j: (0, j))
    in_specs, args = [cur, prev, wspec], [x, x, w]
    if glu:
        vb = voff // tC
        in_specs += [pl.BlockSpec((1, tC), lambda i, j: (0, j)), pl.BlockSpec((tT, tC), lambda i, j: (i, vb + j))]
        args += [b, x]
    return pl.pallas_call(
        body, name=name, grid=(T // tT, W // tC), in_specs=in_specs,
        out_specs=pl.BlockSpec((tT, tC), lambda i, j: (i, j)), out_shape=jax.ShapeDtypeStruct((T, W), out_dtype),
        scratch_shapes=[pltpu.VMEM((tT + HALO, tC), F32)], compiler_params=_cp("parallel", "parallel"),
    )(*args)


def _dwconv_bwd(x, xoff, W, w, woff, b, act, voff, dy, name):
    T = x.shape[0]
    K = w.shape[0]
    tT, tC = _tile(T, CONV_TILES), _tile(W, CONV_TILES)
    xb, wb, hb, nT = xoff // tC, woff // tC, tT // HALO, T // tT
    last_halo = T // HALO - 1
    glu = act == 'glu'

    def body(*refs):
        if glu:
            (x_ref, xp_ref, xn_ref, dy_ref, dyn_ref, w_ref, b_ref, v_ref, vn_ref,
             dx_ref, dv_ref, dw_ref, db_ref, xbuf, dybuf, dbuf, vbuf) = refs
        else:
            x_ref, xp_ref, xn_ref, dy_ref, dyn_ref, w_ref, dx_ref, dw_ref, xbuf, dybuf, dbuf = refs
        i = pl.program_id(1)

        @pl.when(i == 0)
        def _():
            dw_ref[...] = jnp.zeros_like(dw_ref)
            if glu:
                db_ref[...] = jnp.zeros_like(db_ref)

        ext = tT + HALO
        xbuf[0:HALO, :] = jnp.where(i > 0, xp_ref[...], 0.0)
        xbuf[HALO:HALO + tT, :] = x_ref[...]
        xbuf[HALO + tT:, :] = xn_ref[...]
        dybuf[0:tT, :] = dy_ref[...].astype(F32)
        dybuf[tT:, :] = jnp.where(i < nT - 1, dyn_ref[...].astype(F32), 0.0)
        conv = w_ref[0:1, :] * xbuf[pl.ds(HALO - (K - 1), ext), :]
        for k in range(1, K):
            conv = conv + w_ref[k:k + 1, :] * xbuf[pl.ds(HALO - (K - 1) + k, ext), :]
        dyv = dybuf[...]
        if glu:
            vbuf[0:tT, :] = v_ref[...]
            vbuf[tT:, :] = vn_ref[...]
            z = conv + b_ref[...]
            dconv = dyv * vbuf[...] * _gelu_grad(z)
            dv_ref[...] = (dyv[0:tT, :] * _gelu(z[0:tT, :])).astype(dv_ref.dtype)
        else:
            sg = _sigmoid(conv)
            dconv = dyv * (sg * (1.0 + conv * (1.0 - sg)))
        dbuf[...] = dconv
        dx = w_ref[0:1, :] * dbuf[pl.ds(K - 1, tT), :]
        for k in range(1, K):
            dx = dx + w_ref[k:k + 1, :] * dbuf[pl.ds(K - 1 - k, tT), :]
        dx_ref[...] = dx.astype(dx_ref.dtype)
        dc = dconv[0:tT, :]
        for k in range(K):
            dw_ref[k:k + 1, :] += jnp.sum(dc * xbuf[pl.ds(HALO - (K - 1) + k, tT), :], axis=0, keepdims=True)
        if glu:
            db_ref[...] += jnp.sum(dc, axis=0, keepdims=True)

    def cur(off):
        return pl.BlockSpec((tT, tC), lambda j, i: (i, off + j))

    def nxt(off):
        return pl.BlockSpec((HALO, tC), lambda j, i: (jnp.minimum((i + 1) * hb, last_halo), off + j))

    prev = pl.BlockSpec((HALO, tC), lambda j, i: (jnp.maximum(i * hb - 1, 0), xb + j))
    wspec = pl.BlockSpec((K, tC), lambda j, i: (0, wb + j))
    acc_w = pl.BlockSpec((K, tC), lambda j, i: (0, j))
    acc_b = pl.BlockSpec((1, tC), lambda j, i: (0, j))
    in_specs = [cur(xb), prev, nxt(xb), cur(0), nxt(0), wspec]
    args = [x, x, x, dy, dy, w]
    out_specs = [cur(0)]
    out_shape = [jax.ShapeDtypeStruct((T, W), BF16)]
    scratch = [pltpu.VMEM((tT + 2 * HALO, tC), F32), pltpu.VMEM((tT + HALO, tC), F32), pltpu.VMEM((tT + HALO, tC), F32)]
    if glu:
        vb = voff // tC
        in_specs += [pl.BlockSpec((1, tC), lambda j, i: (0, wb + j)), cur(vb), nxt(vb)]
        args += [b, x, x]
        out_specs += [cur(0), acc_w, acc_b]
        out_shape += [jax.ShapeDtypeStruct((T, W), BF16), jax.ShapeDtypeStruct((K, W), F32),
                      jax.ShapeDtypeStruct((1, W), F32)]
        scratch += [pltpu.VMEM((tT + HALO, tC), F32)]
    else:
        out_specs += [acc_w]
        out_shape += [jax.ShapeDtypeStruct((K, W), F32)]
    return pl.pallas_call(
        body, name=name, grid=(W // tC, nT), in_specs=in_specs, out_specs=out_specs, out_shape=out_shape,
        scratch_shapes=scratch, compiler_params=_cp("parallel", "arbitrary"),
    )(*args)


V_PER_K = 2


GDN_PREP_CHUNKS = 4
GDN_SCAN_CHUNKS = 4


def _b3(a, b, mode='nn', precision=None):
    c = {'nn': ((2,), (1,)), 'nt': ((2,), (2,)), 'tn': ((1,), (1,))}[mode]
    return lax.dot_general(a, b, (c, ((0,), (0,))), precision=precision, preferred_element_type=F32)


def _bb3(a, b, mode='nn'):
    return _b3(a.astype(BF16), b.astype(BF16), mode)


def _hb3(a, b, mode='nn'):
    return _b3(a, b, mode, precision=HIGHEST)


def _split_bf16(a):
    hi = a.astype(BF16)
    return hi, (a - hi.astype(F32)).astype(BF16)


def _nb3(a, b, mode='nn'):
    ah, al = _split_bf16(a)
    bh, bl = _split_bf16(b)
    return _b3(ah, bh, mode) + _b3(ah, bl, mode) + _b3(al, bh, mode)


def _to_batch(x, nc):
    C = GDN_CHUNK
    return jnp.concatenate([x[:, j * HD:(j + 1) * HD].reshape(nc, C, HD) for j in range(V_PER_K)], axis=0)


def _from_batch(x, nc):
    C = GDN_CHUNK
    return jnp.concatenate([x[j * nc:(j + 1) * nc].reshape(nc * C, HD) for j in range(V_PER_K)], axis=1)


def _both_heads(x, nc):
    xc = x.reshape(nc, GDN_CHUNK, HD)
    return jnp.concatenate([xc] * V_PER_K, axis=0)


def _gdn_local(q2, k2, gb, bb):
    B, C, _ = k2.shape
    ri = lax.broadcasted_iota(jnp.int32, (B, C, C), 1)
    ci = lax.broadcasted_iota(jnp.int32, (B, C, C), 2)
    lower, strict = ri >= ci, ri > ci
    pick0 = (lax.broadcasted_iota(jnp.int32, (B, C, HD), 2) == 0).astype(F32)
    g_cols = _hb3(pick0, gb, 'nt')
    dm = jnp.exp(jnp.where(lower, gb[:, :, :C] - g_cols, NEG))
    kk = _bb3(k2, k2, 'nt')
    a = jnp.where(strict, kk * dm * bb[:, :, :C], 0.0)
    eg = jnp.exp(gb)
    gl = gb[:, C - 1:C, :]
    return dict(lower=lower, strict=strict, eye=(ri == ci).astype(F32), dm=dm, kk=kk, a=a, eg=eg, gl=gl,
                qd=q2 * eg, kd=k2 * jnp.exp(gl - gb))


def _gdn_specs(T, Hk, voff, nc, rev=False):
    C = GDN_CHUNK
    nb = T // (nc * C)
    vb = voff // (V_PER_K * HD)
    ix = (lambda i: nb - 1 - i) if rev else (lambda i: i)
    kspec = pl.BlockSpec((nc * C, HD), lambda h, i: (ix(i), h))
    pair = pl.BlockSpec((nc * C, V_PER_K * HD), lambda h, i: (ix(i), h))
    vspec = pl.BlockSpec((nc * C, V_PER_K * HD), lambda h, i: (ix(i), vb + h))
    cc = pl.BlockSpec((V_PER_K, nc, C, C), lambda h, i: (h, ix(i), 0, 0))
    state = pl.BlockSpec((V_PER_K, nc, HD, HD), lambda h, i: (h, ix(i), 0, 0))
    scal = pl.BlockSpec((V_PER_K, nc, SUBLANES, HD), lambda h, i: (h, ix(i), 0, 0))
    return nb, kspec, pair, vspec, cc, state, scal


def _gdn_prep(qn, kn, qkvc, voff, gcb, betab, Hk, name):
    T = qn.shape[0]
    C, nc = GDN_CHUNK, GDN_PREP_CHUNKS
    Hv, N = Hk * V_PER_K, T // C
    nb, kspec, pair, vspec, cc, _, _ = _gdn_specs(T, Hk, voff, nc)

    def body(q_ref, k_ref, v_ref, g_ref, b_ref, u_ref, w_ref, tm_ref, qkm_ref):
        q2, k2 = _both_heads(q_ref[...], nc), _both_heads(k_ref[...], nc)
        v2, gb, bb = _to_batch(v_ref[...], nc), _to_batch(g_ref[...], nc), _to_batch(b_ref[...], nc)
        lc = _gdn_local(q2, k2, gb, bb)
        p = -lc['a']
        tm = lc['eye'] + p
        for _ in range(5):
            p = _nb3(p, p)
            tm = tm + _nb3(tm, p)
        u_ref[...] = _from_batch(_nb3(tm, v2 * bb), nc)
        w_ref[...] = _from_batch(_nb3(tm, k2 * (bb * lc['eg'])), nc)
        tm_ref[...] = tm.reshape(V_PER_K, nc, C, C)
        qkm_ref[...] = jnp.where(lc['lower'], _bb3(q2, k2, 'nt') * lc['dm'], 0.0).reshape(V_PER_K, nc, C, C)

    return pl.pallas_call(
        body, name=name, grid=(Hk, nb), in_specs=[kspec, kspec, vspec, pair, pair], out_specs=[pair, pair, cc, cc],
        out_shape=[jax.ShapeDtypeStruct((T, Hv * HD), F32)] * 2 + [jax.ShapeDtypeStruct((Hv, N, C, C), F32)] * 2,
        compiler_params=_cp("parallel", "parallel"),
    )(qn, kn, qkvc, gcb, betab)


def _scan_chunk(q_ref, k_ref, g_ref, rows):
    C = GDN_CHUNK
    gb = jnp.stack([g_ref[rows, j * HD:(j + 1) * HD] for j in range(V_PER_K)])
    gl = gb[:, C - 1:C, :]
    return q_ref[rows, :][None] * jnp.exp(gb), k_ref[rows, :][None] * jnp.exp(gl - gb), jnp.exp(gl)


def _heads(ref, rows):
    return jnp.stack([ref[rows, j * HD:(j + 1) * HD] for j in range(V_PER_K)])


def _put_heads(ref, rows, x):
    for j in range(V_PER_K):
        ref[rows, j * HD:(j + 1) * HD] = x[j]


def _gdn_scan(qn, kn, gcb, u, w, qkm, Hk, name):
    T = qn.shape[0]
    C, ns = GDN_CHUNK, GDN_SCAN_CHUNKS
    Hv, N = Hk * V_PER_K, T // C
    nb, kspec, pair, _, cc, state, _ = _gdn_specs(T, Hk, 0, ns)

    def body(q_ref, k_ref, g_ref, u_ref, w_ref, qkm_ref, o_ref, sp_ref, s_s):
        @pl.when(pl.program_id(1) == 0)
        def _():
            s_s[...] = jnp.zeros_like(s_s)

        s = s_s[...]
        for t in range(ns):
            rows = slice(t * C, (t + 1) * C)
            qd, kd, egl = _scan_chunk(q_ref, k_ref, g_ref, rows)
            sp_ref[:, t] = s
            vn = _heads(u_ref, rows) - _bb3(_heads(w_ref, rows), s)
            _put_heads(o_ref, rows, _bb3(qd, s) + _bb3(qkm_ref[:, t], vn))
            s = s * egl + _bb3(kd, vn, 'tn')
        s_s[...] = s

    return pl.pallas_call(
        body, name=name, grid=(Hk, nb), in_specs=[kspec, kspec, pair, pair, pair, cc], out_specs=[pair, state],
        out_shape=[jax.ShapeDtypeStruct((T, Hv * HD), F32), jax.ShapeDtypeStruct((Hv, N, HD, HD), F32)],
        scratch_shapes=[pltpu.VMEM((V_PER_K, HD, HD), F32)], compiler_params=_cp("parallel", "arbitrary"),
    )(qn, kn, gcb, u, w, qkm)


def _gdn_scan_bwd(qn, kn, gcb, u, w, qkm, sprev, do, Hk, name):
    T = qn.shape[0]
    C, ns = GDN_CHUNK, GDN_SCAN_CHUNKS
    Hv, N = Hk * V_PER_K, T // C
    nb, kspec, pair, _, cc, state, scal = _gdn_specs(T, Hk, 0, ns, rev=True)

    def body(q_ref, k_ref, g_ref, u_ref, w_ref, qkm_ref, sp_ref, do_ref,
             dqd_ref, dkd_ref, du_ref, dw_ref, dqkm_ref, dgl_ref, ds_s):
        @pl.when(pl.program_id(1) == 0)
        def _():
            ds_s[...] = jnp.zeros_like(ds_s)

        lower = lax.broadcasted_iota(jnp.int32, (V_PER_K, C, C), 1) >= lax.broadcasted_iota(jnp.int32, (V_PER_K, C, C), 2)
        ds = ds_s[...]
        for t in reversed(range(ns)):
            rows = slice(t * C, (t + 1) * C)
            qd, kd, egl = _scan_chunk(q_ref, k_ref, g_ref, rows)
            s, w_, qkm_, dov = sp_ref[:, t], _heads(w_ref, rows), qkm_ref[:, t], _heads(do_ref, rows)
            vn = _heads(u_ref, rows) - _bb3(w_, s)
            _put_heads(dqd_ref, rows, _bb3(dov, s, 'nt'))
            dqkm_ref[:, t] = jnp.where(lower, _bb3(dov, vn, 'nt'), 0.0)
            dvn = _bb3(qkm_, dov, 'tn') + _bb3(kd, ds)
            _put_heads(dkd_ref, rows, _bb3(vn, ds, 'nt'))
            dgl = jnp.sum(jnp.sum(ds * s, axis=1, keepdims=True) * egl, axis=2, keepdims=True)
            dgl_ref[:, t] = jnp.broadcast_to(dgl, (V_PER_K, SUBLANES, HD))
            _put_heads(du_ref, rows, dvn)
            _put_heads(dw_ref, rows, -_bb3(dvn, s, 'nt'))
            ds = ds * egl + _bb3(qd, dov, 'tn') - _bb3(w_, dvn, 'tn')
        ds_s[...] = ds

    return pl.pallas_call(
        body, name=name, grid=(Hk, nb), in_specs=[kspec, kspec, pair, pair, pair, cc, state, pair],
        out_specs=[pair] * 4 + [cc, scal],
        out_shape=[jax.ShapeDtypeStruct((T, Hv * HD), F32)] * 4
        + [jax.ShapeDtypeStruct((Hv, N, C, C), F32), jax.ShapeDtypeStruct((Hv, N, SUBLANES, HD), F32)],
        scratch_shapes=[pltpu.VMEM((V_PER_K, HD, HD), F32)], compiler_params=_cp("parallel", "arbitrary"),
    )(qn, kn, gcb, u, w, qkm, sprev, do)


def _gdn_prep_bwd(qn, kn, qkvc, voff, gcb, betab, tm, u, w, qkm, dqd, dkd, du, dw, dqkm, dgl, Hk, name):
    T = qn.shape[0]
    C, nc = GDN_CHUNK, GDN_PREP_CHUNKS
    Hv = Hk * V_PER_K
    B = V_PER_K * nc
    nb, kspec, pair, vspec, cc, _, scal = _gdn_specs(T, Hk, voff, nc)

    def body(q_ref, k_ref, v_ref, g_ref, b_ref, tm_ref, u_ref, w_ref, qkm_ref, dqd_ref, dkd_ref, du_ref, dw_ref,
             dqkm_ref, dgl_ref, dq_ref, dk_ref, dv_ref, dg_ref, dbe_ref):
        q2, k2 = _both_heads(q_ref[...], nc), _both_heads(k_ref[...], nc)
        v2, gb, bb = _to_batch(v_ref[...], nc), _to_batch(g_ref[...], nc), _to_batch(b_ref[...], nc)
        lc = _gdn_local(q2, k2, gb, bb)
        dm, eg, gl = lc['dm'], lc['eg'], lc['gl']
        tm_, qkm_, dqkm_ = (r[...].reshape(B, C, C) for r in (tm_ref, qkm_ref, dqkm_ref))
        u_, w_, dqd_, dkd_, du_, dw_ = (_to_batch(r[...], nc) for r in (u_ref, w_ref, dqd_ref, dkd_ref, du_ref, dw_ref))
        dgl_ = dgl_ref[...].reshape(B, SUBLANES, HD)[:, :1, :1]
        rowsum = lambda x: jnp.sum(x, axis=-1, keepdims=True)
        dbv = _nb3(tm_, du_, 'tn')
        dbk = _nb3(tm_, dw_, 'tn')
        da = jnp.where(lc['strict'], -(_bb3(dbv, u_, 'nt') + _bb3(dbk, w_, 'nt')), 0.0)
        rk = rowsum(dbk * k2)
        dbeta = rowsum(dbv * v2) + rk * eg[:, :, :1] + rowsum(da * lc['kk'] * dm)
        dkk = da * dm * bb[:, :, :C]
        dqkr = dqkm_ * dm
        dk = dbk * (bb * eg) + _bb3(dkk, k2) + _bb3(dkk, k2, 'tn') + _bb3(dqkr, q2, 'tn') + dkd_ * jnp.exp(gl - gb)
        dq = _bb3(dqkr, k2) + dqd_ * eg
        de = da * lc['a'] + dqkm_ * qkm_
        sk = rowsum(dkd_ * lc['kd'])
        dg = rk * (bb[:, :, :1] * eg[:, :, :1]) + rowsum(de) - _hb3(de, jnp.ones((B, C, HD), F32), 'tn')[:, :, :1] \
            + rowsum(dqd_ * lc['qd']) - sk
        last = (lax.broadcasted_iota(jnp.int32, (B, C, HD), 1) == C - 1).astype(F32)
        dgb = jnp.broadcast_to(dg, (B, C, HD)) + last * (dgl_ + jnp.sum(sk, axis=1, keepdims=True))
        suffix = (lax.broadcasted_iota(jnp.int32, (B, C, C), 2) >= lax.broadcasted_iota(jnp.int32, (B, C, C), 1)).astype(F32)
        dq_ref[...] = _from_batch(dq, nc)
        dk_ref[...] = _from_batch(dk, nc)
        dv_ref[...] = _from_batch(dbv * bb, nc)
        dg_ref[...] = _from_batch(_hb3(suffix, dgb), nc)
        dbe_ref[...] = _from_batch(jnp.broadcast_to(dbeta, (B, C, HD)), nc)

    return pl.pallas_call(
        body, name=name, grid=(Hk, nb),
        in_specs=[kspec, kspec, vspec, pair, pair, cc, pair, pair, cc, pair, pair, pair, pair, cc, scal],
        out_specs=[pair] * 5, out_shape=[jax.ShapeDtypeStruct((T, Hv * HD), F32)] * 5,
        compiler_params=_cp("parallel", "parallel"),
    )(qn, kn, qkvc, gcb, betab, tm, u, w, qkm, dqd, dkd, du, dw, dqkm, dgl)


def _adamw_math(w, g, m, v):
    m = ADAM_B1 * m + (1.0 - ADAM_B1) * g
    v = ADAM_B2 * v + (1.0 - ADAM_B2) * jnp.square(g)
    m_hat = m / (1.0 - ADAM_B1 ** ADAM_STEP)
    v_hat = v / (1.0 - ADAM_B2 ** ADAM_STEP)
    delta = -ADAM_LR * (m_hat / (jnp.sqrt(v_hat) + ADAM_EPS) + ADAM_WD * w)
    return delta, m, v


STREAM_BLOCK_BYTES = 1 << 20


def _stream_rows(R, C, mult=SUBLANES):
    for tr in (512, 256, 128, 64, 32, 16, 8):
        if R % tr == 0 and tr % mult == 0 and tr * C * 4 <= STREAM_BLOCK_BYTES:
            return tr
    return R


def _adamw(w, m, v, gs, name):
    R, C = w.shape
    tr = _stream_rows(R, C)
    n = len(gs)

    def body(*refs):
        w_ref, m_ref, v_ref = refs[:3]
        g_refs = refs[3:3 + n]
        g_out, d_out, m_out, v_out = refs[3 + n:]
        g = g_refs[0][...]
        for r_ in g_refs[1:]:
            g = g + r_[...]
        g_out[...] = g
        d_out[...], m_out[...], v_out[...] = _adamw_math(w_ref[...], g, m_ref[...], v_ref[...])

    spec = pl.BlockSpec((tr, C), lambda i: (i, 0))
    return pl.pallas_call(
        body, name=name, grid=(R // tr,), in_specs=[spec] * (3 + n), out_specs=[spec] * 4,
        out_shape=[jax.ShapeDtypeStruct((R, C), F32)] * 4, compiler_params=_cp("parallel"),
    )(w, m, v, *gs)


def _sum_devices(g8, name):
    _, M, C = g8.shape
    tr = _tile(M, (512, 256, 128, 64, 32, 16, 8))

    def body(g_ref, o_ref):
        acc = g_ref[0]
        for d in range(1, N_DEV):
            acc = acc + g_ref[d]
        o_ref[...] = acc

    return pl.pallas_call(
        body, name=name, grid=(M // tr,), in_specs=[pl.BlockSpec((N_DEV, tr, C), lambda i: (0, i, 0))],
        out_specs=pl.BlockSpec((tr, C), lambda i: (i, 0)), out_shape=jax.ShapeDtypeStruct((M, C), F32),
        compiler_params=_cp("parallel"),
    )(g8)


def _ada_w_update(c_all, dm, w, m, v, name):
    n_mod, D, Ns = w.shape
    tr = _tile(D, (256, 128))

    def body(c_ref, dm_ref, w_ref, m_ref, v_ref, g_out, d_out, m_out, v_out):
        g = _hdot(_silu(c_ref[...]), dm_ref[...], 'tn')
        g_out[...] = g
        d_out[...], m_out[...], v_out[...] = _adamw_math(w_ref[...], g, m_ref[...], v_ref[...])

    wspec = pl.BlockSpec((None, tr, Ns), lambda i, r: (i, r, 0))
    return pl.pallas_call(
        body, name=name, grid=(n_mod, D // tr),
        in_specs=[pl.BlockSpec((N_DEV, tr), lambda i, r: (0, r)), pl.BlockSpec((None, N_DEV, Ns), lambda i, r: (i, 0, 0)),
                  wspec, wspec, wspec],
        out_specs=[wspec] * 4, out_shape=[jax.ShapeDtypeStruct((n_mod, D, Ns), F32)] * 4,
        compiler_params=_cp("parallel", "parallel"),
    )(c_all, dm, w, m, v)


def _place():
    return lax.axis_index("x"), lax.axis_index("y"), lax.axis_index("c")


def _allgather8(x_shard, name):
    m_per, n = x_shard.shape

    def body(x_ref, out_ref, send_sems, recv_sems, local_sem):
        x, y, c = _place()
        me, sibling = (x, y, c), (x, y, 1 - c)
        chips = [(1 - x, y), (x, 1 - y), (1 - x, 1 - y)]

        def rows(px, py, pc):
            return out_ref.at[pl.ds((4 * px + 2 * py + pc) * m_per, m_per), :]

        def copy(k, block, to, src=None):
            return pltpu.make_async_remote_copy(
                src_ref=rows(*block) if src is None else src, dst_ref=rows(*block),
                send_sem=send_sems.at[k], recv_sem=recv_sems.at[k], device_id=to, device_id_type=MESH)

        mine = pltpu.make_async_copy(x_ref, rows(*me), local_sem)
        mine.start()
        first = [copy(0, me, sibling, src=x_ref)]
        first += [copy(1 + j, me, (*chip, c), src=x_ref) for j, chip in enumerate(chips)]
        for cp in first:
            cp.start()
        passed = [copy(4 + j, (*chip, c), sibling) for j, chip in enumerate(chips)]
        for j, chip in enumerate(chips):
            copy(1 + j, (*chip, c), me).wait_recv()
            passed[j].start()
        copy(0, sibling, me).wait_recv()
        for j, chip in enumerate(chips):
            copy(4 + j, (*chip, 1 - c), me).wait_recv()
        for cp in first + passed:
            cp.wait_send()
        mine.wait()

    return pl.pallas_call(
        body, name=name, out_shape=jax.ShapeDtypeStruct((N_DEV * m_per, n), x_shard.dtype),
        in_specs=[pl.BlockSpec(memory_space=pltpu.VMEM)], out_specs=pl.BlockSpec(memory_space=pltpu.VMEM),
        scratch_shapes=[pltpu.SemaphoreType.DMA((7,)), pltpu.SemaphoreType.DMA((7,)), pltpu.SemaphoreType.DMA],
        compiler_params=pltpu.CompilerParams(vmem_limit_bytes=VMEM_LIMIT),
    )(x_shard)


HBM_SPEC = pl.BlockSpec(memory_space=pltpu.HBM)


def _gather_weights(w_flat, name):
    R, C = w_flat.shape
    half = R // 2

    def body(w_ref, out_ref, send_sems, recv_sems):
        x, y, c = _place()
        sibling = (x, y, 1 - c)
        chips = [(1 - x, y), (x, 1 - y), (1 - x, 1 - y)]

        def part(cx, cy, hc):
            return out_ref.at[2 * cx + cy, pl.ds(hc * half, half), :]

        def copy(k, block, to, src=None):
            return pltpu.make_async_remote_copy(
                src_ref=part(*block) if src is None else src, dst_ref=part(*block),
                send_sem=send_sems.at[k], recv_sem=recv_sems.at[k], device_id=to, device_id_type=MESH)

        first = [copy(j, (x, y, c), (*chip, c), src=w_ref.at[pl.ds(c * half, half), :]) for j, chip in enumerate(chips)]
        for cp in first:
            cp.start()
        passed = [copy(3 + j, (*chip, c), sibling) for j, chip in enumerate(chips)]
        for j, chip in enumerate(chips):
            copy(j, (*chip, c), (x, y, c)).wait_recv()
            passed[j].start()
        for j, chip in enumerate(chips):
            copy(3 + j, (*chip, 1 - c), (x, y, c)).wait_recv()
        for cp in first + passed:
            cp.wait_send()

    return pl.pallas_call(
        body, name=name, out_shape=jax.ShapeDtypeStruct((N_CHIPS, R, C), w_flat.dtype),
        in_specs=[HBM_SPEC], out_specs=HBM_SPEC,
        scratch_shapes=[pltpu.SemaphoreType.DMA((6,)), pltpu.SemaphoreType.DMA((6,))],
    )(w_flat)


def _swap_halves(g, name):
    n, R, C = g.shape
    half = R // 2

    def body(g_ref, got_ref, send_sem, recv_sem):
        x, y, c = _place()
        cp = pltpu.make_async_remote_copy(
            src_ref=g_ref.at[:, pl.ds((1 - c) * half, half), :], dst_ref=got_ref,
            send_sem=send_sem, recv_sem=recv_sem, device_id=(x, y, 1 - c), device_id_type=MESH)
        cp.start()
        cp.wait()

    return pl.pallas_call(
        body, name=name, out_shape=jax.ShapeDtypeStruct((n, half, C), g.dtype),
        in_specs=[HBM_SPEC], out_specs=HBM_SPEC,
        scratch_shapes=[pltpu.SemaphoreType.DMA, pltpu.SemaphoreType.DMA],
    )(g)


def _scatter_chips(q, name):
    n, R2, C = q.shape

    def body(q_ref, got_ref, send_sems, recv_sems):
        x, y, c = _place()
        chips = [(1 - x, y), (x, 1 - y), (1 - x, 1 - y)]
        cps = [pltpu.make_async_remote_copy(
            src_ref=q_ref.at[2 * cx + cy], dst_ref=got_ref.at[j], send_sem=send_sems.at[j], recv_sem=recv_sems.at[j],
            device_id=(cx, cy, c), device_id_type=MESH) for j, (cx, cy) in enumerate(chips)]
        for cp in cps:
            cp.start()
        for cp in cps:
            cp.wait()

    return pl.pallas_call(
        body, name=name, out_shape=jax.ShapeDtypeStruct((3, R2, C), q.dtype),
        in_specs=[HBM_SPEC], out_specs=HBM_SPEC,
        scratch_shapes=[pltpu.SemaphoreType.DMA((3,)), pltpu.SemaphoreType.DMA((3,))],
    )(q)


def _join_halves(h, name):
    R, C = h.shape
    R2 = R // 2

    def body(h_ref, out_ref, send_sem, recv_sem):
        x, y, c = _place()
        cp = pltpu.make_async_remote_copy(
            src_ref=h_ref.at[pl.ds(c * R2, R2), :], dst_ref=out_ref.at[pl.ds(c * R2, R2), :],
            send_sem=send_sem, recv_sem=recv_sem, device_id=(x, y, 1 - c), device_id_type=MESH)
        cp.start()
        cp.wait()

    return pl.pallas_call(
        body, name=name, out_shape=jax.ShapeDtypeStruct((R, C), h.dtype),
        in_specs=[HBM_SPEC], out_specs=HBM_SPEC, input_output_aliases={0: 0},
        scratch_shapes=[pltpu.SemaphoreType.DMA, pltpu.SemaphoreType.DMA],
    )(h)


def _add_halves(g, got, c_idx, name):
    n, R, C = g.shape
    half = R // 2
    tr = _stream_rows(half, C, 2 * SUBLANES)
    nb = half // tr

    def body(c_ref, g_ref, got_ref, o_ref):
        o_ref[...] = (g_ref[...] + got_ref[...]).astype(o_ref.dtype)

    grid_spec = pltpu.PrefetchScalarGridSpec(
        num_scalar_prefetch=1, grid=(n, nb),
        in_specs=[pl.BlockSpec((None, tr, C), lambda s, i, c_ref: (s, c_ref[0] * nb + i, 0)),
                  pl.BlockSpec((None, tr, C), lambda s, i, c_ref: (s, i, 0))],
        out_specs=pl.BlockSpec((None, tr, C), lambda s, i, c_ref: (s, i, 0)))
    return pl.pallas_call(
        body, name=name, grid_spec=grid_spec, out_shape=jax.ShapeDtypeStruct((n, half, C), BF16),
        compiler_params=_cp("parallel", "parallel"),
    )(c_idx, g, got)


def _add_chips(q, got, sc_idx, name):
    n, R2, C = q.shape
    tr = _stream_rows(R2, C, 2 * SUBLANES)
    nb = R2 // tr

    def body(s_ref, q_ref, g0_ref, g1_ref, g2_ref, o_ref):
        o_ref[...] = ((q_ref[...].astype(F32) + g0_ref[...].astype(F32)) + g1_ref[...].astype(F32)) \
            + g2_ref[...].astype(F32)

    def got_spec(j):
        return pl.BlockSpec((None, tr, C), lambda i, s_ref: (j, i, 0))

    grid_spec = pltpu.PrefetchScalarGridSpec(
        num_scalar_prefetch=1, grid=(nb,),
        in_specs=[pl.BlockSpec((None, tr, C), lambda i, s_ref: (s_ref[0], i, 0)), got_spec(0), got_spec(1), got_spec(2)],
        out_specs=pl.BlockSpec((tr, C), lambda i, s_ref: (s_ref[1] * nb + i, 0)))
    return pl.pallas_call(
        body, name=name, grid_spec=grid_spec, out_shape=jax.ShapeDtypeStruct((2 * R2, C), F32),
        compiler_params=_cp("parallel"),
    )(sc_idx, q, got, got, got)


def _pack_lanes(arrs):
    rows = []
    for a in arrs:
        f = a.reshape(-1)
        n = -(-f.shape[0] // LANES) * LANES
        rows.append(jnp.pad(f, (0, n - f.shape[0])).reshape(-1, LANES))
    out = jnp.concatenate(rows, axis=0)
    pad = -out.shape[0] % SUBLANES
    return jnp.pad(out, ((0, pad), (0, 0)))


def _unpack_lanes(packed, shapes):
    out, r = [], 0
    for shp in shapes:
        n = math.prod(shp)
        nr = -(-n // LANES)
        out.append(packed[r:r + nr].reshape(-1)[:n].reshape(shp))
        r += nr
    return out


def _shards_to_full(sh, axis):
    return jnp.concatenate([sh[i] for i in range(N_CHIPS)], axis=axis)


def _full_to_shards(full, axis):
    return jnp.stack(jnp.split(full, N_CHIPS, axis=axis), axis=0)


def _pad_cols(a, n):
    return jnp.pad(a, ((0, 0), (0, n - a.shape[1])))


def _lane_bcast(a):
    return jnp.repeat(a, HD, axis=1)


def _split_mod(mod):
    D = mod.shape[0] // 3
    return mod[None, :D], mod[None, D:2 * D], mod[None, 2 * D:]


def _fox_fwd(h, w, tag):
    T, D = h.shape
    H = D // HD
    proj = _mm(h, w['cat'], 'nn', F32, tag + '_proj')
    flog = _mm(h, w['f'], 'nn', F32, tag + '_flog')
    qn = _headnorm(proj, 0, H, w['q_norm'], 1.0 / HD, HD ** -0.5, BF16, tag + '_qnorm')
    kn = _headnorm(proj, H, H, w['k_norm'], 1.0 / HD, 1.0, BF16, tag + '_knorm')
    vb = proj[:, 2 * D:3 * D].astype(BF16)
    fcum = _cumsum(_logsig(flog, w['f_bias'], tag + '_logf'), None, False, tag + '_fcum')
    tq = _tile(T, ATT_TILES)
    f0 = fcum[::tq, :H].T.reshape(H, T // tq, 1, 1)
    fkr = fcum[:, :H].T.reshape(H, 1, T)
    ao, lse = _flash_fwd(qn, kn, vb, f0, fkr, H, tag + '_att')
    gated = _fox_gate(ao, proj, 3 * H, H, tag + '_ogate')
    y = _mm(gated, w['o'], 'nn', F32, tag + '_out')
    return y, dict(h=h, proj=proj, flog=flog, qn=qn, kn=kn, vb=vb, f0=f0, fkr=fkr, ao=ao, lse=lse, gated=gated)


def _fox_bwd(dy, w, sv, tag):
    h, proj = sv['h'], sv['proj']
    T, D = h.shape
    H = D // HD
    g = {}
    g['o'] = _mm(sv['gated'], dy, 'tn', F32, tag + '_dwo')
    dgated = _mm(dy, w['o'], 'nt', F32, tag + '_dgated')
    dao, dog, delta = _fox_gate_bwd(dgated, sv['ao'], proj, 3 * H, H, tag + '_ogate_bwd')
    dq, dfq, dk, dv, dfk = _flash_bwd(sv['qn'], sv['kn'], sv['vb'], sv['f0'], sv['fkr'], dao, sv['lse'], delta, H,
                                      tag + '_att_bwd')
    dfcum = _pad_cols(dfq[:, ::HD] + dfk.reshape(H, T).T, LANES)
    dlogf = _cumsum(dfcum, None, True, tag + '_fcum_bwd')
    dflog, g['f_bias'] = _logsig_bwd(dlogf, sv['flog'], w['f_bias'], tag + '_logf_bwd')
    dqr, g['q_norm'] = _headnorm_bwd([(dq, 1, 0)], proj, 0, H, w['q_norm'], 1.0 / HD, HD ** -0.5, BF16,
                                     tag + '_qnorm_bwd')
    dkr, g['k_norm'] = _headnorm_bwd([(dk, 1, 0)], proj, H, H, w['k_norm'], 1.0 / HD, 1.0, BF16, tag + '_knorm_bwd')
    dproj = jnp.concatenate([dqr, dkr, dv, dog], axis=1)
    g['cat'] = _mm(h, dproj, 'tn', F32, tag + '_dwcat')
    g['f'] = _mm(h, dflog, 'tn', F32, tag + '_dwf')
    dh = [_mm(dproj, w['cat'], 'nt', F32, tag + '_dh'), _mm(dflog, w['f'], 'nt', F32, tag + '_dh_f')]
    return dh, g


def _gdn_fwd(h, w, tag):
    T, D = h.shape
    Hk = D // HD
    Hv = V_PER_K * Hk
    proj = _mm(h, w['cat'], 'nn', F32, tag + '_proj')
    ab = _mm(h, w['ab'], 'nn', F32, tag + '_ab')
    qkvc = _dwconv(proj, 0, 4 * D, w['conv'], None, 'silu', 0, F32, tag + '_conv')
    qn = _headnorm(qkvc, 0, Hk, None, 1.0, HD ** -0.5, F32, tag + '_qnorm')
    kn = _headnorm(qkvc, Hk, Hk, None, 1.0, 1.0, F32, tag + '_knorm')
    graw, beta = _gdn_gates(ab, w['a_log'], w['dt_bias'], tag + '_gates')
    gc = _cumsum(graw, GDN_CHUNK, False, tag + '_gcum')
    gcb = _lane_bcast(gc[:, :Hv])
    betab = _lane_bcast(beta[:, Hv:2 * Hv])
    u, wk, tm, qkm = _gdn_prep(qn, kn, qkvc, 2 * D, gcb, betab, Hk, tag + '_prep')
    o, sprev = _gdn_scan(qn, kn, gcb, u, wk, qkm, Hk, tag + '_scan')
    go = _gdn_out(o, proj, 4 * Hk, Hv, w['out_norm'], tag + '_onorm')
    y = _mm(go, w['o'], 'nn', F32, tag + '_out')
    return y, dict(h=h, proj=proj, ab=ab, qkvc=qkvc, qn=qn, kn=kn, gcb=gcb, betab=betab, o=o, sprev=sprev, go=go,
                   u=u, wk=wk, tm=tm, qkm=qkm)


def _gdn_bwd(dy, w, sv, tag):
    h, proj, qkvc = sv['h'], sv['proj'], sv['qkvc']
    T, D = h.shape
    Hk = D // HD
    Hv = V_PER_K * Hk
    g = {}
    g['o'] = _mm(sv['go'], dy, 'tn', F32, tag + '_dwo')
    dgo = _mm(dy, w['o'], 'nt', F32, tag + '_dgo')
    do, dz, g['out_norm'] = _gdn_out_bwd(dgo, sv['o'], proj, 4 * Hk, Hv, w['out_norm'], tag + '_onorm_bwd')
    local = (sv['u'], sv['wk'], sv['qkm'])
    dqd, dkd, du, dwk, dqkm, dgl = _gdn_scan_bwd(sv['qn'], sv['kn'], sv['gcb'], *local, sv['sprev'], do, Hk,
                                                 tag + '_scan_bwd')
    dqp, dkp, dv, dgb, dbetab = _gdn_prep_bwd(sv['qn'], sv['kn'], qkvc, 2 * D, sv['gcb'], sv['betab'], sv['tm'], *local,
                                              dqd, dkd, du, dwk, dqkm, dgl, Hk, tag + '_prep_bwd')
    pairs = lambda a: [(a, V_PER_K, j) for j in range(V_PER_K)]
    dqc, _ = _headnorm_bwd(pairs(dqp), qkvc, 0, Hk, None, 1.0, HD ** -0.5, F32, tag + '_qnorm_bwd')
    dkc, _ = _headnorm_bwd(pairs(dkp), qkvc, Hk, Hk, None, 1.0, 1.0, F32, tag + '_knorm_bwd')
    zeros = jnp.zeros((T, Hv), F32)
    dg_pad = _pad_cols(dgb[:, ::HD], LANES)
    dbeta_pad = _pad_cols(jnp.concatenate([zeros, dbetab[:, ::HD]], axis=1), LANES)
    dab, g['a_log'], g['dt_bias'] = _gdn_gates_bwd(dg_pad, dbeta_pad, sv['ab'], w['a_log'], w['dt_bias'], tag + '_gates_bwd')
    dpq, dwq = _dwconv_bwd(proj, 0, D, w['conv'], 0, None, 'silu', 0, dqc, tag + '_conv_bwd_q')
    dpk, dwk = _dwconv_bwd(proj, D, D, w['conv'], D, None, 'silu', 0, dkc, tag + '_conv_bwd_k')
    dpv, dwv = _dwconv_bwd(proj, 2 * D, 2 * D, w['conv'], 2 * D, None, 'silu', 0, dv, tag + '_conv_bwd_v')
    g['conv'] = jnp.concatenate([dwq, dwk, dwv], axis=1)
    dproj = jnp.concatenate([dpq, dpk, dpv, dz], axis=1)
    g['cat'] = _mm(h, dproj, 'tn', F32, tag + '_dwcat')
    g['ab'] = _mm(h, dab, 'tn', F32, tag + '_dwab')
    dh = [_mm(dproj, w['cat'], 'nt', F32, tag + '_dh'), _mm(dab, w['ab'], 'nt', F32, tag + '_dh_ab')]
    return dh, g


def _ffn_fwd(h, w, tag):
    dff = w['down'].shape[0]
    up = _mm(h, w['up'], 'nn', F32, tag + '_up')
    act = _dwconv(up, 0, dff, w['conv'], w['conv_b'], 'glu', dff, BF16, tag + '_conv')
    y = _mm(act, w['down'], 'nn', F32, tag + '_down')
    return y, dict(h=h, up=up, act=act)


def _ffn_bwd(dy, w, sv, tag):
    h, up = sv['h'], sv['up']
    dff = w['down'].shape[0]
    g = {}
    g['down'] = _mm(sv['act'], dy, 'tn', F32, tag + '_dwdown')
    dact = _mm(dy, w['down'], 'nt', F32, tag + '_dact')
    dgate, dval, g['conv'], g['conv_b'] = _dwconv_bwd(up, 0, dff, w['conv'], 0, w['conv_b'], 'glu', dff, dact,
                                                      tag + '_conv_bwd')
    dup = jnp.concatenate([dgate, dval], axis=1)
    g['up'] = _mm(h, dup, 'tn', F32, tag + '_dwup')
    dh = [_mm(dup, w['up'], 'nt', F32, tag + '_dh')]
    return dh, g


def _local_step(x, target, mods, norm_g, wf, wg, wffn):
    mixers = [(_fox_fwd, _fox_bwd, wf, 'fox'), (_gdn_fwd, _gdn_bwd, wg, 'gdn')]
    tape = []
    for i in range(2):
        for sub in range(2):
            if sub == 0:
                fwd, bwd, w, tag = mixers[i]
            else:
                fwd, bwd, w, tag = _ffn_fwd, _ffn_bwd, wffn[i], 'ffn%d' % i
            shift, scale, gate = _split_mod(mods[i, sub])
            g_pre, g_post = norm_g[i, 2 * sub][None], norm_g[i, 2 * sub + 1][None]
            h = _pre_norm(x, g_pre, scale, shift, tag + '_prenorm')
            y, sv = fwd(h, w, tag)
            x_out = _post_res(x, y, gate, g_post, tag + '_postnorm')
            tape.append((bwd, w, tag, sv, x, y, g_pre, g_post, scale, gate))
            x = x_out
    dx, lsum = _loss_head(x, target, 'loss_head')
    loss = lsum[0, 0]
    dmods = [[None, None], [None, None]]
    dnorm = [[None] * 4, [None] * 4]
    wgrads = {}
    for idx in reversed(range(4)):
        i, sub = divmod(idx, 2)
        bwd, w, tag, sv, x_in, y, g_pre, g_post, scale, gate = tape[idx]
        dy, dgate, dgpost = _post_res_bwd(dx, y, gate, g_post, tag + '_postnorm_bwd')
        dh, wgrads[tag] = bwd(dy, w, sv, tag)
        dx, dshift, dscale, dgpre = _pre_norm_bwd(dh, x_in, g_pre, scale, dx, tag + '_prenorm_bwd')
        dmods[i][sub] = jnp.concatenate([dshift[0], dscale[0], dgate[0]])
        dnorm[i][2 * sub], dnorm[i][2 * sub + 1] = dgpre[0], dgpost[0]
    dmods = jnp.stack([jnp.stack(r) for r in dmods])
    dnorm = jnp.stack([jnp.stack(r) for r in dnorm])
    return loss, dx, dmods, dnorm, wgrads


def _unpack_lanes_dev(packed, shapes):
    n = packed.shape[0]
    out, r = [], 0
    for shp in shapes:
        k = math.prod(shp)
        nr = -(-k // LANES)
        out.append(packed[:, r:r + nr].reshape(n, -1)[:, :k].reshape((n,) + tuple(shp)))
        r += nr
    return out


def _gather_lanes(arrs, name):
    packed = _pack_lanes(arrs)
    got = _allgather8(packed, name).reshape(N_DEV, packed.shape[0], LANES)
    return _unpack_lanes_dev(got, [a.shape for a in arrs]), got


def kernel(x, c, ada_w, ada_b, norm_g, fox_w_in, fox_f_bias, fox_q_norm, fox_k_norm, fox_w_o, gdn_w_in, gdn_conv_w, gdn_a_log, gdn_dt_bias, gdn_out_norm, gdn_w_o, ffn_w_up, ffn_conv_w, ffn_conv_b, ffn_w_down, loss_target, m_ada_w, m_ada_b, m_norm_g, m_fox_w_in, m_fox_f_bias, m_fox_q_norm, m_fox_k_norm, m_fox_w_o, m_gdn_w_in, m_gdn_conv_w, m_gdn_a_log, m_gdn_dt_bias, m_gdn_out_norm, m_gdn_w_o, m_ffn_w_up, m_ffn_conv_w, m_ffn_conv_b, m_ffn_w_down, v_ada_w, v_ada_b, v_norm_g, v_fox_w_in, v_fox_f_bias, v_fox_q_norm, v_fox_k_norm, v_fox_w_o, v_gdn_w_in, v_gdn_conv_w, v_gdn_a_log, v_gdn_dt_bias, v_gdn_out_norm, v_gdn_w_o, v_ffn_w_up, v_ffn_conv_w, v_ffn_conv_b, v_ffn_w_down):
    args = locals()
    w = {n: args[n] for n in WEIGHTS}
    mom = {n: args['m_' + n] for n in WEIGHTS}
    var = {n: args['v_' + n] for n in WEIGHTS}
    _, T, D = x.shape
    H = D // HD
    Hv = V_PER_K * H
    xi, yi, ci = _place()
    s_idx = 2 * xi + yi
    b_idx = 4 * xi + 2 * yi + ci
    sc_arr = jnp.stack([s_idx, ci]).astype(jnp.int32)
    c_arr = jnp.reshape(ci, (1,)).astype(jnp.int32)

    (c_all, ab_all, ng_all, gcw_all, fcw_all), _ = _gather_lanes(
        [jnp.tile(c, (SUBLANES, 1)), ada_b, norm_g, gdn_conv_w, ffn_conv_w], 'gather_small')
    c_all = c_all[:, 0, :]
    chips = lambda a: jnp.concatenate([a[2 * s] for s in range(N_CHIPS)], axis=-1)
    norm_g_full, gdn_conv_full, ffn_conv_full = chips(ng_all), chips(gcw_all), chips(fcw_all)

    Ns = ada_w.shape[-1]
    ada_w4 = ada_w.reshape(4, D, Ns)
    part = jnp.stack([_mm(c_all, ada_w4[i], 'nn', F32, 'ada_proj%d' % i, a_act='silu') for i in range(4)])
    part = part + ada_b.reshape(4, 1, Ns)
    (part_all,), _ = _gather_lanes([part], 'gather_mods')
    mine = lax.dynamic_index_in_dim(part_all[0::2], b_idx, axis=2, keepdims=False)
    mods = mine.transpose(1, 0, 2).reshape(2, 2, N_CHIPS * Ns)

    as2d = lambda a: a.reshape(-1, a.shape[-1])
    full = {}
    for n in BIG:
        own = as2d(w[n]).astype(BF16)
        shards = lax.dynamic_update_index_in_dim(_gather_weights(own, 'gather_' + n), own, s_idx, 0)
        full[n] = _shards_to_full(shards.reshape((N_CHIPS,) + w[n].shape), BIG_SHARD_AXIS[n])
    fw, gw = full['fox_w_in'][0], full['gdn_w_in'][0]
    wf = dict(cat=jnp.concatenate([fw[:, :3 * D], fw[:, 3 * D + H:]], axis=1), f=_pad_cols(fw[:, 3 * D:3 * D + H], LANES),
              f_bias=_pad_cols(fox_f_bias, LANES), q_norm=fox_q_norm, k_norm=fox_k_norm, o=full['fox_w_o'][0])
    wg = dict(cat=gw[:, :6 * D], ab=_pad_cols(gw[:, 6 * D:], LANES), conv=gdn_conv_full[0],
              a_log=_pad_cols(gdn_a_log, LANES), dt_bias=_pad_cols(gdn_dt_bias, LANES), out_norm=gdn_out_norm,
              o=full['gdn_w_o'][0])
    wffn = [dict(up=full['ffn_w_up'][i], conv=ffn_conv_full[i], conv_b=ffn_conv_b[i][None], down=full['ffn_w_down'][i])
            for i in range(2)]

    loss, dx, dmods, dnorm, g = _local_step(x[0], loss_target[0], mods, norm_g_full, wf, wg, wffn)
    loss = lax.psum(loss, ('x', 'y', 'c'))

    gf, gg = g['fox'], g['gdn']
    big_grads = {
        'fox_w_in': jnp.concatenate([gf['cat'][:, :3 * D], gf['f'][:, :H], gf['cat'][:, 3 * D:]], axis=1)[None],
        'fox_w_o': gf['o'][None],
        'gdn_w_in': jnp.concatenate([gg['cat'], gg['ab'][:, :2 * Hv]], axis=1)[None],
        'gdn_w_o': gg['o'][None],
        'ffn_w_up': jnp.stack([g['ffn0']['up'], g['ffn1']['up']]),
        'ffn_w_down': jnp.stack([g['ffn0']['down'], g['ffn1']['down']]),
    }
    big_out = {}
    for n in BIG:
        shards = _full_to_shards(big_grads[n], BIG_SHARD_AXIS[n])
        shards = shards.reshape(N_CHIPS, -1, shards.shape[-1])
        pair_sum = _add_halves(shards, _swap_halves(shards, n + '_to_sibling'), c_arr, n + '_add_sibling')
        half_sum = _add_chips(pair_sum, _scatter_chips(pair_sum, n + '_to_chips'), sc_arr, n + '_add_chips')
        g_shard = _join_halves(half_sum, n + '_join')
        big_out[n] = [o.reshape(w[n].shape)
                      for o in _adamw(as2d(w[n]), as2d(mom[n]), as2d(var[n]), [g_shard], 'adamw_' + n)]

    small_part = [dmods, dnorm, gf['f_bias'][:, :H], gf['q_norm'], gf['k_norm'], gg['conv'][None],
                  gg['a_log'][:, :Hv], gg['dt_bias'][:, :Hv], gg['out_norm'],
                  jnp.stack([g['ffn0']['conv'], g['ffn1']['conv']]),
                  jnp.concatenate([g['ffn0']['conv_b'], g['ffn1']['conv_b']], axis=0)]
    (dmods_all, *_), got = _gather_lanes(small_part, 'gather_small_grads')
    tot = _unpack_lanes(_sum_devices(got, 'sum_small_grads'), [a.shape for a in small_part])
    small_full = dict(zip(SMALL, tot))
    small_g = {n: (lax.dynamic_slice_in_dim(small_full[n], s_idx * w[n].shape[-1], w[n].shape[-1], axis=-1)
                   if n in SMALL_SHARDED else small_full[n]) for n in SMALL}
    packs = lambda d: _pack_lanes([d[n] for n in SMALL])
    small_shapes = [w[n].shape for n in SMALL]
    small_out = [_unpack_lanes(o, small_shapes)
                 for o in _adamw(packs(w), packs(mom), packs(var), [packs(small_g)], 'adamw_small')]

    dm = lax.dynamic_slice_in_dim(dmods_all.reshape(N_DEV, 4, N_CHIPS * Ns), s_idx * Ns, Ns, axis=-1).transpose(1, 0, 2)
    ada_out = [o.reshape(ada_w.shape) for o in
               _ada_w_update(c_all, dm, ada_w4, m_ada_w.reshape(4, D, Ns), v_ada_w.reshape(4, D, Ns), 'adamw_ada_w')]

    outs = []
    for k in range(4):
        by_name = {'ada_w': ada_out[k]}
        by_name.update({n: big_out[n][k] for n in BIG})
        by_name.update(zip(SMALL, small_out[k]))
        outs += [by_name[n] for n in WEIGHTS]
    return (loss, dx[None], *outs)
```

```python
import functools
import math

import jax
import jax.numpy as jnp
from jax import lax
from jax.experimental import pallas as pl
from jax.experimental.pallas import tpu as pltpu

F32 = jnp.float32
BF16 = jnp.bfloat16
EPS = 1e-6
HD = 128
GDN_CHUNK = 64
GDN_CONV = 4
FFN_CONV = 3
LANES = 128
SUBLANES = 8
VMEM_LIMIT = 56 * 1024 * 1024
HIGHEST = lax.Precision.HIGHEST
NEG = -1e30

ADAM_LR = 0.001
ADAM_B1 = 0.9
ADAM_B2 = 0.999
ADAM_EPS = 1e-08
ADAM_WD = 0.01
ADAM_STEP = 10

WEIGHTS = ['ada_w', 'ada_b', 'norm_g', 'fox_w_in', 'fox_f_bias', 'fox_q_norm', 'fox_k_norm', 'fox_w_o',
           'gdn_w_in', 'gdn_conv_w', 'gdn_a_log', 'gdn_dt_bias', 'gdn_out_norm', 'gdn_w_o',
           'ffn_w_up', 'ffn_conv_w', 'ffn_conv_b', 'ffn_w_down']
BIG = ['fox_w_in', 'fox_w_o', 'gdn_w_in', 'gdn_w_o', 'ffn_w_up', 'ffn_w_down']
BIG_SHARD_AXIS = {'fox_w_in': 2, 'fox_w_o': 1, 'gdn_w_in': 2, 'gdn_w_o': 1, 'ffn_w_up': 2, 'ffn_w_down': 1}
SMALL = ['ada_b', 'norm_g', 'fox_f_bias', 'fox_q_norm', 'fox_k_norm', 'gdn_conv_w', 'gdn_a_log',
         'gdn_dt_bias', 'gdn_out_norm', 'ffn_conv_w', 'ffn_conv_b']
SMALL_SHARDED = ['ada_b', 'norm_g', 'gdn_conv_w', 'ffn_conv_w']
N_CHIPS = 4
N_DEV = 8
MESH = pl.DeviceIdType.MESH


def _tile(n, cands):
    for c in cands:
        if n % c == 0:
            return c
    return n


def _cp(*sem):
    return pltpu.CompilerParams(dimension_semantics=sem, vmem_limit_bytes=VMEM_LIMIT)


def _dot(a, b, mode='nn', precision=None):
    dims = {'nn': (((1,), (0,)), ((), ())), 'nt': (((1,), (1,)), ((), ())), 'tn': (((0,), (0,)), ((), ()))}[mode]
    return lax.dot_general(a, b, dims, precision=precision, preferred_element_type=F32)


def _bdot(a, b, mode='nn'):
    return _dot(a.astype(BF16), b.astype(BF16), mode)


def _hdot(a, b, mode='nn'):
    return _dot(a, b, mode, precision=HIGHEST)


def _sigmoid(x):
    return 1.0 / (1.0 + jnp.exp(-x))


def _silu(x):
    return x * _sigmoid(x)


def _softplus(x):
    return jnp.maximum(x, 0.0) + jnp.log(1.0 + jnp.exp(-jnp.abs(x)))


def _erf(x):
    return lax.erf(x)


def _gelu(x):
    return 0.5 * x * (1.0 + _erf(x * (2.0 ** -0.5)))


def _normal_cdf_pdf(x):
    cdf = 0.5 * (1.0 + _erf(x * (2.0 ** -0.5)))
    pdf = jnp.exp(-0.5 * x * x) * (1.0 / math.sqrt(2.0 * math.pi))
    return cdf, pdf


MM_K_CAP = 2816


def _k_tile(K, cap):
    for t in range(cap - cap % LANES, 0, -LANES):
        if K % t == 0:
            return t
    return K


def _mm(a, b, mode, out_dtype, name, a_act=None):
    if mode == 'nn':
        (M, K), (_, N) = a.shape, b.shape
    elif mode == 'nt':
        (M, K), (N, _) = a.shape, b.shape
    else:
        (K, M), (_, N) = a.shape, b.shape
    big = (1024, 512, 256, 128)
    narrow = a.dtype.itemsize == 2 and b.dtype.itemsize == 2
    tm, tn, tk = _tile(M, big), _tile(N, big), _k_tile(K, MM_K_CAP if narrow else MM_K_CAP // 2)
    nk = K // tk

    def body(a_ref, b_ref, o_ref, *acc):
        av = a_ref[...]
        if a_act == 'silu':
            av = _silu(av.astype(F32))
        part = _bdot(av, b_ref[...], mode)
        if nk == 1:
            o_ref[...] = part.astype(o_ref.dtype)
            return
        acc_ref, = acc
        k = pl.program_id(2)

        @pl.when(k == 0)
        def _():
            acc_ref[...] = part

        @pl.when(k > 0)
        def _():
            acc_ref[...] += part

        @pl.when(k == nk - 1)
        def _():
            o_ref[...] = acc_ref[...].astype(o_ref.dtype)

    if mode == 'nn':
        a_spec = pl.BlockSpec((tm, tk), lambda i, j, k: (i, k))
        b_spec = pl.BlockSpec((tk, tn), lambda i, j, k: (k, j))
    elif mode == 'nt':
        a_spec = pl.BlockSpec((tm, tk), lambda i, j, k: (i, k))
        b_spec = pl.BlockSpec((tn, tk), lambda i, j, k: (j, k))
    else:
        a_spec = pl.BlockSpec((tk, tm), lambda i, j, k: (k, i))
        b_spec = pl.BlockSpec((tk, tn), lambda i, j, k: (k, j))
    return pl.pallas_call(
        body, name=name, grid=(M // tm, N // tn, nk),
        in_specs=[a_spec, b_spec],
        out_specs=pl.BlockSpec((tm, tn), lambda i, j, k: (i, j)),
        out_shape=jax.ShapeDtypeStruct((M, N), out_dtype),
        scratch_shapes=[pltpu.VMEM((tm, tn), F32)] if nk > 1 else [],
        compiler_params=_cp("parallel", "parallel", "arbitrary"),
    )(a, b)


ROW_TILES = (256, 128, 64, 32, 16, 8)


def _row_spec(tT, D):
    return pl.BlockSpec((tT, D), lambda i: (i, 0))


def _vec_spec(D):
    return pl.BlockSpec((1, D), lambda i: (0, 0))


def _pre_norm(x, g, scale, shift, name):
    T, D = x.shape
    tT = _tile(T, ROW_TILES)

    def body(x_ref, g_ref, sc_ref, sh_ref, h_ref):
        xv = x_ref[...]
        r = lax.rsqrt(jnp.mean(xv * xv, axis=-1, keepdims=True) + EPS)
        h_ref[...] = ((xv * r) * g_ref[...] * (1.0 + sc_ref[...]) + sh_ref[...]).astype(h_ref.dtype)

    return pl.pallas_call(
        body, name=name, grid=(T // tT,),
        in_specs=[_row_spec(tT, D), _vec_spec(D), _vec_spec(D), _vec_spec(D)],
        out_specs=_row_spec(tT, D), out_shape=jax.ShapeDtypeStruct((T, D), BF16),
        compiler_params=_cp("parallel"),
    )(x, g, scale, shift)


def _post_res(x, y, gate, g, name):
    T, D = x.shape
    tT = _tile(T, ROW_TILES)

    def body(x_ref, y_ref, gate_ref, g_ref, o_ref):
        yv = y_ref[...]
        r = lax.rsqrt(jnp.mean(yv * yv, axis=-1, keepdims=True) + EPS)
        o_ref[...] = x_ref[...] + gate_ref[...] * ((yv * r) * g_ref[...])

    return pl.pallas_call(
        body, name=name, grid=(T // tT,),
        in_specs=[_row_spec(tT, D), _row_spec(tT, D), _vec_spec(D), _vec_spec(D)],
        out_specs=_row_spec(tT, D), out_shape=jax.ShapeDtypeStruct((T, D), F32),
        compiler_params=_cp("parallel"),
    )(x, y, gate, g)


def _post_res_bwd(dout, y, gate, g, name):
    T, D = y.shape
    tT = _tile(T, ROW_TILES)

    def body(do_ref, y_ref, gate_ref, g_ref, dy_ref, dgate_ref, dg_ref):
        @pl.when(pl.program_id(0) == 0)
        def _():
            dgate_ref[...] = jnp.zeros_like(dgate_ref)
            dg_ref[...] = jnp.zeros_like(dg_ref)

        yv, dov, gatev, gv = y_ref[...], do_ref[...], gate_ref[...], g_ref[...]
        r = lax.rsqrt(jnp.mean(yv * yv, axis=-1, keepdims=True) + EPS)
        yn = yv * r
        t = dov * yn
        dgate_ref[...] += jnp.sum(t * gv, axis=0, keepdims=True)
        dg_ref[...] += jnp.sum(t * gatev, axis=0, keepdims=True)
        dyn = dov * (gatev * gv)
        dy_ref[...] = (r * (dyn - yn * jnp.mean(dyn * yn, axis=-1, keepdims=True))).astype(dy_ref.dtype)

    return pl.pallas_call(
        body, name=name, grid=(T // tT,),
        in_specs=[_row_spec(tT, D), _row_spec(tT, D), _vec_spec(D), _vec_spec(D)],
        out_specs=[_row_spec(tT, D), _vec_spec(D), _vec_spec(D)],
        out_shape=[jax.ShapeDtypeStruct((T, D), BF16), jax.ShapeDtypeStruct((1, D), F32),
                   jax.ShapeDtypeStruct((1, D), F32)],
        compiler_params=_cp("arbitrary"),
    )(dout, y, gate, g)


def _pre_norm_bwd(dhs, x, g, scale, dres, name):
    T, D = x.shape
    tT = _tile(T, ROW_TILES)
    n = len(dhs)

    def body(*refs):
        dh_refs = refs[:n]
        x_ref, g_ref, sc_ref, dres_ref, dx_ref, dsh_ref, dsc_ref, dg_ref = refs[n:]

        @pl.when(pl.program_id(0) == 0)
        def _():
            dsh_ref[...] = jnp.zeros_like(dsh_ref)
            dsc_ref[...] = jnp.zeros_like(dsc_ref)
            dg_ref[...] = jnp.zeros_like(dg_ref)

        dh = dh_refs[0][...]
        for r_ in dh_refs[1:]:
            dh = dh + r_[...]
        xv, gv, scv = x_ref[...], g_ref[...], sc_ref[...]
        r = lax.rsqrt(jnp.mean(xv * xv, axis=-1, keepdims=True) + EPS)
        xn = xv * r
        t = dh * xn
        dsh_ref[...] += jnp.sum(dh, axis=0, keepdims=True)
        dsc_ref[...] += jnp.sum(t * gv, axis=0, keepdims=True)
        dg_ref[...] += jnp.sum(t * (1.0 + scv), axis=0, keepdims=True)
        dxn = dh * (gv * (1.0 + scv))
        dx_ref[...] = dres_ref[...] + r * (dxn - xn * jnp.mean(dxn * xn, axis=-1, keepdims=True))

    return pl.pallas_call(
        body, name=name, grid=(T // tT,),
        in_specs=[_row_spec(tT, D)] * n + [_row_spec(tT, D), _vec_spec(D), _vec_spec(D), _row_spec(tT, D)],
        out_specs=[_row_spec(tT, D), _vec_spec(D), _vec_spec(D), _vec_spec(D)],
        out_shape=[jax.ShapeDtypeStruct((T, D), F32)] + [jax.ShapeDtypeStruct((1, D), F32)] * 3,
        compiler_params=_cp("arbitrary"),
    )(*dhs, x, g, scale, dres)


def _loss_head(y, target, name):
    T, D = y.shape
    tT = _tile(T, ROW_TILES)

    def body(y_ref, t_ref, dy_ref, l_ref):
        @pl.when(pl.program_id(0) == 0)
        def _():
            l_ref[...] = jnp.zeros_like(l_ref)

        e = y_ref[...] - t_ref[...]
        dy_ref[...] = e * (1.0 / D)
        s = jnp.sum(jnp.mean(e * e, axis=-1, keepdims=True), axis=0, keepdims=True)
        l_ref[...] += 0.5 * s

    return pl.pallas_call(
        body, name=name, grid=(T // tT,),
        in_specs=[_row_spec(tT, D), _row_spec(tT, D)],
        out_specs=[_row_spec(tT, D), pl.BlockSpec((SUBLANES, LANES), lambda i: (0, 0))],
        out_shape=[jax.ShapeDtypeStruct((T, D), F32), jax.ShapeDtypeStruct((SUBLANES, LANES), F32)],
        compiler_params=_cp("arbitrary"),
    )(y, target)


HEAD_ROW_TILES = (1024, 512, 256, 128, 64)


def _hb(tT, off=0):
    return pl.BlockSpec((tT, HD), lambda i, h: (i, off + h))


def _hvec():
    return pl.BlockSpec((1, HD), lambda i, h: (0, 0))


def _headnorm(x, off, H, g, c1, post, out_dtype, name):
    T = x.shape[0]
    tT = _tile(T, HEAD_ROW_TILES)
    has_g = g is not None

    def body(*refs):
        x_ref = refs[0]
        o_ref = refs[-1]
        xv = x_ref[...]
        yv = xv * lax.rsqrt(c1 * jnp.sum(xv * xv, axis=-1, keepdims=True) + EPS)
        if has_g:
            yv = yv * refs[1][...]
        if post != 1.0:
            yv = yv * post
        o_ref[...] = yv.astype(o_ref.dtype)

    return pl.pallas_call(
        body, name=name, grid=(T // tT, H),
        in_specs=[_hb(tT, off)] + ([_hvec()] if has_g else []),
        out_specs=_hb(tT), out_shape=jax.ShapeDtypeStruct((T, H * HD), out_dtype),
        compiler_params=_cp("parallel", "parallel"),
    )(*([x, g] if has_g else [x]))


def _headnorm_bwd(dys, x, off, H, g, c1, post, out_dtype, name):
    T = x.shape[0]
    tT = _tile(T, HEAD_ROW_TILES)
    n = len(dys)
    has_g = g is not None

    def body(*refs):
        dy_refs = refs[:n]
        x_ref = refs[n]
        g_ref = refs[n + 1] if has_g else None
        dx_ref, dg_ref = refs[-2], refs[-1]

        @pl.when((pl.program_id(0) == 0) & (pl.program_id(1) == 0))
        def _():
            dg_ref[...] = jnp.zeros_like(dg_ref)

        dy = dy_refs[0][...].astype(F32)
        for r_ in dy_refs[1:]:
            dy = dy + r_[...].astype(F32)
        if post != 1.0:
            dy = dy * post
        xv = x_ref[...]
        r = lax.rsqrt(c1 * jnp.sum(xv * xv, axis=-1, keepdims=True) + EPS)
        xn = xv * r
        if has_g:
            dg_ref[...] += jnp.sum(dy * xn, axis=0, keepdims=True)
            dy = dy * g_ref[...]
        dx_ref[...] = (r * (dy - xn * (c1 * jnp.sum(dy * xn, axis=-1, keepdims=True)))).astype(dx_ref.dtype)

    dy_specs = [pl.BlockSpec((tT, HD), lambda i, h, st=st, of=of: (i, st * h + of)) for (_, st, of) in dys]
    return pl.pallas_call(
        body, name=name, grid=(T // tT, H),
        in_specs=dy_specs + [_hb(tT, off)] + ([_hvec()] if has_g else []),
        out_specs=[_hb(tT), _hvec()],
        out_shape=[jax.ShapeDtypeStruct((T, H * HD), out_dtype), jax.ShapeDtypeStruct((1, HD), F32)],
        compiler_params=_cp("arbitrary", "arbitrary"),
    )(*[d[0] for d in dys], x, *([g] if has_g else []))


def _fox_gate(ao, proj, og_off, H, name):
    T = ao.shape[0]
    tT = _tile(T, HEAD_ROW_TILES)

    def body(ao_ref, og_ref, o_ref):
        o_ref[...] = (ao_ref[...] * _sigmoid(og_ref[...])).astype(o_ref.dtype)

    return pl.pallas_call(
        body, name=name, grid=(T // tT, H),
        in_specs=[_hb(tT), _hb(tT, og_off)], out_specs=_hb(tT),
        out_shape=jax.ShapeDtypeStruct((T, H * HD), BF16), compiler_params=_cp("parallel", "parallel"),
    )(ao, proj)


def _fox_gate_bwd(dgated, ao, proj, og_off, H, name):
    T = ao.shape[0]
    tT = _tile(T, HEAD_ROW_TILES)

    def body(dg_ref, ao_ref, og_ref, dao_ref, dog_ref, delta_ref):
        dg, aov = dg_ref[...], ao_ref[...]
        sg = _sigmoid(og_ref[...])
        dao = dg * sg
        dao_ref[...] = dao.astype(dao_ref.dtype)
        dog_ref[...] = (dg * aov * sg * (1.0 - sg)).astype(dog_ref.dtype)
        delta_ref[...] = jnp.broadcast_to(jnp.sum(dao * aov, axis=-1, keepdims=True), delta_ref.shape)

    return pl.pallas_call(
        body, name=name, grid=(T // tT, H),
        in_specs=[_hb(tT), _hb(tT), _hb(tT, og_off)], out_specs=[_hb(tT)] * 3,
        out_shape=[jax.ShapeDtypeStruct((T, H * HD), BF16), jax.ShapeDtypeStruct((T, H * HD), BF16),
                   jax.ShapeDtypeStruct((T, H * HD), F32)],
        compiler_params=_cp("parallel", "parallel"),
    )(dgated, ao, proj)


def _gdn_out(o, proj, z_off, Hv, g, name):
    T = o.shape[0]
    tT = _tile(T, HEAD_ROW_TILES)

    def body(o_ref, z_ref, g_ref, y_ref):
        ov, zv = o_ref[...], z_ref[...]
        r = lax.rsqrt(jnp.mean(ov * ov, axis=-1, keepdims=True) + EPS)
        y_ref[...] = (((ov * r) * g_ref[...]) * _silu(zv)).astype(y_ref.dtype)

    return pl.pallas_call(
        body, name=name, grid=(T // tT, Hv),
        in_specs=[_hb(tT), _hb(tT, z_off), _hvec()], out_specs=_hb(tT),
        out_shape=jax.ShapeDtypeStruct((T, Hv * HD), BF16), compiler_params=_cp("parallel", "parallel"),
    )(o, proj, g)


def _gdn_out_bwd(dy, o, proj, z_off, Hv, g, name):
    T = o.shape[0]
    tT = _tile(T, HEAD_ROW_TILES)

    def body(dy_ref, o_ref, z_ref, g_ref, do_ref, dz_ref, dg_ref):
        @pl.when((pl.program_id(0) == 0) & (pl.program_id(1) == 0))
        def _():
            dg_ref[...] = jnp.zeros_like(dg_ref)

        dyv, ov, zv, gv = dy_ref[...], o_ref[...], z_ref[...], g_ref[...]
        r = lax.rsqrt(jnp.mean(ov * ov, axis=-1, keepdims=True) + EPS)
        on = ov * r
        sg = _sigmoid(zv)
        sz = zv * sg
        dz_ref[...] = (dyv * (on * gv) * (sg * (1.0 + zv * (1.0 - sg)))).astype(dz_ref.dtype)
        t = dyv * sz
        dg_ref[...] += jnp.sum(t * on, axis=0, keepdims=True)
        don = t * gv
        do_ref[...] = r * (don - on * jnp.mean(don * on, axis=-1, keepdims=True))

    return pl.pallas_call(
        body, name=name, grid=(T // tT, Hv),
        in_specs=[_hb(tT), _hb(tT), _hb(tT, z_off), _hvec()], out_specs=[_hb(tT), _hb(tT), _hvec()],
        out_shape=[jax.ShapeDtypeStruct((T, Hv * HD), F32), jax.ShapeDtypeStruct((T, Hv * HD), BF16),
                   jax.ShapeDtypeStruct((1, HD), F32)],
        compiler_params=_cp("arbitrary", "arbitrary"),
    )(dy, o, proj, g)


def _lrow(tT):
    return pl.BlockSpec((tT, LANES), lambda i: (i, 0))


def _lvec():
    return pl.BlockSpec((1, LANES), lambda i: (0, 0))


def _logsig(x, b, name):
    T = x.shape[0]
    tT = _tile(T, HEAD_ROW_TILES)

    def body(x_ref, b_ref, o_ref):
        o_ref[...] = -_softplus(-(x_ref[...] + b_ref[...]))

    return pl.pallas_call(body, name=name, grid=(T // tT,), in_specs=[_lrow(tT), _lvec()], out_specs=_lrow(tT),
                          out_shape=jax.ShapeDtypeStruct((T, LANES), F32), compiler_params=_cp("parallel"))(x, b)


def _logsig_bwd(dy, x, b, name):
    T = x.shape[0]
    tT = _tile(T, HEAD_ROW_TILES)

    def body(dy_ref, x_ref, b_ref, dx_ref, db_ref):
        @pl.when(pl.program_id(0) == 0)
        def _():
            db_ref[...] = jnp.zeros_like(db_ref)

        dx = dy_ref[...] * _sigmoid(-(x_ref[...] + b_ref[...]))
        dx_ref[...] = dx.astype(dx_ref.dtype)
        db_ref[...] += jnp.sum(dx, axis=0, keepdims=True)

    return pl.pallas_call(
        body, name=name, grid=(T // tT,), in_specs=[_lrow(tT), _lrow(tT), _lvec()], out_specs=[_lrow(tT), _lvec()],
        out_shape=[jax.ShapeDtypeStruct((T, LANES), BF16), jax.ShapeDtypeStruct((1, LANES), F32)],
        compiler_params=_cp("arbitrary"))(dy, x, b)


def _gdn_gates(ab, alog, dtb, name):
    T = ab.shape[0]
    tT = _tile(T, HEAD_ROW_TILES)

    def body(ab_ref, al_ref, dt_ref, g_ref, be_ref):
        v = ab_ref[...]
        g_ref[...] = -jnp.exp(al_ref[...]) * _softplus(v + dt_ref[...])
        be_ref[...] = _sigmoid(v)

    return pl.pallas_call(
        body, name=name, grid=(T // tT,), in_specs=[_lrow(tT), _lvec(), _lvec()], out_specs=[_lrow(tT)] * 2,
        out_shape=[jax.ShapeDtypeStruct((T, LANES), F32)] * 2, compiler_params=_cp("parallel"))(ab, alog, dtb)


def _gdn_gates_bwd(dg, dbeta, ab, alog, dtb, name):
    T = ab.shape[0]
    tT = _tile(T, HEAD_ROW_TILES)

    def body(dg_ref, dbe_ref, ab_ref, al_ref, dt_ref, dab_ref, dal_ref, ddt_ref):
        @pl.when(pl.program_id(0) == 0)
        def _():
            dal_ref[...] = jnp.zeros_like(dal_ref)
            ddt_ref[...] = jnp.zeros_like(ddt_ref)

        v, dgv = ab_ref[...], dg_ref[...]
        ea = jnp.exp(al_ref[...])
        z = v + dt_ref[...]
        da = dgv * (-ea * _sigmoid(z))
        sb = _sigmoid(v)
        dab_ref[...] = (da + dbe_ref[...] * sb * (1.0 - sb)).astype(dab_ref.dtype)
        dal_ref[...] += jnp.sum(dgv * (-ea * _softplus(z)), axis=0, keepdims=True)
        ddt_ref[...] += jnp.sum(da, axis=0, keepdims=True)

    return pl.pallas_call(
        body, name=name, grid=(T // tT,), in_specs=[_lrow(tT), _lrow(tT), _lrow(tT), _lvec(), _lvec()],
        out_specs=[_lrow(tT), _lvec(), _lvec()],
        out_shape=[jax.ShapeDtypeStruct((T, LANES), BF16), jax.ShapeDtypeStruct((1, LANES), F32),
                   jax.ShapeDtypeStruct((1, LANES), F32)],
        compiler_params=_cp("arbitrary"))(dg, dbeta, ab, alog, dtb)


def _cumsum(x, seg, reverse, name):
    T = x.shape[0]
    tb = _tile(T, (256, 128, 64))
    nb = T // tb
    carry = seg is None

    def body(x_ref, o_ref, c_ref):
        @pl.when(pl.program_id(0) == 0)
        def _():
            c_ref[...] = jnp.zeros_like(c_ref)

        ri = lax.broadcasted_iota(jnp.int32, (tb, tb), 0)
        ci = lax.broadcasted_iota(jnp.int32, (tb, tb), 1)
        keep = (ci >= ri) if reverse else (ci <= ri)
        if seg is not None:
            keep = keep & ((ri // seg) == (ci // seg))
        y = _hdot(keep.astype(F32), x_ref[...])
        if carry:
            y = y + c_ref[...]
            c_ref[...] = y[0:1, :] if reverse else y[tb - 1:tb, :]
        o_ref[...] = y

    imap = (lambda i: (nb - 1 - i, 0)) if reverse else (lambda i: (i, 0))
    return pl.pallas_call(
        body, name=name, grid=(nb,), in_specs=[pl.BlockSpec((tb, LANES), imap)],
        out_specs=pl.BlockSpec((tb, LANES), imap), out_shape=jax.ShapeDtypeStruct((T, LANES), F32),
        scratch_shapes=[pltpu.VMEM((1, LANES), F32)], compiler_params=_cp("arbitrary"))(x)


ATT_TILES = (1024, 512, 256, 128)


def _att_scores(q, k, f0, fk):
    return _dot(q, k, 'nt') - (fk - f0)


def _diag_keep(tq):
    return lax.broadcasted_iota(jnp.int32, (tq, tq), 1) <= lax.broadcasted_iota(jnp.int32, (tq, tq), 0)


def _tri_pairs(nq, by_key):
    if by_key:
        pairs = [(qi, ki) for ki in range(nq) for qi in range(ki, nq)]
    else:
        pairs = [(qi, ki) for qi in range(nq) for ki in range(qi + 1)]
    return jnp.asarray([p[0] for p in pairs], jnp.int32), jnp.asarray([p[1] for p in pairs], jnp.int32)


def _att_specs(tq):
    qspec = pl.BlockSpec((tq, HD), lambda h, p, qt, kt: (qt[p], h))
    kspec = pl.BlockSpec((tq, HD), lambda h, p, qt, kt: (kt[p], h))
    f0spec = pl.BlockSpec((None, None, 1, 1), lambda h, p, qt, kt: (h, qt[p], 0, 0))
    fkspec = pl.BlockSpec((None, 1, tq), lambda h, p, qt, kt: (h, 0, kt[p]))
    return qspec, kspec, f0spec, fkspec


def _flash_fwd(qn, kn, vb, f0, fkr, H, name, carry=()):
    T = qn.shape[0]
    tq = _tile(T, ATT_TILES)
    nq = T // tq
    qt, kt = _tri_pairs(nq, by_key=False)
    n_pairs, n_carry = qt.shape[0], len(carry)

    def body(qt_ref, kt_ref, q_ref, k_ref, v_ref, f0_ref, fk_ref, *rest):
        w_refs, (o_ref, lse_ref), g_refs = rest[:n_carry], rest[n_carry:n_carry + 2], rest[n_carry + 2:2 * n_carry + 2]
        m_s, l_s, acc_s = rest[2 * n_carry + 2:2 * n_carry + 5]
        jobs = [_gather_protocol(w_refs[i], g_refs[i], *rest[2 * n_carry + 5:], GATHER_SEMS * i) for i in range(n_carry)]
        qi, ki = qt_ref[pl.program_id(1)], kt_ref[pl.program_id(1)]

        if jobs:
            @pl.when((pl.program_id(0) == 0) & (pl.program_id(1) == 0))
            def _():
                for start, _ in jobs:
                    start()

        @pl.when(ki == 0)
        def _():
            m_s[...] = jnp.full_like(m_s, NEG)
            l_s[...] = jnp.zeros_like(l_s)
            acc_s[...] = jnp.zeros_like(acc_s)

        def step(diagonal):
            s = _att_scores(q_ref[...], k_ref[...], f0_ref[...], fk_ref[...])
            if diagonal:
                s = jnp.where(_diag_keep(tq), s, NEG)
            m_prev = m_s[...]
            m_new = jnp.maximum(m_prev, jnp.max(s, axis=1, keepdims=True))
            alpha = jnp.exp(m_prev - m_new)
            p = jnp.exp(s - m_new[:, :1])
            l_s[...] = alpha * l_s[...] + jnp.sum(p, axis=1, keepdims=True)
            acc_s[...] = acc_s[...] * alpha + _bdot(p, v_ref[...])
            m_s[...] = m_new

        @pl.when(ki < qi)
        def _():
            step(False)

        @pl.when(ki == qi)
        def _():
            step(True)
            o_ref[...] = acc_s[...] / l_s[...]
            lse_ref[...] = m_s[...] + jnp.log(l_s[...])

        if jobs:
            @pl.when((pl.program_id(0) == H - 1) & (pl.program_id(1) == n_pairs - 1))
            def _():
                for _, finish in jobs:
                    finish()

    qspec, kspec, f0spec, fkspec = _att_specs(tq)
    sems = [pltpu.SemaphoreType.DMA((GATHER_SEMS * n_carry,))] * 2 if n_carry else []
    grid_spec = pltpu.PrefetchScalarGridSpec(
        num_scalar_prefetch=2, grid=(H, n_pairs),
        in_specs=[qspec, kspec, kspec, f0spec, fkspec] + [HBM_SPEC] * n_carry,
        out_specs=[qspec, qspec] + [HBM_SPEC] * n_carry,
        scratch_shapes=[pltpu.VMEM((tq, HD), F32)] * 3 + sems)
    out = pl.pallas_call(
        body, name=name, grid_spec=grid_spec,
        out_shape=[jax.ShapeDtypeStruct((T, H * HD), F32)] * 2
        + [jax.ShapeDtypeStruct((N_CHIPS,) + a.shape, a.dtype) for a in carry],
        compiler_params=_cp("arbitrary", "arbitrary") if n_carry else _cp("parallel", "arbitrary"),
    )(qt, kt, qn, kn, vb, f0, fkr, *carry)
    return out[0], out[1], list(out[2:])


def _flash_bwd(qn, kn, vb, f0, fkr, dao, lse, delta, H, name):
    T = qn.shape[0]
    tq = _tile(T, ATT_TILES)
    nq = T // tq
    qt, kt = _tri_pairs(nq, by_key=True)

    def body(qt_ref, kt_ref, q_ref, k_ref, v_ref, f0_ref, fk_ref, do_ref, lse_ref, dl_ref,
             dq_ref, dfq_ref, dk_ref, dv_ref, dfk_ref, dk_s, dv_s, dfk_s):
        qi, ki = qt_ref[pl.program_id(1)], kt_ref[pl.program_id(1)]

        @pl.when(pl.program_id(1) == 0)
        def _():
            dq_ref[...] = jnp.zeros_like(dq_ref)
            dfq_ref[...] = jnp.zeros_like(dfq_ref)

        @pl.when(qi == ki)
        def _():
            dk_s[...] = jnp.zeros_like(dk_s)
            dv_s[...] = jnp.zeros_like(dv_s)
            dfk_s[...] = jnp.zeros_like(dfk_s)

        def step(diagonal):
            s = _att_scores(q_ref[...], k_ref[...], f0_ref[...], fk_ref[...])
            p = jnp.exp(s - lse_ref[...][:, :1])
            if diagonal:
                p = jnp.where(_diag_keep(tq), p, 0.0)
            dp = _dot(do_ref[...], v_ref[...], 'nt')
            ds = p * (dp - dl_ref[...][:, :1])
            dv_s[...] += _bdot(p, do_ref[...], 'tn')
            dk_s[...] += _bdot(ds, q_ref[...], 'tn')
            dfk_s[...] -= jnp.sum(ds, axis=0, keepdims=True)
            rows = pl.ds(pl.multiple_of(qi * tq, tq), tq)
            dq_ref[rows, :] += _bdot(ds, k_ref[...])
            dfq_ref[rows, :] += jnp.broadcast_to(jnp.sum(ds, axis=1, keepdims=True), (tq, HD))

        @pl.when(qi > ki)
        def _():
            step(False)

        @pl.when(qi == ki)
        def _():
            step(True)

        @pl.when(qi == nq - 1)
        def _():
            dk_ref[...] = dk_s[...]
            dv_ref[...] = dv_s[...].astype(dv_ref.dtype)
            dfk_ref[...] = dfk_s[...]

    qspec, kspec, f0spec, fkspec = _att_specs(tq)
    head = pl.BlockSpec((T, HD), lambda h, p, qt, kt: (0, h))
    grid_spec = pltpu.PrefetchScalarGridSpec(
        num_scalar_prefetch=2, grid=(H, qt.shape[0]),
        in_specs=[qspec, kspec, kspec, f0spec, fkspec, qspec, qspec, qspec],
        out_specs=[head, head, kspec, kspec, fkspec],
        scratch_shapes=[pltpu.VMEM((tq, HD), F32), pltpu.VMEM((tq, HD), F32), pltpu.VMEM((1, tq), F32)])
    return pl.pallas_call(
        body, name=name, grid_spec=grid_spec,
        out_shape=[jax.ShapeDtypeStruct((T, H * HD), F32)] * 3
        + [jax.ShapeDtypeStruct((T, H * HD), BF16), jax.ShapeDtypeStruct((H, 1, T), F32)],
        compiler_params=_cp("parallel", "arbitrary"),
    )(qt, kt, qn, kn, vb, f0, fkr, dao, lse, delta)


CONV_TILES = (512, 256, 128, 64)
HALO = SUBLANES


def _dwconv(x, xoff, W, w, b, act, voff, out_dtype, name):
    T = x.shape[0]
    K = w.shape[0]
    tT, tC = _tile(T, CONV_TILES), _tile(W, CONV_TILES)
    xb, hb = xoff // tC, tT // HALO
    glu = act == 'glu'

    def body(*refs):
        if glu:
            x_ref, xp_ref, w_ref, b_ref, v_ref, o_ref, buf = refs
        else:
            x_ref, xp_ref, w_ref, o_ref, buf = refs
        i = pl.program_id(0)
        buf[0:HALO, :] = jnp.where(i > 0, xp_ref[...], 0.0)
        buf[HALO:, :] = x_ref[...]
        conv = w_ref[0:1, :] * buf[pl.ds(HALO - (K - 1), tT), :]
        for k in range(1, K):
            conv = conv + w_ref[k:k + 1, :] * buf[pl.ds(HALO - (K - 1) + k, tT), :]
        if glu:
            o_ref[...] = (_gelu(conv + b_ref[...]) * v_ref[...]).astype(o_ref.dtype)
        else:
            o_ref[...] = _silu(conv).astype(o_ref.dtype)

    cur = pl.BlockSpec((tT, tC), lambda i, j: (i, xb + j))
    prev = pl.BlockSpec((HALO, tC), lambda i, j: (jnp.maximum(i * hb - 1, 0), xb + j))
    wspec = pl.BlockSpec((K, tC), lambda i, j: (0, j))
    in_specs, args = [cur, prev, wspec], [x, x, w]
    if glu:
        vb = voff // tC
        in_specs += [pl.BlockSpec((1, tC), lambda i, j: (0, j)), pl.BlockSpec((tT, tC), lambda i, j: (i, vb + j))]
        args += [b, x]
    return pl.pallas_call(
        body, name=name, grid=(T // tT, W // tC), in_specs=in_specs,
        out_specs=pl.BlockSpec((tT, tC), lambda i, j: (i, j)), out_shape=jax.ShapeDtypeStruct((T, W), out_dtype),
        scratch_shapes=[pltpu.VMEM((tT + HALO, tC), F32)], compiler_params=_cp("parallel", "parallel"),
    )(*args)


def _dwconv_bwd(x, xoff, W, w, woff, b, act, voff, dy, name):
    T = x.shape[0]
    K = w.shape[0]
    tT, tC = _tile(T, CONV_TILES), _tile(W, CONV_TILES)
    xb, wb, hb, nT = xoff // tC, woff // tC, tT // HALO, T // tT
    last_halo = T // HALO - 1
    glu = act == 'glu'

    def body(*refs):
        if glu:
            (x_ref, xp_ref, xn_ref, dy_ref, dyn_ref, w_ref, b_ref, v_ref, vn_ref,
             dx_ref, dv_ref, dw_ref, db_ref, xbuf, dybuf, dbuf, vbuf) = refs
        else:
            x_ref, xp_ref, xn_ref, dy_ref, dyn_ref, w_ref, dx_ref, dw_ref, xbuf, dybuf, dbuf = refs
        i = pl.program_id(1)

        @pl.when(i == 0)
        def _():
            dw_ref[...] = jnp.zeros_like(dw_ref)
            if glu:
                db_ref[...] = jnp.zeros_like(db_ref)

        ext = tT + HALO
        xbuf[0:HALO, :] = jnp.where(i > 0, xp_ref[...], 0.0)
        xbuf[HALO:HALO + tT, :] = x_ref[...]
        xbuf[HALO + tT:, :] = xn_ref[...]
        dybuf[0:tT, :] = dy_ref[...].astype(F32)
        dybuf[tT:, :] = jnp.where(i < nT - 1, dyn_ref[...].astype(F32), 0.0)
        conv = w_ref[0:1, :] * xbuf[pl.ds(HALO - (K - 1), ext), :]
        for k in range(1, K):
            conv = conv + w_ref[k:k + 1, :] * xbuf[pl.ds(HALO - (K - 1) + k, ext), :]
        dyv = dybuf[...]
        if glu:
            vbuf[0:tT, :] = v_ref[...]
            vbuf[tT:, :] = vn_ref[...]
            z = conv + b_ref[...]
            cdf, pdf = _normal_cdf_pdf(z)
            dconv = dyv * vbuf[...] * (cdf + z * pdf)
            dv_ref[...] = (dyv[0:tT, :] * (z[0:tT, :] * cdf[0:tT, :])).astype(dv_ref.dtype)
        else:
            sg = _sigmoid(conv)
            dconv = dyv * (sg * (1.0 + conv * (1.0 - sg)))
        dbuf[...] = dconv
        dx = w_ref[0:1, :] * dbuf[pl.ds(K - 1, tT), :]
        for k in range(1, K):
            dx = dx + w_ref[k:k + 1, :] * dbuf[pl.ds(K - 1 - k, tT), :]
        dx_ref[...] = dx.astype(dx_ref.dtype)
        dc = dconv[0:tT, :]
        for k in range(K):
            dw_ref[k:k + 1, :] += jnp.sum(dc * xbuf[pl.ds(HALO - (K - 1) + k, tT), :], axis=0, keepdims=True)
        if glu:
            db_ref[...] += jnp.sum(dc, axis=0, keepdims=True)

    def cur(off):
        return pl.BlockSpec((tT, tC), lambda j, i: (i, off + j))

    def nxt(off):
        return pl.BlockSpec((HALO, tC), lambda j, i: (jnp.minimum((i + 1) * hb, last_halo), off + j))

    prev = pl.BlockSpec((HALO, tC), lambda j, i: (jnp.maximum(i * hb - 1, 0), xb + j))
    wspec = pl.BlockSpec((K, tC), lambda j, i: (0, wb + j))
    acc_w = pl.BlockSpec((K, tC), lambda j, i: (0, j))
    acc_b = pl.BlockSpec((1, tC), lambda j, i: (0, j))
    in_specs = [cur(xb), prev, nxt(xb), cur(0), nxt(0), wspec]
    args = [x, x, x, dy, dy, w]
    out_specs = [cur(0)]
    out_shape = [jax.ShapeDtypeStruct((T, W), BF16)]
    scratch = [pltpu.VMEM((tT + 2 * HALO, tC), F32), pltpu.VMEM((tT + HALO, tC), F32), pltpu.VMEM((tT + HALO, tC), F32)]
    if glu:
        vb = voff // tC
        in_specs += [pl.BlockSpec((1, tC), lambda j, i: (0, wb + j)), cur(vb), nxt(vb)]
        args += [b, x, x]
        out_specs += [cur(0), acc_w, acc_b]
        out_shape += [jax.ShapeDtypeStruct((T, W), BF16), jax.ShapeDtypeStruct((K, W), F32),
                      jax.ShapeDtypeStruct((1, W), F32)]
        scratch += [pltpu.VMEM((tT + HALO, tC), F32)]
    else:
        out_specs += [acc_w]
        out_shape += [jax.ShapeDtypeStruct((K, W), F32)]
    return pl.pallas_call(
        body, name=name, grid=(W // tC, nT), in_specs=in_specs, out_specs=out_specs, out_shape=out_shape,
        scratch_shapes=scratch, compiler_params=_cp("parallel", "arbitrary"),
    )(*args)


V_PER_K = 2


GDN_PREP_CHUNKS = 4
GDN_SCAN_CHUNKS = 4


def _b3(a, b, mode='nn', precision=None):
    c = {'nn': ((2,), (1,)), 'nt': ((2,), (2,)), 'tn': ((1,), (1,))}[mode]
    return lax.dot_general(a, b, (c, ((0,), (0,))), precision=precision, preferred_element_type=F32)


def _bb3(a, b, mode='nn'):
    return _b3(a.astype(BF16), b.astype(BF16), mode)


def _hb3(a, b, mode='nn'):
    return _b3(a, b, mode, precision=HIGHEST)


def _split_bf16(a):
    hi = a.astype(BF16)
    return hi, (a - hi.astype(F32)).astype(BF16)


def _nb3(a, b, mode='nn'):
    ah, al = _split_bf16(a)
    bh, bl = _split_bf16(b)
    return _b3(ah, bh, mode) + _b3(ah, bl, mode) + _b3(al, bh, mode)


def _to_batch(x, nc):
    C = GDN_CHUNK
    return jnp.concatenate([x[:, j * HD:(j + 1) * HD].reshape(nc, C, HD) for j in range(V_PER_K)], axis=0)


def _from_batch(x, nc):
    C = GDN_CHUNK
    return jnp.concatenate([x[j * nc:(j + 1) * nc].reshape(nc * C, HD) for j in range(V_PER_K)], axis=1)


def _both_heads(x, nc):
    xc = x.reshape(nc, GDN_CHUNK, HD)
    return jnp.concatenate([xc] * V_PER_K, axis=0)


def _gdn_local(q2, k2, gb, bb):
    B, C, _ = k2.shape
    ri = lax.broadcasted_iota(jnp.int32, (B, C, C), 1)
    ci = lax.broadcasted_iota(jnp.int32, (B, C, C), 2)
    lower, strict = ri >= ci, ri > ci
    pick0 = (lax.broadcasted_iota(jnp.int32, (B, C, HD), 2) == 0).astype(F32)
    g_cols = _hb3(pick0, gb, 'nt')
    dm = jnp.exp(jnp.where(lower, gb[:, :, :C] - g_cols, NEG))
    kk = _bb3(k2, k2, 'nt')
    a = jnp.where(strict, kk * dm * bb[:, :, :C], 0.0)
    eg = jnp.exp(gb)
    gl = gb[:, C - 1:C, :]
    return dict(lower=lower, strict=strict, eye=(ri == ci).astype(F32), dm=dm, kk=kk, a=a, eg=eg, gl=gl,
                qd=q2 * eg, kd=k2 * jnp.exp(gl - gb))


def _gdn_specs(T, Hk, voff, nc, rev=False):
    C = GDN_CHUNK
    nb = T // (nc * C)
    vb = voff // (V_PER_K * HD)
    ix = (lambda i: nb - 1 - i) if rev else (lambda i: i)
    kspec = pl.BlockSpec((nc * C, HD), lambda h, i: (ix(i), h))
    pair = pl.BlockSpec((nc * C, V_PER_K * HD), lambda h, i: (ix(i), h))
    vspec = pl.BlockSpec((nc * C, V_PER_K * HD), lambda h, i: (ix(i), vb + h))
    cc = pl.BlockSpec((V_PER_K, nc, C, C), lambda h, i: (h, ix(i), 0, 0))
    state = pl.BlockSpec((V_PER_K, nc, HD, HD), lambda h, i: (h, ix(i), 0, 0))
    scal = pl.BlockSpec((V_PER_K, nc, SUBLANES, HD), lambda h, i: (h, ix(i), 0, 0))
    return nb, kspec, pair, vspec, cc, state, scal


def _gdn_prep(qn, kn, qkvc, voff, gcb, betab, Hk, name):
    T = qn.shape[0]
    C, nc = GDN_CHUNK, GDN_PREP_CHUNKS
    Hv, N = Hk * V_PER_K, T // C
    nb, kspec, pair, vspec, cc, _, _ = _gdn_specs(T, Hk, voff, nc)

    def body(q_ref, k_ref, v_ref, g_ref, b_ref, u_ref, w_ref, tm_ref, qkm_ref):
        q2, k2 = _both_heads(q_ref[...], nc), _both_heads(k_ref[...], nc)
        v2, gb, bb = _to_batch(v_ref[...], nc), _to_batch(g_ref[...], nc), _to_batch(b_ref[...], nc)
        lc = _gdn_local(q2, k2, gb, bb)
        p = -lc['a']
        tm = lc['eye'] + p
        for _ in range(5):
            p = _nb3(p, p)
            tm = tm + _nb3(tm, p)
        u_ref[...] = _from_batch(_nb3(tm, v2 * bb), nc)
        w_ref[...] = _from_batch(_nb3(tm, k2 * (bb * lc['eg'])), nc)
        tm_ref[...] = tm.reshape(V_PER_K, nc, C, C)
        qkm_ref[...] = jnp.where(lc['lower'], _bb3(q2, k2, 'nt') * lc['dm'], 0.0).reshape(V_PER_K, nc, C, C)

    return pl.pallas_call(
        body, name=name, grid=(Hk, nb), in_specs=[kspec, kspec, vspec, pair, pair], out_specs=[pair, pair, cc, cc],
        out_shape=[jax.ShapeDtypeStruct((T, Hv * HD), F32)] * 2 + [jax.ShapeDtypeStruct((Hv, N, C, C), F32)] * 2,
        compiler_params=_cp("parallel", "parallel"),
    )(qn, kn, qkvc, gcb, betab)


def _scan_chunk(q_ref, k_ref, g_ref, rows):
    C = GDN_CHUNK
    gb = jnp.stack([g_ref[rows, j * HD:(j + 1) * HD] for j in range(V_PER_K)])
    gl = gb[:, C - 1:C, :]
    return q_ref[rows, :][None] * jnp.exp(gb), k_ref[rows, :][None] * jnp.exp(gl - gb), jnp.exp(gl)


def _heads(ref, rows):
    return jnp.stack([ref[rows, j * HD:(j + 1) * HD] for j in range(V_PER_K)])


def _put_heads(ref, rows, x):
    for j in range(V_PER_K):
        ref[rows, j * HD:(j + 1) * HD] = x[j]


def _gdn_scan(qn, kn, gcb, u, w, qkm, Hk, name):
    T = qn.shape[0]
    C, ns = GDN_CHUNK, GDN_SCAN_CHUNKS
    Hv, N = Hk * V_PER_K, T // C
    nb, kspec, pair, _, cc, state, _ = _gdn_specs(T, Hk, 0, ns)

    def body(q_ref, k_ref, g_ref, u_ref, w_ref, qkm_ref, o_ref, sp_ref, s_s):
        @pl.when(pl.program_id(1) == 0)
        def _():
            s_s[...] = jnp.zeros_like(s_s)

        s = s_s[...]
        for t in range(ns):
            rows = slice(t * C, (t + 1) * C)
            qd, kd, egl = _scan_chunk(q_ref, k_ref, g_ref, rows)
            sp_ref[:, t] = s
            vn = _heads(u_ref, rows) - _bb3(_heads(w_ref, rows), s)
            _put_heads(o_ref, rows, _bb3(qd, s) + _bb3(qkm_ref[:, t], vn))
            s = s * egl + _bb3(kd, vn, 'tn')
        s_s[...] = s

    return pl.pallas_call(
        body, name=name, grid=(Hk, nb), in_specs=[kspec, kspec, pair, pair, pair, cc], out_specs=[pair, state],
        out_shape=[jax.ShapeDtypeStruct((T, Hv * HD), F32), jax.ShapeDtypeStruct((Hv, N, HD, HD), F32)],
        scratch_shapes=[pltpu.VMEM((V_PER_K, HD, HD), F32)], compiler_params=_cp("parallel", "arbitrary"),
    )(qn, kn, gcb, u, w, qkm)


def _gdn_scan_bwd(qn, kn, gcb, u, w, qkm, sprev, do, Hk, name):
    T = qn.shape[0]
    C, ns = GDN_CHUNK, GDN_SCAN_CHUNKS
    Hv, N = Hk * V_PER_K, T // C
    nb, kspec, pair, _, cc, state, scal = _gdn_specs(T, Hk, 0, ns, rev=True)

    def body(q_ref, k_ref, g_ref, u_ref, w_ref, qkm_ref, sp_ref, do_ref,
             dqd_ref, dkd_ref, du_ref, dw_ref, dqkm_ref, dgl_ref, ds_s):
        @pl.when(pl.program_id(1) == 0)
        def _():
            ds_s[...] = jnp.zeros_like(ds_s)

        lower = lax.broadcasted_iota(jnp.int32, (V_PER_K, C, C), 1) >= lax.broadcasted_iota(jnp.int32, (V_PER_K, C, C), 2)
        ds = ds_s[...]
        for t in reversed(range(ns)):
            rows = slice(t * C, (t + 1) * C)
            qd, kd, egl = _scan_chunk(q_ref, k_ref, g_ref, rows)
            s, w_, qkm_, dov = sp_ref[:, t], _heads(w_ref, rows), qkm_ref[:, t], _heads(do_ref, rows)
            vn = _heads(u_ref, rows) - _bb3(w_, s)
            _put_heads(dqd_ref, rows, _bb3(dov, s, 'nt'))
            dqkm_ref[:, t] = jnp.where(lower, _bb3(dov, vn, 'nt'), 0.0)
            dvn = _bb3(qkm_, dov, 'tn') + _bb3(kd, ds)
            _put_heads(dkd_ref, rows, _bb3(vn, ds, 'nt'))
            dgl = jnp.sum(jnp.sum(ds * s, axis=1, keepdims=True) * egl, axis=2, keepdims=True)
            dgl_ref[:, t] = jnp.broadcast_to(dgl, (V_PER_K, SUBLANES, HD))
            _put_heads(du_ref, rows, dvn)
            _put_heads(dw_ref, rows, -_bb3(dvn, s, 'nt'))
            ds = ds * egl + _bb3(qd, dov, 'tn') - _bb3(w_, dvn, 'tn')
        ds_s[...] = ds

    return pl.pallas_call(
        body, name=name, grid=(Hk, nb), in_specs=[kspec, kspec, pair, pair, pair, cc, state, pair],
        out_specs=[pair] * 4 + [cc, scal],
        out_shape=[jax.ShapeDtypeStruct((T, Hv * HD), F32)] * 4
        + [jax.ShapeDtypeStruct((Hv, N, C, C), F32), jax.ShapeDtypeStruct((Hv, N, SUBLANES, HD), F32)],
        scratch_shapes=[pltpu.VMEM((V_PER_K, HD, HD), F32)], compiler_params=_cp("parallel", "arbitrary"),
    )(qn, kn, gcb, u, w, qkm, sprev, do)


def _gdn_prep_bwd(qn, kn, qkvc, voff, gcb, betab, tm, u, w, qkm, dqd, dkd, du, dw, dqkm, dgl, Hk, name):
    T = qn.shape[0]
    C, nc = GDN_CHUNK, GDN_PREP_CHUNKS
    Hv = Hk * V_PER_K
    B = V_PER_K * nc
    nb, kspec, pair, vspec, cc, _, scal = _gdn_specs(T, Hk, voff, nc)

    def body(q_ref, k_ref, v_ref, g_ref, b_ref, tm_ref, u_ref, w_ref, qkm_ref, dqd_ref, dkd_ref, du_ref, dw_ref,
             dqkm_ref, dgl_ref, dq_ref, dk_ref, dv_ref, dg_ref, dbe_ref):
        q2, k2 = _both_heads(q_ref[...], nc), _both_heads(k_ref[...], nc)
        v2, gb, bb = _to_batch(v_ref[...], nc), _to_batch(g_ref[...], nc), _to_batch(b_ref[...], nc)
        lc = _gdn_local(q2, k2, gb, bb)
        dm, eg, gl = lc['dm'], lc['eg'], lc['gl']
        tm_, qkm_, dqkm_ = (r[...].reshape(B, C, C) for r in (tm_ref, qkm_ref, dqkm_ref))
        u_, w_, dqd_, dkd_, du_, dw_ = (_to_batch(r[...], nc) for r in (u_ref, w_ref, dqd_ref, dkd_ref, du_ref, dw_ref))
        dgl_ = dgl_ref[...].reshape(B, SUBLANES, HD)[:, :1, :1]
        rowsum = lambda x: jnp.sum(x, axis=-1, keepdims=True)
        dbv = _nb3(tm_, du_, 'tn')
        dbk = _nb3(tm_, dw_, 'tn')
        da = jnp.where(lc['strict'], -(_bb3(dbv, u_, 'nt') + _bb3(dbk, w_, 'nt')), 0.0)
        rk = rowsum(dbk * k2)
        dbeta = rowsum(dbv * v2) + rk * eg[:, :, :1] + rowsum(da * lc['kk'] * dm)
        dkk = da * dm * bb[:, :, :C]
        dqkr = dqkm_ * dm
        dk = dbk * (bb * eg) + _bb3(dkk, k2) + _bb3(dkk, k2, 'tn') + _bb3(dqkr, q2, 'tn') + dkd_ * jnp.exp(gl - gb)
        dq = _bb3(dqkr, k2) + dqd_ * eg
        de = da * lc['a'] + dqkm_ * qkm_
        sk = rowsum(dkd_ * lc['kd'])
        dg = rk * (bb[:, :, :1] * eg[:, :, :1]) + rowsum(de) - _hb3(de, jnp.ones((B, C, HD), F32), 'tn')[:, :, :1] \
            + rowsum(dqd_ * lc['qd']) - sk
        last = (lax.broadcasted_iota(jnp.int32, (B, C, HD), 1) == C - 1).astype(F32)
        dgb = jnp.broadcast_to(dg, (B, C, HD)) + last * (dgl_ + jnp.sum(sk, axis=1, keepdims=True))
        suffix = (lax.broadcasted_iota(jnp.int32, (B, C, C), 2) >= lax.broadcasted_iota(jnp.int32, (B, C, C), 1)).astype(F32)
        dq_ref[...] = _from_batch(dq, nc)
        dk_ref[...] = _from_batch(dk, nc)
        dv_ref[...] = _from_batch(dbv * bb, nc)
        dg_ref[...] = _from_batch(_hb3(suffix, dgb), nc)
        dbe_ref[...] = _from_batch(jnp.broadcast_to(dbeta, (B, C, HD)), nc)

    return pl.pallas_call(
        body, name=name, grid=(Hk, nb),
        in_specs=[kspec, kspec, vspec, pair, pair, cc, pair, pair, cc, pair, pair, pair, pair, cc, scal],
        out_specs=[pair] * 5, out_shape=[jax.ShapeDtypeStruct((T, Hv * HD), F32)] * 5,
        compiler_params=_cp("parallel", "parallel"),
    )(qn, kn, qkvc, gcb, betab, tm, u, w, qkm, dqd, dkd, du, dw, dqkm, dgl)


def _adamw_math(w, g, m, v):
    m = ADAM_B1 * m + (1.0 - ADAM_B1) * g
    v = ADAM_B2 * v + (1.0 - ADAM_B2) * jnp.square(g)
    m_hat = m / (1.0 - ADAM_B1 ** ADAM_STEP)
    v_hat = v / (1.0 - ADAM_B2 ** ADAM_STEP)
    delta = -ADAM_LR * (m_hat / (jnp.sqrt(v_hat) + ADAM_EPS) + ADAM_WD * w)
    return delta, m, v


STREAM_BLOCK_BYTES = 1 << 20


def _stream_rows(R, C, mult=SUBLANES):
    for tr in (512, 256, 128, 64, 32, 16, 8):
        if R % tr == 0 and tr % mult == 0 and tr * C * 4 <= STREAM_BLOCK_BYTES:
            return tr
    return R


def _adamw(w, m, v, gs, name):
    R, C = w.shape
    tr = _stream_rows(R, C)
    n = len(gs)

    def body(*refs):
        w_ref, m_ref, v_ref = refs[:3]
        g_refs = refs[3:3 + n]
        g_out, d_out, m_out, v_out = refs[3 + n:]
        g = g_refs[0][...]
        for r_ in g_refs[1:]:
            g = g + r_[...]
        g_out[...] = g
        d_out[...], m_out[...], v_out[...] = _adamw_math(w_ref[...], g, m_ref[...], v_ref[...])

    spec = pl.BlockSpec((tr, C), lambda i: (i, 0))
    return pl.pallas_call(
        body, name=name, grid=(R // tr,), in_specs=[spec] * (3 + n), out_specs=[spec] * 4,
        out_shape=[jax.ShapeDtypeStruct((R, C), F32)] * 4, compiler_params=_cp("parallel"),
    )(w, m, v, *gs)


def _sum_devices(g8, name):
    _, M, C = g8.shape
    tr = _tile(M, (512, 256, 128, 64, 32, 16, 8))

    def body(g_ref, o_ref):
        acc = g_ref[0]
        for d in range(1, N_DEV):
            acc = acc + g_ref[d]
        o_ref[...] = acc

    return pl.pallas_call(
        body, name=name, grid=(M // tr,), in_specs=[pl.BlockSpec((N_DEV, tr, C), lambda i: (0, i, 0))],
        out_specs=pl.BlockSpec((tr, C), lambda i: (i, 0)), out_shape=jax.ShapeDtypeStruct((M, C), F32),
        compiler_params=_cp("parallel"),
    )(g8)


def _ada_w_update(c_all, dm, w, m, v, name):
    n_mod, D, Ns = w.shape
    tr = _tile(D, (256, 128))

    def body(c_ref, dm_ref, w_ref, m_ref, v_ref, g_out, d_out, m_out, v_out):
        g = _hdot(_silu(c_ref[...]), dm_ref[...], 'tn')
        g_out[...] = g
        d_out[...], m_out[...], v_out[...] = _adamw_math(w_ref[...], g, m_ref[...], v_ref[...])

    wspec = pl.BlockSpec((None, tr, Ns), lambda i, r: (i, r, 0))
    return pl.pallas_call(
        body, name=name, grid=(n_mod, D // tr),
        in_specs=[pl.BlockSpec((N_DEV, tr), lambda i, r: (0, r)), pl.BlockSpec((None, N_DEV, Ns), lambda i, r: (i, 0, 0)),
                  wspec, wspec, wspec],
        out_specs=[wspec] * 4, out_shape=[jax.ShapeDtypeStruct((n_mod, D, Ns), F32)] * 4,
        compiler_params=_cp("parallel", "parallel"),
    )(c_all, dm, w, m, v)


def _place():
    return lax.axis_index("x"), lax.axis_index("y"), lax.axis_index("c")


def _allgather8(x_shard, name):
    m_per, n = x_shard.shape

    def body(x_ref, out_ref, send_sems, recv_sems, local_sem):
        x, y, c = _place()
        me, sibling = (x, y, c), (x, y, 1 - c)
        chips = [(1 - x, y), (x, 1 - y), (1 - x, 1 - y)]

        def rows(px, py, pc):
            return out_ref.at[pl.ds((4 * px + 2 * py + pc) * m_per, m_per), :]

        def copy(k, block, to, src=None):
            return pltpu.make_async_remote_copy(
                src_ref=rows(*block) if src is None else src, dst_ref=rows(*block),
                send_sem=send_sems.at[k], recv_sem=recv_sems.at[k], device_id=to, device_id_type=MESH)

        mine = pltpu.make_async_copy(x_ref, rows(*me), local_sem)
        mine.start()
        first = [copy(0, me, sibling, src=x_ref)]
        first += [copy(1 + j, me, (*chip, c), src=x_ref) for j, chip in enumerate(chips)]
        for cp in first:
            cp.start()
        passed = [copy(4 + j, (*chip, c), sibling) for j, chip in enumerate(chips)]
        for j, chip in enumerate(chips):
            copy(1 + j, (*chip, c), me).wait_recv()
            passed[j].start()
        copy(0, sibling, me).wait_recv()
        for j, chip in enumerate(chips):
            copy(4 + j, (*chip, 1 - c), me).wait_recv()
        for cp in first + passed:
            cp.wait_send()
        mine.wait()

    return pl.pallas_call(
        body, name=name, out_shape=jax.ShapeDtypeStruct((N_DEV * m_per, n), x_shard.dtype),
        in_specs=[pl.BlockSpec(memory_space=pltpu.VMEM)], out_specs=pl.BlockSpec(memory_space=pltpu.VMEM),
        scratch_shapes=[pltpu.SemaphoreType.DMA((7,)), pltpu.SemaphoreType.DMA((7,)), pltpu.SemaphoreType.DMA],
        compiler_params=pltpu.CompilerParams(vmem_limit_bytes=VMEM_LIMIT),
    )(x_shard)


HBM_SPEC = pl.BlockSpec(memory_space=pltpu.HBM)


GATHER_SEMS = 6


def _gather_protocol(w_ref, out_ref, send_sems, recv_sems, k0):
    half = w_ref.shape[0] // 2
    x, y, c = _place()
    me, sibling = (x, y, c), (x, y, 1 - c)
    chips = [(1 - x, y), (x, 1 - y), (1 - x, 1 - y)]

    def part(cx, cy, hc):
        return out_ref.at[2 * cx + cy, pl.ds(hc * half, half), :]

    def copy(k, block, to, src=None):
        return pltpu.make_async_remote_copy(
            src_ref=part(*block) if src is None else src, dst_ref=part(*block),
            send_sem=send_sems.at[k0 + k], recv_sem=recv_sems.at[k0 + k], device_id=to, device_id_type=MESH)

    def first():
        return [copy(j, me, (*chip, c), src=w_ref.at[pl.ds(c * half, half), :]) for j, chip in enumerate(chips)]

    def start():
        for cp in first():
            cp.start()

    def finish():
        passed = [copy(3 + j, (*chip, c), sibling) for j, chip in enumerate(chips)]
        for j, chip in enumerate(chips):
            copy(j, (*chip, c), me).wait_recv()
            passed[j].start()
        for j, chip in enumerate(chips):
            copy(3 + j, (*chip, 1 - c), me).wait_recv()
        for cp in first() + passed:
            cp.wait_send()

    return start, finish


def _gather_weights(w_flat, name):
    R, C = w_flat.shape

    def body(w_ref, out_ref, send_sems, recv_sems):
        start, finish = _gather_protocol(w_ref, out_ref, send_sems, recv_sems, 0)
        start()
        finish()

    return pl.pallas_call(
        body, name=name, out_shape=jax.ShapeDtypeStruct((N_CHIPS, R, C), w_flat.dtype),
        in_specs=[HBM_SPEC], out_specs=HBM_SPEC,
        scratch_shapes=[pltpu.SemaphoreType.DMA((GATHER_SEMS,)), pltpu.SemaphoreType.DMA((GATHER_SEMS,))],
    )(w_flat)


def _swap_halves(g, name):
    n, R, C = g.shape
    half = R // 2

    def body(g_ref, got_ref, send_sem, recv_sem):
        x, y, c = _place()
        cp = pltpu.make_async_remote_copy(
            src_ref=g_ref.at[:, pl.ds((1 - c) * half, half), :], dst_ref=got_ref,
            send_sem=send_sem, recv_sem=recv_sem, device_id=(x, y, 1 - c), device_id_type=MESH)
        cp.start()
        cp.wait()

    return pl.pallas_call(
        body, name=name, out_shape=jax.ShapeDtypeStruct((n, half, C), g.dtype),
        in_specs=[HBM_SPEC], out_specs=HBM_SPEC,
        scratch_shapes=[pltpu.SemaphoreType.DMA, pltpu.SemaphoreType.DMA],
    )(g)


def _scatter_chips(q, name):
    n, R2, C = q.shape

    def body(q_ref, got_ref, send_sems, recv_sems):
        x, y, c = _place()
        chips = [(1 - x, y), (x, 1 - y), (1 - x, 1 - y)]
        cps = [pltpu.make_async_remote_copy(
            src_ref=q_ref.at[2 * cx + cy], dst_ref=got_ref.at[j], send_sem=send_sems.at[j], recv_sem=recv_sems.at[j],
            device_id=(cx, cy, c), device_id_type=MESH) for j, (cx, cy) in enumerate(chips)]
        for cp in cps:
            cp.start()
        for cp in cps:
            cp.wait()

    return pl.pallas_call(
        body, name=name, out_shape=jax.ShapeDtypeStruct((3, R2, C), q.dtype),
        in_specs=[HBM_SPEC], out_specs=HBM_SPEC,
        scratch_shapes=[pltpu.SemaphoreType.DMA((3,)), pltpu.SemaphoreType.DMA((3,))],
    )(q)


def _join_halves(h, name):
    R, C = h.shape
    R2 = R // 2

    def body(h_ref, out_ref, send_sem, recv_sem):
        x, y, c = _place()
        cp = pltpu.make_async_remote_copy(
            src_ref=h_ref.at[pl.ds(c * R2, R2), :], dst_ref=out_ref.at[pl.ds(c * R2, R2), :],
            send_sem=send_sem, recv_sem=recv_sem, device_id=(x, y, 1 - c), device_id_type=MESH)
        cp.start()
        cp.wait()

    return pl.pallas_call(
        body, name=name, out_shape=jax.ShapeDtypeStruct((R, C), h.dtype),
        in_specs=[HBM_SPEC], out_specs=HBM_SPEC, input_output_aliases={0: 0},
        scratch_shapes=[pltpu.SemaphoreType.DMA, pltpu.SemaphoreType.DMA],
    )(h)


def _add_halves(g, got, c_idx, name):
    n, R, C = g.shape
    half = R // 2
    tr = _stream_rows(half, C, 2 * SUBLANES)
    nb = half // tr

    def body(c_ref, g_ref, got_ref, o_ref):
        o_ref[...] = (g_ref[...] + got_ref[...]).astype(o_ref.dtype)

    grid_spec = pltpu.PrefetchScalarGridSpec(
        num_scalar_prefetch=1, grid=(n, nb),
        in_specs=[pl.BlockSpec((None, tr, C), lambda s, i, c_ref: (s, c_ref[0] * nb + i, 0)),
                  pl.BlockSpec((None, tr, C), lambda s, i, c_ref: (s, i, 0))],
        out_specs=pl.BlockSpec((None, tr, C), lambda s, i, c_ref: (s, i, 0)))
    return pl.pallas_call(
        body, name=name, grid_spec=grid_spec, out_shape=jax.ShapeDtypeStruct((n, half, C), BF16),
        compiler_params=_cp("parallel", "parallel"),
    )(c_idx, g, got)


def _add_chips(q, got, sc_idx, name):
    n, R2, C = q.shape
    tr = _stream_rows(R2, C, 2 * SUBLANES)
    nb = R2 // tr

    def body(s_ref, q_ref, g0_ref, g1_ref, g2_ref, o_ref):
        o_ref[...] = ((q_ref[...].astype(F32) + g0_ref[...].astype(F32)) + g1_ref[...].astype(F32)) \
            + g2_ref[...].astype(F32)

    def got_spec(j):
        return pl.BlockSpec((None, tr, C), lambda i, s_ref: (j, i, 0))

    grid_spec = pltpu.PrefetchScalarGridSpec(
        num_scalar_prefetch=1, grid=(nb,),
        in_specs=[pl.BlockSpec((None, tr, C), lambda i, s_ref: (s_ref[0], i, 0)), got_spec(0), got_spec(1), got_spec(2)],
        out_specs=pl.BlockSpec((tr, C), lambda i, s_ref: (s_ref[1] * nb + i, 0)))
    return pl.pallas_call(
        body, name=name, grid_spec=grid_spec, out_shape=jax.ShapeDtypeStruct((2 * R2, C), F32),
        compiler_params=_cp("parallel"),
    )(sc_idx, q, got, got, got)


def _pack_lanes(arrs):
    rows = []
    for a in arrs:
        f = a.reshape(-1)
        n = -(-f.shape[0] // LANES) * LANES
        rows.append(jnp.pad(f, (0, n - f.shape[0])).reshape(-1, LANES))
    out = jnp.concatenate(rows, axis=0)
    pad = -out.shape[0] % SUBLANES
    return jnp.pad(out, ((0, pad), (0, 0)))


def _unpack_lanes(packed, shapes):
    out, r = [], 0
    for shp in shapes:
        n = math.prod(shp)
        nr = -(-n // LANES)
        out.append(packed[r:r + nr].reshape(-1)[:n].reshape(shp))
        r += nr
    return out


def _shards_to_full(sh, axis):
    return jnp.concatenate([sh[i] for i in range(N_CHIPS)], axis=axis)


def _full_to_shards(full, axis):
    return jnp.stack(jnp.split(full, N_CHIPS, axis=axis), axis=0)


def _pad_cols(a, n):
    return jnp.pad(a, ((0, 0), (0, n - a.shape[1])))


def _lane_bcast(a):
    return jnp.repeat(a, HD, axis=1)


def _split_mod(mod):
    D = mod.shape[0] // 3
    return mod[None, :D], mod[None, D:2 * D], mod[None, 2 * D:]


def _fox_fwd(h, w, tag, carry=()):
    T, D = h.shape
    H = D // HD
    proj = _mm(h, w['cat'], 'nn', F32, tag + '_proj')
    flog = _mm(h, w['f'], 'nn', F32, tag + '_flog')
    qn = _headnorm(proj, 0, H, w['q_norm'], 1.0 / HD, HD ** -0.5, BF16, tag + '_qnorm')
    kn = _headnorm(proj, H, H, w['k_norm'], 1.0 / HD, 1.0, BF16, tag + '_knorm')
    vb = proj[:, 2 * D:3 * D].astype(BF16)
    fcum = _cumsum(_logsig(flog, w['f_bias'], tag + '_logf'), None, False, tag + '_fcum')
    tq = _tile(T, ATT_TILES)
    f0 = fcum[::tq, :H].T.reshape(H, T // tq, 1, 1)
    fkr = fcum[:, :H].T.reshape(H, 1, T)
    ao, lse, carried = _flash_fwd(qn, kn, vb, f0, fkr, H, tag + '_att', carry)
    gated = _fox_gate(ao, proj, 3 * H, H, tag + '_ogate')
    y = _mm(gated, w['o'], 'nn', F32, tag + '_out')
    sv = dict(h=h, proj=proj, flog=flog, qn=qn, kn=kn, vb=vb, f0=f0, fkr=fkr, ao=ao, lse=lse, gated=gated)
    return (y, sv, carried) if carry else (y, sv)


def _fox_bwd(dy, w, sv, tag):
    h, proj = sv['h'], sv['proj']
    T, D = h.shape
    H = D // HD
    g = {}
    g['o'] = _mm(sv['gated'], dy, 'tn', F32, tag + '_dwo')
    dgated = _mm(dy, w['o'], 'nt', F32, tag + '_dgated')
    dao, dog, delta = _fox_gate_bwd(dgated, sv['ao'], proj, 3 * H, H, tag + '_ogate_bwd')
    dq, dfq, dk, dv, dfk = _flash_bwd(sv['qn'], sv['kn'], sv['vb'], sv['f0'], sv['fkr'], dao, sv['lse'], delta, H,
                                      tag + '_att_bwd')
    dfcum = _pad_cols(dfq[:, ::HD] + dfk.reshape(H, T).T, LANES)
    dlogf = _cumsum(dfcum, None, True, tag + '_fcum_bwd')
    dflog, g['f_bias'] = _logsig_bwd(dlogf, sv['flog'], w['f_bias'], tag + '_logf_bwd')
    dqr, g['q_norm'] = _headnorm_bwd([(dq, 1, 0)], proj, 0, H, w['q_norm'], 1.0 / HD, HD ** -0.5, BF16,
                                     tag + '_qnorm_bwd')
    dkr, g['k_norm'] = _headnorm_bwd([(dk, 1, 0)], proj, H, H, w['k_norm'], 1.0 / HD, 1.0, BF16, tag + '_knorm_bwd')
    dproj = jnp.concatenate([dqr, dkr, dv, dog], axis=1)
    g['cat'] = _mm(h, dproj, 'tn', F32, tag + '_dwcat')
    g['f'] = _mm(h, dflog, 'tn', F32, tag + '_dwf')
    dh = [_mm(dproj, w['cat'], 'nt', F32, tag + '_dh'), _mm(dflog, w['f'], 'nt', F32, tag + '_dh_f')]
    return dh, g


def _gdn_fwd(h, w, tag):
    T, D = h.shape
    Hk = D // HD
    Hv = V_PER_K * Hk
    proj = _mm(h, w['cat'], 'nn', F32, tag + '_proj')
    ab = _mm(h, w['ab'], 'nn', F32, tag + '_ab')
    qkvc = _dwconv(proj, 0, 4 * D, w['conv'], None, 'silu', 0, F32, tag + '_conv')
    qn = _headnorm(qkvc, 0, Hk, None, 1.0, HD ** -0.5, F32, tag + '_qnorm')
    kn = _headnorm(qkvc, Hk, Hk, None, 1.0, 1.0, F32, tag + '_knorm')
    graw, beta = _gdn_gates(ab, w['a_log'], w['dt_bias'], tag + '_gates')
    gc = _cumsum(graw, GDN_CHUNK, False, tag + '_gcum')
    gcb = _lane_bcast(gc[:, :Hv])
    betab = _lane_bcast(beta[:, Hv:2 * Hv])
    u, wk, tm, qkm = _gdn_prep(qn, kn, qkvc, 2 * D, gcb, betab, Hk, tag + '_prep')
    o, sprev = _gdn_scan(qn, kn, gcb, u, wk, qkm, Hk, tag + '_scan')
    go = _gdn_out(o, proj, 4 * Hk, Hv, w['out_norm'], tag + '_onorm')
    y = _mm(go, w['o'], 'nn', F32, tag + '_out')
    return y, dict(h=h, proj=proj, ab=ab, qkvc=qkvc, qn=qn, kn=kn, gcb=gcb, betab=betab, o=o, sprev=sprev, go=go,
                   u=u, wk=wk, tm=tm, qkm=qkm)


def _gdn_bwd(dy, w, sv, tag):
    h, proj, qkvc = sv['h'], sv['proj'], sv['qkvc']
    T, D = h.shape
    Hk = D // HD
    Hv = V_PER_K * Hk
    g = {}
    g['o'] = _mm(sv['go'], dy, 'tn', F32, tag + '_dwo')
    dgo = _mm(dy, w['o'], 'nt', F32, tag + '_dgo')
    do, dz, g['out_norm'] = _gdn_out_bwd(dgo, sv['o'], proj, 4 * Hk, Hv, w['out_norm'], tag + '_onorm_bwd')
    local = (sv['u'], sv['wk'], sv['qkm'])
    dqd, dkd, du, dwk, dqkm, dgl = _gdn_scan_bwd(sv['qn'], sv['kn'], sv['gcb'], *local, sv['sprev'], do, Hk,
                                                 tag + '_scan_bwd')
    dqp, dkp, dv, dgb, dbetab = _gdn_prep_bwd(sv['qn'], sv['kn'], qkvc, 2 * D, sv['gcb'], sv['betab'], sv['tm'], *local,
                                              dqd, dkd, du, dwk, dqkm, dgl, Hk, tag + '_prep_bwd')
    pairs = lambda a: [(a, V_PER_K, j) for j in range(V_PER_K)]
    dqc, _ = _headnorm_bwd(pairs(dqp), qkvc, 0, Hk, None, 1.0, HD ** -0.5, F32, tag + '_qnorm_bwd')
    dkc, _ = _headnorm_bwd(pairs(dkp), qkvc, Hk, Hk, None, 1.0, 1.0, F32, tag + '_knorm_bwd')
    zeros = jnp.zeros((T, Hv), F32)
    dg_pad = _pad_cols(dgb[:, ::HD], LANES)
    dbeta_pad = _pad_cols(jnp.concatenate([zeros, dbetab[:, ::HD]], axis=1), LANES)
    dab, g['a_log'], g['dt_bias'] = _gdn_gates_bwd(dg_pad, dbeta_pad, sv['ab'], w['a_log'], w['dt_bias'], tag + '_gates_bwd')
    dpq, dwq = _dwconv_bwd(proj, 0, D, w['conv'], 0, None, 'silu', 0, dqc, tag + '_conv_bwd_q')
    dpk, dwk = _dwconv_bwd(proj, D, D, w['conv'], D, None, 'silu', 0, dkc, tag + '_conv_bwd_k')
    dpv, dwv = _dwconv_bwd(proj, 2 * D, 2 * D, w['conv'], 2 * D, None, 'silu', 0, dv, tag + '_conv_bwd_v')
    g['conv'] = jnp.concatenate([dwq, dwk, dwv], axis=1)
    dproj = jnp.concatenate([dpq, dpk, dpv, dz], axis=1)
    g['cat'] = _mm(h, dproj, 'tn', F32, tag + '_dwcat')
    g['ab'] = _mm(h, dab, 'tn', F32, tag + '_dwab')
    dh = [_mm(dproj, w['cat'], 'nt', F32, tag + '_dh'), _mm(dab, w['ab'], 'nt', F32, tag + '_dh_ab')]
    return dh, g


def _ffn_fwd(h, w, tag):
    dff = w['down'].shape[0]
    up = _mm(h, w['up'], 'nn', F32, tag + '_up')
    act = _dwconv(up, 0, dff, w['conv'], w['conv_b'], 'glu', dff, BF16, tag + '_conv')
    y = _mm(act, w['down'], 'nn', F32, tag + '_down')
    return y, dict(h=h, up=up, act=act)


def _ffn_bwd(dy, w, sv, tag):
    h, up = sv['h'], sv['up']
    dff = w['down'].shape[0]
    g = {}
    g['down'] = _mm(sv['act'], dy, 'tn', F32, tag + '_dwdown')
    dact = _mm(dy, w['down'], 'nt', F32, tag + '_dact')
    dgate, dval, g['conv'], g['conv_b'] = _dwconv_bwd(up, 0, dff, w['conv'], 0, w['conv_b'], 'glu', dff, dact,
                                                      tag + '_conv_bwd')
    dup = jnp.concatenate([dgate, dval], axis=1)
    g['up'] = _mm(h, dup, 'tn', F32, tag + '_dwup')
    dh = [_mm(dup, w['up'], 'nt', F32, tag + '_dh')]
    return dh, g


def _local_step(x, target, mods, norm_g, wf, wg, wffn, late=None):
    tape = []
    for i in range(2):
        for sub in range(2):
            if sub == 0:
                fwd, bwd, w, tag = [(_fox_fwd, _fox_bwd, wf, 'fox'), (_gdn_fwd, _gdn_bwd, wg, 'gdn')][i]
            else:
                fwd, bwd, w, tag = _ffn_fwd, _ffn_bwd, wffn[i], 'ffn%d' % i
            shift, scale, gate = _split_mod(mods[i, sub])
            g_pre, g_post = norm_g[i, 2 * sub][None], norm_g[i, 2 * sub + 1][None]
            h = _pre_norm(x, g_pre, scale, shift, tag + '_prenorm')
            if late is not None and (i, sub) == (0, 0):
                y, sv, gathered = fwd(h, w, tag, late[0])
                wg, wffn = late[1](gathered)
            else:
                y, sv = fwd(h, w, tag)
            x_out = _post_res(x, y, gate, g_post, tag + '_postnorm')
            tape.append((bwd, w, tag, sv, x, y, g_pre, g_post, scale, gate))
            x = x_out
    dx, lsum = _loss_head(x, target, 'loss_head')
    loss = lsum[0, 0]
    dmods = [[None, None], [None, None]]
    dnorm = [[None] * 4, [None] * 4]
    wgrads = {}
    for idx in reversed(range(4)):
        i, sub = divmod(idx, 2)
        bwd, w, tag, sv, x_in, y, g_pre, g_post, scale, gate = tape[idx]
        dy, dgate, dgpost = _post_res_bwd(dx, y, gate, g_post, tag + '_postnorm_bwd')
        dh, wgrads[tag] = bwd(dy, w, sv, tag)
        dx, dshift, dscale, dgpre = _pre_norm_bwd(dh, x_in, g_pre, scale, dx, tag + '_prenorm_bwd')
        dmods[i][sub] = jnp.concatenate([dshift[0], dscale[0], dgate[0]])
        dnorm[i][2 * sub], dnorm[i][2 * sub + 1] = dgpre[0], dgpost[0]
    dmods = jnp.stack([jnp.stack(r) for r in dmods])
    dnorm = jnp.stack([jnp.stack(r) for r in dnorm])
    return loss, dx, dmods, dnorm, wgrads


def _unpack_lanes_dev(packed, shapes):
    n = packed.shape[0]
    out, r = [], 0
    for shp in shapes:
        k = math.prod(shp)
        nr = -(-k // LANES)
        out.append(packed[:, r:r + nr].reshape(n, -1)[:, :k].reshape((n,) + tuple(shp)))
        r += nr
    return out


def _gather_lanes(arrs, name):
    packed = _pack_lanes(arrs)
    got = _allgather8(packed, name).reshape(N_DEV, packed.shape[0], LANES)
    return _unpack_lanes_dev(got, [a.shape for a in arrs]), got


def kernel(x, c, ada_w, ada_b, norm_g, fox_w_in, fox_f_bias, fox_q_norm, fox_k_norm, fox_w_o, gdn_w_in, gdn_conv_w, gdn_a_log, gdn_dt_bias, gdn_out_norm, gdn_w_o, ffn_w_up, ffn_conv_w, ffn_conv_b, ffn_w_down, loss_target, m_ada_w, m_ada_b, m_norm_g, m_fox_w_in, m_fox_f_bias, m_fox_q_norm, m_fox_k_norm, m_fox_w_o, m_gdn_w_in, m_gdn_conv_w, m_gdn_a_log, m_gdn_dt_bias, m_gdn_out_norm, m_gdn_w_o, m_ffn_w_up, m_ffn_conv_w, m_ffn_conv_b, m_ffn_w_down, v_ada_w, v_ada_b, v_norm_g, v_fox_w_in, v_fox_f_bias, v_fox_q_norm, v_fox_k_norm, v_fox_w_o, v_gdn_w_in, v_gdn_conv_w, v_gdn_a_log, v_gdn_dt_bias, v_gdn_out_norm, v_gdn_w_o, v_ffn_w_up, v_ffn_conv_w, v_ffn_conv_b, v_ffn_w_down):
    args = locals()
    w = {n: args[n] for n in WEIGHTS}
    mom = {n: args['m_' + n] for n in WEIGHTS}
    var = {n: args['v_' + n] for n in WEIGHTS}
    _, T, D = x.shape
    H = D // HD
    Hv = V_PER_K * H
    xi, yi, ci = _place()
    s_idx = 2 * xi + yi
    b_idx = 4 * xi + 2 * yi + ci
    sc_arr = jnp.stack([s_idx, ci]).astype(jnp.int32)
    c_arr = jnp.reshape(ci, (1,)).astype(jnp.int32)

    (c_all, ab_all, ng_all, gcw_all, fcw_all), _ = _gather_lanes(
        [jnp.tile(c, (SUBLANES, 1)), ada_b, norm_g, gdn_conv_w, ffn_conv_w], 'gather_small')
    c_all = c_all[:, 0, :]
    chips = lambda a: jnp.concatenate([a[2 * s] for s in range(N_CHIPS)], axis=-1)
    norm_g_full, gdn_conv_full, ffn_conv_full = chips(ng_all), chips(gcw_all), chips(fcw_all)

    Ns = ada_w.shape[-1]
    ada_w4 = ada_w.reshape(4, D, Ns)
    part = jnp.stack([_mm(c_all, ada_w4[i], 'nn', F32, 'ada_proj%d' % i, a_act='silu') for i in range(4)])
    part = part + ada_b.reshape(4, 1, Ns)
    (part_all,), _ = _gather_lanes([part], 'gather_mods')
    mine = lax.dynamic_index_in_dim(part_all[0::2], b_idx, axis=2, keepdims=False)
    mods = mine.transpose(1, 0, 2).reshape(2, 2, N_CHIPS * Ns)

    as2d = lambda a: a.reshape(-1, a.shape[-1])
    own = {n: as2d(w[n]).astype(BF16) for n in BIG}

    def whole(n, gathered):
        shards = lax.dynamic_update_index_in_dim(gathered, own[n], s_idx, 0)
        return _shards_to_full(shards.reshape((N_CHIPS,) + w[n].shape), BIG_SHARD_AXIS[n])

    fw = whole('fox_w_in', _gather_weights(own['fox_w_in'], 'gather_fox_w_in'))[0]
    wf = dict(cat=jnp.concatenate([fw[:, :3 * D], fw[:, 3 * D + H:]], axis=1), f=_pad_cols(fw[:, 3 * D:3 * D + H], LANES),
              f_bias=_pad_cols(fox_f_bias, LANES), q_norm=fox_q_norm, k_norm=fox_k_norm,
              o=whole('fox_w_o', _gather_weights(own['fox_w_o'], 'gather_fox_w_o'))[0])
    later = [n for n in BIG if not n.startswith('fox')]

    def later_weights(gathered):
        full = {n: whole(n, a) for n, a in zip(later, gathered)}
        gw = full['gdn_w_in'][0]
        wg = dict(cat=gw[:, :6 * D], ab=_pad_cols(gw[:, 6 * D:], LANES), conv=gdn_conv_full[0],
                  a_log=_pad_cols(gdn_a_log, LANES), dt_bias=_pad_cols(gdn_dt_bias, LANES), out_norm=gdn_out_norm,
                  o=full['gdn_w_o'][0])
        wffn = [dict(up=full['ffn_w_up'][i], conv=ffn_conv_full[i], conv_b=ffn_conv_b[i][None],
                     down=full['ffn_w_down'][i]) for i in range(2)]
        return wg, wffn

    loss, dx, dmods, dnorm, g = _local_step(x[0], loss_target[0], mods, norm_g_full, wf, None, None,
                                            late=([own[n] for n in later], later_weights))
    loss = lax.psum(loss, ('x', 'y', 'c'))

    gf, gg = g['fox'], g['gdn']
    big_grads = {
        'fox_w_in': jnp.concatenate([gf['cat'][:, :3 * D], gf['f'][:, :H], gf['cat'][:, 3 * D:]], axis=1)[None],
        'fox_w_o': gf['o'][None],
        'gdn_w_in': jnp.concatenate([gg['cat'], gg['ab'][:, :2 * Hv]], axis=1)[None],
        'gdn_w_o': gg['o'][None],
        'ffn_w_up': jnp.stack([g['ffn0']['up'], g['ffn1']['up']]),
        'ffn_w_down': jnp.stack([g['ffn0']['down'], g['ffn1']['down']]),
    }
    big_out = {}
    for n in BIG:
        shards = _full_to_shards(big_grads[n], BIG_SHARD_AXIS[n])
        shards = shards.reshape(N_CHIPS, -1, shards.shape[-1])
        pair_sum = _add_halves(shards, _swap_halves(shards, n + '_to_sibling'), c_arr, n + '_add_sibling')
        half_sum = _add_chips(pair_sum, _scatter_chips(pair_sum, n + '_to_chips'), sc_arr, n + '_add_chips')
        g_shard = _join_halves(half_sum, n + '_join')
        big_out[n] = [o.reshape(w[n].shape)
                      for o in _adamw(as2d(w[n]), as2d(mom[n]), as2d(var[n]), [g_shard], 'adamw_' + n)]

    small_part = [dmods, dnorm, gf['f_bias'][:, :H], gf['q_norm'], gf['k_norm'], gg['conv'][None],
                  gg['a_log'][:, :Hv], gg['dt_bias'][:, :Hv], gg['out_norm'],
                  jnp.stack([g['ffn0']['conv'], g['ffn1']['conv']]),
                  jnp.concatenate([g['ffn0']['conv_b'], g['ffn1']['conv_b']], axis=0)]
    (dmods_all, *_), got = _gather_lanes(small_part, 'gather_small_grads')
    tot = _unpack_lanes(_sum_devices(got, 'sum_small_grads'), [a.shape for a in small_part])
    small_full = dict(zip(SMALL, tot))
    small_g = {n: (lax.dynamic_slice_in_dim(small_full[n], s_idx * w[n].shape[-1], w[n].shape[-1], axis=-1)
                   if n in SMALL_SHARDED else small_full[n]) for n in SMALL}
    packs = lambda d: _pack_lanes([d[n] for n in SMALL])
    small_shapes = [w[n].shape for n in SMALL]
    small_out = [_unpack_lanes(o, small_shapes)
                 for o in _adamw(packs(w), packs(mom), packs(var), [packs(small_g)], 'adamw_small')]

    dm = lax.dynamic_slice_in_dim(dmods_all.reshape(N_DEV, 4, N_CHIPS * Ns), s_idx * Ns, Ns, axis=-1).transpose(1, 0, 2)
    ada_out = [o.reshape(ada_w.shape) for o in
               _ada_w_update(c_all, dm, ada_w4, m_ada_w.reshape(4, D, Ns), v_ada_w.reshape(4, D, Ns), 'adamw_ada_w')]

    outs = []
    for k in range(4):
        by_name = {'ada_w': ada_out[k]}
        by_name.update({n: big_out[n][k] for n in BIG})
        by_name.update(zip(SMALL, small_out[k]))
        outs += [by_name[n] for n in WEIGHTS]
    return (loss, dx[None], *outs)
```

```python
import functools
import math

import jax
import jax.numpy as jnp
from jax import lax
from jax.experimental import pallas as pl
from jax.experimental.pallas import tpu as pltpu

F32 = jnp.float32
BF16 = jnp.bfloat16
EPS = 1e-6
HD = 128
GDN_CHUNK = 64
GDN_CONV = 4
FFN_CONV = 3
LANES = 128
SUBLANES = 8
VMEM_LIMIT = 56 * 1024 * 1024
HIGHEST = lax.Precision.HIGHEST
NEG = -1e30

ADAM_LR = 0.001
ADAM_B1 = 0.9
ADAM_B2 = 0.999
ADAM_EPS = 1e-08
ADAM_WD = 0.01
ADAM_STEP = 10

WEIGHTS = ['ada_w', 'ada_b', 'norm_g', 'fox_w_in', 'fox_f_bias', 'fox_q_norm', 'fox_k_norm', 'fox_w_o',
           'gdn_w_in', 'gdn_conv_w', 'gdn_a_log', 'gdn_dt_bias', 'gdn_out_norm', 'gdn_w_o',
           'ffn_w_up', 'ffn_conv_w', 'ffn_conv_b', 'ffn_w_down']
BIG = ['fox_w_in', 'fox_w_o', 'gdn_w_in', 'gdn_w_o', 'ffn_w_up', 'ffn_w_down']
BIG_SHARD_AXIS = {'fox_w_in': 2, 'fox_w_o': 1, 'gdn_w_in': 2, 'gdn_w_o': 1, 'ffn_w_up': 2, 'ffn_w_down': 1}
SMALL = ['ada_b', 'norm_g', 'fox_f_bias', 'fox_q_norm', 'fox_k_norm', 'gdn_conv_w', 'gdn_a_log',
         'gdn_dt_bias', 'gdn_out_norm', 'ffn_conv_w', 'ffn_conv_b']
SMALL_SHARDED = ['ada_b', 'norm_g', 'gdn_conv_w', 'ffn_conv_w']
N_CHIPS = 4
N_DEV = 8
MESH = pl.DeviceIdType.MESH


def _tile(n, cands):
    for c in cands:
        if n % c == 0:
            return c
    return n


def _cp(*sem):
    return pltpu.CompilerParams(dimension_semantics=sem, vmem_limit_bytes=VMEM_LIMIT)


def _dot(a, b, mode='nn', precision=None):
    dims = {'nn': (((1,), (0,)), ((), ())), 'nt': (((1,), (1,)), ((), ())), 'tn': (((0,), (0,)), ((), ()))}[mode]
    return lax.dot_general(a, b, dims, precision=precision, preferred_element_type=F32)


def _bdot(a, b, mode='nn'):
    return _dot(a.astype(BF16), b.astype(BF16), mode)


def _hdot(a, b, mode='nn'):
    return _dot(a, b, mode, precision=HIGHEST)


def _sigmoid(x):
    return 1.0 / (1.0 + jnp.exp(-x))


def _silu(x):
    return x * _sigmoid(x)


def _softplus(x):
    return jnp.maximum(x, 0.0) + jnp.log(1.0 + jnp.exp(-jnp.abs(x)))


def _erf(x):
    return lax.erf(x)


def _gelu(x):
    return 0.5 * x * (1.0 + _erf(x * (2.0 ** -0.5)))


def _normal_cdf_pdf(x):
    cdf = 0.5 * (1.0 + _erf(x * (2.0 ** -0.5)))
    pdf = jnp.exp(-0.5 * x * x) * (1.0 / math.sqrt(2.0 * math.pi))
    return cdf, pdf


MM_K_CAP = 2816


def _k_tile(K, cap):
    for t in range(cap - cap % LANES, 0, -LANES):
        if K % t == 0:
            return t
    return K


def _mm(a, b, mode, out_dtype, name, a_act=None, scatter=()):
    if mode == 'nn':
        (M, K), (_, N) = a.shape, b.shape
    elif mode == 'nt':
        (M, K), (N, _) = a.shape, b.shape
    else:
        (K, M), (_, N) = a.shape, b.shape
    big = (1024, 512, 256, 128)
    narrow = a.dtype.itemsize == 2 and b.dtype.itemsize == 2
    tm, tn, tk = _tile(M, big), _tile(N, big), _k_tile(K, MM_K_CAP if narrow else MM_K_CAP // 2)
    nk, ns = K // tk, len(scatter)
    grid = (M // tm, N // tn, nk)

    def body(a_ref, b_ref, *rest):
        q_refs, o_ref, got_refs = rest[:ns], rest[ns], rest[ns + 1:2 * ns + 1]
        acc = rest[2 * ns + 1:2 * ns + 1 + (nk > 1)]
        jobs = [_scatter_protocol(q_refs[i], got_refs[i], *rest[len(rest) - 2:], SCATTER_SEMS * i) for i in range(ns)]
        step = [pl.program_id(d) for d in range(3)]
        if jobs:
            @pl.when((step[0] == 0) & (step[1] == 0) & (step[2] == 0))
            def _():
                for start, _ in jobs:
                    start()

        av = a_ref[...]
        if a_act == 'silu':
            av = _silu(av.astype(F32))
        part = _bdot(av, b_ref[...], mode)
        if nk == 1:
            o_ref[...] = part.astype(o_ref.dtype)
        else:
            acc_ref, = acc
            k = step[2]

            @pl.when(k == 0)
            def _():
                acc_ref[...] = part

            @pl.when(k > 0)
            def _():
                acc_ref[...] += part

            @pl.when(k == nk - 1)
            def _():
                o_ref[...] = acc_ref[...].astype(o_ref.dtype)

        if jobs:
            @pl.when((step[0] == grid[0] - 1) & (step[1] == grid[1] - 1) & (step[2] == grid[2] - 1))
            def _():
                for _, finish in jobs:
                    finish()

    if mode == 'nn':
        a_spec = pl.BlockSpec((tm, tk), lambda i, j, k: (i, k))
        b_spec = pl.BlockSpec((tk, tn), lambda i, j, k: (k, j))
    elif mode == 'nt':
        a_spec = pl.BlockSpec((tm, tk), lambda i, j, k: (i, k))
        b_spec = pl.BlockSpec((tn, tk), lambda i, j, k: (j, k))
    else:
        a_spec = pl.BlockSpec((tk, tm), lambda i, j, k: (k, i))
        b_spec = pl.BlockSpec((tk, tn), lambda i, j, k: (k, j))
    sems = [pltpu.SemaphoreType.DMA((SCATTER_SEMS * ns,))] * 2 if ns else []
    out = pl.pallas_call(
        body, name=name, grid=grid,
        in_specs=[a_spec, b_spec] + [HBM_SPEC] * ns,
        out_specs=[pl.BlockSpec((tm, tn), lambda i, j, k: (i, j))] + [HBM_SPEC] * ns,
        out_shape=[jax.ShapeDtypeStruct((M, N), out_dtype)] + _scatter_shapes(scatter),
        scratch_shapes=([pltpu.VMEM((tm, tn), F32)] if nk > 1 else []) + sems,
        compiler_params=_cp("arbitrary", "arbitrary", "arbitrary") if ns else _cp("parallel", "parallel", "arbitrary"),
    )(a, b, *scatter)
    return (out[0], list(out[1:])) if ns else out[0]


ROW_TILES = (256, 128, 64, 32, 16, 8)


def _row_spec(tT, D):
    return pl.BlockSpec((tT, D), lambda i: (i, 0))


def _vec_spec(D):
    return pl.BlockSpec((1, D), lambda i: (0, 0))


def _pre_norm(x, g, scale, shift, name):
    T, D = x.shape
    tT = _tile(T, ROW_TILES)

    def body(x_ref, g_ref, sc_ref, sh_ref, h_ref):
        xv = x_ref[...]
        r = lax.rsqrt(jnp.mean(xv * xv, axis=-1, keepdims=True) + EPS)
        h_ref[...] = ((xv * r) * g_ref[...] * (1.0 + sc_ref[...]) + sh_ref[...]).astype(h_ref.dtype)

    return pl.pallas_call(
        body, name=name, grid=(T // tT,),
        in_specs=[_row_spec(tT, D), _vec_spec(D), _vec_spec(D), _vec_spec(D)],
        out_specs=_row_spec(tT, D), out_shape=jax.ShapeDtypeStruct((T, D), BF16),
        compiler_params=_cp("parallel"),
    )(x, g, scale, shift)


def _post_res(x, y, gate, g, name):
    T, D = x.shape
    tT = _tile(T, ROW_TILES)

    def body(x_ref, y_ref, gate_ref, g_ref, o_ref):
        yv = y_ref[...]
        r = lax.rsqrt(jnp.mean(yv * yv, axis=-1, keepdims=True) + EPS)
        o_ref[...] = x_ref[...] + gate_ref[...] * ((yv * r) * g_ref[...])

    return pl.pallas_call(
        body, name=name, grid=(T // tT,),
        in_specs=[_row_spec(tT, D), _row_spec(tT, D), _vec_spec(D), _vec_spec(D)],
        out_specs=_row_spec(tT, D), out_shape=jax.ShapeDtypeStruct((T, D), F32),
        compiler_params=_cp("parallel"),
    )(x, y, gate, g)


def _post_res_bwd(dout, y, gate, g, name):
    T, D = y.shape
    tT = _tile(T, ROW_TILES)

    def body(do_ref, y_ref, gate_ref, g_ref, dy_ref, dgate_ref, dg_ref):
        @pl.when(pl.program_id(0) == 0)
        def _():
            dgate_ref[...] = jnp.zeros_like(dgate_ref)
            dg_ref[...] = jnp.zeros_like(dg_ref)

        yv, dov, gatev, gv = y_ref[...], do_ref[...], gate_ref[...], g_ref[...]
        r = lax.rsqrt(jnp.mean(yv * yv, axis=-1, keepdims=True) + EPS)
        yn = yv * r
        t = dov * yn
        dgate_ref[...] += jnp.sum(t * gv, axis=0, keepdims=True)
        dg_ref[...] += jnp.sum(t * gatev, axis=0, keepdims=True)
        dyn = dov * (gatev * gv)
        dy_ref[...] = (r * (dyn - yn * jnp.mean(dyn * yn, axis=-1, keepdims=True))).astype(dy_ref.dtype)

    return pl.pallas_call(
        body, name=name, grid=(T // tT,),
        in_specs=[_row_spec(tT, D), _row_spec(tT, D), _vec_spec(D), _vec_spec(D)],
        out_specs=[_row_spec(tT, D), _vec_spec(D), _vec_spec(D)],
        out_shape=[jax.ShapeDtypeStruct((T, D), BF16), jax.ShapeDtypeStruct((1, D), F32),
                   jax.ShapeDtypeStruct((1, D), F32)],
        compiler_params=_cp("arbitrary"),
    )(dout, y, gate, g)


def _pre_norm_bwd(dhs, x, g, scale, dres, name):
    T, D = x.shape
    tT = _tile(T, ROW_TILES)
    n = len(dhs)

    def body(*refs):
        dh_refs = refs[:n]
        x_ref, g_ref, sc_ref, dres_ref, dx_ref, dsh_ref, dsc_ref, dg_ref = refs[n:]

        @pl.when(pl.program_id(0) == 0)
        def _():
            dsh_ref[...] = jnp.zeros_like(dsh_ref)
            dsc_ref[...] = jnp.zeros_like(dsc_ref)
            dg_ref[...] = jnp.zeros_like(dg_ref)

        dh = dh_refs[0][...]
        for r_ in dh_refs[1:]:
            dh = dh + r_[...]
        xv, gv, scv = x_ref[...], g_ref[...], sc_ref[...]
        r = lax.rsqrt(jnp.mean(xv * xv, axis=-1, keepdims=True) + EPS)
        xn = xv * r
        t = dh * xn
        dsh_ref[...] += jnp.sum(dh, axis=0, keepdims=True)
        dsc_ref[...] += jnp.sum(t * gv, axis=0, keepdims=True)
        dg_ref[...] += jnp.sum(t * (1.0 + scv), axis=0, keepdims=True)
        dxn = dh * (gv * (1.0 + scv))
        dx_ref[...] = dres_ref[...] + r * (dxn - xn * jnp.mean(dxn * xn, axis=-1, keepdims=True))

    return pl.pallas_call(
        body, name=name, grid=(T // tT,),
        in_specs=[_row_spec(tT, D)] * n + [_row_spec(tT, D), _vec_spec(D), _vec_spec(D), _row_spec(tT, D)],
        out_specs=[_row_spec(tT, D), _vec_spec(D), _vec_spec(D), _vec_spec(D)],
        out_shape=[jax.ShapeDtypeStruct((T, D), F32)] + [jax.ShapeDtypeStruct((1, D), F32)] * 3,
        compiler_params=_cp("arbitrary"),
    )(*dhs, x, g, scale, dres)


def _loss_head(y, target, name):
    T, D = y.shape
    tT = _tile(T, ROW_TILES)

    def body(y_ref, t_ref, dy_ref, l_ref):
        @pl.when(pl.program_id(0) == 0)
        def _():
            l_ref[...] = jnp.zeros_like(l_ref)

        e = y_ref[...] - t_ref[...]
        dy_ref[...] = e * (1.0 / D)
        s = jnp.sum(jnp.mean(e * e, axis=-1, keepdims=True), axis=0, keepdims=True)
        l_ref[...] += 0.5 * s

    return pl.pallas_call(
        body, name=name, grid=(T // tT,),
        in_specs=[_row_spec(tT, D), _row_spec(tT, D)],
        out_specs=[_row_spec(tT, D), pl.BlockSpec((SUBLANES, LANES), lambda i: (0, 0))],
        out_shape=[jax.ShapeDtypeStruct((T, D), F32), jax.ShapeDtypeStruct((SUBLANES, LANES), F32)],
        compiler_params=_cp("arbitrary"),
    )(y, target)


HEAD_ROW_TILES = (1024, 512, 256, 128, 64)


def _hb(tT, off=0):
    return pl.BlockSpec((tT, HD), lambda i, h: (i, off + h))


def _hvec():
    return pl.BlockSpec((1, HD), lambda i, h: (0, 0))


def _headnorm(x, off, H, g, c1, post, out_dtype, name):
    T = x.shape[0]
    tT = _tile(T, HEAD_ROW_TILES)
    has_g = g is not None

    def body(*refs):
        x_ref = refs[0]
        o_ref = refs[-1]
        xv = x_ref[...]
        yv = xv * lax.rsqrt(c1 * jnp.sum(xv * xv, axis=-1, keepdims=True) + EPS)
        if has_g:
            yv = yv * refs[1][...]
        if post != 1.0:
            yv = yv * post
        o_ref[...] = yv.astype(o_ref.dtype)

    return pl.pallas_call(
        body, name=name, grid=(T // tT, H),
        in_specs=[_hb(tT, off)] + ([_hvec()] if has_g else []),
        out_specs=_hb(tT), out_shape=jax.ShapeDtypeStruct((T, H * HD), out_dtype),
        compiler_params=_cp("parallel", "parallel"),
    )(*([x, g] if has_g else [x]))


def _headnorm_bwd(dys, x, off, H, g, c1, post, out_dtype, name):
    T = x.shape[0]
    tT = _tile(T, HEAD_ROW_TILES)
    n = len(dys)
    has_g = g is not None

    def body(*refs):
        dy_refs = refs[:n]
        x_ref = refs[n]
        g_ref = refs[n + 1] if has_g else None
        dx_ref, dg_ref = refs[-2], refs[-1]

        @pl.when((pl.program_id(0) == 0) & (pl.program_id(1) == 0))
        def _():
            dg_ref[...] = jnp.zeros_like(dg_ref)

        dy = dy_refs[0][...].astype(F32)
        for r_ in dy_refs[1:]:
            dy = dy + r_[...].astype(F32)
        if post != 1.0:
            dy = dy * post
        xv = x_ref[...]
        r = lax.rsqrt(c1 * jnp.sum(xv * xv, axis=-1, keepdims=True) + EPS)
        xn = xv * r
        if has_g:
            dg_ref[...] += jnp.sum(dy * xn, axis=0, keepdims=True)
            dy = dy * g_ref[...]
        dx_ref[...] = (r * (dy - xn * (c1 * jnp.sum(dy * xn, axis=-1, keepdims=True)))).astype(dx_ref.dtype)

    dy_specs = [pl.BlockSpec((tT, HD), lambda i, h, st=st, of=of: (i, st * h + of)) for (_, st, of) in dys]
    return pl.pallas_call(
        body, name=name, grid=(T // tT, H),
        in_specs=dy_specs + [_hb(tT, off)] + ([_hvec()] if has_g else []),
        out_specs=[_hb(tT), _hvec()],
        out_shape=[jax.ShapeDtypeStruct((T, H * HD), out_dtype), jax.ShapeDtypeStruct((1, HD), F32)],
        compiler_params=_cp("arbitrary", "arbitrary"),
    )(*[d[0] for d in dys], x, *([g] if has_g else []))


def _fox_gate(ao, proj, og_off, H, name):
    T = ao.shape[0]
    tT = _tile(T, HEAD_ROW_TILES)

    def body(ao_ref, og_ref, o_ref):
        o_ref[...] = (ao_ref[...] * _sigmoid(og_ref[...])).astype(o_ref.dtype)

    return pl.pallas_call(
        body, name=name, grid=(T // tT, H),
        in_specs=[_hb(tT), _hb(tT, og_off)], out_specs=_hb(tT),
        out_shape=jax.ShapeDtypeStruct((T, H * HD), BF16), compiler_params=_cp("parallel", "parallel"),
    )(ao, proj)


def _fox_gate_bwd(dgated, ao, proj, og_off, H, name):
    T = ao.shape[0]
    tT = _tile(T, HEAD_ROW_TILES)

    def body(dg_ref, ao_ref, og_ref, dao_ref, dog_ref, delta_ref):
        dg, aov = dg_ref[...], ao_ref[...]
        sg = _sigmoid(og_ref[...])
        dao = dg * sg
        dao_ref[...] = dao.astype(dao_ref.dtype)
        dog_ref[...] = (dg * aov * sg * (1.0 - sg)).astype(dog_ref.dtype)
        delta_ref[...] = jnp.broadcast_to(jnp.sum(dao * aov, axis=-1, keepdims=True), delta_ref.shape)

    return pl.pallas_call(
        body, name=name, grid=(T // tT, H),
        in_specs=[_hb(tT), _hb(tT), _hb(tT, og_off)], out_specs=[_hb(tT)] * 3,
        out_shape=[jax.ShapeDtypeStruct((T, H * HD), BF16), jax.ShapeDtypeStruct((T, H * HD), BF16),
                   jax.ShapeDtypeStruct((T, H * HD), F32)],
        compiler_params=_cp("parallel", "parallel"),
    )(dgated, ao, proj)


def _gdn_out(o, proj, z_off, Hv, g, name):
    T = o.shape[0]
    tT = _tile(T, HEAD_ROW_TILES)

    def body(o_ref, z_ref, g_ref, y_ref):
        ov, zv = o_ref[...], z_ref[...]
        r = lax.rsqrt(jnp.mean(ov * ov, axis=-1, keepdims=True) + EPS)
        y_ref[...] = (((ov * r) * g_ref[...]) * _silu(zv)).astype(y_ref.dtype)

    return pl.pallas_call(
        body, name=name, grid=(T // tT, Hv),
        in_specs=[_hb(tT), _hb(tT, z_off), _hvec()], out_specs=_hb(tT),
        out_shape=jax.ShapeDtypeStruct((T, Hv * HD), BF16), compiler_params=_cp("parallel", "parallel"),
    )(o, proj, g)


def _gdn_out_bwd(dy, o, proj, z_off, Hv, g, name):
    T = o.shape[0]
    tT = _tile(T, HEAD_ROW_TILES)

    def body(dy_ref, o_ref, z_ref, g_ref, do_ref, dz_ref, dg_ref):
        @pl.when((pl.program_id(0) == 0) & (pl.program_id(1) == 0))
        def _():
            dg_ref[...] = jnp.zeros_like(dg_ref)

        dyv, ov, zv, gv = dy_ref[...], o_ref[...], z_ref[...], g_ref[...]
        r = lax.rsqrt(jnp.mean(ov * ov, axis=-1, keepdims=True) + EPS)
        on = ov * r
        sg = _sigmoid(zv)
        sz = zv * sg
        dz_ref[...] = (dyv * (on * gv) * (sg * (1.0 + zv * (1.0 - sg)))).astype(dz_ref.dtype)
        t = dyv * sz
        dg_ref[...] += jnp.sum(t * on, axis=0, keepdims=True)
        don = t * gv
        do_ref[...] = r * (don - on * jnp.mean(don * on, axis=-1, keepdims=True))

    return pl.pallas_call(
        body, name=name, grid=(T // tT, Hv),
        in_specs=[_hb(tT), _hb(tT), _hb(tT, z_off), _hvec()], out_specs=[_hb(tT), _hb(tT), _hvec()],
        out_shape=[jax.ShapeDtypeStruct((T, Hv * HD), F32), jax.ShapeDtypeStruct((T, Hv * HD), BF16),
                   jax.ShapeDtypeStruct((1, HD), F32)],
        compiler_params=_cp("arbitrary", "arbitrary"),
    )(dy, o, proj, g)


def _lrow(tT):
    return pl.BlockSpec((tT, LANES), lambda i: (i, 0))


def _lvec():
    return pl.BlockSpec((1, LANES), lambda i: (0, 0))


def _logsig(x, b, name):
    T = x.shape[0]
    tT = _tile(T, HEAD_ROW_TILES)

    def body(x_ref, b_ref, o_ref):
        o_ref[...] = -_softplus(-(x_ref[...] + b_ref[...]))

    return pl.pallas_call(body, name=name, grid=(T // tT,), in_specs=[_lrow(tT), _lvec()], out_specs=_lrow(tT),
                          out_shape=jax.ShapeDtypeStruct((T, LANES), F32), compiler_params=_cp("parallel"))(x, b)


def _logsig_bwd(dy, x, b, name):
    T = x.shape[0]
    tT = _tile(T, HEAD_ROW_TILES)

    def body(dy_ref, x_ref, b_ref, dx_ref, db_ref):
        @pl.when(pl.program_id(0) == 0)
        def _():
            db_ref[...] = jnp.zeros_like(db_ref)

        dx = dy_ref[...] * _sigmoid(-(x_ref[...] + b_ref[...]))
        dx_ref[...] = dx.astype(dx_ref.dtype)
        db_ref[...] += jnp.sum(dx, axis=0, keepdims=True)

    return pl.pallas_call(
        body, name=name, grid=(T // tT,), in_specs=[_lrow(tT), _lrow(tT), _lvec()], out_specs=[_lrow(tT), _lvec()],
        out_shape=[jax.ShapeDtypeStruct((T, LANES), BF16), jax.ShapeDtypeStruct((1, LANES), F32)],
        compiler_params=_cp("arbitrary"))(dy, x, b)


def _gdn_gates(ab, alog, dtb, name):
    T = ab.shape[0]
    tT = _tile(T, HEAD_ROW_TILES)

    def body(ab_ref, al_ref, dt_ref, g_ref, be_ref):
        v = ab_ref[...]
        g_ref[...] = -jnp.exp(al_ref[...]) * _softplus(v + dt_ref[...])
        be_ref[...] = _sigmoid(v)

    return pl.pallas_call(
        body, name=name, grid=(T // tT,), in_specs=[_lrow(tT), _lvec(), _lvec()], out_specs=[_lrow(tT)] * 2,
        out_shape=[jax.ShapeDtypeStruct((T, LANES), F32)] * 2, compiler_params=_cp("parallel"))(ab, alog, dtb)


def _gdn_gates_bwd(dg, dbeta, ab, alog, dtb, name):
    T = ab.shape[0]
    tT = _tile(T, HEAD_ROW_TILES)

    def body(dg_ref, dbe_ref, ab_ref, al_ref, dt_ref, dab_ref, dal_ref, ddt_ref):
        @pl.when(pl.program_id(0) == 0)
        def _():
            dal_ref[...] = jnp.zeros_like(dal_ref)
            ddt_ref[...] = jnp.zeros_like(ddt_ref)

        v, dgv = ab_ref[...], dg_ref[...]
        ea = jnp.exp(al_ref[...])
        z = v + dt_ref[...]
        da = dgv * (-ea * _sigmoid(z))
        sb = _sigmoid(v)
        dab_ref[...] = (da + dbe_ref[...] * sb * (1.0 - sb)).astype(dab_ref.dtype)
        dal_ref[...] += jnp.sum(dgv * (-ea * _softplus(z)), axis=0, keepdims=True)
        ddt_ref[...] += jnp.sum(da, axis=0, keepdims=True)

    return pl.pallas_call(
        body, name=name, grid=(T // tT,), in_specs=[_lrow(tT), _lrow(tT), _lrow(tT), _lvec(), _lvec()],
        out_specs=[_lrow(tT), _lvec(), _lvec()],
        out_shape=[jax.ShapeDtypeStruct((T, LANES), BF16), jax.ShapeDtypeStruct((1, LANES), F32),
                   jax.ShapeDtypeStruct((1, LANES), F32)],
        compiler_params=_cp("arbitrary"))(dg, dbeta, ab, alog, dtb)


def _cumsum(x, seg, reverse, name):
    T = x.shape[0]
    tb = _tile(T, (256, 128, 64))
    nb = T // tb
    carry = seg is None

    def body(x_ref, o_ref, c_ref):
        @pl.when(pl.program_id(0) == 0)
        def _():
            c_ref[...] = jnp.zeros_like(c_ref)

        ri = lax.broadcasted_iota(jnp.int32, (tb, tb), 0)
        ci = lax.broadcasted_iota(jnp.int32, (tb, tb), 1)
        keep = (ci >= ri) if reverse else (ci <= ri)
        if seg is not None:
            keep = keep & ((ri // seg) == (ci // seg))
        y = _hdot(keep.astype(F32), x_ref[...])
        if carry:
            y = y + c_ref[...]
            c_ref[...] = y[0:1, :] if reverse else y[tb - 1:tb, :]
        o_ref[...] = y

    imap = (lambda i: (nb - 1 - i, 0)) if reverse else (lambda i: (i, 0))
    return pl.pallas_call(
        body, name=name, grid=(nb,), in_specs=[pl.BlockSpec((tb, LANES), imap)],
        out_specs=pl.BlockSpec((tb, LANES), imap), out_shape=jax.ShapeDtypeStruct((T, LANES), F32),
        scratch_shapes=[pltpu.VMEM((1, LANES), F32)], compiler_params=_cp("arbitrary"))(x)


ATT_TILES = (1024, 512, 256, 128)


def _att_scores(q, k, f0, fk):
    return _dot(q, k, 'nt') - (fk - f0)


def _diag_keep(tq):
    return lax.broadcasted_iota(jnp.int32, (tq, tq), 1) <= lax.broadcasted_iota(jnp.int32, (tq, tq), 0)


def _tri_pairs(nq, by_key):
    if by_key:
        pairs = [(qi, ki) for ki in range(nq) for qi in range(ki, nq)]
    else:
        pairs = [(qi, ki) for qi in range(nq) for ki in range(qi + 1)]
    return jnp.asarray([p[0] for p in pairs], jnp.int32), jnp.asarray([p[1] for p in pairs], jnp.int32)


def _att_specs(tq):
    qspec = pl.BlockSpec((tq, HD), lambda h, p, qt, kt: (qt[p], h))
    kspec = pl.BlockSpec((tq, HD), lambda h, p, qt, kt: (kt[p], h))
    f0spec = pl.BlockSpec((None, None, 1, 1), lambda h, p, qt, kt: (h, qt[p], 0, 0))
    fkspec = pl.BlockSpec((None, 1, tq), lambda h, p, qt, kt: (h, 0, kt[p]))
    return qspec, kspec, f0spec, fkspec


def _flash_fwd(qn, kn, vb, f0, fkr, H, name, carry=()):
    T = qn.shape[0]
    tq = _tile(T, ATT_TILES)
    nq = T // tq
    qt, kt = _tri_pairs(nq, by_key=False)
    n_pairs, n_carry = qt.shape[0], len(carry)

    def body(qt_ref, kt_ref, q_ref, k_ref, v_ref, f0_ref, fk_ref, *rest):
        w_refs, (o_ref, lse_ref), g_refs = rest[:n_carry], rest[n_carry:n_carry + 2], rest[n_carry + 2:2 * n_carry + 2]
        m_s, l_s, acc_s = rest[2 * n_carry + 2:2 * n_carry + 5]
        jobs = [_gather_protocol(w_refs[i], g_refs[i], *rest[2 * n_carry + 5:], GATHER_SEMS * i) for i in range(n_carry)]
        qi, ki = qt_ref[pl.program_id(1)], kt_ref[pl.program_id(1)]

        if jobs:
            @pl.when((pl.program_id(0) == 0) & (pl.program_id(1) == 0))
            def _():
                for start, _ in jobs:
                    start()

        @pl.when(ki == 0)
        def _():
            m_s[...] = jnp.full_like(m_s, NEG)
            l_s[...] = jnp.zeros_like(l_s)
            acc_s[...] = jnp.zeros_like(acc_s)

        def step(diagonal):
            s = _att_scores(q_ref[...], k_ref[...], f0_ref[...], fk_ref[...])
            if diagonal:
                s = jnp.where(_diag_keep(tq), s, NEG)
            m_prev = m_s[...]
            m_new = jnp.maximum(m_prev, jnp.max(s, axis=1, keepdims=True))
            alpha = jnp.exp(m_prev - m_new)
            p = jnp.exp(s - m_new[:, :1])
            l_s[...] = alpha * l_s[...] + jnp.sum(p, axis=1, keepdims=True)
            acc_s[...] = acc_s[...] * alpha + _bdot(p, v_ref[...])
            m_s[...] = m_new

        @pl.when(ki < qi)
        def _():
            step(False)

        @pl.when(ki == qi)
        def _():
            step(True)
            o_ref[...] = acc_s[...] / l_s[...]
            lse_ref[...] = m_s[...] + jnp.log(l_s[...])

        if jobs:
            @pl.when((pl.program_id(0) == H - 1) & (pl.program_id(1) == n_pairs - 1))
            def _():
                for _, finish in jobs:
                    finish()

    qspec, kspec, f0spec, fkspec = _att_specs(tq)
    sems = [pltpu.SemaphoreType.DMA((GATHER_SEMS * n_carry,))] * 2 if n_carry else []
    grid_spec = pltpu.PrefetchScalarGridSpec(
        num_scalar_prefetch=2, grid=(H, n_pairs),
        in_specs=[qspec, kspec, kspec, f0spec, fkspec] + [HBM_SPEC] * n_carry,
        out_specs=[qspec, qspec] + [HBM_SPEC] * n_carry,
        scratch_shapes=[pltpu.VMEM((tq, HD), F32)] * 3 + sems)
    out = pl.pallas_call(
        body, name=name, grid_spec=grid_spec,
        out_shape=[jax.ShapeDtypeStruct((T, H * HD), F32)] * 2
        + [jax.ShapeDtypeStruct((N_CHIPS,) + a.shape, a.dtype) for a in carry],
        compiler_params=_cp("arbitrary", "arbitrary") if n_carry else _cp("parallel", "arbitrary"),
    )(qt, kt, qn, kn, vb, f0, fkr, *carry)
    return out[0], out[1], list(out[2:])


def _flash_bwd(qn, kn, vb, f0, fkr, dao, lse, delta, H, name):
    T = qn.shape[0]
    tq = _tile(T, ATT_TILES)
    nq = T // tq
    qt, kt = _tri_pairs(nq, by_key=True)

    def body(qt_ref, kt_ref, q_ref, k_ref, v_ref, f0_ref, fk_ref, do_ref, lse_ref, dl_ref,
             dq_ref, dfq_ref, dk_ref, dv_ref, dfk_ref, dk_s, dv_s, dfk_s):
        qi, ki = qt_ref[pl.program_id(1)], kt_ref[pl.program_id(1)]

        @pl.when(pl.program_id(1) == 0)
        def _():
            dq_ref[...] = jnp.zeros_like(dq_ref)
            dfq_ref[...] = jnp.zeros_like(dfq_ref)

        @pl.when(qi == ki)
        def _():
            dk_s[...] = jnp.zeros_like(dk_s)
            dv_s[...] = jnp.zeros_like(dv_s)
            dfk_s[...] = jnp.zeros_like(dfk_s)

        def step(diagonal):
            s = _att_scores(q_ref[...], k_ref[...], f0_ref[...], fk_ref[...])
            p = jnp.exp(s - lse_ref[...][:, :1])
            if diagonal:
                p = jnp.where(_diag_keep(tq), p, 0.0)
            dp = _dot(do_ref[...], v_ref[...], 'nt')
            ds = p * (dp - dl_ref[...][:, :1])
            dv_s[...] += _bdot(p, do_ref[...], 'tn')
            dk_s[...] += _bdot(ds, q_ref[...], 'tn')
            dfk_s[...] -= jnp.sum(ds, axis=0, keepdims=True)
            rows = pl.ds(pl.multiple_of(qi * tq, tq), tq)
            dq_ref[rows, :] += _bdot(ds, k_ref[...])
            dfq_ref[rows, :] += jnp.broadcast_to(jnp.sum(ds, axis=1, keepdims=True), (tq, HD))

        @pl.when(qi > ki)
        def _():
            step(False)

        @pl.when(qi == ki)
        def _():
            step(True)

        @pl.when(qi == nq - 1)
        def _():
            dk_ref[...] = dk_s[...]
            dv_ref[...] = dv_s[...].astype(dv_ref.dtype)
            dfk_ref[...] = dfk_s[...]

    qspec, kspec, f0spec, fkspec = _att_specs(tq)
    head = pl.BlockSpec((T, HD), lambda h, p, qt, kt: (0, h))
    grid_spec = pltpu.PrefetchScalarGridSpec(
        num_scalar_prefetch=2, grid=(H, qt.shape[0]),
        in_specs=[qspec, kspec, kspec, f0spec, fkspec, qspec, qspec, qspec],
        out_specs=[head, head, kspec, kspec, fkspec],
        scratch_shapes=[pltpu.VMEM((tq, HD), F32), pltpu.VMEM((tq, HD), F32), pltpu.VMEM((1, tq), F32)])
    return pl.pallas_call(
        body, name=name, grid_spec=grid_spec,
        out_shape=[jax.ShapeDtypeStruct((T, H * HD), F32)] * 3
        + [jax.ShapeDtypeStruct((T, H * HD), BF16), jax.ShapeDtypeStruct((H, 1, T), F32)],
        compiler_params=_cp("parallel", "arbitrary"),
    )(qt, kt, qn, kn, vb, f0, fkr, dao, lse, delta)


CONV_TILES = (512, 256, 128, 64)
HALO = SUBLANES


def _dwconv(x, xoff, W, w, b, act, voff, out_dtype, name):
    T = x.shape[0]
    K = w.shape[0]
    tT, tC = _tile(T, CONV_TILES), _tile(W, CONV_TILES)
    xb, hb = xoff // tC, tT // HALO
    glu = act == 'glu'

    def body(*refs):
        if glu:
            x_ref, xp_ref, w_ref, b_ref, v_ref, o_ref, buf = refs
        else:
            x_ref, xp_ref, w_ref, o_ref, buf = refs
        i = pl.program_id(0)
        buf[0:HALO, :] = jnp.where(i > 0, xp_ref[...], 0.0)
        buf[HALO:, :] = x_ref[...]
        conv = w_ref[0:1, :] * buf[pl.ds(HALO - (K - 1), tT), :]
        for k in range(1, K):
            conv = conv + w_ref[k:k + 1, :] * buf[pl.ds(HALO - (K - 1) + k, tT), :]
        if glu:
            o_ref[...] = (_gelu(conv + b_ref[...]) * v_ref[...]).astype(o_ref.dtype)
        else:
            o_ref[...] = _silu(conv).astype(o_ref.dtype)

    cur = pl.BlockSpec((tT, tC), lambda i, j: (i, xb + j))
    prev = pl.BlockSpec((HALO, tC), lambda i, j: (jnp.maximum(i * hb - 1, 0), xb + j))
    wspec = pl.BlockSpec((K, tC), lambda i, j: (0, j))
    in_specs, args = [cur, prev, wspec], [x, x, w]
    if glu:
        vb = voff // tC
        in_specs += [pl.BlockSpec((1, tC), lambda i, j: (0, j)), pl.BlockSpec((tT, tC), lambda i, j: (i, vb + j))]
        args += [b, x]
    return pl.pallas_call(
        body, name=name, grid=(T // tT, W // tC), in_specs=in_specs,
        out_specs=pl.BlockSpec((tT, tC), lambda i, j: (i, j)), out_shape=jax.ShapeDtypeStruct((T, W), out_dtype),
        scratch_shapes=[pltpu.VMEM((tT + HALO, tC), F32)], compiler_params=_cp("parallel", "parallel"),
    )(*args)


def _dwconv_bwd(x, xoff, W, w, woff, b, act, voff, dy, name):
    T = x.shape[0]
    K = w.shape[0]
    tT, tC = _tile(T, CONV_TILES), _tile(W, CONV_TILES)
    xb, wb, hb, nT = xoff // tC, woff // tC, tT // HALO, T // tT
    last_halo = T // HALO - 1
    glu = act == 'glu'

    def body(*refs):
        if glu:
            (x_ref, xp_ref, xn_ref, dy_ref, dyn_ref, w_ref, b_ref, v_ref, vn_ref,
             dx_ref, dv_ref, dw_ref, db_ref, xbuf, dybuf, dbuf, vbuf) = refs
        else:
            x_ref, xp_ref, xn_ref, dy_ref, dyn_ref, w_ref, dx_ref, dw_ref, xbuf, dybuf, dbuf = refs
        i = pl.program_id(1)

        @pl.when(i == 0)
        def _():
            dw_ref[...] = jnp.zeros_like(dw_ref)
            if glu:
                db_ref[...] = jnp.zeros_like(db_ref)

        ext = tT + HALO
        xbuf[0:HALO, :] = jnp.where(i > 0, xp_ref[...], 0.0)
        xbuf[HALO:HALO + tT, :] = x_ref[...]
        xbuf[HALO + tT:, :] = xn_ref[...]
        dybuf[0:tT, :] = dy_ref[...].astype(F32)
        dybuf[tT:, :] = jnp.where(i < nT - 1, dyn_ref[...].astype(F32), 0.0)
        conv = w_ref[0:1, :] * xbuf[pl.ds(HALO - (K - 1), ext), :]
        for k in range(1, K):
            conv = conv + w_ref[k:k + 1, :] * xbuf[pl.ds(HALO - (K - 1) + k, ext), :]
        dyv = dybuf[...]
        if glu:
            vbuf[0:tT, :] = v_ref[...]
            vbuf[tT:, :] = vn_ref[...]
            z = conv + b_ref[...]
            cdf, pdf = _normal_cdf_pdf(z)
            dconv = dyv * vbuf[...] * (cdf + z * pdf)
            dv_ref[...] = (dyv[0:tT, :] * (z[0:tT, :] * cdf[0:tT, :])).astype(dv_ref.dtype)
        else:
            sg = _sigmoid(conv)
            dconv = dyv * (sg * (1.0 + conv * (1.0 - sg)))
        dbuf[...] = dconv
        dx = w_ref[0:1, :] * dbuf[pl.ds(K - 1, tT), :]
        for k in range(1, K):
            dx = dx + w_ref[k:k + 1, :] * dbuf[pl.ds(K - 1 - k, tT), :]
        dx_ref[...] = dx.astype(dx_ref.dtype)
        dc = dconv[0:tT, :]
        for k in range(K):
            dw_ref[k:k + 1, :] += jnp.sum(dc * xbuf[pl.ds(HALO - (K - 1) + k, tT), :], axis=0, keepdims=True)
        if glu:
            db_ref[...] += jnp.sum(dc, axis=0, keepdims=True)

    def cur(off):
        return pl.BlockSpec((tT, tC), lambda j, i: (i, off + j))

    def nxt(off):
        return pl.BlockSpec((HALO, tC), lambda j, i: (jnp.minimum((i + 1) * hb, last_halo), off + j))

    prev = pl.BlockSpec((HALO, tC), lambda j, i: (jnp.maximum(i * hb - 1, 0), xb + j))
    wspec = pl.BlockSpec((K, tC), lambda j, i: (0, wb + j))
    acc_w = pl.BlockSpec((K, tC), lambda j, i: (0, j))
    acc_b = pl.BlockSpec((1, tC), lambda j, i: (0, j))
    in_specs = [cur(xb), prev, nxt(xb), cur(0), nxt(0), wspec]
    args = [x, x, x, dy, dy, w]
    out_specs = [cur(0)]
    out_shape = [jax.ShapeDtypeStruct((T, W), BF16)]
    scratch = [pltpu.VMEM((tT + 2 * HALO, tC), F32), pltpu.VMEM((tT + HALO, tC), F32), pltpu.VMEM((tT + HALO, tC), F32)]
    if glu:
        vb = voff // tC
        in_specs += [pl.BlockSpec((1, tC), lambda j, i: (0, wb + j)), cur(vb), nxt(vb)]
        args += [b, x, x]
        out_specs += [cur(0), acc_w, acc_b]
        out_shape += [jax.ShapeDtypeStruct((T, W), BF16), jax.ShapeDtypeStruct((K, W), F32),
                      jax.ShapeDtypeStruct((1, W), F32)]
        scratch += [pltpu.VMEM((tT + HALO, tC), F32)]
    else:
        out_specs += [acc_w]
        out_shape += [jax.ShapeDtypeStruct((K, W), F32)]
    return pl.pallas_call(
        body, name=name, grid=(W // tC, nT), in_specs=in_specs, out_specs=out_specs, out_shape=out_shape,
        scratch_shapes=scratch, compiler_params=_cp("parallel", "arbitrary"),
    )(*args)


V_PER_K = 2


GDN_PREP_CHUNKS = 4
GDN_SCAN_CHUNKS = 4


def _b3(a, b, mode='nn', precision=None):
    c = {'nn': ((2,), (1,)), 'nt': ((2,), (2,)), 'tn': ((1,), (1,))}[mode]
    return lax.dot_general(a, b, (c, ((0,), (0,))), precision=precision, preferred_element_type=F32)


def _bb3(a, b, mode='nn'):
    return _b3(a.astype(BF16), b.astype(BF16), mode)


def _hb3(a, b, mode='nn'):
    return _b3(a, b, mode, precision=HIGHEST)


def _split_bf16(a):
    hi = a.astype(BF16)
    return hi, (a - hi.astype(F32)).astype(BF16)


def _nb3(a, b, mode='nn'):
    ah, al = _split_bf16(a)
    bh, bl = _split_bf16(b)
    return _b3(ah, bh, mode) + _b3(ah, bl, mode) + _b3(al, bh, mode)


def _to_batch(x, nc):
    C = GDN_CHUNK
    return jnp.concatenate([x[:, j * HD:(j + 1) * HD].reshape(nc, C, HD) for j in range(V_PER_K)], axis=0)


def _from_batch(x, nc):
    C = GDN_CHUNK
    return jnp.concatenate([x[j * nc:(j + 1) * nc].reshape(nc * C, HD) for j in range(V_PER_K)], axis=1)


def _both_heads(x, nc):
    xc = x.reshape(nc, GDN_CHUNK, HD)
    return jnp.concatenate([xc] * V_PER_K, axis=0)


def _gdn_local(q2, k2, gb, bb):
    B, C, _ = k2.shape
    ri = lax.broadcasted_iota(jnp.int32, (B, C, C), 1)
    ci = lax.broadcasted_iota(jnp.int32, (B, C, C), 2)
    lower, strict = ri >= ci, ri > ci
    pick0 = (lax.broadcasted_iota(jnp.int32, (B, C, HD), 2) == 0).astype(F32)
    g_cols = _hb3(pick0, gb, 'nt')
    dm = jnp.exp(jnp.where(lower, gb[:, :, :C] - g_cols, NEG))
    kk = _bb3(k2, k2, 'nt')
    a = jnp.where(strict, kk * dm * bb[:, :, :C], 0.0)
    eg = jnp.exp(gb)
    gl = gb[:, C - 1:C, :]
    return dict(lower=lower, strict=strict, eye=(ri == ci).astype(F32), dm=dm, kk=kk, a=a, eg=eg, gl=gl,
                qd=q2 * eg, kd=k2 * jnp.exp(gl - gb))


def _gdn_specs(T, Hk, voff, nc, rev=False):
    C = GDN_CHUNK
    nb = T // (nc * C)
    vb = voff // (V_PER_K * HD)
    ix = (lambda i: nb - 1 - i) if rev else (lambda i: i)
    kspec = pl.BlockSpec((nc * C, HD), lambda h, i: (ix(i), h))
    pair = pl.BlockSpec((nc * C, V_PER_K * HD), lambda h, i: (ix(i), h))
    vspec = pl.BlockSpec((nc * C, V_PER_K * HD), lambda h, i: (ix(i), vb + h))
    cc = pl.BlockSpec((V_PER_K, nc, C, C), lambda h, i: (h, ix(i), 0, 0))
    state = pl.BlockSpec((V_PER_K, nc, HD, HD), lambda h, i: (h, ix(i), 0, 0))
    scal = pl.BlockSpec((V_PER_K, nc, SUBLANES, HD), lambda h, i: (h, ix(i), 0, 0))
    return nb, kspec, pair, vspec, cc, state, scal


def _gdn_prep(qn, kn, qkvc, voff, gcb, betab, Hk, name):
    T = qn.shape[0]
    C, nc = GDN_CHUNK, GDN_PREP_CHUNKS
    Hv, N = Hk * V_PER_K, T // C
    nb, kspec, pair, vspec, cc, _, _ = _gdn_specs(T, Hk, voff, nc)

    def body(q_ref, k_ref, v_ref, g_ref, b_ref, u_ref, w_ref, tm_ref, qkm_ref):
        q2, k2 = _both_heads(q_ref[...], nc), _both_heads(k_ref[...], nc)
        v2, gb, bb = _to_batch(v_ref[...], nc), _to_batch(g_ref[...], nc), _to_batch(b_ref[...], nc)
        lc = _gdn_local(q2, k2, gb, bb)
        p = -lc['a']
        tm = lc['eye'] + p
        for _ in range(5):
            p = _nb3(p, p)
            tm = tm + _nb3(tm, p)
        u_ref[...] = _from_batch(_nb3(tm, v2 * bb), nc)
        w_ref[...] = _from_batch(_nb3(tm, k2 * (bb * lc['eg'])), nc)
        tm_ref[...] = tm.reshape(V_PER_K, nc, C, C)
        qkm_ref[...] = jnp.where(lc['lower'], _bb3(q2, k2, 'nt') * lc['dm'], 0.0).reshape(V_PER_K, nc, C, C)

    return pl.pallas_call(
        body, name=name, grid=(Hk, nb), in_specs=[kspec, kspec, vspec, pair, pair], out_specs=[pair, pair, cc, cc],
        out_shape=[jax.ShapeDtypeStruct((T, Hv * HD), F32)] * 2 + [jax.ShapeDtypeStruct((Hv, N, C, C), F32)] * 2,
        compiler_params=_cp("parallel", "parallel"),
    )(qn, kn, qkvc, gcb, betab)


SCAN_K_HEADS = 2
SCAN_V_HEADS = SCAN_K_HEADS * V_PER_K


def _heads(ref, rows, per=1):
    return jnp.stack([ref[rows, (j // per) * HD:(j // per + 1) * HD] for j in range(SCAN_V_HEADS)])


def _put_heads(ref, rows, x):
    for j in range(SCAN_V_HEADS):
        ref[rows, j * HD:(j + 1) * HD] = x[j]


def _scan_chunk(q_ref, k_ref, g_ref, rows):
    C = GDN_CHUNK
    gb = _heads(g_ref, rows)
    gl = gb[:, C - 1:C, :]
    return _heads(q_ref, rows, V_PER_K) * jnp.exp(gb), _heads(k_ref, rows, V_PER_K) * jnp.exp(gl - gb), jnp.exp(gl)


def _scan_specs(T, ns, rev=False):
    C = GDN_CHUNK
    nb = T // (ns * C)
    ix = (lambda i: nb - 1 - i) if rev else (lambda i: i)
    kspec = pl.BlockSpec((ns * C, SCAN_K_HEADS * HD), lambda h, i: (ix(i), h))
    vspec = pl.BlockSpec((ns * C, SCAN_V_HEADS * HD), lambda h, i: (ix(i), h))
    per_chunk = lambda *tail: pl.BlockSpec((SCAN_V_HEADS, ns) + tail, lambda h, i: (h, ix(i), 0, 0))
    return nb, kspec, vspec, per_chunk(C, C), per_chunk(HD, HD), per_chunk(SUBLANES, HD)


def _gdn_scan(qn, kn, gcb, u, w, qkm, Hk, name):
    T = qn.shape[0]
    C, ns = GDN_CHUNK, GDN_SCAN_CHUNKS
    Hv, N = Hk * V_PER_K, T // C
    nb, kspec, pair, cc, state, _ = _scan_specs(T, ns)

    def body(q_ref, k_ref, g_ref, u_ref, w_ref, qkm_ref, o_ref, sp_ref, s_s):
        @pl.when(pl.program_id(1) == 0)
        def _():
            s_s[...] = jnp.zeros_like(s_s)

        s = s_s[...]
        for t in range(ns):
            rows = slice(t * C, (t + 1) * C)
            qd, kd, egl = _scan_chunk(q_ref, k_ref, g_ref, rows)
            sp_ref[:, t] = s
            vn = _heads(u_ref, rows) - _bb3(_heads(w_ref, rows), s)
            _put_heads(o_ref, rows, _bb3(qd, s) + _bb3(qkm_ref[:, t], vn))
            s = s * egl + _bb3(kd, vn, 'tn')
        s_s[...] = s

    return pl.pallas_call(
        body, name=name, grid=(Hk // SCAN_K_HEADS, nb), in_specs=[kspec, kspec, pair, pair, pair, cc],
        out_specs=[pair, state],
        out_shape=[jax.ShapeDtypeStruct((T, Hv * HD), F32), jax.ShapeDtypeStruct((Hv, N, HD, HD), F32)],
        scratch_shapes=[pltpu.VMEM((SCAN_V_HEADS, HD, HD), F32)], compiler_params=_cp("parallel", "arbitrary"),
    )(qn, kn, gcb, u, w, qkm)


def _gdn_scan_bwd(qn, kn, gcb, u, w, qkm, sprev, do, Hk, name):
    T = qn.shape[0]
    C, ns = GDN_CHUNK, GDN_SCAN_CHUNKS
    Hv, N = Hk * V_PER_K, T // C
    nb, kspec, pair, cc, state, scal = _scan_specs(T, ns, rev=True)
    nv = SCAN_V_HEADS

    def body(q_ref, k_ref, g_ref, u_ref, w_ref, qkm_ref, sp_ref, do_ref,
             dqd_ref, dkd_ref, du_ref, dw_ref, dqkm_ref, dgl_ref, ds_s):
        @pl.when(pl.program_id(1) == 0)
        def _():
            ds_s[...] = jnp.zeros_like(ds_s)

        lower = lax.broadcasted_iota(jnp.int32, (nv, C, C), 1) >= lax.broadcasted_iota(jnp.int32, (nv, C, C), 2)
        ds = ds_s[...]
        for t in reversed(range(ns)):
            rows = slice(t * C, (t + 1) * C)
            qd, kd, egl = _scan_chunk(q_ref, k_ref, g_ref, rows)
            s, w_, qkm_, dov = sp_ref[:, t], _heads(w_ref, rows), qkm_ref[:, t], _heads(do_ref, rows)
            vn = _heads(u_ref, rows) - _bb3(w_, s)
            _put_heads(dqd_ref, rows, _bb3(dov, s, 'nt'))
            dqkm_ref[:, t] = jnp.where(lower, _bb3(dov, vn, 'nt'), 0.0)
            dvn = _bb3(qkm_, dov, 'tn') + _bb3(kd, ds)
            _put_heads(dkd_ref, rows, _bb3(vn, ds, 'nt'))
            dgl = jnp.sum(jnp.sum(ds * s, axis=1, keepdims=True) * egl, axis=2, keepdims=True)
            dgl_ref[:, t] = jnp.broadcast_to(dgl, (nv, SUBLANES, HD))
            _put_heads(du_ref, rows, dvn)
            _put_heads(dw_ref, rows, -_bb3(dvn, s, 'nt'))
            ds = ds * egl + _bb3(qd, dov, 'tn') - _bb3(w_, dvn, 'tn')
        ds_s[...] = ds

    return pl.pallas_call(
        body, name=name, grid=(Hk // SCAN_K_HEADS, nb), in_specs=[kspec, kspec, pair, pair, pair, cc, state, pair],
        out_specs=[pair] * 4 + [cc, scal],
        out_shape=[jax.ShapeDtypeStruct((T, Hv * HD), F32)] * 4
        + [jax.ShapeDtypeStruct((Hv, N, C, C), F32), jax.ShapeDtypeStruct((Hv, N, SUBLANES, HD), F32)],
        scratch_shapes=[pltpu.VMEM((nv, HD, HD), F32)], compiler_params=_cp("parallel", "arbitrary"),
    )(qn, kn, gcb, u, w, qkm, sprev, do)


def _gdn_prep_bwd(qn, kn, qkvc, voff, gcb, betab, tm, u, w, qkm, dqd, dkd, du, dw, dqkm, dgl, Hk, name):
    T = qn.shape[0]
    C, nc = GDN_CHUNK, GDN_PREP_CHUNKS
    Hv = Hk * V_PER_K
    B = V_PER_K * nc
    nb, kspec, pair, vspec, cc, _, scal = _gdn_specs(T, Hk, voff, nc)

    def body(q_ref, k_ref, v_ref, g_ref, b_ref, tm_ref, u_ref, w_ref, qkm_ref, dqd_ref, dkd_ref, du_ref, dw_ref,
             dqkm_ref, dgl_ref, dq_ref, dk_ref, dv_ref, dg_ref, dbe_ref):
        q2, k2 = _both_heads(q_ref[...], nc), _both_heads(k_ref[...], nc)
        v2, gb, bb = _to_batch(v_ref[...], nc), _to_batch(g_ref[...], nc), _to_batch(b_ref[...], nc)
        lc = _gdn_local(q2, k2, gb, bb)
        dm, eg, gl = lc['dm'], lc['eg'], lc['gl']
        tm_, qkm_, dqkm_ = (r[...].reshape(B, C, C) for r in (tm_ref, qkm_ref, dqkm_ref))
        u_, w_, dqd_, dkd_, du_, dw_ = (_to_batch(r[...], nc) for r in (u_ref, w_ref, dqd_ref, dkd_ref, du_ref, dw_ref))
        dgl_ = dgl_ref[...].reshape(B, SUBLANES, HD)[:, :1, :1]
        rowsum = lambda x: jnp.sum(x, axis=-1, keepdims=True)
        dbv = _nb3(tm_, du_, 'tn')
        dbk = _nb3(tm_, dw_, 'tn')
        da = jnp.where(lc['strict'], -(_bb3(dbv, u_, 'nt') + _bb3(dbk, w_, 'nt')), 0.0)
        rk = rowsum(dbk * k2)
        dbeta = rowsum(dbv * v2) + rk * eg[:, :, :1] + rowsum(da * lc['kk'] * dm)
        dkk = da * dm * bb[:, :, :C]
        dqkr = dqkm_ * dm
        dk = dbk * (bb * eg) + _bb3(dkk, k2) + _bb3(dkk, k2, 'tn') + _bb3(dqkr, q2, 'tn') + dkd_ * jnp.exp(gl - gb)
        dq = _bb3(dqkr, k2) + dqd_ * eg
        de = da * lc['a'] + dqkm_ * qkm_
        sk = rowsum(dkd_ * lc['kd'])
        dg = rk * (bb[:, :, :1] * eg[:, :, :1]) + rowsum(de) - _hb3(de, jnp.ones((B, C, HD), F32), 'tn')[:, :, :1] \
            + rowsum(dqd_ * lc['qd']) - sk
        last = (lax.broadcasted_iota(jnp.int32, (B, C, HD), 1) == C - 1).astype(F32)
        dgb = jnp.broadcast_to(dg, (B, C, HD)) + last * (dgl_ + jnp.sum(sk, axis=1, keepdims=True))
        suffix = (lax.broadcasted_iota(jnp.int32, (B, C, C), 2) >= lax.broadcasted_iota(jnp.int32, (B, C, C), 1)).astype(F32)
        dq_ref[...] = _from_batch(dq, nc)
        dk_ref[...] = _from_batch(dk, nc)
        dv_ref[...] = _from_batch(dbv * bb, nc)
        dg_ref[...] = _from_batch(_hb3(suffix, dgb), nc)
        dbe_ref[...] = _from_batch(jnp.broadcast_to(dbeta, (B, C, HD)), nc)

    return pl.pallas_call(
        body, name=name, grid=(Hk, nb),
        in_specs=[kspec, kspec, vspec, pair, pair, cc, pair, pair, cc, pair, pair, pair, pair, cc, scal],
        out_specs=[pair] * 5, out_shape=[jax.ShapeDtypeStruct((T, Hv * HD), F32)] * 5,
        compiler_params=_cp("parallel", "parallel"),
    )(qn, kn, qkvc, gcb, betab, tm, u, w, qkm, dqd, dkd, du, dw, dqkm, dgl)


def _adamw_math(w, g, m, v):
    m = ADAM_B1 * m + (1.0 - ADAM_B1) * g
    v = ADAM_B2 * v + (1.0 - ADAM_B2) * jnp.square(g)
    m_hat = m / (1.0 - ADAM_B1 ** ADAM_STEP)
    v_hat = v / (1.0 - ADAM_B2 ** ADAM_STEP)
    delta = -ADAM_LR * (m_hat / (jnp.sqrt(v_hat) + ADAM_EPS) + ADAM_WD * w)
    return delta, m, v


STREAM_BLOCK_BYTES = 1 << 20


def _stream_rows(R, C, mult=SUBLANES):
    for tr in (512, 256, 128, 64, 32, 16, 8):
        if R % tr == 0 and tr % mult == 0 and tr * C * 4 <= STREAM_BLOCK_BYTES:
            return tr
    return R


def _adamw(w, m, v, gs, name):
    R, C = w.shape
    tr = _stream_rows(R, C)
    n = len(gs)

    def body(*refs):
        w_ref, m_ref, v_ref = refs[:3]
        g_refs = refs[3:3 + n]
        g_out, d_out, m_out, v_out = refs[3 + n:]
        g = g_refs[0][...]
        for r_ in g_refs[1:]:
            g = g + r_[...]
        g_out[...] = g
        d_out[...], m_out[...], v_out[...] = _adamw_math(w_ref[...], g, m_ref[...], v_ref[...])

    spec = pl.BlockSpec((tr, C), lambda i: (i, 0))
    return pl.pallas_call(
        body, name=name, grid=(R // tr,), in_specs=[spec] * (3 + n), out_specs=[spec] * 4,
        out_shape=[jax.ShapeDtypeStruct((R, C), F32)] * 4, compiler_params=_cp("parallel"),
    )(w, m, v, *gs)


def _sum_devices(g8, name):
    _, M, C = g8.shape
    tr = _tile(M, (512, 256, 128, 64, 32, 16, 8))

    def body(g_ref, o_ref):
        acc = g_ref[0]
        for d in range(1, N_DEV):
            acc = acc + g_ref[d]
        o_ref[...] = acc

    return pl.pallas_call(
        body, name=name, grid=(M // tr,), in_specs=[pl.BlockSpec((N_DEV, tr, C), lambda i: (0, i, 0))],
        out_specs=pl.BlockSpec((tr, C), lambda i: (i, 0)), out_shape=jax.ShapeDtypeStruct((M, C), F32),
        compiler_params=_cp("parallel"),
    )(g8)


def _ada_w_update(c_all, dm, w, m, v, name):
    n_mod, D, Ns = w.shape
    tr = _tile(D, (256, 128))

    def body(c_ref, dm_ref, w_ref, m_ref, v_ref, g_out, d_out, m_out, v_out):
        g = _hdot(_silu(c_ref[...]), dm_ref[...], 'tn')
        g_out[...] = g
        d_out[...], m_out[...], v_out[...] = _adamw_math(w_ref[...], g, m_ref[...], v_ref[...])

    wspec = pl.BlockSpec((None, tr, Ns), lambda i, r: (i, r, 0))
    return pl.pallas_call(
        body, name=name, grid=(n_mod, D // tr),
        in_specs=[pl.BlockSpec((N_DEV, tr), lambda i, r: (0, r)), pl.BlockSpec((None, N_DEV, Ns), lambda i, r: (i, 0, 0)),
                  wspec, wspec, wspec],
        out_specs=[wspec] * 4, out_shape=[jax.ShapeDtypeStruct((n_mod, D, Ns), F32)] * 4,
        compiler_params=_cp("parallel", "parallel"),
    )(c_all, dm, w, m, v)


def _place():
    return lax.axis_index("x"), lax.axis_index("y"), lax.axis_index("c")


def _allgather8(x_shard, name):
    m_per, n = x_shard.shape

    def body(x_ref, out_ref, send_sems, recv_sems, local_sem):
        x, y, c = _place()
        me, sibling = (x, y, c), (x, y, 1 - c)
        chips = [(1 - x, y), (x, 1 - y), (1 - x, 1 - y)]

        def rows(px, py, pc):
            return out_ref.at[pl.ds((4 * px + 2 * py + pc) * m_per, m_per), :]

        def copy(k, block, to, src=None):
            return pltpu.make_async_remote_copy(
                src_ref=rows(*block) if src is None else src, dst_ref=rows(*block),
                send_sem=send_sems.at[k], recv_sem=recv_sems.at[k], device_id=to, device_id_type=MESH)

        mine = pltpu.make_async_copy(x_ref, rows(*me), local_sem)
        mine.start()
        first = [copy(0, me, sibling, src=x_ref)]
        first += [copy(1 + j, me, (*chip, c), src=x_ref) for j, chip in enumerate(chips)]
        for cp in first:
            cp.start()
        passed = [copy(4 + j, (*chip, c), sibling) for j, chip in enumerate(chips)]
        for j, chip in enumerate(chips):
            copy(1 + j, (*chip, c), me).wait_recv()
            passed[j].start()
        copy(0, sibling, me).wait_recv()
        for j, chip in enumerate(chips):
            copy(4 + j, (*chip, 1 - c), me).wait_recv()
        for cp in first + passed:
            cp.wait_send()
        mine.wait()

    return pl.pallas_call(
        body, name=name, out_shape=jax.ShapeDtypeStruct((N_DEV * m_per, n), x_shard.dtype),
        in_specs=[pl.BlockSpec(memory_space=pltpu.VMEM)], out_specs=pl.BlockSpec(memory_space=pltpu.VMEM),
        scratch_shapes=[pltpu.SemaphoreType.DMA((7,)), pltpu.SemaphoreType.DMA((7,)), pltpu.SemaphoreType.DMA],
        compiler_params=pltpu.CompilerParams(vmem_limit_bytes=VMEM_LIMIT),
    )(x_shard)


HBM_SPEC = pl.BlockSpec(memory_space=pltpu.HBM)


GATHER_SEMS = 6


def _gather_protocol(w_ref, out_ref, send_sems, recv_sems, k0):
    half = w_ref.shape[0] // 2
    x, y, c = _place()
    me, sibling = (x, y, c), (x, y, 1 - c)
    chips = [(1 - x, y), (x, 1 - y), (1 - x, 1 - y)]

    def part(cx, cy, hc):
        return out_ref.at[2 * cx + cy, pl.ds(hc * half, half), :]

    def copy(k, block, to, src=None):
        return pltpu.make_async_remote_copy(
            src_ref=part(*block) if src is None else src, dst_ref=part(*block),
            send_sem=send_sems.at[k0 + k], recv_sem=recv_sems.at[k0 + k], device_id=to, device_id_type=MESH)

    def first():
        return [copy(j, me, (*chip, c), src=w_ref.at[pl.ds(c * half, half), :]) for j, chip in enumerate(chips)]

    def start():
        for cp in first():
            cp.start()

    def finish():
        passed = [copy(3 + j, (*chip, c), sibling) for j, chip in enumerate(chips)]
        for j, chip in enumerate(chips):
            copy(j, (*chip, c), me).wait_recv()
            passed[j].start()
        for j, chip in enumerate(chips):
            copy(3 + j, (*chip, 1 - c), me).wait_recv()
        for cp in first() + passed:
            cp.wait_send()

    return start, finish


def _gather_weights(w_flat, name):
    R, C = w_flat.shape

    def body(w_ref, out_ref, send_sems, recv_sems):
        start, finish = _gather_protocol(w_ref, out_ref, send_sems, recv_sems, 0)
        start()
        finish()

    return pl.pallas_call(
        body, name=name, out_shape=jax.ShapeDtypeStruct((N_CHIPS, R, C), w_flat.dtype),
        in_specs=[HBM_SPEC], out_specs=HBM_SPEC,
        scratch_shapes=[pltpu.SemaphoreType.DMA((GATHER_SEMS,)), pltpu.SemaphoreType.DMA((GATHER_SEMS,))],
    )(w_flat)


def _swap_halves(g, name):
    n, R, C = g.shape
    half = R // 2

    def body(g_ref, got_ref, send_sem, recv_sem):
        x, y, c = _place()
        cp = pltpu.make_async_remote_copy(
            src_ref=g_ref.at[:, pl.ds((1 - c) * half, half), :], dst_ref=got_ref,
            send_sem=send_sem, recv_sem=recv_sem, device_id=(x, y, 1 - c), device_id_type=MESH)
        cp.start()
        cp.wait()

    return pl.pallas_call(
        body, name=name, out_shape=jax.ShapeDtypeStruct((n, half, C), g.dtype),
        in_specs=[HBM_SPEC], out_specs=HBM_SPEC,
        scratch_shapes=[pltpu.SemaphoreType.DMA, pltpu.SemaphoreType.DMA],
    )(g)


SCATTER_SEMS = 3


def _scatter_protocol(q_ref, got_ref, send_sems, recv_sems, k0):
    x, y, c = _place()

    def copies():
        return [pltpu.make_async_remote_copy(
            src_ref=q_ref.at[2 * cx + cy], dst_ref=got_ref.at[j], send_sem=send_sems.at[k0 + j],
            recv_sem=recv_sems.at[k0 + j], device_id=(cx, cy, c), device_id_type=MESH)
            for j, (cx, cy) in enumerate([(1 - x, y), (x, 1 - y), (1 - x, 1 - y)])]

    def start():
        for cp in copies():
            cp.start()

    def finish():
        for cp in copies():
            cp.wait()

    return start, finish


def _scatter_shapes(qs):
    return [jax.ShapeDtypeStruct((N_CHIPS - 1,) + q.shape[1:], q.dtype) for q in qs]


def _scatter_chips(q, name):
    def body(q_ref, got_ref, send_sems, recv_sems):
        start, finish = _scatter_protocol(q_ref, got_ref, send_sems, recv_sems, 0)
        start()
        finish()

    return pl.pallas_call(
        body, name=name, out_shape=_scatter_shapes([q])[0], in_specs=[HBM_SPEC], out_specs=HBM_SPEC,
        scratch_shapes=[pltpu.SemaphoreType.DMA((SCATTER_SEMS,)), pltpu.SemaphoreType.DMA((SCATTER_SEMS,))],
    )(q)


def _join_halves(h, name):
    R, C = h.shape
    R2 = R // 2

    def body(h_ref, out_ref, send_sem, recv_sem):
        x, y, c = _place()
        cp = pltpu.make_async_remote_copy(
            src_ref=h_ref.at[pl.ds(c * R2, R2), :], dst_ref=out_ref.at[pl.ds(c * R2, R2), :],
            send_sem=send_sem, recv_sem=recv_sem, device_id=(x, y, 1 - c), device_id_type=MESH)
        cp.start()
        cp.wait()

    return pl.pallas_call(
        body, name=name, out_shape=jax.ShapeDtypeStruct((R, C), h.dtype),
        in_specs=[HBM_SPEC], out_specs=HBM_SPEC, input_output_aliases={0: 0},
        scratch_shapes=[pltpu.SemaphoreType.DMA, pltpu.SemaphoreType.DMA],
    )(h)


def _add_halves(g, got, c_idx, name):
    n, R, C = g.shape
    half = R // 2
    tr = _stream_rows(half, C, 2 * SUBLANES)
    nb = half // tr

    def body(c_ref, g_ref, got_ref, o_ref):
        o_ref[...] = (g_ref[...] + got_ref[...]).astype(o_ref.dtype)

    grid_spec = pltpu.PrefetchScalarGridSpec(
        num_scalar_prefetch=1, grid=(n, nb),
        in_specs=[pl.BlockSpec((None, tr, C), lambda s, i, c_ref: (s, c_ref[0] * nb + i, 0)),
                  pl.BlockSpec((None, tr, C), lambda s, i, c_ref: (s, i, 0))],
        out_specs=pl.BlockSpec((None, tr, C), lambda s, i, c_ref: (s, i, 0)))
    return pl.pallas_call(
        body, name=name, grid_spec=grid_spec, out_shape=jax.ShapeDtypeStruct((n, half, C), BF16),
        compiler_params=_cp("parallel", "parallel"),
    )(c_idx, g, got)


def _add_chips(q, got, sc_idx, name):
    n, R2, C = q.shape
    tr = _stream_rows(R2, C, 2 * SUBLANES)
    nb = R2 // tr

    def body(s_ref, q_ref, g0_ref, g1_ref, g2_ref, o_ref):
        o_ref[...] = ((q_ref[...].astype(F32) + g0_ref[...].astype(F32)) + g1_ref[...].astype(F32)) \
            + g2_ref[...].astype(F32)

    def got_spec(j):
        return pl.BlockSpec((None, tr, C), lambda i, s_ref: (j, i, 0))

    grid_spec = pltpu.PrefetchScalarGridSpec(
        num_scalar_prefetch=1, grid=(nb,),
        in_specs=[pl.BlockSpec((None, tr, C), lambda i, s_ref: (s_ref[0], i, 0)), got_spec(0), got_spec(1), got_spec(2)],
        out_specs=pl.BlockSpec((tr, C), lambda i, s_ref: (s_ref[1] * nb + i, 0)))
    return pl.pallas_call(
        body, name=name, grid_spec=grid_spec, out_shape=jax.ShapeDtypeStruct((2 * R2, C), F32),
        compiler_params=_cp("parallel"),
    )(sc_idx, q, got, got, got)


def _pack_lanes(arrs):
    rows = []
    for a in arrs:
        f = a.reshape(-1)
        n = -(-f.shape[0] // LANES) * LANES
        rows.append(jnp.pad(f, (0, n - f.shape[0])).reshape(-1, LANES))
    out = jnp.concatenate(rows, axis=0)
    pad = -out.shape[0] % SUBLANES
    return jnp.pad(out, ((0, pad), (0, 0)))


def _unpack_lanes(packed, shapes):
    out, r = [], 0
    for shp in shapes:
        n = math.prod(shp)
        nr = -(-n // LANES)
        out.append(packed[r:r + nr].reshape(-1)[:n].reshape(shp))
        r += nr
    return out


def _shards_to_full(sh, axis):
    return jnp.concatenate([sh[i] for i in range(N_CHIPS)], axis=axis)


def _full_to_shards(full, axis):
    return jnp.stack(jnp.split(full, N_CHIPS, axis=axis), axis=0)


def _pad_cols(a, n):
    return jnp.pad(a, ((0, 0), (0, n - a.shape[1])))


def _lane_bcast(a):
    return jnp.repeat(a, HD, axis=1)


def _split_mod(mod):
    D = mod.shape[0] // 3
    return mod[None, :D], mod[None, D:2 * D], mod[None, 2 * D:]


def _fox_fwd(h, w, tag, carry=()):
    T, D = h.shape
    H = D // HD
    proj = _mm(h, w['cat'], 'nn', F32, tag + '_proj')
    flog = _mm(h, w['f'], 'nn', F32, tag + '_flog')
    qn = _headnorm(proj, 0, H, w['q_norm'], 1.0 / HD, HD ** -0.5, BF16, tag + '_qnorm')
    kn = _headnorm(proj, H, H, w['k_norm'], 1.0 / HD, 1.0, BF16, tag + '_knorm')
    vb = proj[:, 2 * D:3 * D].astype(BF16)
    fcum = _cumsum(_logsig(flog, w['f_bias'], tag + '_logf'), None, False, tag + '_fcum')
    tq = _tile(T, ATT_TILES)
    f0 = fcum[::tq, :H].T.reshape(H, T // tq, 1, 1)
    fkr = fcum[:, :H].T.reshape(H, 1, T)
    ao, lse, carried = _flash_fwd(qn, kn, vb, f0, fkr, H, tag + '_att', carry)
    gated = _fox_gate(ao, proj, 3 * H, H, tag + '_ogate')
    y = _mm(gated, w['o'], 'nn', F32, tag + '_out')
    sv = dict(h=h, proj=proj, flog=flog, qn=qn, kn=kn, vb=vb, f0=f0, fkr=fkr, ao=ao, lse=lse, gated=gated)
    return (y, sv, carried) if carry else (y, sv)


def _mm_carry(a, b, mode, out_dtype, name, scatter):
    if scatter:
        return _mm(a, b, mode, out_dtype, name, scatter=tuple(scatter))
    return _mm(a, b, mode, out_dtype, name), []


def _fox_bwd(dy, w, sv, tag, scatter=()):
    h, proj = sv['h'], sv['proj']
    T, D = h.shape
    H = D // HD
    g = {}
    g['o'] = _mm(sv['gated'], dy, 'tn', F32, tag + '_dwo')
    dgated = _mm(dy, w['o'], 'nt', F32, tag + '_dgated')
    dao, dog, delta = _fox_gate_bwd(dgated, sv['ao'], proj, 3 * H, H, tag + '_ogate_bwd')
    dq, dfq, dk, dv, dfk = _flash_bwd(sv['qn'], sv['kn'], sv['vb'], sv['f0'], sv['fkr'], dao, sv['lse'], delta, H,
                                      tag + '_att_bwd')
    dfcum = _pad_cols(dfq[:, ::HD] + dfk.reshape(H, T).T, LANES)
    dlogf = _cumsum(dfcum, None, True, tag + '_fcum_bwd')
    dflog, g['f_bias'] = _logsig_bwd(dlogf, sv['flog'], w['f_bias'], tag + '_logf_bwd')
    dqr, g['q_norm'] = _headnorm_bwd([(dq, 1, 0)], proj, 0, H, w['q_norm'], 1.0 / HD, HD ** -0.5, BF16,
                                     tag + '_qnorm_bwd')
    dkr, g['k_norm'] = _headnorm_bwd([(dk, 1, 0)], proj, H, H, w['k_norm'], 1.0 / HD, 1.0, BF16, tag + '_knorm_bwd')
    dproj = jnp.concatenate([dqr, dkr, dv, dog], axis=1)
    g['cat'], got0 = _mm_carry(h, dproj, 'tn', F32, tag + '_dwcat', scatter[:1])
    g['f'] = _mm(h, dflog, 'tn', F32, tag + '_dwf')
    dh, got1 = _mm_carry(dproj, w['cat'], 'nt', F32, tag + '_dh', scatter[1:])
    return [dh, _mm(dflog, w['f'], 'nt', F32, tag + '_dh_f')], g, got0 + got1


def _gdn_fwd(h, w, tag):
    T, D = h.shape
    Hk = D // HD
    Hv = V_PER_K * Hk
    proj = _mm(h, w['cat'], 'nn', F32, tag + '_proj')
    ab = _mm(h, w['ab'], 'nn', F32, tag + '_ab')
    qkvc = _dwconv(proj, 0, 4 * D, w['conv'], None, 'silu', 0, F32, tag + '_conv')
    qn = _headnorm(qkvc, 0, Hk, None, 1.0, HD ** -0.5, F32, tag + '_qnorm')
    kn = _headnorm(qkvc, Hk, Hk, None, 1.0, 1.0, F32, tag + '_knorm')
    graw, beta = _gdn_gates(ab, w['a_log'], w['dt_bias'], tag + '_gates')
    gc = _cumsum(graw, GDN_CHUNK, False, tag + '_gcum')
    gcb = _lane_bcast(gc[:, :Hv])
    betab = _lane_bcast(beta[:, Hv:2 * Hv])
    u, wk, tm, qkm = _gdn_prep(qn, kn, qkvc, 2 * D, gcb, betab, Hk, tag + '_prep')
    o, sprev = _gdn_scan(qn, kn, gcb, u, wk, qkm, Hk, tag + '_scan')
    go = _gdn_out(o, proj, 4 * Hk, Hv, w['out_norm'], tag + '_onorm')
    y = _mm(go, w['o'], 'nn', F32, tag + '_out')
    return y, dict(h=h, proj=proj, ab=ab, qkvc=qkvc, qn=qn, kn=kn, gcb=gcb, betab=betab, o=o, sprev=sprev, go=go,
                   u=u, wk=wk, tm=tm, qkm=qkm)


def _gdn_bwd(dy, w, sv, tag, scatter=()):
    h, proj, qkvc = sv['h'], sv['proj'], sv['qkvc']
    T, D = h.shape
    Hk = D // HD
    Hv = V_PER_K * Hk
    g = {}
    g['o'] = _mm(sv['go'], dy, 'tn', F32, tag + '_dwo')
    dgo = _mm(dy, w['o'], 'nt', F32, tag + '_dgo')
    do, dz, g['out_norm'] = _gdn_out_bwd(dgo, sv['o'], proj, 4 * Hk, Hv, w['out_norm'], tag + '_onorm_bwd')
    local = (sv['u'], sv['wk'], sv['qkm'])
    dqd, dkd, du, dwk, dqkm, dgl = _gdn_scan_bwd(sv['qn'], sv['kn'], sv['gcb'], *local, sv['sprev'], do, Hk,
                                                 tag + '_scan_bwd')
    dqp, dkp, dv, dgb, dbetab = _gdn_prep_bwd(sv['qn'], sv['kn'], qkvc, 2 * D, sv['gcb'], sv['betab'], sv['tm'], *local,
                                              dqd, dkd, du, dwk, dqkm, dgl, Hk, tag + '_prep_bwd')
    pairs = lambda a: [(a, V_PER_K, j) for j in range(V_PER_K)]
    dqc, _ = _headnorm_bwd(pairs(dqp), qkvc, 0, Hk, None, 1.0, HD ** -0.5, F32, tag + '_qnorm_bwd')
    dkc, _ = _headnorm_bwd(pairs(dkp), qkvc, Hk, Hk, None, 1.0, 1.0, F32, tag + '_knorm_bwd')
    zeros = jnp.zeros((T, Hv), F32)
    dg_pad = _pad_cols(dgb[:, ::HD], LANES)
    dbeta_pad = _pad_cols(jnp.concatenate([zeros, dbetab[:, ::HD]], axis=1), LANES)
    dab, g['a_log'], g['dt_bias'] = _gdn_gates_bwd(dg_pad, dbeta_pad, sv['ab'], w['a_log'], w['dt_bias'], tag + '_gates_bwd')
    dpq, dwq = _dwconv_bwd(proj, 0, D, w['conv'], 0, None, 'silu', 0, dqc, tag + '_conv_bwd_q')
    dpk, dwk = _dwconv_bwd(proj, D, D, w['conv'], D, None, 'silu', 0, dkc, tag + '_conv_bwd_k')
    dpv, dwv = _dwconv_bwd(proj, 2 * D, 2 * D, w['conv'], 2 * D, None, 'silu', 0, dv, tag + '_conv_bwd_v')
    g['conv'] = jnp.concatenate([dwq, dwk, dwv], axis=1)
    dproj = jnp.concatenate([dpq, dpk, dpv, dz], axis=1)
    g['cat'], got0 = _mm_carry(h, dproj, 'tn', F32, tag + '_dwcat', scatter[:1])
    g['ab'] = _mm(h, dab, 'tn', F32, tag + '_dwab')
    dh, got1 = _mm_carry(dproj, w['cat'], 'nt', F32, tag + '_dh', scatter[1:])
    return [dh, _mm(dab, w['ab'], 'nt', F32, tag + '_dh_ab')], g, got0 + got1


def _ffn_fwd(h, w, tag):
    dff = w['down'].shape[0]
    up = _mm(h, w['up'], 'nn', F32, tag + '_up')
    act = _dwconv(up, 0, dff, w['conv'], w['conv_b'], 'glu', dff, BF16, tag + '_conv')
    y = _mm(act, w['down'], 'nn', F32, tag + '_down')
    return y, dict(h=h, up=up, act=act)


def _ffn_bwd(dy, w, sv, tag, scatter=()):
    h, up = sv['h'], sv['up']
    dff = w['down'].shape[0]
    g = {}
    g['down'] = _mm(sv['act'], dy, 'tn', F32, tag + '_dwdown')
    dact = _mm(dy, w['down'], 'nt', F32, tag + '_dact')
    dgate, dval, g['conv'], g['conv_b'] = _dwconv_bwd(up, 0, dff, w['conv'], 0, w['conv_b'], 'glu', dff, dact,
                                                      tag + '_conv_bwd')
    dup = jnp.concatenate([dgate, dval], axis=1)
    g['up'], got0 = _mm_carry(h, dup, 'tn', F32, tag + '_dwup', scatter[:1])
    dh, got1 = _mm_carry(dup, w['up'], 'nt', F32, tag + '_dh', scatter[1:])
    return [dh], g, got0 + got1


def _local_step(x, target, mods, norm_g, wf, wg, wffn, late=None, reduce_pairs=None):
    tape = []
    for i in range(2):
        for sub in range(2):
            if sub == 0:
                fwd, bwd, w, tag = [(_fox_fwd, _fox_bwd, wf, 'fox'), (_gdn_fwd, _gdn_bwd, wg, 'gdn')][i]
            else:
                fwd, bwd, w, tag = _ffn_fwd, _ffn_bwd, wffn[i], 'ffn%d' % i
            shift, scale, gate = _split_mod(mods[i, sub])
            g_pre, g_post = norm_g[i, 2 * sub][None], norm_g[i, 2 * sub + 1][None]
            h = _pre_norm(x, g_pre, scale, shift, tag + '_prenorm')
            if late is not None and (i, sub) == (0, 0):
                y, sv, gathered = fwd(h, w, tag, late[0])
                wg, wffn = late[1](gathered)
            else:
                y, sv = fwd(h, w, tag)
            x_out = _post_res(x, y, gate, g_post, tag + '_postnorm')
            tape.append((bwd, w, tag, sv, x, y, g_pre, g_post, scale, gate))
            x = x_out
    dx, lsum = _loss_head(x, target, 'loss_head')
    loss = lsum[0, 0]
    dmods = [[None, None], [None, None]]
    dnorm = [[None] * 4, [None] * 4]
    wgrads = {}
    pending, exchanged = [], {}
    for idx in reversed(range(4)):
        i, sub = divmod(idx, 2)
        bwd, w, tag, sv, x_in, y, g_pre, g_post, scale, gate = tape[idx]
        dy, dgate, dgpost = _post_res_bwd(dx, y, gate, g_post, tag + '_postnorm_bwd')
        dh, wgrads[tag], got = bwd(dy, w, sv, tag, [q for _, q in pending])
        exchanged.update({key: (q, r) for (key, q), r in zip(pending, got)})
        pending = reduce_pairs(tag, wgrads[tag]) if reduce_pairs else []
        dx, dshift, dscale, dgpre = _pre_norm_bwd(dh, x_in, g_pre, scale, dx, tag + '_prenorm_bwd')
        dmods[i][sub] = jnp.concatenate([dshift[0], dscale[0], dgate[0]])
        dnorm[i][2 * sub], dnorm[i][2 * sub + 1] = dgpre[0], dgpost[0]
    exchanged.update({key: (q, _scatter_chips(q, '%s%d_to_chips' % key)) for key, q in pending})
    dmods = jnp.stack([jnp.stack(r) for r in dmods])
    dnorm = jnp.stack([jnp.stack(r) for r in dnorm])
    return loss, dx, dmods, dnorm, wgrads, exchanged


def _unpack_lanes_dev(packed, shapes):
    n = packed.shape[0]
    out, r = [], 0
    for shp in shapes:
        k = math.prod(shp)
        nr = -(-k // LANES)
        out.append(packed[:, r:r + nr].reshape(n, -1)[:, :k].reshape((n,) + tuple(shp)))
        r += nr
    return out


def _gather_lanes(arrs, name):
    packed = _pack_lanes(arrs)
    got = _allgather8(packed, name).reshape(N_DEV, packed.shape[0], LANES)
    return _unpack_lanes_dev(got, [a.shape for a in arrs]), got


def kernel(x, c, ada_w, ada_b, norm_g, fox_w_in, fox_f_bias, fox_q_norm, fox_k_norm, fox_w_o, gdn_w_in, gdn_conv_w, gdn_a_log, gdn_dt_bias, gdn_out_norm, gdn_w_o, ffn_w_up, ffn_conv_w, ffn_conv_b, ffn_w_down, loss_target, m_ada_w, m_ada_b, m_norm_g, m_fox_w_in, m_fox_f_bias, m_fox_q_norm, m_fox_k_norm, m_fox_w_o, m_gdn_w_in, m_gdn_conv_w, m_gdn_a_log, m_gdn_dt_bias, m_gdn_out_norm, m_gdn_w_o, m_ffn_w_up, m_ffn_conv_w, m_ffn_conv_b, m_ffn_w_down, v_ada_w, v_ada_b, v_norm_g, v_fox_w_in, v_fox_f_bias, v_fox_q_norm, v_fox_k_norm, v_fox_w_o, v_gdn_w_in, v_gdn_conv_w, v_gdn_a_log, v_gdn_dt_bias, v_gdn_out_norm, v_gdn_w_o, v_ffn_w_up, v_ffn_conv_w, v_ffn_conv_b, v_ffn_w_down):
    args = locals()
    w = {n: args[n] for n in WEIGHTS}
    mom = {n: args['m_' + n] for n in WEIGHTS}
    var = {n: args['v_' + n] for n in WEIGHTS}
    _, T, D = x.shape
    H = D // HD
    Hv = V_PER_K * H
    xi, yi, ci = _place()
    s_idx = 2 * xi + yi
    b_idx = 4 * xi + 2 * yi + ci
    sc_arr = jnp.stack([s_idx, ci]).astype(jnp.int32)
    c_arr = jnp.reshape(ci, (1,)).astype(jnp.int32)

    (c_all, ab_all, ng_all, gcw_all, fcw_all), _ = _gather_lanes(
        [jnp.tile(c, (SUBLANES, 1)), ada_b, norm_g, gdn_conv_w, ffn_conv_w], 'gather_small')
    c_all = c_all[:, 0, :]
    chips = lambda a: jnp.concatenate([a[2 * s] for s in range(N_CHIPS)], axis=-1)
    norm_g_full, gdn_conv_full, ffn_conv_full = chips(ng_all), chips(gcw_all), chips(fcw_all)

    Ns = ada_w.shape[-1]
    ada_w4 = ada_w.reshape(4, D, Ns)
    part = jnp.stack([_mm(c_all, ada_w4[i], 'nn', F32, 'ada_proj%d' % i, a_act='silu') for i in range(4)])
    part = part + ada_b.reshape(4, 1, Ns)
    (part_all,), _ = _gather_lanes([part], 'gather_mods')
    mine = lax.dynamic_index_in_dim(part_all[0::2], b_idx, axis=2, keepdims=False)
    mods = mine.transpose(1, 0, 2).reshape(2, 2, N_CHIPS * Ns)

    as2d = lambda a: a.reshape(-1, a.shape[-1])
    own = {n: as2d(w[n]).astype(BF16) for n in BIG}

    def whole(n, gathered):
        shards = lax.dynamic_update_index_in_dim(gathered, own[n], s_idx, 0)
        return _shards_to_full(shards.reshape((N_CHIPS,) + w[n].shape), BIG_SHARD_AXIS[n])

    fw = whole('fox_w_in', _gather_weights(own['fox_w_in'], 'gather_fox_w_in'))[0]
    wf = dict(cat=jnp.concatenate([fw[:, :3 * D], fw[:, 3 * D + H:]], axis=1), f=_pad_cols(fw[:, 3 * D:3 * D + H], LANES),
              f_bias=_pad_cols(fox_f_bias, LANES), q_norm=fox_q_norm, k_norm=fox_k_norm,
              o=whole('fox_w_o', _gather_weights(own['fox_w_o'], 'gather_fox_w_o'))[0])
    later = [n for n in BIG if not n.startswith('fox')]

    def later_weights(gathered):
        full = {n: whole(n, a) for n, a in zip(later, gathered)}
        gw = full['gdn_w_in'][0]
        wg = dict(cat=gw[:, :6 * D], ab=_pad_cols(gw[:, 6 * D:], LANES), conv=gdn_conv_full[0],
                  a_log=_pad_cols(gdn_a_log, LANES), dt_bias=_pad_cols(gdn_dt_bias, LANES), out_norm=gdn_out_norm,
                  o=full['gdn_w_o'][0])
        wffn = [dict(up=full['ffn_w_up'][i], conv=ffn_conv_full[i], conv_b=ffn_conv_b[i][None],
                     down=full['ffn_w_down'][i]) for i in range(2)]
        return wg, wffn

    def reduce_pairs(tag, gr):
        if tag == 'fox':
            full = {('fox_w_in', 0): jnp.concatenate([gr['cat'][:, :3 * D], gr['f'][:, :H], gr['cat'][:, 3 * D:]], axis=1),
                    ('fox_w_o', 0): gr['o']}
        elif tag == 'gdn':
            full = {('gdn_w_in', 0): jnp.concatenate([gr['cat'], gr['ab'][:, :2 * Hv]], axis=1), ('gdn_w_o', 0): gr['o']}
        else:
            layer = int(tag[-1])
            full = {('ffn_w_up', layer): gr['up'], ('ffn_w_down', layer): gr['down']}
        out = []
        for (n, layer), a in full.items():
            shards = _full_to_shards(a, BIG_SHARD_AXIS[n] - 1)
            name = '%s%d' % (n, layer)
            got = _swap_halves(shards, name + '_to_sibling')
            out.append(((n, layer), _add_halves(shards, got, c_arr, name + '_add_sibling')))
        return out

    loss, dx, dmods, dnorm, g, exchanged = _local_step(
        x[0], loss_target[0], mods, norm_g_full, wf, None, None,
        late=([own[n] for n in later], later_weights), reduce_pairs=reduce_pairs)
    loss = lax.psum(loss, ('x', 'y', 'c'))

    gf, gg = g['fox'], g['gdn']
    big_out = {}
    for n in BIG:
        parts = []
        for layer in range(w[n].shape[0]):
            tag = '%s%d' % (n, layer)
            pair_sum, received = exchanged[(n, layer)]
            half_sum = _add_chips(pair_sum, received, sc_arr, tag + '_add_chips')
            parts.append(_join_halves(half_sum, tag + '_join'))
        g_shard = parts[0] if len(parts) == 1 else jnp.concatenate(parts, axis=0)
        big_out[n] = [o.reshape(w[n].shape)
                      for o in _adamw(as2d(w[n]), as2d(mom[n]), as2d(var[n]), [g_shard], 'adamw_' + n)]

    small_part = [dmods, dnorm, gf['f_bias'][:, :H], gf['q_norm'], gf['k_norm'], gg['conv'][None],
                  gg['a_log'][:, :Hv], gg['dt_bias'][:, :Hv], gg['out_norm'],
                  jnp.stack([g['ffn0']['conv'], g['ffn1']['conv']]),
                  jnp.concatenate([g['ffn0']['conv_b'], g['ffn1']['conv_b']], axis=0)]
    (dmods_all, *_), got = _gather_lanes(small_part, 'gather_small_grads')
    tot = _unpack_lanes(_sum_devices(got, 'sum_small_grads'), [a.shape for a in small_part])
    small_full = dict(zip(SMALL, tot))
    small_g = {n: (lax.dynamic_slice_in_dim(small_full[n], s_idx * w[n].shape[-1], w[n].shape[-1], axis=-1)
                   if n in SMALL_SHARDED else small_full[n]) for n in SMALL}
    packs = lambda d: _pack_lanes([d[n] for n in SMALL])
    small_shapes = [w[n].shape for n in SMALL]
    small_out = [_unpack_lanes(o, small_shapes)
                 for o in _adamw(packs(w), packs(mom), packs(var), [packs(small_g)], 'adamw_small')]

    dm = lax.dynamic_slice_in_dim(dmods_all.reshape(N_DEV, 4, N_CHIPS * Ns), s_idx * Ns, Ns, axis=-1).transpose(1, 0, 2)
    ada_out = [o.reshape(ada_w.shape) for o in
               _ada_w_update(c_all, dm, ada_w4, m_ada_w.reshape(4, D, Ns), v_ada_w.reshape(4, D, Ns), 'adamw_ada_w')]

    outs = []
    for k in range(4):
        by_name = {'ada_w': ada_out[k]}
        by_name.update({n: big_out[n][k] for n in BIG})
        by_name.update(zip(SMALL, small_out[k]))
        outs += [by_name[n] for n in WEIGHTS]
    return (loss, dx[None], *outs)
```

```python
import functools
import math

import jax
import jax.numpy as jnp
from jax import lax
from jax.experimental import pallas as pl
from jax.experimental.pallas import tpu as pltpu

F32 = jnp.float32
BF16 = jnp.bfloat16
EPS = 1e-6
HD = 128
GDN_CHUNK = 64
GDN_CONV = 4
FFN_CONV = 3
LANES = 128
SUBLANES = 8
VMEM_LIMIT = 56 * 1024 * 1024
HIGHEST = lax.Precision.HIGHEST
NEG = -1e30

ADAM_LR = 0.001
ADAM_B1 = 0.9
ADAM_B2 = 0.999
ADAM_EPS = 1e-08
ADAM_WD = 0.01
ADAM_STEP = 10

WEIGHTS = ['ada_w', 'ada_b', 'norm_g', 'fox_w_in', 'fox_f_bias', 'fox_q_norm', 'fox_k_norm', 'fox_w_o',
           'gdn_w_in', 'gdn_conv_w', 'gdn_a_log', 'gdn_dt_bias', 'gdn_out_norm', 'gdn_w_o',
           'ffn_w_up', 'ffn_conv_w', 'ffn_conv_b', 'ffn_w_down']
BIG = ['fox_w_in', 'fox_w_o', 'gdn_w_in', 'gdn_w_o', 'ffn_w_up', 'ffn_w_down']
BIG_SHARD_AXIS = {'fox_w_in': 2, 'fox_w_o': 1, 'gdn_w_in': 2, 'gdn_w_o': 1, 'ffn_w_up': 2, 'ffn_w_down': 1}
SMALL = ['ada_b', 'norm_g', 'fox_f_bias', 'fox_q_norm', 'fox_k_norm', 'gdn_conv_w', 'gdn_a_log',
         'gdn_dt_bias', 'gdn_out_norm', 'ffn_conv_w', 'ffn_conv_b']
SMALL_SHARDED = ['ada_b', 'norm_g', 'gdn_conv_w', 'ffn_conv_w']
N_CHIPS = 4
N_DEV = 8
MESH = pl.DeviceIdType.MESH


def _tile(n, cands):
    for c in cands:
        if n % c == 0:
            return c
    return n


def _cp(*sem):
    return pltpu.CompilerParams(dimension_semantics=sem, vmem_limit_bytes=VMEM_LIMIT)


def _dot(a, b, mode='nn', precision=None):
    dims = {'nn': (((1,), (0,)), ((), ())), 'nt': (((1,), (1,)), ((), ())), 'tn': (((0,), (0,)), ((), ()))}[mode]
    return lax.dot_general(a, b, dims, precision=precision, preferred_element_type=F32)


def _bdot(a, b, mode='nn'):
    return _dot(a.astype(BF16), b.astype(BF16), mode)


def _hdot(a, b, mode='nn'):
    return _dot(a, b, mode, precision=HIGHEST)


def _sigmoid(x):
    return 1.0 / (1.0 + jnp.exp(-x))


def _silu(x):
    return x * _sigmoid(x)


def _softplus(x):
    return jnp.maximum(x, 0.0) + jnp.log(1.0 + jnp.exp(-jnp.abs(x)))


def _erf(x):
    return lax.erf(x)


def _gelu(x):
    return 0.5 * x * (1.0 + _erf(x * (2.0 ** -0.5)))


def _normal_cdf_pdf(x):
    cdf = 0.5 * (1.0 + _erf(x * (2.0 ** -0.5)))
    pdf = jnp.exp(-0.5 * x * x) * (1.0 / math.sqrt(2.0 * math.pi))
    return cdf, pdf


MM_K_CAP = 2816


def _k_tile(K, cap):
    for t in range(cap - cap % LANES, 0, -LANES):
        if K % t == 0:
            return t
    return K


def _mm(a, b, mode, out_dtype, name, a_act=None, scatter=()):
    if mode == 'nn':
        (M, K), (_, N) = a.shape, b.shape
    elif mode == 'nt':
        (M, K), (N, _) = a.shape, b.shape
    else:
        (K, M), (_, N) = a.shape, b.shape
    big = (1024, 512, 256, 128)
    narrow = a.dtype.itemsize == 2 and b.dtype.itemsize == 2
    tm, tn, tk = _tile(M, big), _tile(N, big), _k_tile(K, MM_K_CAP if narrow else MM_K_CAP // 2)
    nk, ns = K // tk, len(scatter)
    grid = (M // tm, N // tn, nk)

    def body(a_ref, b_ref, *rest):
        q_refs, o_ref, got_refs = rest[:ns], rest[ns], rest[ns + 1:2 * ns + 1]
        acc = rest[2 * ns + 1:2 * ns + 1 + (nk > 1)]
        jobs = [_scatter_protocol(q_refs[i], got_refs[i], *rest[len(rest) - 2:], SCATTER_SEMS * i) for i in range(ns)]
        step = [pl.program_id(d) for d in range(3)]
        if jobs:
            @pl.when((step[0] == 0) & (step[1] == 0) & (step[2] == 0))
            def _():
                for start, _ in jobs:
                    start()

        av = a_ref[...]
        if a_act == 'silu':
            av = _silu(av.astype(F32))
        part = _bdot(av, b_ref[...], mode)
        if nk == 1:
            o_ref[...] = part.astype(o_ref.dtype)
        else:
            acc_ref, = acc
            k = step[2]

            @pl.when(k == 0)
            def _():
                acc_ref[...] = part

            @pl.when(k > 0)
            def _():
                acc_ref[...] += part

            @pl.when(k == nk - 1)
            def _():
                o_ref[...] = acc_ref[...].astype(o_ref.dtype)

        if jobs:
            @pl.when((step[0] == grid[0] - 1) & (step[1] == grid[1] - 1) & (step[2] == grid[2] - 1))
            def _():
                for _, finish in jobs:
                    finish()

    if mode == 'nn':
        a_spec = pl.BlockSpec((tm, tk), lambda i, j, k: (i, k))
        b_spec = pl.BlockSpec((tk, tn), lambda i, j, k: (k, j))
    elif mode == 'nt':
        a_spec = pl.BlockSpec((tm, tk), lambda i, j, k: (i, k))
        b_spec = pl.BlockSpec((tn, tk), lambda i, j, k: (j, k))
    else:
        a_spec = pl.BlockSpec((tk, tm), lambda i, j, k: (k, i))
        b_spec = pl.BlockSpec((tk, tn), lambda i, j, k: (k, j))
    sems = [pltpu.SemaphoreType.DMA((SCATTER_SEMS * ns,))] * 2 if ns else []
    out = pl.pallas_call(
        body, name=name, grid=grid,
        in_specs=[a_spec, b_spec] + [HBM_SPEC] * ns,
        out_specs=[pl.BlockSpec((tm, tn), lambda i, j, k: (i, j))] + [HBM_SPEC] * ns,
        out_shape=[jax.ShapeDtypeStruct((M, N), out_dtype)] + _scatter_shapes(scatter),
        scratch_shapes=([pltpu.VMEM((tm, tn), F32)] if nk > 1 else []) + sems,
        compiler_params=_cp("arbitrary", "arbitrary", "arbitrary") if ns else _cp("parallel", "parallel", "arbitrary"),
    )(a, b, *scatter)
    return (out[0], list(out[1:])) if ns else out[0]


ROW_TILES = (256, 128, 64, 32, 16, 8)


def _row_spec(tT, D):
    return pl.BlockSpec((tT, D), lambda i: (i, 0))


def _vec_spec(D):
    return pl.BlockSpec((1, D), lambda i: (0, 0))


def _pre_norm(x, g, scale, shift, name):
    T, D = x.shape
    tT = _tile(T, ROW_TILES)

    def body(x_ref, g_ref, sc_ref, sh_ref, h_ref):
        xv = x_ref[...]
        r = lax.rsqrt(jnp.mean(xv * xv, axis=-1, keepdims=True) + EPS)
        h_ref[...] = ((xv * r) * g_ref[...] * (1.0 + sc_ref[...]) + sh_ref[...]).astype(h_ref.dtype)

    return pl.pallas_call(
        body, name=name, grid=(T // tT,),
        in_specs=[_row_spec(tT, D), _vec_spec(D), _vec_spec(D), _vec_spec(D)],
        out_specs=_row_spec(tT, D), out_shape=jax.ShapeDtypeStruct((T, D), BF16),
        compiler_params=_cp("parallel"),
    )(x, g, scale, shift)


def _post_res(x, y, gate, g, name):
    T, D = x.shape
    tT = _tile(T, ROW_TILES)

    def body(x_ref, y_ref, gate_ref, g_ref, o_ref):
        yv = y_ref[...]
        r = lax.rsqrt(jnp.mean(yv * yv, axis=-1, keepdims=True) + EPS)
        o_ref[...] = x_ref[...] + gate_ref[...] * ((yv * r) * g_ref[...])

    return pl.pallas_call(
        body, name=name, grid=(T // tT,),
        in_specs=[_row_spec(tT, D), _row_spec(tT, D), _vec_spec(D), _vec_spec(D)],
        out_specs=_row_spec(tT, D), out_shape=jax.ShapeDtypeStruct((T, D), F32),
        compiler_params=_cp("parallel"),
    )(x, y, gate, g)


def _post_res_bwd(dout, y, gate, g, name):
    T, D = y.shape
    tT = _tile(T, ROW_TILES)

    def body(do_ref, y_ref, gate_ref, g_ref, dy_ref, dgate_ref, dg_ref):
        @pl.when(pl.program_id(0) == 0)
        def _():
            dgate_ref[...] = jnp.zeros_like(dgate_ref)
            dg_ref[...] = jnp.zeros_like(dg_ref)

        yv, dov, gatev, gv = y_ref[...], do_ref[...], gate_ref[...], g_ref[...]
        r = lax.rsqrt(jnp.mean(yv * yv, axis=-1, keepdims=True) + EPS)
        yn = yv * r
        t = dov * yn
        dgate_ref[...] += jnp.sum(t * gv, axis=0, keepdims=True)
        dg_ref[...] += jnp.sum(t * gatev, axis=0, keepdims=True)
        dyn = dov * (gatev * gv)
        dy_ref[...] = (r * (dyn - yn * jnp.mean(dyn * yn, axis=-1, keepdims=True))).astype(dy_ref.dtype)

    return pl.pallas_call(
        body, name=name, grid=(T // tT,),
        in_specs=[_row_spec(tT, D), _row_spec(tT, D), _vec_spec(D), _vec_spec(D)],
        out_specs=[_row_spec(tT, D), _vec_spec(D), _vec_spec(D)],
        out_shape=[jax.ShapeDtypeStruct((T, D), BF16), jax.ShapeDtypeStruct((1, D), F32),
                   jax.ShapeDtypeStruct((1, D), F32)],
        compiler_params=_cp("arbitrary"),
    )(dout, y, gate, g)


def _pre_norm_bwd(dhs, x, g, scale, dres, name):
    T, D = x.shape
    tT = _tile(T, ROW_TILES)
    n = len(dhs)

    def body(*refs):
        dh_refs = refs[:n]
        x_ref, g_ref, sc_ref, dres_ref, dx_ref, dsh_ref, dsc_ref, dg_ref = refs[n:]

        @pl.when(pl.program_id(0) == 0)
        def _():
            dsh_ref[...] = jnp.zeros_like(dsh_ref)
            dsc_ref[...] = jnp.zeros_like(dsc_ref)
            dg_ref[...] = jnp.zeros_like(dg_ref)

        dh = dh_refs[0][...]
        for r_ in dh_refs[1:]:
            dh = dh + r_[...]
        xv, gv, scv = x_ref[...], g_ref[...], sc_ref[...]
        r = lax.rsqrt(jnp.mean(xv * xv, axis=-1, keepdims=True) + EPS)
        xn = xv * r
        t = dh * xn
        dsh_ref[...] += jnp.sum(dh, axis=0, keepdims=True)
        dsc_ref[...] += jnp.sum(t * gv, axis=0, keepdims=True)
        dg_ref[...] += jnp.sum(t * (1.0 + scv), axis=0, keepdims=True)
        dxn = dh * (gv * (1.0 + scv))
        dx_ref[...] = dres_ref[...] + r * (dxn - xn * jnp.mean(dxn * xn, axis=-1, keepdims=True))

    return pl.pallas_call(
        body, name=name, grid=(T // tT,),
        in_specs=[_row_spec(tT, D)] * n + [_row_spec(tT, D), _vec_spec(D), _vec_spec(D), _row_spec(tT, D)],
        out_specs=[_row_spec(tT, D), _vec_spec(D), _vec_spec(D), _vec_spec(D)],
        out_shape=[jax.ShapeDtypeStruct((T, D), F32)] + [jax.ShapeDtypeStruct((1, D), F32)] * 3,
        compiler_params=_cp("arbitrary"),
    )(*dhs, x, g, scale, dres)


def _loss_head(y, target, name):
    T, D = y.shape
    tT = _tile(T, ROW_TILES)

    def body(y_ref, t_ref, dy_ref, l_ref):
        @pl.when(pl.program_id(0) == 0)
        def _():
            l_ref[...] = jnp.zeros_like(l_ref)

        e = y_ref[...] - t_ref[...]
        dy_ref[...] = e * (1.0 / D)
        s = jnp.sum(jnp.mean(e * e, axis=-1, keepdims=True), axis=0, keepdims=True)
        l_ref[...] += 0.5 * s

    return pl.pallas_call(
        body, name=name, grid=(T // tT,),
        in_specs=[_row_spec(tT, D), _row_spec(tT, D)],
        out_specs=[_row_spec(tT, D), pl.BlockSpec((SUBLANES, LANES), lambda i: (0, 0))],
        out_shape=[jax.ShapeDtypeStruct((T, D), F32), jax.ShapeDtypeStruct((SUBLANES, LANES), F32)],
        compiler_params=_cp("arbitrary"),
    )(y, target)


HEAD_ROW_TILES = (2048, 1024, 512, 256, 128, 64)


def _hb(tT, off=0):
    return pl.BlockSpec((tT, HD), lambda i, h: (i, off + h))


def _hvec():
    return pl.BlockSpec((1, HD), lambda i, h: (0, 0))


def _headnorm(x, off, H, g, c1, post, out_dtype, name):
    T = x.shape[0]
    tT = _tile(T, HEAD_ROW_TILES)
    has_g = g is not None

    def body(*refs):
        x_ref = refs[0]
        o_ref = refs[-1]
        xv = x_ref[...]
        yv = xv * lax.rsqrt(c1 * jnp.sum(xv * xv, axis=-1, keepdims=True) + EPS)
        if has_g:
            yv = yv * refs[1][...]
        if post != 1.0:
            yv = yv * post
        o_ref[...] = yv.astype(o_ref.dtype)

    return pl.pallas_call(
        body, name=name, grid=(T // tT, H),
        in_specs=[_hb(tT, off)] + ([_hvec()] if has_g else []),
        out_specs=_hb(tT), out_shape=jax.ShapeDtypeStruct((T, H * HD), out_dtype),
        compiler_params=_cp("parallel", "parallel"),
    )(*([x, g] if has_g else [x]))


def _headnorm_bwd(dys, x, off, H, g, c1, post, out_dtype, name):
    T = x.shape[0]
    tT = _tile(T, HEAD_ROW_TILES)
    n = len(dys)
    has_g = g is not None

    def body(*refs):
        dy_refs = refs[:n]
        x_ref = refs[n]
        g_ref = refs[n + 1] if has_g else None
        dx_ref, dg_ref = refs[-2], refs[-1]

        @pl.when((pl.program_id(0) == 0) & (pl.program_id(1) == 0))
        def _():
            dg_ref[...] = jnp.zeros_like(dg_ref)

        dy = dy_refs[0][...].astype(F32)
        for r_ in dy_refs[1:]:
            dy = dy + r_[...].astype(F32)
        if post != 1.0:
            dy = dy * post
        xv = x_ref[...]
        r = lax.rsqrt(c1 * jnp.sum(xv * xv, axis=-1, keepdims=True) + EPS)
        xn = xv * r
        if has_g:
            dg_ref[...] += jnp.sum(dy * xn, axis=0, keepdims=True)
            dy = dy * g_ref[...]
        dx_ref[...] = (r * (dy - xn * (c1 * jnp.sum(dy * xn, axis=-1, keepdims=True)))).astype(dx_ref.dtype)

    dy_specs = [pl.BlockSpec((tT, HD), lambda i, h, st=st, of=of: (i, st * h + of)) for (_, st, of) in dys]
    return pl.pallas_call(
        body, name=name, grid=(T // tT, H),
        in_specs=dy_specs + [_hb(tT, off)] + ([_hvec()] if has_g else []),
        out_specs=[_hb(tT), _hvec()],
        out_shape=[jax.ShapeDtypeStruct((T, H * HD), out_dtype), jax.ShapeDtypeStruct((1, HD), F32)],
        compiler_params=_cp("arbitrary", "arbitrary"),
    )(*[d[0] for d in dys], x, *([g] if has_g else []))


def _fox_gate(ao, proj, og_off, H, name):
    T = ao.shape[0]
    tT = _tile(T, HEAD_ROW_TILES)

    def body(ao_ref, og_ref, o_ref):
        o_ref[...] = (ao_ref[...] * _sigmoid(og_ref[...])).astype(o_ref.dtype)

    return pl.pallas_call(
        body, name=name, grid=(T // tT, H),
        in_specs=[_hb(tT), _hb(tT, og_off)], out_specs=_hb(tT),
        out_shape=jax.ShapeDtypeStruct((T, H * HD), BF16), compiler_params=_cp("parallel", "parallel"),
    )(ao, proj)


def _fox_gate_bwd(dgated, ao, proj, og_off, H, name):
    T = ao.shape[0]
    tT = _tile(T, HEAD_ROW_TILES)

    def body(dg_ref, ao_ref, og_ref, dao_ref, dog_ref, delta_ref):
        dg, aov = dg_ref[...], ao_ref[...]
        sg = _sigmoid(og_ref[...])
        dao = dg * sg
        dao_ref[...] = dao.astype(dao_ref.dtype)
        dog_ref[...] = (dg * aov * sg * (1.0 - sg)).astype(dog_ref.dtype)
        delta_ref[...] = jnp.broadcast_to(jnp.sum(dao * aov, axis=-1, keepdims=True), delta_ref.shape)

    return pl.pallas_call(
        body, name=name, grid=(T // tT, H),
        in_specs=[_hb(tT), _hb(tT), _hb(tT, og_off)], out_specs=[_hb(tT)] * 3,
        out_shape=[jax.ShapeDtypeStruct((T, H * HD), BF16), jax.ShapeDtypeStruct((T, H * HD), BF16),
                   jax.ShapeDtypeStruct((T, H * HD), F32)],
        compiler_params=_cp("parallel", "parallel"),
    )(dgated, ao, proj)


def _gdn_out(o, proj, z_off, Hv, g, name):
    T = o.shape[0]
    tT = _tile(T, HEAD_ROW_TILES)

    def body(o_ref, z_ref, g_ref, y_ref):
        ov, zv = o_ref[...], z_ref[...]
        r = lax.rsqrt(jnp.mean(ov * ov, axis=-1, keepdims=True) + EPS)
        y_ref[...] = (((ov * r) * g_ref[...]) * _silu(zv)).astype(y_ref.dtype)

    return pl.pallas_call(
        body, name=name, grid=(T // tT, Hv),
        in_specs=[_hb(tT), _hb(tT, z_off), _hvec()], out_specs=_hb(tT),
        out_shape=jax.ShapeDtypeStruct((T, Hv * HD), BF16), compiler_params=_cp("parallel", "parallel"),
    )(o, proj, g)


def _gdn_out_bwd(dy, o, proj, z_off, Hv, g, name):
    T = o.shape[0]
    tT = _tile(T, HEAD_ROW_TILES)

    def body(dy_ref, o_ref, z_ref, g_ref, do_ref, dz_ref, dg_ref):
        @pl.when((pl.program_id(0) == 0) & (pl.program_id(1) == 0))
        def _():
            dg_ref[...] = jnp.zeros_like(dg_ref)

        dyv, ov, zv, gv = dy_ref[...], o_ref[...], z_ref[...], g_ref[...]
        r = lax.rsqrt(jnp.mean(ov * ov, axis=-1, keepdims=True) + EPS)
        on = ov * r
        sg = _sigmoid(zv)
        sz = zv * sg
        dz_ref[...] = (dyv * (on * gv) * (sg * (1.0 + zv * (1.0 - sg)))).astype(dz_ref.dtype)
        t = dyv * sz
        dg_ref[...] += jnp.sum(t * on, axis=0, keepdims=True)
        don = t * gv
        do_ref[...] = r * (don - on * jnp.mean(don * on, axis=-1, keepdims=True))

    return pl.pallas_call(
        body, name=name, grid=(T // tT, Hv),
        in_specs=[_hb(tT), _hb(tT), _hb(tT, z_off), _hvec()], out_specs=[_hb(tT), _hb(tT), _hvec()],
        out_shape=[jax.ShapeDtypeStruct((T, Hv * HD), F32), jax.ShapeDtypeStruct((T, Hv * HD), BF16),
                   jax.ShapeDtypeStruct((1, HD), F32)],
        compiler_params=_cp("arbitrary", "arbitrary"),
    )(dy, o, proj, g)


def _lrow(tT):
    return pl.BlockSpec((tT, LANES), lambda i: (i, 0))


def _lvec():
    return pl.BlockSpec((1, LANES), lambda i: (0, 0))


def _logsig(x, b, name):
    T = x.shape[0]
    tT = _tile(T, HEAD_ROW_TILES)

    def body(x_ref, b_ref, o_ref):
        o_ref[...] = -_softplus(-(x_ref[...] + b_ref[...]))

    return pl.pallas_call(body, name=name, grid=(T // tT,), in_specs=[_lrow(tT), _lvec()], out_specs=_lrow(tT),
                          out_shape=jax.ShapeDtypeStruct((T, LANES), F32), compiler_params=_cp("parallel"))(x, b)


def _logsig_bwd(dy, x, b, name):
    T = x.shape[0]
    tT = _tile(T, HEAD_ROW_TILES)

    def body(dy_ref, x_ref, b_ref, dx_ref, db_ref):
        @pl.when(pl.program_id(0) == 0)
        def _():
            db_ref[...] = jnp.zeros_like(db_ref)

        dx = dy_ref[...] * _sigmoid(-(x_ref[...] + b_ref[...]))
        dx_ref[...] = dx.astype(dx_ref.dtype)
        db_ref[...] += jnp.sum(dx, axis=0, keepdims=True)

    return pl.pallas_call(
        body, name=name, grid=(T // tT,), in_specs=[_lrow(tT), _lrow(tT), _lvec()], out_specs=[_lrow(tT), _lvec()],
        out_shape=[jax.ShapeDtypeStruct((T, LANES), BF16), jax.ShapeDtypeStruct((1, LANES), F32)],
        compiler_params=_cp("arbitrary"))(dy, x, b)


def _gdn_gates(ab, alog, dtb, name):
    T = ab.shape[0]
    tT = _tile(T, HEAD_ROW_TILES)

    def body(ab_ref, al_ref, dt_ref, g_ref, be_ref):
        v = ab_ref[...]
        g_ref[...] = -jnp.exp(al_ref[...]) * _softplus(v + dt_ref[...])
        be_ref[...] = _sigmoid(v)

    return pl.pallas_call(
        body, name=name, grid=(T // tT,), in_specs=[_lrow(tT), _lvec(), _lvec()], out_specs=[_lrow(tT)] * 2,
        out_shape=[jax.ShapeDtypeStruct((T, LANES), F32)] * 2, compiler_params=_cp("parallel"))(ab, alog, dtb)


def _gdn_gates_bwd(dg, dbeta, ab, alog, dtb, name):
    T = ab.shape[0]
    tT = _tile(T, HEAD_ROW_TILES)

    def body(dg_ref, dbe_ref, ab_ref, al_ref, dt_ref, dab_ref, dal_ref, ddt_ref):
        @pl.when(pl.program_id(0) == 0)
        def _():
            dal_ref[...] = jnp.zeros_like(dal_ref)
            ddt_ref[...] = jnp.zeros_like(ddt_ref)

        v, dgv = ab_ref[...], dg_ref[...]
        ea = jnp.exp(al_ref[...])
        z = v + dt_ref[...]
        da = dgv * (-ea * _sigmoid(z))
        sb = _sigmoid(v)
        dab_ref[...] = (da + dbe_ref[...] * sb * (1.0 - sb)).astype(dab_ref.dtype)
        dal_ref[...] += jnp.sum(dgv * (-ea * _softplus(z)), axis=0, keepdims=True)
        ddt_ref[...] += jnp.sum(da, axis=0, keepdims=True)

    return pl.pallas_call(
        body, name=name, grid=(T // tT,), in_specs=[_lrow(tT), _lrow(tT), _lrow(tT), _lvec(), _lvec()],
        out_specs=[_lrow(tT), _lvec(), _lvec()],
        out_shape=[jax.ShapeDtypeStruct((T, LANES), BF16), jax.ShapeDtypeStruct((1, LANES), F32),
                   jax.ShapeDtypeStruct((1, LANES), F32)],
        compiler_params=_cp("arbitrary"))(dg, dbeta, ab, alog, dtb)


def _cumsum(x, seg, reverse, name):
    T = x.shape[0]
    tb = _tile(T, (256, 128, 64))
    nb = T // tb
    carry = seg is None

    def body(x_ref, o_ref, c_ref):
        @pl.when(pl.program_id(0) == 0)
        def _():
            c_ref[...] = jnp.zeros_like(c_ref)

        ri = lax.broadcasted_iota(jnp.int32, (tb, tb), 0)
        ci = lax.broadcasted_iota(jnp.int32, (tb, tb), 1)
        keep = (ci >= ri) if reverse else (ci <= ri)
        if seg is not None:
            keep = keep & ((ri // seg) == (ci // seg))
        y = _hdot(keep.astype(F32), x_ref[...])
        if carry:
            y = y + c_ref[...]
            c_ref[...] = y[0:1, :] if reverse else y[tb - 1:tb, :]
        o_ref[...] = y

    imap = (lambda i: (nb - 1 - i, 0)) if reverse else (lambda i: (i, 0))
    return pl.pallas_call(
        body, name=name, grid=(nb,), in_specs=[pl.BlockSpec((tb, LANES), imap)],
        out_specs=pl.BlockSpec((tb, LANES), imap), out_shape=jax.ShapeDtypeStruct((T, LANES), F32),
        scratch_shapes=[pltpu.VMEM((1, LANES), F32)], compiler_params=_cp("arbitrary"))(x)


ATT_TILES = (1024, 512, 256, 128)


def _att_scores(q, k, f0, fk):
    return _dot(q, k, 'nt') - (fk - f0)


def _diag_keep(tq):
    return lax.broadcasted_iota(jnp.int32, (tq, tq), 1) <= lax.broadcasted_iota(jnp.int32, (tq, tq), 0)


def _tri_pairs(nq, by_key):
    if by_key:
        pairs = [(qi, ki) for ki in range(nq) for qi in range(ki, nq)]
    else:
        pairs = [(qi, ki) for qi in range(nq) for ki in range(qi + 1)]
    return jnp.asarray([p[0] for p in pairs], jnp.int32), jnp.asarray([p[1] for p in pairs], jnp.int32)


def _att_specs(tq):
    qspec = pl.BlockSpec((tq, HD), lambda h, p, qt, kt: (qt[p], h))
    kspec = pl.BlockSpec((tq, HD), lambda h, p, qt, kt: (kt[p], h))
    f0spec = pl.BlockSpec((None, None, 1, 1), lambda h, p, qt, kt: (h, qt[p], 0, 0))
    fkspec = pl.BlockSpec((None, 1, tq), lambda h, p, qt, kt: (h, 0, kt[p]))
    return qspec, kspec, f0spec, fkspec


def _flash_fwd(qn, kn, vb, f0, fkr, H, name, carry=()):
    T = qn.shape[0]
    tq = _tile(T, ATT_TILES)
    nq = T // tq
    qt, kt = _tri_pairs(nq, by_key=False)
    n_pairs, n_carry = qt.shape[0], len(carry)

    def body(qt_ref, kt_ref, q_ref, k_ref, v_ref, f0_ref, fk_ref, *rest):
        w_refs, (o_ref, lse_ref), g_refs = rest[:n_carry], rest[n_carry:n_carry + 2], rest[n_carry + 2:2 * n_carry + 2]
        m_s, l_s, acc_s = rest[2 * n_carry + 2:2 * n_carry + 5]
        jobs = [_gather_protocol(w_refs[i], g_refs[i], *rest[2 * n_carry + 5:], GATHER_SEMS * i) for i in range(n_carry)]
        qi, ki = qt_ref[pl.program_id(1)], kt_ref[pl.program_id(1)]

        if jobs:
            @pl.when((pl.program_id(0) == 0) & (pl.program_id(1) == 0))
            def _():
                for start, _ in jobs:
                    start()

        @pl.when(ki == 0)
        def _():
            m_s[...] = jnp.full_like(m_s, NEG)
            l_s[...] = jnp.zeros_like(l_s)
            acc_s[...] = jnp.zeros_like(acc_s)

        def step(diagonal):
            s = _att_scores(q_ref[...], k_ref[...], f0_ref[...], fk_ref[...])
            if diagonal:
                s = jnp.where(_diag_keep(tq), s, NEG)
            m_prev = m_s[...]
            m_new = jnp.maximum(m_prev, jnp.max(s, axis=1, keepdims=True))
            alpha = jnp.exp(m_prev - m_new)
            p = jnp.exp(s - m_new[:, :1])
            l_s[...] = alpha * l_s[...] + jnp.sum(p, axis=1, keepdims=True)
            acc_s[...] = acc_s[...] * alpha + _bdot(p, v_ref[...])
            m_s[...] = m_new

        @pl.when(ki < qi)
        def _():
            step(False)

        @pl.when(ki == qi)
        def _():
            step(True)
            o_ref[...] = acc_s[...] / l_s[...]
            lse_ref[...] = m_s[...] + jnp.log(l_s[...])

        if jobs:
            @pl.when((pl.program_id(0) == H - 1) & (pl.program_id(1) == n_pairs - 1))
            def _():
                for _, finish in jobs:
                    finish()

    qspec, kspec, f0spec, fkspec = _att_specs(tq)
    sems = [pltpu.SemaphoreType.DMA((GATHER_SEMS * n_carry,))] * 2 if n_carry else []
    grid_spec = pltpu.PrefetchScalarGridSpec(
        num_scalar_prefetch=2, grid=(H, n_pairs),
        in_specs=[qspec, kspec, kspec, f0spec, fkspec] + [HBM_SPEC] * n_carry,
        out_specs=[qspec, qspec] + [HBM_SPEC] * n_carry,
        scratch_shapes=[pltpu.VMEM((tq, HD), F32)] * 3 + sems)
    out = pl.pallas_call(
        body, name=name, grid_spec=grid_spec,
        out_shape=[jax.ShapeDtypeStruct((T, H * HD), F32)] * 2
        + [jax.ShapeDtypeStruct((N_CHIPS,) + a.shape, a.dtype) for a in carry],
        compiler_params=_cp("arbitrary", "arbitrary") if n_carry else _cp("parallel", "arbitrary"),
    )(qt, kt, qn, kn, vb, f0, fkr, *carry)
    return out[0], out[1], list(out[2:])


def _flash_bwd(qn, kn, vb, f0, fkr, dao, lse, delta, H, name):
    T = qn.shape[0]
    tq = _tile(T, ATT_TILES)
    nq = T // tq
    qt, kt = _tri_pairs(nq, by_key=True)

    def body(qt_ref, kt_ref, q_ref, k_ref, v_ref, f0_ref, fk_ref, do_ref, lse_ref, dl_ref,
             dq_ref, dfq_ref, dk_ref, dv_ref, dfk_ref, dk_s, dv_s, dfk_s):
        qi, ki = qt_ref[pl.program_id(1)], kt_ref[pl.program_id(1)]

        @pl.when(pl.program_id(1) == 0)
        def _():
            dq_ref[...] = jnp.zeros_like(dq_ref)
            dfq_ref[...] = jnp.zeros_like(dfq_ref)

        @pl.when(qi == ki)
        def _():
            dk_s[...] = jnp.zeros_like(dk_s)
            dv_s[...] = jnp.zeros_like(dv_s)
            dfk_s[...] = jnp.zeros_like(dfk_s)

        def step(diagonal):
            s = _att_scores(q_ref[...], k_ref[...], f0_ref[...], fk_ref[...])
            p = jnp.exp(s - lse_ref[...][:, :1])
            if diagonal:
                p = jnp.where(_diag_keep(tq), p, 0.0)
            dp = _dot(do_ref[...], v_ref[...], 'nt')
            ds = p * (dp - dl_ref[...][:, :1])
            dv_s[...] += _bdot(p, do_ref[...], 'tn')
            dk_s[...] += _bdot(ds, q_ref[...], 'tn')
            dfk_s[...] -= jnp.sum(ds, axis=0, keepdims=True)
            rows = pl.ds(pl.multiple_of(qi * tq, tq), tq)
            dq_ref[rows, :] += _bdot(ds, k_ref[...])
            dfq_ref[rows, :] += jnp.broadcast_to(jnp.sum(ds, axis=1, keepdims=True), (tq, HD))

        @pl.when(qi > ki)
        def _():
            step(False)

        @pl.when(qi == ki)
        def _():
            step(True)

        @pl.when(qi == nq - 1)
        def _():
            dk_ref[...] = dk_s[...]
            dv_ref[...] = dv_s[...].astype(dv_ref.dtype)
            dfk_ref[...] = dfk_s[...]

    qspec, kspec, f0spec, fkspec = _att_specs(tq)
    head = pl.BlockSpec((T, HD), lambda h, p, qt, kt: (0, h))
    grid_spec = pltpu.PrefetchScalarGridSpec(
        num_scalar_prefetch=2, grid=(H, qt.shape[0]),
        in_specs=[qspec, kspec, kspec, f0spec, fkspec, qspec, qspec, qspec],
        out_specs=[head, head, kspec, kspec, fkspec],
        scratch_shapes=[pltpu.VMEM((tq, HD), F32), pltpu.VMEM((tq, HD), F32), pltpu.VMEM((1, tq), F32)])
    return pl.pallas_call(
        body, name=name, grid_spec=grid_spec,
        out_shape=[jax.ShapeDtypeStruct((T, H * HD), F32)] * 3
        + [jax.ShapeDtypeStruct((T, H * HD), BF16), jax.ShapeDtypeStruct((H, 1, T), F32)],
        compiler_params=_cp("parallel", "arbitrary"),
    )(qt, kt, qn, kn, vb, f0, fkr, dao, lse, delta)


CONV_TILES = (512, 256, 128, 64)
HALO = SUBLANES


def _dwconv(x, xoff, W, w, b, act, voff, out_dtype, name):
    T = x.shape[0]
    K = w.shape[0]
    tT, tC = _tile(T, CONV_TILES), _tile(W, CONV_TILES)
    xb, hb = xoff // tC, tT // HALO
    glu = act == 'glu'

    def body(*refs):
        if glu:
            x_ref, xp_ref, w_ref, b_ref, v_ref, o_ref, buf = refs
        else:
            x_ref, xp_ref, w_ref, o_ref, buf = refs
        i = pl.program_id(0)
        buf[0:HALO, :] = jnp.where(i > 0, xp_ref[...], 0.0)
        buf[HALO:, :] = x_ref[...]
        conv = w_ref[0:1, :] * buf[pl.ds(HALO - (K - 1), tT), :]
        for k in range(1, K):
            conv = conv + w_ref[k:k + 1, :] * buf[pl.ds(HALO - (K - 1) + k, tT), :]
        if glu:
            o_ref[...] = (_gelu(conv + b_ref[...]) * v_ref[...]).astype(o_ref.dtype)
        else:
            o_ref[...] = _silu(conv).astype(o_ref.dtype)

    cur = pl.BlockSpec((tT, tC), lambda i, j: (i, xb + j))
    prev = pl.BlockSpec((HALO, tC), lambda i, j: (jnp.maximum(i * hb - 1, 0), xb + j))
    wspec = pl.BlockSpec((K, tC), lambda i, j: (0, j))
    in_specs, args = [cur, prev, wspec], [x, x, w]
    if glu:
        vb = voff // tC
        in_specs += [pl.BlockSpec((1, tC), lambda i, j: (0, j)), pl.BlockSpec((tT, tC), lambda i, j: (i, vb + j))]
        args += [b, x]
    return pl.pallas_call(
        body, name=name, grid=(T // tT, W // tC), in_specs=in_specs,
        out_specs=pl.BlockSpec((tT, tC), lambda i, j: (i, j)), out_shape=jax.ShapeDtypeStruct((T, W), out_dtype),
        scratch_shapes=[pltpu.VMEM((tT + HALO, tC), F32)], compiler_params=_cp("parallel", "parallel"),
    )(*args)


def _dwconv_bwd(x, xoff, W, w, woff, b, act, voff, dy, name):
    T = x.shape[0]
    K = w.shape[0]
    tT, tC = _tile(T, CONV_TILES), _tile(W, CONV_TILES)
    xb, wb, hb, nT = xoff // tC, woff // tC, tT // HALO, T // tT
    last_halo = T // HALO - 1
    glu = act == 'glu'

    def body(*refs):
        if glu:
            (x_ref, xp_ref, xn_ref, dy_ref, dyn_ref, w_ref, b_ref, v_ref, vn_ref,
             dx_ref, dv_ref, dw_ref, db_ref, xbuf, dybuf, dbuf, vbuf) = refs
        else:
            x_ref, xp_ref, xn_ref, dy_ref, dyn_ref, w_ref, dx_ref, dw_ref, xbuf, dybuf, dbuf = refs
        i = pl.program_id(1)

        @pl.when(i == 0)
        def _():
            dw_ref[...] = jnp.zeros_like(dw_ref)
            if glu:
                db_ref[...] = jnp.zeros_like(db_ref)

        ext = tT + HALO
        xbuf[0:HALO, :] = jnp.where(i > 0, xp_ref[...], 0.0)
        xbuf[HALO:HALO + tT, :] = x_ref[...]
        xbuf[HALO + tT:, :] = xn_ref[...]
        dybuf[0:tT, :] = dy_ref[...].astype(F32)
        dybuf[tT:, :] = jnp.where(i < nT - 1, dyn_ref[...].astype(F32), 0.0)
        conv = w_ref[0:1, :] * xbuf[pl.ds(HALO - (K - 1), ext), :]
        for k in range(1, K):
            conv = conv + w_ref[k:k + 1, :] * xbuf[pl.ds(HALO - (K - 1) + k, ext), :]
        dyv = dybuf[...]
        if glu:
            vbuf[0:tT, :] = v_ref[...]
            vbuf[tT:, :] = vn_ref[...]
            z = conv + b_ref[...]
            cdf, pdf = _normal_cdf_pdf(z)
            dconv = dyv * vbuf[...] * (cdf + z * pdf)
            dv_ref[...] = (dyv[0:tT, :] * (z[0:tT, :] * cdf[0:tT, :])).astype(dv_ref.dtype)
        else:
            sg = _sigmoid(conv)
            dconv = dyv * (sg * (1.0 + conv * (1.0 - sg)))
        dbuf[...] = dconv
        dx = w_ref[0:1, :] * dbuf[pl.ds(K - 1, tT), :]
        for k in range(1, K):
            dx = dx + w_ref[k:k + 1, :] * dbuf[pl.ds(K - 1 - k, tT), :]
        dx_ref[...] = dx.astype(dx_ref.dtype)
        dc = dconv[0:tT, :]
        for k in range(K):
            dw_ref[k:k + 1, :] += jnp.sum(dc * xbuf[pl.ds(HALO - (K - 1) + k, tT), :], axis=0, keepdims=True)
        if glu:
            db_ref[...] += jnp.sum(dc, axis=0, keepdims=True)

    def cur(off):
        return pl.BlockSpec((tT, tC), lambda j, i: (i, off + j))

    def nxt(off):
        return pl.BlockSpec((HALO, tC), lambda j, i: (jnp.minimum((i + 1) * hb, last_halo), off + j))

    prev = pl.BlockSpec((HALO, tC), lambda j, i: (jnp.maximum(i * hb - 1, 0), xb + j))
    wspec = pl.BlockSpec((K, tC), lambda j, i: (0, wb + j))
    acc_w = pl.BlockSpec((K, tC), lambda j, i: (0, j))
    acc_b = pl.BlockSpec((1, tC), lambda j, i: (0, j))
    in_specs = [cur(xb), prev, nxt(xb), cur(0), nxt(0), wspec]
    args = [x, x, x, dy, dy, w]
    out_specs = [cur(0)]
    out_shape = [jax.ShapeDtypeStruct((T, W), BF16)]
    scratch = [pltpu.VMEM((tT + 2 * HALO, tC), F32), pltpu.VMEM((tT + HALO, tC), F32), pltpu.VMEM((tT + HALO, tC), F32)]
    if glu:
        vb = voff // tC
        in_specs += [pl.BlockSpec((1, tC), lambda j, i: (0, wb + j)), cur(vb), nxt(vb)]
        args += [b, x, x]
        out_specs += [cur(0), acc_w, acc_b]
        out_shape += [jax.ShapeDtypeStruct((T, W), BF16), jax.ShapeDtypeStruct((K, W), F32),
                      jax.ShapeDtypeStruct((1, W), F32)]
        scratch += [pltpu.VMEM((tT + HALO, tC), F32)]
    else:
        out_specs += [acc_w]
        out_shape += [jax.ShapeDtypeStruct((K, W), F32)]
    return pl.pallas_call(
        body, name=name, grid=(W // tC, nT), in_specs=in_specs, out_specs=out_specs, out_shape=out_shape,
        scratch_shapes=scratch, compiler_params=_cp("parallel", "arbitrary"),
    )(*args)


V_PER_K = 2


GDN_PREP_CHUNKS = 8
GDN_SCAN_CHUNKS = 4


def _b3(a, b, mode='nn', precision=None):
    c = {'nn': ((2,), (1,)), 'nt': ((2,), (2,)), 'tn': ((1,), (1,))}[mode]
    return lax.dot_general(a, b, (c, ((0,), (0,))), precision=precision, preferred_element_type=F32)


def _bb3(a, b, mode='nn'):
    return _b3(a.astype(BF16), b.astype(BF16), mode)


def _hb3(a, b, mode='nn'):
    return _b3(a, b, mode, precision=HIGHEST)


def _split_bf16(a):
    hi = a.astype(BF16)
    return hi, (a - hi.astype(F32)).astype(BF16)


def _nb3(a, b, mode='nn'):
    ah, al = _split_bf16(a)
    bh, bl = _split_bf16(b)
    return _b3(ah, bh, mode) + _b3(ah, bl, mode) + _b3(al, bh, mode)


def _to_batch(x, nc):
    C = GDN_CHUNK
    return jnp.concatenate([x[:, j * HD:(j + 1) * HD].reshape(nc, C, HD) for j in range(V_PER_K)], axis=0)


def _from_batch(x, nc):
    C = GDN_CHUNK
    return jnp.concatenate([x[j * nc:(j + 1) * nc].reshape(nc * C, HD) for j in range(V_PER_K)], axis=1)


def _both_heads(x, nc):
    xc = x.reshape(nc, GDN_CHUNK, HD)
    return jnp.concatenate([xc] * V_PER_K, axis=0)


def _gdn_local(q2, k2, gb, bb):
    B, C, _ = k2.shape
    ri = lax.broadcasted_iota(jnp.int32, (B, C, C), 1)
    ci = lax.broadcasted_iota(jnp.int32, (B, C, C), 2)
    lower, strict = ri >= ci, ri > ci
    pick0 = (lax.broadcasted_iota(jnp.int32, (B, C, HD), 2) == 0).astype(F32)
    g_cols = _hb3(pick0, gb, 'nt')
    dm = jnp.exp(jnp.where(lower, gb[:, :, :C] - g_cols, NEG))
    kk = _bb3(k2, k2, 'nt')
    a = jnp.where(strict, kk * dm * bb[:, :, :C], 0.0)
    eg = jnp.exp(gb)
    gl = gb[:, C - 1:C, :]
    return dict(lower=lower, strict=strict, eye=(ri == ci).astype(F32), dm=dm, kk=kk, a=a, eg=eg, gl=gl,
                qd=q2 * eg, kd=k2 * jnp.exp(gl - gb))


def _gdn_specs(T, Hk, voff, nc, rev=False):
    C = GDN_CHUNK
    nb = T // (nc * C)
    vb = voff // (V_PER_K * HD)
    ix = (lambda i: nb - 1 - i) if rev else (lambda i: i)
    kspec = pl.BlockSpec((nc * C, HD), lambda h, i: (ix(i), h))
    pair = pl.BlockSpec((nc * C, V_PER_K * HD), lambda h, i: (ix(i), h))
    vspec = pl.BlockSpec((nc * C, V_PER_K * HD), lambda h, i: (ix(i), vb + h))
    cc = pl.BlockSpec((V_PER_K, nc, C, C), lambda h, i: (h, ix(i), 0, 0))
    state = pl.BlockSpec((V_PER_K, nc, HD, HD), lambda h, i: (h, ix(i), 0, 0))
    scal = pl.BlockSpec((V_PER_K, nc, SUBLANES, HD), lambda h, i: (h, ix(i), 0, 0))
    return nb, kspec, pair, vspec, cc, state, scal


def _gdn_prep(qn, kn, qkvc, voff, gcb, betab, Hk, name):
    T = qn.shape[0]
    C, nc = GDN_CHUNK, GDN_PREP_CHUNKS
    Hv, N = Hk * V_PER_K, T // C
    nb, kspec, pair, vspec, cc, _, _ = _gdn_specs(T, Hk, voff, nc)

    def body(q_ref, k_ref, v_ref, g_ref, b_ref, u_ref, w_ref, tm_ref, qkm_ref):
        q2, k2 = _both_heads(q_ref[...], nc), _both_heads(k_ref[...], nc)
        v2, gb, bb = _to_batch(v_ref[...], nc), _to_batch(g_ref[...], nc), _to_batch(b_ref[...], nc)
        lc = _gdn_local(q2, k2, gb, bb)
        p = -lc['a']
        tm = lc['eye'] + p
        for _ in range(5):
            p = _nb3(p, p)
            tm = tm + _nb3(tm, p)
        u_ref[...] = _from_batch(_nb3(tm, v2 * bb), nc)
        w_ref[...] = _from_batch(_nb3(tm, k2 * (bb * lc['eg'])), nc)
        tm_ref[...] = tm.reshape(V_PER_K, nc, C, C)
        qkm_ref[...] = jnp.where(lc['lower'], _bb3(q2, k2, 'nt') * lc['dm'], 0.0).reshape(V_PER_K, nc, C, C)

    return pl.pallas_call(
        body, name=name, grid=(Hk, nb), in_specs=[kspec, kspec, vspec, pair, pair], out_specs=[pair, pair, cc, cc],
        out_shape=[jax.ShapeDtypeStruct((T, Hv * HD), F32)] * 2 + [jax.ShapeDtypeStruct((Hv, N, C, C), F32)] * 2,
        compiler_params=_cp("parallel", "parallel"),
    )(qn, kn, qkvc, gcb, betab)


SCAN_K_HEADS = 2
SCAN_V_HEADS = SCAN_K_HEADS * V_PER_K


def _heads(ref, rows, per=1):
    return jnp.stack([ref[rows, (j // per) * HD:(j // per + 1) * HD] for j in range(SCAN_V_HEADS)])


def _put_heads(ref, rows, x):
    for j in range(SCAN_V_HEADS):
        ref[rows, j * HD:(j + 1) * HD] = x[j]


def _scan_chunk(q_ref, k_ref, g_ref, rows):
    C = GDN_CHUNK
    gb = _heads(g_ref, rows)
    gl = gb[:, C - 1:C, :]
    return _heads(q_ref, rows, V_PER_K) * jnp.exp(gb), _heads(k_ref, rows, V_PER_K) * jnp.exp(gl - gb), jnp.exp(gl)


def _scan_specs(T, ns, rev=False):
    C = GDN_CHUNK
    nb = T // (ns * C)
    ix = (lambda i: nb - 1 - i) if rev else (lambda i: i)
    kspec = pl.BlockSpec((ns * C, SCAN_K_HEADS * HD), lambda h, i: (ix(i), h))
    vspec = pl.BlockSpec((ns * C, SCAN_V_HEADS * HD), lambda h, i: (ix(i), h))
    per_chunk = lambda *tail: pl.BlockSpec((SCAN_V_HEADS, ns) + tail, lambda h, i: (h, ix(i), 0, 0))
    return nb, kspec, vspec, per_chunk(C, C), per_chunk(HD, HD), per_chunk(SUBLANES, HD)


def _gdn_scan(qn, kn, gcb, u, w, qkm, Hk, name):
    T = qn.shape[0]
    C, ns = GDN_CHUNK, GDN_SCAN_CHUNKS
    Hv, N = Hk * V_PER_K, T // C
    nb, kspec, pair, cc, state, _ = _scan_specs(T, ns)

    def body(q_ref, k_ref, g_ref, u_ref, w_ref, qkm_ref, o_ref, sp_ref, s_s):
        @pl.when(pl.program_id(1) == 0)
        def _():
            s_s[...] = jnp.zeros_like(s_s)

        s = s_s[...]
        for t in range(ns):
            rows = slice(t * C, (t + 1) * C)
            qd, kd, egl = _scan_chunk(q_ref, k_ref, g_ref, rows)
            sp_ref[:, t] = s
            vn = _heads(u_ref, rows) - _bb3(_heads(w_ref, rows), s)
            _put_heads(o_ref, rows, _bb3(qd, s) + _bb3(qkm_ref[:, t], vn))
            s = s * egl + _bb3(kd, vn, 'tn')
        s_s[...] = s

    return pl.pallas_call(
        body, name=name, grid=(Hk // SCAN_K_HEADS, nb), in_specs=[kspec, kspec, pair, pair, pair, cc],
        out_specs=[pair, state],
        out_shape=[jax.ShapeDtypeStruct((T, Hv * HD), F32), jax.ShapeDtypeStruct((Hv, N, HD, HD), F32)],
        scratch_shapes=[pltpu.VMEM((SCAN_V_HEADS, HD, HD), F32)], compiler_params=_cp("parallel", "arbitrary"),
    )(qn, kn, gcb, u, w, qkm)


def _gdn_scan_bwd(qn, kn, gcb, u, w, qkm, sprev, do, Hk, name):
    T = qn.shape[0]
    C, ns = GDN_CHUNK, GDN_SCAN_CHUNKS
    Hv, N = Hk * V_PER_K, T // C
    nb, kspec, pair, cc, state, scal = _scan_specs(T, ns, rev=True)
    nv = SCAN_V_HEADS

    def body(q_ref, k_ref, g_ref, u_ref, w_ref, qkm_ref, sp_ref, do_ref,
             dqd_ref, dkd_ref, du_ref, dw_ref, dqkm_ref, dgl_ref, ds_s):
        @pl.when(pl.program_id(1) == 0)
        def _():
            ds_s[...] = jnp.zeros_like(ds_s)

        lower = lax.broadcasted_iota(jnp.int32, (nv, C, C), 1) >= lax.broadcasted_iota(jnp.int32, (nv, C, C), 2)
        ds = ds_s[...]
        for t in reversed(range(ns)):
            rows = slice(t * C, (t + 1) * C)
            qd, kd, egl = _scan_chunk(q_ref, k_ref, g_ref, rows)
            s, w_, qkm_, dov = sp_ref[:, t], _heads(w_ref, rows), qkm_ref[:, t], _heads(do_ref, rows)
            vn = _heads(u_ref, rows) - _bb3(w_, s)
            _put_heads(dqd_ref, rows, _bb3(dov, s, 'nt'))
            dqkm_ref[:, t] = jnp.where(lower, _bb3(dov, vn, 'nt'), 0.0)
            dvn = _bb3(qkm_, dov, 'tn') + _bb3(kd, ds)
            _put_heads(dkd_ref, rows, _bb3(vn, ds, 'nt'))
            dgl = jnp.sum(jnp.sum(ds * s, axis=1, keepdims=True) * egl, axis=2, keepdims=True)
            dgl_ref[:, t] = jnp.broadcast_to(dgl, (nv, SUBLANES, HD))
            _put_heads(du_ref, rows, dvn)
            _put_heads(dw_ref, rows, -_bb3(dvn, s, 'nt'))
            ds = ds * egl + _bb3(qd, dov, 'tn') - _bb3(w_, dvn, 'tn')
        ds_s[...] = ds

    return pl.pallas_call(
        body, name=name, grid=(Hk // SCAN_K_HEADS, nb), in_specs=[kspec, kspec, pair, pair, pair, cc, state, pair],
        out_specs=[pair] * 4 + [cc, scal],
        out_shape=[jax.ShapeDtypeStruct((T, Hv * HD), F32)] * 4
        + [jax.ShapeDtypeStruct((Hv, N, C, C), F32), jax.ShapeDtypeStruct((Hv, N, SUBLANES, HD), F32)],
        scratch_shapes=[pltpu.VMEM((nv, HD, HD), F32)], compiler_params=_cp("parallel", "arbitrary"),
    )(qn, kn, gcb, u, w, qkm, sprev, do)


def _gdn_prep_bwd(qn, kn, qkvc, voff, gcb, betab, tm, u, w, qkm, dqd, dkd, du, dw, dqkm, dgl, Hk, name):
    T = qn.shape[0]
    C, nc = GDN_CHUNK, GDN_PREP_CHUNKS
    Hv = Hk * V_PER_K
    B = V_PER_K * nc
    nb, kspec, pair, vspec, cc, _, scal = _gdn_specs(T, Hk, voff, nc)

    def body(q_ref, k_ref, v_ref, g_ref, b_ref, tm_ref, u_ref, w_ref, qkm_ref, dqd_ref, dkd_ref, du_ref, dw_ref,
             dqkm_ref, dgl_ref, dq_ref, dk_ref, dv_ref, dg_ref, dbe_ref):
        q2, k2 = _both_heads(q_ref[...], nc), _both_heads(k_ref[...], nc)
        v2, gb, bb = _to_batch(v_ref[...], nc), _to_batch(g_ref[...], nc), _to_batch(b_ref[...], nc)
        lc = _gdn_local(q2, k2, gb, bb)
        dm, eg, gl = lc['dm'], lc['eg'], lc['gl']
        tm_, qkm_, dqkm_ = (r[...].reshape(B, C, C) for r in (tm_ref, qkm_ref, dqkm_ref))
        u_, w_, dqd_, dkd_, du_, dw_ = (_to_batch(r[...], nc) for r in (u_ref, w_ref, dqd_ref, dkd_ref, du_ref, dw_ref))
        dgl_ = dgl_ref[...].reshape(B, SUBLANES, HD)[:, :1, :1]
        rowsum = lambda x: jnp.sum(x, axis=-1, keepdims=True)
        dbv = _nb3(tm_, du_, 'tn')
        dbk = _nb3(tm_, dw_, 'tn')
        da = jnp.where(lc['strict'], -(_bb3(dbv, u_, 'nt') + _bb3(dbk, w_, 'nt')), 0.0)
        rk = rowsum(dbk * k2)
        dbeta = rowsum(dbv * v2) + rk * eg[:, :, :1] + rowsum(da * lc['kk'] * dm)
        dkk = da * dm * bb[:, :, :C]
        dqkr = dqkm_ * dm
        dk = dbk * (bb * eg) + _bb3(dkk, k2) + _bb3(dkk, k2, 'tn') + _bb3(dqkr, q2, 'tn') + dkd_ * jnp.exp(gl - gb)
        dq = _bb3(dqkr, k2) + dqd_ * eg
        de = da * lc['a'] + dqkm_ * qkm_
        sk = rowsum(dkd_ * lc['kd'])
        dg = rk * (bb[:, :, :1] * eg[:, :, :1]) + rowsum(de) - _hb3(de, jnp.ones((B, C, HD), F32), 'tn')[:, :, :1] \
            + rowsum(dqd_ * lc['qd']) - sk
        last = (lax.broadcasted_iota(jnp.int32, (B, C, HD), 1) == C - 1).astype(F32)
        dgb = jnp.broadcast_to(dg, (B, C, HD)) + last * (dgl_ + jnp.sum(sk, axis=1, keepdims=True))
        suffix = (lax.broadcasted_iota(jnp.int32, (B, C, C), 2) >= lax.broadcasted_iota(jnp.int32, (B, C, C), 1)).astype(F32)
        dq_ref[...] = _from_batch(dq, nc)
        dk_ref[...] = _from_batch(dk, nc)
        dv_ref[...] = _from_batch(dbv * bb, nc)
        dg_ref[...] = _from_batch(_hb3(suffix, dgb), nc)
        dbe_ref[...] = _from_batch(jnp.broadcast_to(dbeta, (B, C, HD)), nc)

    return pl.pallas_call(
        body, name=name, grid=(Hk, nb),
        in_specs=[kspec, kspec, vspec, pair, pair, cc, pair, pair, cc, pair, pair, pair, pair, cc, scal],
        out_specs=[pair] * 5, out_shape=[jax.ShapeDtypeStruct((T, Hv * HD), F32)] * 5,
        compiler_params=_cp("parallel", "parallel"),
    )(qn, kn, qkvc, gcb, betab, tm, u, w, qkm, dqd, dkd, du, dw, dqkm, dgl)


def _adamw_math(w, g, m, v):
    m = ADAM_B1 * m + (1.0 - ADAM_B1) * g
    v = ADAM_B2 * v + (1.0 - ADAM_B2) * jnp.square(g)
    m_hat = m / (1.0 - ADAM_B1 ** ADAM_STEP)
    v_hat = v / (1.0 - ADAM_B2 ** ADAM_STEP)
    delta = -ADAM_LR * (m_hat / (jnp.sqrt(v_hat) + ADAM_EPS) + ADAM_WD * w)
    return delta, m, v


STREAM_BLOCK_BYTES = 2 << 20


def _stream_rows(R, C, mult=SUBLANES):
    for tr in (512, 256, 128, 64, 32, 16, 8):
        if R % tr == 0 and tr % mult == 0 and tr * C * 4 <= STREAM_BLOCK_BYTES:
            return tr
    return R


def _adamw(w, m, v, gs, name):
    R, C = w.shape
    tr = _stream_rows(R, C)
    n = len(gs)

    def body(*refs):
        w_ref, m_ref, v_ref = refs[:3]
        g_refs = refs[3:3 + n]
        g_out, d_out, m_out, v_out = refs[3 + n:]
        g = g_refs[0][...]
        for r_ in g_refs[1:]:
            g = g + r_[...]
        g_out[...] = g
        d_out[...], m_out[...], v_out[...] = _adamw_math(w_ref[...], g, m_ref[...], v_ref[...])

    spec = pl.BlockSpec((tr, C), lambda i: (i, 0))
    return pl.pallas_call(
        body, name=name, grid=(R // tr,), in_specs=[spec] * (3 + n), out_specs=[spec] * 4,
        out_shape=[jax.ShapeDtypeStruct((R, C), F32)] * 4, compiler_params=_cp("parallel"),
    )(w, m, v, *gs)


def _sum_devices(g8, name):
    _, M, C = g8.shape
    tr = _tile(M, (512, 256, 128, 64, 32, 16, 8))

    def body(g_ref, o_ref):
        acc = g_ref[0]
        for d in range(1, N_DEV):
            acc = acc + g_ref[d]
        o_ref[...] = acc

    return pl.pallas_call(
        body, name=name, grid=(M // tr,), in_specs=[pl.BlockSpec((N_DEV, tr, C), lambda i: (0, i, 0))],
        out_specs=pl.BlockSpec((tr, C), lambda i: (i, 0)), out_shape=jax.ShapeDtypeStruct((M, C), F32),
        compiler_params=_cp("parallel"),
    )(g8)


def _ada_w_update(c_all, dm, w, m, v, name):
    n_mod, D, Ns = w.shape
    tr = _tile(D, (256, 128))

    def body(c_ref, dm_ref, w_ref, m_ref, v_ref, g_out, d_out, m_out, v_out):
        g = _hdot(_silu(c_ref[...]), dm_ref[...], 'tn')
        g_out[...] = g
        d_out[...], m_out[...], v_out[...] = _adamw_math(w_ref[...], g, m_ref[...], v_ref[...])

    wspec = pl.BlockSpec((None, tr, Ns), lambda i, r: (i, r, 0))
    return pl.pallas_call(
        body, name=name, grid=(n_mod, D // tr),
        in_specs=[pl.BlockSpec((N_DEV, tr), lambda i, r: (0, r)), pl.BlockSpec((None, N_DEV, Ns), lambda i, r: (i, 0, 0)),
                  wspec, wspec, wspec],
        out_specs=[wspec] * 4, out_shape=[jax.ShapeDtypeStruct((n_mod, D, Ns), F32)] * 4,
        compiler_params=_cp("parallel", "parallel"),
    )(c_all, dm, w, m, v)


def _place():
    return lax.axis_index("x"), lax.axis_index("y"), lax.axis_index("c")


def _allgather8(x_shard, name):
    m_per, n = x_shard.shape

    def body(x_ref, out_ref, send_sems, recv_sems, local_sem):
        x, y, c = _place()
        me, sibling = (x, y, c), (x, y, 1 - c)
        chips = [(1 - x, y), (x, 1 - y), (1 - x, 1 - y)]

        def rows(px, py, pc):
            return out_ref.at[pl.ds((4 * px + 2 * py + pc) * m_per, m_per), :]

        def copy(k, block, to, src=None):
            return pltpu.make_async_remote_copy(
                src_ref=rows(*block) if src is None else src, dst_ref=rows(*block),
                send_sem=send_sems.at[k], recv_sem=recv_sems.at[k], device_id=to, device_id_type=MESH)

        mine = pltpu.make_async_copy(x_ref, rows(*me), local_sem)
        mine.start()
        first = [copy(0, me, sibling, src=x_ref)]
        first += [copy(1 + j, me, (*chip, c), src=x_ref) for j, chip in enumerate(chips)]
        for cp in first:
            cp.start()
        passed = [copy(4 + j, (*chip, c), sibling) for j, chip in enumerate(chips)]
        for j, chip in enumerate(chips):
            copy(1 + j, (*chip, c), me).wait_recv()
            passed[j].start()
        copy(0, sibling, me).wait_recv()
        for j, chip in enumerate(chips):
            copy(4 + j, (*chip, 1 - c), me).wait_recv()
        for cp in first + passed:
            cp.wait_send()
        mine.wait()

    return pl.pallas_call(
        body, name=name, out_shape=jax.ShapeDtypeStruct((N_DEV * m_per, n), x_shard.dtype),
        in_specs=[pl.BlockSpec(memory_space=pltpu.VMEM)], out_specs=pl.BlockSpec(memory_space=pltpu.VMEM),
        scratch_shapes=[pltpu.SemaphoreType.DMA((7,)), pltpu.SemaphoreType.DMA((7,)), pltpu.SemaphoreType.DMA],
        compiler_params=pltpu.CompilerParams(vmem_limit_bytes=VMEM_LIMIT),
    )(x_shard)


HBM_SPEC = pl.BlockSpec(memory_space=pltpu.HBM)


GATHER_SEMS = 6


def _gather_protocol(w_ref, out_ref, send_sems, recv_sems, k0):
    half = w_ref.shape[0] // 2
    x, y, c = _place()
    me, sibling = (x, y, c), (x, y, 1 - c)
    chips = [(1 - x, y), (x, 1 - y), (1 - x, 1 - y)]

    def part(cx, cy, hc):
        return out_ref.at[2 * cx + cy, pl.ds(hc * half, half), :]

    def copy(k, block, to, src=None):
        return pltpu.make_async_remote_copy(
            src_ref=part(*block) if src is None else src, dst_ref=part(*block),
            send_sem=send_sems.at[k0 + k], recv_sem=recv_sems.at[k0 + k], device_id=to, device_id_type=MESH)

    def first():
        return [copy(j, me, (*chip, c), src=w_ref.at[pl.ds(c * half, half), :]) for j, chip in enumerate(chips)]

    def start():
        for cp in first():
            cp.start()

    def finish():
        passed = [copy(3 + j, (*chip, c), sibling) for j, chip in enumerate(chips)]
        for j, chip in enumerate(chips):
            copy(j, (*chip, c), me).wait_recv()
            passed[j].start()
        for j, chip in enumerate(chips):
            copy(3 + j, (*chip, 1 - c), me).wait_recv()
        for cp in first() + passed:
            cp.wait_send()

    return start, finish


def _gather_weights(w_flat, name):
    R, C = w_flat.shape

    def body(w_ref, out_ref, send_sems, recv_sems):
        start, finish = _gather_protocol(w_ref, out_ref, send_sems, recv_sems, 0)
        start()
        finish()

    return pl.pallas_call(
        body, name=name, out_shape=jax.ShapeDtypeStruct((N_CHIPS, R, C), w_flat.dtype),
        in_specs=[HBM_SPEC], out_specs=HBM_SPEC,
        scratch_shapes=[pltpu.SemaphoreType.DMA((GATHER_SEMS,)), pltpu.SemaphoreType.DMA((GATHER_SEMS,))],
    )(w_flat)


def _swap_halves(g, name):
    n, R, C = g.shape
    half = R // 2

    def body(g_ref, got_ref, send_sem, recv_sem):
        x, y, c = _place()
        cp = pltpu.make_async_remote_copy(
            src_ref=g_ref.at[:, pl.ds((1 - c) * half, half), :], dst_ref=got_ref,
            send_sem=send_sem, recv_sem=recv_sem, device_id=(x, y, 1 - c), device_id_type=MESH)
        cp.start()
        cp.wait()

    return pl.pallas_call(
        body, name=name, out_shape=jax.ShapeDtypeStruct((n, half, C), g.dtype),
        in_specs=[HBM_SPEC], out_specs=HBM_SPEC,
        scratch_shapes=[pltpu.SemaphoreType.DMA, pltpu.SemaphoreType.DMA],
    )(g)


SCATTER_SEMS = 3


def _scatter_protocol(q_ref, got_ref, send_sems, recv_sems, k0):
    x, y, c = _place()

    def copies():
        return [pltpu.make_async_remote_copy(
            src_ref=q_ref.at[2 * cx + cy], dst_ref=got_ref.at[j], send_sem=send_sems.at[k0 + j],
            recv_sem=recv_sems.at[k0 + j], device_id=(cx, cy, c), device_id_type=MESH)
            for j, (cx, cy) in enumerate([(1 - x, y), (x, 1 - y), (1 - x, 1 - y)])]

    def start():
        for cp in copies():
            cp.start()

    def finish():
        for cp in copies():
            cp.wait()

    return start, finish


def _scatter_shapes(qs):
    return [jax.ShapeDtypeStruct((N_CHIPS - 1,) + q.shape[1:], q.dtype) for q in qs]


def _join_halves(h, name):
    R, C = h.shape
    R2 = R // 2

    def body(h_ref, out_ref, send_sem, recv_sem):
        x, y, c = _place()
        cp = pltpu.make_async_remote_copy(
            src_ref=h_ref.at[pl.ds(c * R2, R2), :], dst_ref=out_ref.at[pl.ds(c * R2, R2), :],
            send_sem=send_sem, recv_sem=recv_sem, device_id=(x, y, 1 - c), device_id_type=MESH)
        cp.start()
        cp.wait()

    return pl.pallas_call(
        body, name=name, out_shape=jax.ShapeDtypeStruct((R, C), h.dtype),
        in_specs=[HBM_SPEC], out_specs=HBM_SPEC, input_output_aliases={0: 0},
        scratch_shapes=[pltpu.SemaphoreType.DMA, pltpu.SemaphoreType.DMA],
    )(h)


def _add_halves(g, got, c_idx, name):
    n, R, C = g.shape
    half = R // 2
    tr = _stream_rows(half, C, 2 * SUBLANES)
    nb = half // tr

    def body(c_ref, g_ref, got_ref, o_ref):
        o_ref[...] = (g_ref[...] + got_ref[...]).astype(o_ref.dtype)

    grid_spec = pltpu.PrefetchScalarGridSpec(
        num_scalar_prefetch=1, grid=(n, nb),
        in_specs=[pl.BlockSpec((None, tr, C), lambda s, i, c_ref: (s, c_ref[0] * nb + i, 0)),
                  pl.BlockSpec((None, tr, C), lambda s, i, c_ref: (s, i, 0))],
        out_specs=pl.BlockSpec((None, tr, C), lambda s, i, c_ref: (s, i, 0)))
    return pl.pallas_call(
        body, name=name, grid_spec=grid_spec, out_shape=jax.ShapeDtypeStruct((n, half, C), BF16),
        compiler_params=_cp("parallel", "parallel"),
    )(c_idx, g, got)


def _add_chips(q, got, sc_idx, name):
    n, R2, C = q.shape
    tr = _stream_rows(R2, C, 2 * SUBLANES)
    nb = R2 // tr

    def body(s_ref, q_ref, g0_ref, g1_ref, g2_ref, o_ref):
        o_ref[...] = ((q_ref[...].astype(F32) + g0_ref[...].astype(F32)) + g1_ref[...].astype(F32)) \
            + g2_ref[...].astype(F32)

    def got_spec(j):
        return pl.BlockSpec((None, tr, C), lambda i, s_ref: (j, i, 0))

    grid_spec = pltpu.PrefetchScalarGridSpec(
        num_scalar_prefetch=1, grid=(nb,),
        in_specs=[pl.BlockSpec((None, tr, C), lambda i, s_ref: (s_ref[0], i, 0)), got_spec(0), got_spec(1), got_spec(2)],
        out_specs=pl.BlockSpec((tr, C), lambda i, s_ref: (s_ref[1] * nb + i, 0)))
    return pl.pallas_call(
        body, name=name, grid_spec=grid_spec, out_shape=jax.ShapeDtypeStruct((2 * R2, C), F32),
        compiler_params=_cp("parallel"),
    )(sc_idx, q, got, got, got)


def _pack_lanes(arrs):
    rows = []
    for a in arrs:
        f = a.reshape(-1)
        n = -(-f.shape[0] // LANES) * LANES
        rows.append(jnp.pad(f, (0, n - f.shape[0])).reshape(-1, LANES))
    out = jnp.concatenate(rows, axis=0)
    pad = -out.shape[0] % SUBLANES
    return jnp.pad(out, ((0, pad), (0, 0)))


def _unpack_lanes(packed, shapes):
    out, r = [], 0
    for shp in shapes:
        n = math.prod(shp)
        nr = -(-n // LANES)
        out.append(packed[r:r + nr].reshape(-1)[:n].reshape(shp))
        r += nr
    return out


def _shards_to_full(sh, axis):
    return jnp.concatenate([sh[i] for i in range(N_CHIPS)], axis=axis)


def _full_to_shards(full, axis):
    return jnp.stack(jnp.split(full, N_CHIPS, axis=axis), axis=0)


def _pad_cols(a, n):
    return jnp.pad(a, ((0, 0), (0, n - a.shape[1])))


def _lane_bcast(a):
    return jnp.repeat(a, HD, axis=1)


def _split_mod(mod):
    D = mod.shape[0] // 3
    return mod[None, :D], mod[None, D:2 * D], mod[None, 2 * D:]


def _fox_fwd(h, w, tag, carry=()):
    T, D = h.shape
    H = D // HD
    proj = _mm(h, w['cat'], 'nn', F32, tag + '_proj')
    flog = _mm(h, w['f'], 'nn', F32, tag + '_flog')
    qn = _headnorm(proj, 0, H, w['q_norm'], 1.0 / HD, HD ** -0.5, BF16, tag + '_qnorm')
    kn = _headnorm(proj, H, H, w['k_norm'], 1.0 / HD, 1.0, BF16, tag + '_knorm')
    vb = proj[:, 2 * D:3 * D].astype(BF16)
    fcum = _cumsum(_logsig(flog, w['f_bias'], tag + '_logf'), None, False, tag + '_fcum')
    tq = _tile(T, ATT_TILES)
    f0 = fcum[::tq, :H].T.reshape(H, T // tq, 1, 1)
    fkr = fcum[:, :H].T.reshape(H, 1, T)
    ao, lse, carried = _flash_fwd(qn, kn, vb, f0, fkr, H, tag + '_att', carry)
    gated = _fox_gate(ao, proj, 3 * H, H, tag + '_ogate')
    y = _mm(gated, w['o'], 'nn', F32, tag + '_out')
    sv = dict(h=h, proj=proj, flog=flog, qn=qn, kn=kn, vb=vb, f0=f0, fkr=fkr, ao=ao, lse=lse, gated=gated)
    return (y, sv, carried) if carry else (y, sv)


def _mm_carry(a, b, mode, out_dtype, name, reduce_pairs, tag, grads):
    if reduce_pairs is None:
        return _mm(a, b, mode, out_dtype, name), {}
    pairs = reduce_pairs(tag, grads)
    out, got = _mm(a, b, mode, out_dtype, name, scatter=tuple(q for _, q in pairs))
    return out, {key: (q, r) for (key, q), r in zip(pairs, got)}


def _fox_bwd(dy, w, sv, tag, reduce_pairs=None):
    h, proj = sv['h'], sv['proj']
    T, D = h.shape
    H = D // HD
    g = {}
    g['o'] = _mm(sv['gated'], dy, 'tn', F32, tag + '_dwo')
    dgated = _mm(dy, w['o'], 'nt', F32, tag + '_dgated')
    dao, dog, delta = _fox_gate_bwd(dgated, sv['ao'], proj, 3 * H, H, tag + '_ogate_bwd')
    dq, dfq, dk, dv, dfk = _flash_bwd(sv['qn'], sv['kn'], sv['vb'], sv['f0'], sv['fkr'], dao, sv['lse'], delta, H,
                                      tag + '_att_bwd')
    dfcum = _pad_cols(dfq[:, ::HD] + dfk.reshape(H, T).T, LANES)
    dlogf = _cumsum(dfcum, None, True, tag + '_fcum_bwd')
    dflog, g['f_bias'] = _logsig_bwd(dlogf, sv['flog'], w['f_bias'], tag + '_logf_bwd')
    dqr, g['q_norm'] = _headnorm_bwd([(dq, 1, 0)], proj, 0, H, w['q_norm'], 1.0 / HD, HD ** -0.5, BF16,
                                     tag + '_qnorm_bwd')
    dkr, g['k_norm'] = _headnorm_bwd([(dk, 1, 0)], proj, H, H, w['k_norm'], 1.0 / HD, 1.0, BF16, tag + '_knorm_bwd')
    dproj = jnp.concatenate([dqr, dkr, dv, dog], axis=1)
    g['cat'] = _mm(h, dproj, 'tn', F32, tag + '_dwcat')
    g['f'] = _mm(h, dflog, 'tn', F32, tag + '_dwf')
    dh, exchanged = _mm_carry(dproj, w['cat'], 'nt', F32, tag + '_dh', reduce_pairs, tag, g)
    return [dh, _mm(dflog, w['f'], 'nt', F32, tag + '_dh_f')], g, exchanged


def _gdn_fwd(h, w, tag):
    T, D = h.shape
    Hk = D // HD
    Hv = V_PER_K * Hk
    proj = _mm(h, w['cat'], 'nn', F32, tag + '_proj')
    ab = _mm(h, w['ab'], 'nn', F32, tag + '_ab')
    qkvc = _dwconv(proj, 0, 4 * D, w['conv'], None, 'silu', 0, F32, tag + '_conv')
    qn = _headnorm(qkvc, 0, Hk, None, 1.0, HD ** -0.5, F32, tag + '_qnorm')
    kn = _headnorm(qkvc, Hk, Hk, None, 1.0, 1.0, F32, tag + '_knorm')
    graw, beta = _gdn_gates(ab, w['a_log'], w['dt_bias'], tag + '_gates')
    gc = _cumsum(graw, GDN_CHUNK, False, tag + '_gcum')
    gcb = _lane_bcast(gc[:, :Hv])
    betab = _lane_bcast(beta[:, Hv:2 * Hv])
    u, wk, tm, qkm = _gdn_prep(qn, kn, qkvc, 2 * D, gcb, betab, Hk, tag + '_prep')
    o, sprev = _gdn_scan(qn, kn, gcb, u, wk, qkm, Hk, tag + '_scan')
    go = _gdn_out(o, proj, 4 * Hk, Hv, w['out_norm'], tag + '_onorm')
    y = _mm(go, w['o'], 'nn', F32, tag + '_out')
    return y, dict(h=h, proj=proj, ab=ab, qkvc=qkvc, qn=qn, kn=kn, gcb=gcb, betab=betab, o=o, sprev=sprev, go=go,
                   u=u, wk=wk, tm=tm, qkm=qkm)


def _gdn_bwd(dy, w, sv, tag, reduce_pairs=None):
    h, proj, qkvc = sv['h'], sv['proj'], sv['qkvc']
    T, D = h.shape
    Hk = D // HD
    Hv = V_PER_K * Hk
    g = {}
    g['o'] = _mm(sv['go'], dy, 'tn', F32, tag + '_dwo')
    dgo = _mm(dy, w['o'], 'nt', F32, tag + '_dgo')
    do, dz, g['out_norm'] = _gdn_out_bwd(dgo, sv['o'], proj, 4 * Hk, Hv, w['out_norm'], tag + '_onorm_bwd')
    local = (sv['u'], sv['wk'], sv['qkm'])
    dqd, dkd, du, dwk, dqkm, dgl = _gdn_scan_bwd(sv['qn'], sv['kn'], sv['gcb'], *local, sv['sprev'], do, Hk,
                                                 tag + '_scan_bwd')
    dqp, dkp, dv, dgb, dbetab = _gdn_prep_bwd(sv['qn'], sv['kn'], qkvc, 2 * D, sv['gcb'], sv['betab'], sv['tm'], *local,
                                              dqd, dkd, du, dwk, dqkm, dgl, Hk, tag + '_prep_bwd')
    pairs = lambda a: [(a, V_PER_K, j) for j in range(V_PER_K)]
    dqc, _ = _headnorm_bwd(pairs(dqp), qkvc, 0, Hk, None, 1.0, HD ** -0.5, F32, tag + '_qnorm_bwd')
    dkc, _ = _headnorm_bwd(pairs(dkp), qkvc, Hk, Hk, None, 1.0, 1.0, F32, tag + '_knorm_bwd')
    zeros = jnp.zeros((T, Hv), F32)
    dg_pad = _pad_cols(dgb[:, ::HD], LANES)
    dbeta_pad = _pad_cols(jnp.concatenate([zeros, dbetab[:, ::HD]], axis=1), LANES)
    dab, g['a_log'], g['dt_bias'] = _gdn_gates_bwd(dg_pad, dbeta_pad, sv['ab'], w['a_log'], w['dt_bias'], tag + '_gates_bwd')
    dpq, dwq = _dwconv_bwd(proj, 0, D, w['conv'], 0, None, 'silu', 0, dqc, tag + '_conv_bwd_q')
    dpk, dwk = _dwconv_bwd(proj, D, D, w['conv'], D, None, 'silu', 0, dkc, tag + '_conv_bwd_k')
    dpv, dwv = _dwconv_bwd(proj, 2 * D, 2 * D, w['conv'], 2 * D, None, 'silu', 0, dv, tag + '_conv_bwd_v')
    g['conv'] = jnp.concatenate([dwq, dwk, dwv], axis=1)
    dproj = jnp.concatenate([dpq, dpk, dpv, dz], axis=1)
    g['cat'] = _mm(h, dproj, 'tn', F32, tag + '_dwcat')
    g['ab'] = _mm(h, dab, 'tn', F32, tag + '_dwab')
    dh, exchanged = _mm_carry(dproj, w['cat'], 'nt', F32, tag + '_dh', reduce_pairs, tag, g)
    return [dh, _mm(dab, w['ab'], 'nt', F32, tag + '_dh_ab')], g, exchanged


def _ffn_fwd(h, w, tag):
    dff = w['down'].shape[0]
    up = _mm(h, w['up'], 'nn', F32, tag + '_up')
    act = _dwconv(up, 0, dff, w['conv'], w['conv_b'], 'glu', dff, BF16, tag + '_conv')
    y = _mm(act, w['down'], 'nn', F32, tag + '_down')
    return y, dict(h=h, up=up, act=act)


def _ffn_bwd(dy, w, sv, tag, reduce_pairs=None):
    h, up = sv['h'], sv['up']
    dff = w['down'].shape[0]
    g = {}
    g['down'] = _mm(sv['act'], dy, 'tn', F32, tag + '_dwdown')
    dact = _mm(dy, w['down'], 'nt', F32, tag + '_dact')
    dgate, dval, g['conv'], g['conv_b'] = _dwconv_bwd(up, 0, dff, w['conv'], 0, w['conv_b'], 'glu', dff, dact,
                                                      tag + '_conv_bwd')
    dup = jnp.concatenate([dgate, dval], axis=1)
    g['up'] = _mm(h, dup, 'tn', F32, tag + '_dwup')
    dh, exchanged = _mm_carry(dup, w['up'], 'nt', F32, tag + '_dh', reduce_pairs, tag, g)
    return [dh], g, exchanged


def _local_step(x, target, mods, norm_g, wf, wg, wffn, late=None, reduce_pairs=None):
    tape = []
    for i in range(2):
        for sub in range(2):
            if sub == 0:
                fwd, bwd, w, tag = [(_fox_fwd, _fox_bwd, wf, 'fox'), (_gdn_fwd, _gdn_bwd, wg, 'gdn')][i]
            else:
                fwd, bwd, w, tag = _ffn_fwd, _ffn_bwd, wffn[i], 'ffn%d' % i
            shift, scale, gate = _split_mod(mods[i, sub])
            g_pre, g_post = norm_g[i, 2 * sub][None], norm_g[i, 2 * sub + 1][None]
            h = _pre_norm(x, g_pre, scale, shift, tag + '_prenorm')
            if late is not None and (i, sub) == (0, 0):
                y, sv, gathered = fwd(h, w, tag, late[0])
                wg, wffn = late[1](gathered)
            else:
                y, sv = fwd(h, w, tag)
            x_out = _post_res(x, y, gate, g_post, tag + '_postnorm')
            tape.append((bwd, w, tag, sv, x, y, g_pre, g_post, scale, gate))
            x = x_out
    dx, lsum = _loss_head(x, target, 'loss_head')
    loss = lsum[0, 0]
    dmods = [[None, None], [None, None]]
    dnorm = [[None] * 4, [None] * 4]
    wgrads = {}
    exchanged = {}
    for idx in reversed(range(4)):
        i, sub = divmod(idx, 2)
        bwd, w, tag, sv, x_in, y, g_pre, g_post, scale, gate = tape[idx]
        dy, dgate, dgpost = _post_res_bwd(dx, y, gate, g_post, tag + '_postnorm_bwd')
        dh, wgrads[tag], ex = bwd(dy, w, sv, tag, reduce_pairs)
        exchanged.update(ex)
        dx, dshift, dscale, dgpre = _pre_norm_bwd(dh, x_in, g_pre, scale, dx, tag + '_prenorm_bwd')
        dmods[i][sub] = jnp.concatenate([dshift[0], dscale[0], dgate[0]])
        dnorm[i][2 * sub], dnorm[i][2 * sub + 1] = dgpre[0], dgpost[0]
    dmods = jnp.stack([jnp.stack(r) for r in dmods])
    dnorm = jnp.stack([jnp.stack(r) for r in dnorm])
    return loss, dx, dmods, dnorm, wgrads, exchanged


def _unpack_lanes_dev(packed, shapes):
    n = packed.shape[0]
    out, r = [], 0
    for shp in shapes:
        k = math.prod(shp)
        nr = -(-k // LANES)
        out.append(packed[:, r:r + nr].reshape(n, -1)[:, :k].reshape((n,) + tuple(shp)))
        r += nr
    return out


def _gather_lanes(arrs, name):
    packed = _pack_lanes(arrs)
    got = _allgather8(packed, name).reshape(N_DEV, packed.shape[0], LANES)
    return _unpack_lanes_dev(got, [a.shape for a in arrs]), got


def kernel(x, c, ada_w, ada_b, norm_g, fox_w_in, fox_f_bias, fox_q_norm, fox_k_norm, fox_w_o, gdn_w_in, gdn_conv_w, gdn_a_log, gdn_dt_bias, gdn_out_norm, gdn_w_o, ffn_w_up, ffn_conv_w, ffn_conv_b, ffn_w_down, loss_target, m_ada_w, m_ada_b, m_norm_g, m_fox_w_in, m_fox_f_bias, m_fox_q_norm, m_fox_k_norm, m_fox_w_o, m_gdn_w_in, m_gdn_conv_w, m_gdn_a_log, m_gdn_dt_bias, m_gdn_out_norm, m_gdn_w_o, m_ffn_w_up, m_ffn_conv_w, m_ffn_conv_b, m_ffn_w_down, v_ada_w, v_ada_b, v_norm_g, v_fox_w_in, v_fox_f_bias, v_fox_q_norm, v_fox_k_norm, v_fox_w_o, v_gdn_w_in, v_gdn_conv_w, v_gdn_a_log, v_gdn_dt_bias, v_gdn_out_norm, v_gdn_w_o, v_ffn_w_up, v_ffn_conv_w, v_ffn_conv_b, v_ffn_w_down):
    args = locals()
    w = {n: args[n] for n in WEIGHTS}
    mom = {n: args['m_' + n] for n in WEIGHTS}
    var = {n: args['v_' + n] for n in WEIGHTS}
    _, T, D = x.shape
    H = D // HD
    Hv = V_PER_K * H
    xi, yi, ci = _place()
    s_idx = 2 * xi + yi
    b_idx = 4 * xi + 2 * yi + ci
    sc_arr = jnp.stack([s_idx, ci]).astype(jnp.int32)
    c_arr = jnp.reshape(ci, (1,)).astype(jnp.int32)

    (c_all, ab_all, ng_all, gcw_all, fcw_all), _ = _gather_lanes(
        [jnp.tile(c, (SUBLANES, 1)), ada_b, norm_g, gdn_conv_w, ffn_conv_w], 'gather_small')
    c_all = c_all[:, 0, :]
    chips = lambda a: jnp.concatenate([a[2 * s] for s in range(N_CHIPS)], axis=-1)
    norm_g_full, gdn_conv_full, ffn_conv_full = chips(ng_all), chips(gcw_all), chips(fcw_all)

    Ns = ada_w.shape[-1]
    ada_w4 = ada_w.reshape(4, D, Ns)
    part = jnp.stack([_mm(c_all, ada_w4[i], 'nn', F32, 'ada_proj%d' % i, a_act='silu') for i in range(4)])
    part = part + ada_b.reshape(4, 1, Ns)
    (part_all,), _ = _gather_lanes([part], 'gather_mods')
    mine = lax.dynamic_index_in_dim(part_all[0::2], b_idx, axis=2, keepdims=False)
    mods = mine.transpose(1, 0, 2).reshape(2, 2, N_CHIPS * Ns)

    as2d = lambda a: a.reshape(-1, a.shape[-1])
    own = {n: as2d(w[n]).astype(BF16) for n in BIG}

    def whole(n, gathered):
        shards = lax.dynamic_update_index_in_dim(gathered, own[n], s_idx, 0)
        return _shards_to_full(shards.reshape((N_CHIPS,) + w[n].shape), BIG_SHARD_AXIS[n])

    fw = whole('fox_w_in', _gather_weights(own['fox_w_in'], 'gather_fox_w_in'))[0]
    wf = dict(cat=jnp.concatenate([fw[:, :3 * D], fw[:, 3 * D + H:]], axis=1), f=_pad_cols(fw[:, 3 * D:3 * D + H], LANES),
              f_bias=_pad_cols(fox_f_bias, LANES), q_norm=fox_q_norm, k_norm=fox_k_norm,
              o=whole('fox_w_o', _gather_weights(own['fox_w_o'], 'gather_fox_w_o'))[0])
    later = [n for n in BIG if not n.startswith('fox')]

    def later_weights(gathered):
        full = {n: whole(n, a) for n, a in zip(later, gathered)}
        gw = full['gdn_w_in'][0]
        wg = dict(cat=gw[:, :6 * D], ab=_pad_cols(gw[:, 6 * D:], LANES), conv=gdn_conv_full[0],
                  a_log=_pad_cols(gdn_a_log, LANES), dt_bias=_pad_cols(gdn_dt_bias, LANES), out_norm=gdn_out_norm,
                  o=full['gdn_w_o'][0])
        wffn = [dict(up=full['ffn_w_up'][i], conv=ffn_conv_full[i], conv_b=ffn_conv_b[i][None],
                     down=full['ffn_w_down'][i]) for i in range(2)]
        return wg, wffn

    def reduce_pairs(tag, gr):
        if tag == 'fox':
            full = {('fox_w_in', 0): jnp.concatenate([gr['cat'][:, :3 * D], gr['f'][:, :H], gr['cat'][:, 3 * D:]], axis=1),
                    ('fox_w_o', 0): gr['o']}
        elif tag == 'gdn':
            full = {('gdn_w_in', 0): jnp.concatenate([gr['cat'], gr['ab'][:, :2 * Hv]], axis=1), ('gdn_w_o', 0): gr['o']}
        else:
            layer = int(tag[-1])
            full = {('ffn_w_up', layer): gr['up'], ('ffn_w_down', layer): gr['down']}
        out = []
        for (n, layer), a in full.items():
            shards = _full_to_shards(a, BIG_SHARD_AXIS[n] - 1)
            name = '%s%d' % (n, layer)
            got = _swap_halves(shards, name + '_to_sibling')
            out.append(((n, layer), _add_halves(shards, got, c_arr, name + '_add_sibling')))
        return out

    loss, dx, dmods, dnorm, g, exchanged = _local_step(
        x[0], loss_target[0], mods, norm_g_full, wf, None, None,
        late=([own[n] for n in later], later_weights), reduce_pairs=reduce_pairs)
    loss = lax.psum(loss, ('x', 'y', 'c'))

    gf, gg = g['fox'], g['gdn']
    big_out = {}
    for n in BIG:
        parts = []
        for layer in range(w[n].shape[0]):
            tag = '%s%d' % (n, layer)
            pair_sum, received = exchanged[(n, layer)]
            half_sum = _add_chips(pair_sum, received, sc_arr, tag + '_add_chips')
            parts.append(_join_halves(half_sum, tag + '_join'))
        g_shard = parts[0] if len(parts) == 1 else jnp.concatenate(parts, axis=0)
        big_out[n] = [o.reshape(w[n].shape)
                      for o in _adamw(as2d(w[n]), as2d(mom[n]), as2d(var[n]), [g_shard], 'adamw_' + n)]

    small_part = [dmods, dnorm, gf['f_bias'][:, :H], gf['q_norm'], gf['k_norm'], gg['conv'][None],
                  gg['a_log'][:, :Hv], gg['dt_bias'][:, :Hv], gg['out_norm'],
                  jnp.stack([g['ffn0']['conv'], g['ffn1']['conv']]),
                  jnp.concatenate([g['ffn0']['conv_b'], g['ffn1']['conv_b']], axis=0)]
    (dmods_all, *_), got = _gather_lanes(small_part, 'gather_small_grads')
    tot = _unpack_lanes(_sum_devices(got, 'sum_small_grads'), [a.shape for a in small_part])
    small_full = dict(zip(SMALL, tot))
    small_g = {n: (lax.dynamic_slice_in_dim(small_full[n], s_idx * w[n].shape[-1], w[n].shape[-1], axis=-1)
                   if n in SMALL_SHARDED else small_full[n]) for n in SMALL}
    packs = lambda d: _pack_lanes([d[n] for n in SMALL])
    small_shapes = [w[n].shape for n in SMALL]
    small_out = [_unpack_lanes(o, small_shapes)
                 for o in _adamw(packs(w), packs(mom), packs(var), [packs(small_g)], 'adamw_small')]

    dm = lax.dynamic_slice_in_dim(dmods_all.reshape(N_DEV, 4, N_CHIPS * Ns), s_idx * Ns, Ns, axis=-1).transpose(1, 0, 2)
    ada_out = [o.reshape(ada_w.shape) for o in
               _ada_w_update(c_all, dm, ada_w4, m_ada_w.reshape(4, D, Ns), v_ada_w.reshape(4, D, Ns), 'adamw_ada_w')]

    outs = []
    for k in range(4):
        by_name = {'ada_w': ada_out[k]}
        by_name.update({n: big_out[n][k] for n in BIG})
        by_name.update(zip(SMALL, small_out[k]))
        outs += [by_name[n] for n in WEIGHTS]
    return (loss, dx[None], *outs)
```

```python
import functools
import math

import jax
import jax.numpy as jnp
from jax import lax
from jax.experimental import pallas as pl
from jax.experimental.pallas import tpu as pltpu

F32 = jnp.float32
BF16 = jnp.bfloat16
EPS = 1e-6
HD = 128
GDN_CHUNK = 64
GDN_CONV = 4
FFN_CONV = 3
LANES = 128
SUBLANES = 8
VMEM_LIMIT = 56 * 1024 * 1024
HIGHEST = lax.Precision.HIGHEST
NEG = -1e30

ADAM_LR = 0.001
ADAM_B1 = 0.9
ADAM_B2 = 0.999
ADAM_EPS = 1e-08
ADAM_WD = 0.01
ADAM_STEP = 10

WEIGHTS = ['ada_w', 'ada_b', 'norm_g', 'fox_w_in', 'fox_f_bias', 'fox_q_norm', 'fox_k_norm', 'fox_w_o',
           'gdn_w_in', 'gdn_conv_w', 'gdn_a_log', 'gdn_dt_bias', 'gdn_out_norm', 'gdn_w_o',
           'ffn_w_up', 'ffn_conv_w', 'ffn_conv_b', 'ffn_w_down']
BIG = ['fox_w_in', 'fox_w_o', 'gdn_w_in', 'gdn_w_o', 'ffn_w_up', 'ffn_w_down']
BIG_SHARD_AXIS = {'fox_w_in': 2, 'fox_w_o': 1, 'gdn_w_in': 2, 'gdn_w_o': 1, 'ffn_w_up': 2, 'ffn_w_down': 1}
SMALL = ['ada_b', 'norm_g', 'fox_f_bias', 'fox_q_norm', 'fox_k_norm', 'gdn_conv_w', 'gdn_a_log',
         'gdn_dt_bias', 'gdn_out_norm', 'ffn_conv_w', 'ffn_conv_b']
SMALL_SHARDED = ['ada_b', 'norm_g', 'gdn_conv_w', 'ffn_conv_w']
N_CHIPS = 4
N_DEV = 8
MESH = pl.DeviceIdType.MESH


def _tile(n, cands):
    for c in cands:
        if n % c == 0:
            return c
    return n


def _cp(*sem):
    return pltpu.CompilerParams(dimension_semantics=sem, vmem_limit_bytes=VMEM_LIMIT)


def _dot(a, b, mode='nn', precision=None):
    dims = {'nn': (((1,), (0,)), ((), ())), 'nt': (((1,), (1,)), ((), ())), 'tn': (((0,), (0,)), ((), ()))}[mode]
    return lax.dot_general(a, b, dims, precision=precision, preferred_element_type=F32)


def _bdot(a, b, mode='nn'):
    return _dot(a.astype(BF16), b.astype(BF16), mode)


def _hdot(a, b, mode='nn'):
    return _dot(a, b, mode, precision=HIGHEST)


def _sigmoid(x):
    return 1.0 / (1.0 + jnp.exp(-x))


def _silu(x):
    return x * _sigmoid(x)


def _softplus(x):
    return jnp.maximum(x, 0.0) + jnp.log(1.0 + jnp.exp(-jnp.abs(x)))


def _erf(x):
    return lax.erf(x)


def _gelu(x):
    return 0.5 * x * (1.0 + _erf(x * (2.0 ** -0.5)))


def _normal_cdf_pdf(x):
    cdf = 0.5 * (1.0 + _erf(x * (2.0 ** -0.5)))
    pdf = jnp.exp(-0.5 * x * x) * (1.0 / math.sqrt(2.0 * math.pi))
    return cdf, pdf


MM_K_CAP = 2816


def _k_tile(K, cap):
    for t in range(cap - cap % LANES, 0, -LANES):
        if K % t == 0:
            return t
    return K


def _mm(a, b, mode, out_dtype, name, a_act=None, scatter=()):
    if mode == 'nn':
        (M, K), (_, N) = a.shape, b.shape
    elif mode == 'nt':
        (M, K), (N, _) = a.shape, b.shape
    else:
        (K, M), (_, N) = a.shape, b.shape
    big = (1024, 512, 256, 128)
    narrow = a.dtype.itemsize == 2 and b.dtype.itemsize == 2
    tm, tn, tk = _tile(M, big), _tile(N, big), _k_tile(K, MM_K_CAP if narrow else MM_K_CAP // 2)
    nk, ns = K // tk, len(scatter)
    grid = (M // tm, N // tn, nk)

    def body(a_ref, b_ref, *rest):
        q_refs, o_ref, got_refs = rest[:ns], rest[ns], rest[ns + 1:2 * ns + 1]
        acc = rest[2 * ns + 1:2 * ns + 1 + (nk > 1)]
        jobs = [_scatter_protocol(q_refs[i], got_refs[i], *rest[len(rest) - 2:], SCATTER_SEMS * i) for i in range(ns)]
        step = [pl.program_id(d) for d in range(3)]
        if jobs:
            @pl.when((step[0] == 0) & (step[1] == 0) & (step[2] == 0))
            def _():
                for start, _ in jobs:
                    start()

        av = a_ref[...]
        if a_act == 'silu':
            av = _silu(av.astype(F32))
        part = _bdot(av, b_ref[...], mode)
        if nk == 1:
            o_ref[...] = part.astype(o_ref.dtype)
        else:
            acc_ref, = acc
            k = step[2]

            @pl.when(k == 0)
            def _():
                acc_ref[...] = part

            @pl.when(k > 0)
            def _():
                acc_ref[...] += part

            @pl.when(k == nk - 1)
            def _():
                o_ref[...] = acc_ref[...].astype(o_ref.dtype)

        if jobs:
            @pl.when((step[0] == grid[0] - 1) & (step[1] == grid[1] - 1) & (step[2] == grid[2] - 1))
            def _():
                for _, finish in jobs:
                    finish()

    if mode == 'nn':
        a_spec = pl.BlockSpec((tm, tk), lambda i, j, k: (i, k))
        b_spec = pl.BlockSpec((tk, tn), lambda i, j, k: (k, j))
    elif mode == 'nt':
        a_spec = pl.BlockSpec((tm, tk), lambda i, j, k: (i, k))
        b_spec = pl.BlockSpec((tn, tk), lambda i, j, k: (j, k))
    else:
        a_spec = pl.BlockSpec((tk, tm), lambda i, j, k: (k, i))
        b_spec = pl.BlockSpec((tk, tn), lambda i, j, k: (k, j))
    sems = [pltpu.SemaphoreType.DMA((SCATTER_SEMS * ns,))] * 2 if ns else []
    out = pl.pallas_call(
        body, name=name, grid=grid,
        in_specs=[a_spec, b_spec] + [HBM_SPEC] * ns,
        out_specs=[pl.BlockSpec((tm, tn), lambda i, j, k: (i, j))] + [HBM_SPEC] * ns,
        out_shape=[jax.ShapeDtypeStruct((M, N), out_dtype)] + _scatter_shapes(scatter),
        scratch_shapes=([pltpu.VMEM((tm, tn), F32)] if nk > 1 else []) + sems,
        compiler_params=_cp("arbitrary", "arbitrary", "arbitrary") if ns else _cp("parallel", "parallel", "arbitrary"),
    )(a, b, *scatter)
    return (out[0], list(out[1:])) if ns else out[0]


ROW_TILES = (256, 128, 64, 32, 16, 8)


def _row_spec(tT, D):
    return pl.BlockSpec((tT, D), lambda i: (i, 0))


def _vec_spec(D):
    return pl.BlockSpec((1, D), lambda i: (0, 0))


def _pre_norm(x, g, scale, shift, name):
    T, D = x.shape
    tT = _tile(T, ROW_TILES)

    def body(x_ref, g_ref, sc_ref, sh_ref, h_ref):
        xv = x_ref[...]
        r = lax.rsqrt(jnp.mean(xv * xv, axis=-1, keepdims=True) + EPS)
        h_ref[...] = ((xv * r) * g_ref[...] * (1.0 + sc_ref[...]) + sh_ref[...]).astype(h_ref.dtype)

    return pl.pallas_call(
        body, name=name, grid=(T // tT,),
        in_specs=[_row_spec(tT, D), _vec_spec(D), _vec_spec(D), _vec_spec(D)],
        out_specs=_row_spec(tT, D), out_shape=jax.ShapeDtypeStruct((T, D), BF16),
        compiler_params=_cp("parallel"),
    )(x, g, scale, shift)


def _post_res(x, y, gate, g, name):
    T, D = x.shape
    tT = _tile(T, ROW_TILES)

    def body(x_ref, y_ref, gate_ref, g_ref, o_ref):
        yv = y_ref[...]
        r = lax.rsqrt(jnp.mean(yv * yv, axis=-1, keepdims=True) + EPS)
        o_ref[...] = x_ref[...] + gate_ref[...] * ((yv * r) * g_ref[...])

    return pl.pallas_call(
        body, name=name, grid=(T // tT,),
        in_specs=[_row_spec(tT, D), _row_spec(tT, D), _vec_spec(D), _vec_spec(D)],
        out_specs=_row_spec(tT, D), out_shape=jax.ShapeDtypeStruct((T, D), F32),
        compiler_params=_cp("parallel"),
    )(x, y, gate, g)


def _post_res_bwd(dout, y, gate, g, name):
    T, D = y.shape
    tT = _tile(T, ROW_TILES)

    def body(do_ref, y_ref, gate_ref, g_ref, dy_ref, dgate_ref, dg_ref):
        @pl.when(pl.program_id(0) == 0)
        def _():
            dgate_ref[...] = jnp.zeros_like(dgate_ref)
            dg_ref[...] = jnp.zeros_like(dg_ref)

        yv, dov, gatev, gv = y_ref[...], do_ref[...], gate_ref[...], g_ref[...]
        r = lax.rsqrt(jnp.mean(yv * yv, axis=-1, keepdims=True) + EPS)
        yn = yv * r
        t = dov * yn
        dgate_ref[...] += jnp.sum(t * gv, axis=0, keepdims=True)
        dg_ref[...] += jnp.sum(t * gatev, axis=0, keepdims=True)
        dyn = dov * (gatev * gv)
        dy_ref[...] = (r * (dyn - yn * jnp.mean(dyn * yn, axis=-1, keepdims=True))).astype(dy_ref.dtype)

    return pl.pallas_call(
        body, name=name, grid=(T // tT,),
        in_specs=[_row_spec(tT, D), _row_spec(tT, D), _vec_spec(D), _vec_spec(D)],
        out_specs=[_row_spec(tT, D), _vec_spec(D), _vec_spec(D)],
        out_shape=[jax.ShapeDtypeStruct((T, D), BF16), jax.ShapeDtypeStruct((1, D), F32),
                   jax.ShapeDtypeStruct((1, D), F32)],
        compiler_params=_cp("arbitrary"),
    )(dout, y, gate, g)


def _pre_norm_bwd(dhs, x, g, scale, dres, name):
    T, D = x.shape
    tT = _tile(T, ROW_TILES)
    n = len(dhs)

    def body(*refs):
        dh_refs = refs[:n]
        x_ref, g_ref, sc_ref, dres_ref, dx_ref, dsh_ref, dsc_ref, dg_ref = refs[n:]

        @pl.when(pl.program_id(0) == 0)
        def _():
            dsh_ref[...] = jnp.zeros_like(dsh_ref)
            dsc_ref[...] = jnp.zeros_like(dsc_ref)
            dg_ref[...] = jnp.zeros_like(dg_ref)

        dh = dh_refs[0][...]
        for r_ in dh_refs[1:]:
            dh = dh + r_[...]
        xv, gv, scv = x_ref[...], g_ref[...], sc_ref[...]
        r = lax.rsqrt(jnp.mean(xv * xv, axis=-1, keepdims=True) + EPS)
        xn = xv * r
        t = dh * xn
        dsh_ref[...] += jnp.sum(dh, axis=0, keepdims=True)
        dsc_ref[...] += jnp.sum(t * gv, axis=0, keepdims=True)
        dg_ref[...] += jnp.sum(t * (1.0 + scv), axis=0, keepdims=True)
        dxn = dh * (gv * (1.0 + scv))
        dx_ref[...] = dres_ref[...] + r * (dxn - xn * jnp.mean(dxn * xn, axis=-1, keepdims=True))

    return pl.pallas_call(
        body, name=name, grid=(T // tT,),
        in_specs=[_row_spec(tT, D)] * n + [_row_spec(tT, D), _vec_spec(D), _vec_spec(D), _row_spec(tT, D)],
        out_specs=[_row_spec(tT, D), _vec_spec(D), _vec_spec(D), _vec_spec(D)],
        out_shape=[jax.ShapeDtypeStruct((T, D), F32)] + [jax.ShapeDtypeStruct((1, D), F32)] * 3,
        compiler_params=_cp("arbitrary"),
    )(*dhs, x, g, scale, dres)


def _loss_head(y, target, name):
    T, D = y.shape
    tT = _tile(T, ROW_TILES)

    def body(y_ref, t_ref, dy_ref, l_ref):
        @pl.when(pl.program_id(0) == 0)
        def _():
            l_ref[...] = jnp.zeros_like(l_ref)

        e = y_ref[...] - t_ref[...]
        dy_ref[...] = e * (1.0 / D)
        s = jnp.sum(jnp.mean(e * e, axis=-1, keepdims=True), axis=0, keepdims=True)
        l_ref[...] += 0.5 * s

    return pl.pallas_call(
        body, name=name, grid=(T // tT,),
        in_specs=[_row_spec(tT, D), _row_spec(tT, D)],
        out_specs=[_row_spec(tT, D), pl.BlockSpec((SUBLANES, LANES), lambda i: (0, 0))],
        out_shape=[jax.ShapeDtypeStruct((T, D), F32), jax.ShapeDtypeStruct((SUBLANES, LANES), F32)],
        compiler_params=_cp("arbitrary"),
    )(y, target)


HEAD_ROW_TILES = (2048, 1024, 512, 256, 128, 64)


def _hb(tT, off=0):
    return pl.BlockSpec((tT, HD), lambda i, h: (i, off + h))


def _hvec():
    return pl.BlockSpec((1, HD), lambda i, h: (0, 0))


def _headnorm(x, off, H, g, c1, post, out_dtype, name):
    T = x.shape[0]
    tT = _tile(T, HEAD_ROW_TILES)
    has_g = g is not None

    def body(*refs):
        x_ref = refs[0]
        o_ref = refs[-1]
        xv = x_ref[...]
        yv = xv * lax.rsqrt(c1 * jnp.sum(xv * xv, axis=-1, keepdims=True) + EPS)
        if has_g:
            yv = yv * refs[1][...]
        if post != 1.0:
            yv = yv * post
        o_ref[...] = yv.astype(o_ref.dtype)

    return pl.pallas_call(
        body, name=name, grid=(T // tT, H),
        in_specs=[_hb(tT, off)] + ([_hvec()] if has_g else []),
        out_specs=_hb(tT), out_shape=jax.ShapeDtypeStruct((T, H * HD), out_dtype),
        compiler_params=_cp("parallel", "parallel"),
    )(*([x, g] if has_g else [x]))


def _headnorm_bwd(dys, x, off, H, g, c1, post, out_dtype, name):
    T = x.shape[0]
    tT = _tile(T, HEAD_ROW_TILES)
    n = len(dys)
    has_g = g is not None

    def body(*refs):
        dy_refs = refs[:n]
        x_ref = refs[n]
        g_ref = refs[n + 1] if has_g else None
        dx_ref, dg_ref = refs[-2], refs[-1]

        @pl.when((pl.program_id(0) == 0) & (pl.program_id(1) == 0))
        def _():
            dg_ref[...] = jnp.zeros_like(dg_ref)

        dy = dy_refs[0][...].astype(F32)
        for r_ in dy_refs[1:]:
            dy = dy + r_[...].astype(F32)
        if post != 1.0:
            dy = dy * post
        xv = x_ref[...]
        r = lax.rsqrt(c1 * jnp.sum(xv * xv, axis=-1, keepdims=True) + EPS)
        xn = xv * r
        if has_g:
            dg_ref[...] += jnp.sum(dy * xn, axis=0, keepdims=True)
            dy = dy * g_ref[...]
        dx_ref[...] = (r * (dy - xn * (c1 * jnp.sum(dy * xn, axis=-1, keepdims=True)))).astype(dx_ref.dtype)

    dy_specs = [pl.BlockSpec((tT, HD), lambda i, h, st=st, of=of: (i, st * h + of)) for (_, st, of) in dys]
    return pl.pallas_call(
        body, name=name, grid=(T // tT, H),
        in_specs=dy_specs + [_hb(tT, off)] + ([_hvec()] if has_g else []),
        out_specs=[_hb(tT), _hvec()],
        out_shape=[jax.ShapeDtypeStruct((T, H * HD), out_dtype), jax.ShapeDtypeStruct((1, HD), F32)],
        compiler_params=_cp("arbitrary", "arbitrary"),
    )(*[d[0] for d in dys], x, *([g] if has_g else []))


def _fox_gate(ao, proj, og_off, H, name):
    T = ao.shape[0]
    tT = _tile(T, HEAD_ROW_TILES)

    def body(ao_ref, og_ref, o_ref):
        o_ref[...] = (ao_ref[...] * _sigmoid(og_ref[...])).astype(o_ref.dtype)

    return pl.pallas_call(
        body, name=name, grid=(T // tT, H),
        in_specs=[_hb(tT), _hb(tT, og_off)], out_specs=_hb(tT),
        out_shape=jax.ShapeDtypeStruct((T, H * HD), BF16), compiler_params=_cp("parallel", "parallel"),
    )(ao, proj)


def _fox_gate_bwd(dgated, ao, proj, og_off, H, name):
    T = ao.shape[0]
    tT = _tile(T, HEAD_ROW_TILES)

    def body(dg_ref, ao_ref, og_ref, dao_ref, dog_ref, delta_ref):
        dg, aov = dg_ref[...], ao_ref[...]
        sg = _sigmoid(og_ref[...])
        dao = dg * sg
        dao_ref[...] = dao.astype(dao_ref.dtype)
        dog_ref[...] = (dg * aov * sg * (1.0 - sg)).astype(dog_ref.dtype)
        delta_ref[...] = jnp.broadcast_to(jnp.sum(dao * aov, axis=-1, keepdims=True), delta_ref.shape)

    return pl.pallas_call(
        body, name=name, grid=(T // tT, H),
        in_specs=[_hb(tT), _hb(tT), _hb(tT, og_off)], out_specs=[_hb(tT)] * 3,
        out_shape=[jax.ShapeDtypeStruct((T, H * HD), BF16), jax.ShapeDtypeStruct((T, H * HD), BF16),
                   jax.ShapeDtypeStruct((T, H * HD), F32)],
        compiler_params=_cp("parallel", "parallel"),
    )(dgated, ao, proj)


def _gdn_out(o, proj, z_off, Hv, g, name):
    T = o.shape[0]
    tT = _tile(T, HEAD_ROW_TILES)

    def body(o_ref, z_ref, g_ref, y_ref):
        ov, zv = o_ref[...], z_ref[...]
        r = lax.rsqrt(jnp.mean(ov * ov, axis=-1, keepdims=True) + EPS)
        y_ref[...] = (((ov * r) * g_ref[...]) * _silu(zv)).astype(y_ref.dtype)

    return pl.pallas_call(
        body, name=name, grid=(T // tT, Hv),
        in_specs=[_hb(tT), _hb(tT, z_off), _hvec()], out_specs=_hb(tT),
        out_shape=jax.ShapeDtypeStruct((T, Hv * HD), BF16), compiler_params=_cp("parallel", "parallel"),
    )(o, proj, g)


def _gdn_out_bwd(dy, o, proj, z_off, Hv, g, name):
    T = o.shape[0]
    tT = _tile(T, HEAD_ROW_TILES)

    def body(dy_ref, o_ref, z_ref, g_ref, do_ref, dz_ref, dg_ref):
        @pl.when((pl.program_id(0) == 0) & (pl.program_id(1) == 0))
        def _():
            dg_ref[...] = jnp.zeros_like(dg_ref)

        dyv, ov, zv, gv = dy_ref[...], o_ref[...], z_ref[...], g_ref[...]
        r = lax.rsqrt(jnp.mean(ov * ov, axis=-1, keepdims=True) + EPS)
        on = ov * r
        sg = _sigmoid(zv)
        sz = zv * sg
        dz_ref[...] = (dyv * (on * gv) * (sg * (1.0 + zv * (1.0 - sg)))).astype(dz_ref.dtype)
        t = dyv * sz
        dg_ref[...] += jnp.sum(t * on, axis=0, keepdims=True)
        don = t * gv
        do_ref[...] = r * (don - on * jnp.mean(don * on, axis=-1, keepdims=True))

    return pl.pallas_call(
        body, name=name, grid=(T // tT, Hv),
        in_specs=[_hb(tT), _hb(tT), _hb(tT, z_off), _hvec()], out_specs=[_hb(tT), _hb(tT), _hvec()],
        out_shape=[jax.ShapeDtypeStruct((T, Hv * HD), F32), jax.ShapeDtypeStruct((T, Hv * HD), BF16),
                   jax.ShapeDtypeStruct((1, HD), F32)],
        compiler_params=_cp("arbitrary", "arbitrary"),
    )(dy, o, proj, g)


def _lrow(tT):
    return pl.BlockSpec((tT, LANES), lambda i: (i, 0))


def _lvec():
    return pl.BlockSpec((1, LANES), lambda i: (0, 0))


def _logsig(x, b, name):
    T = x.shape[0]
    tT = _tile(T, HEAD_ROW_TILES)

    def body(x_ref, b_ref, o_ref):
        o_ref[...] = -_softplus(-(x_ref[...] + b_ref[...]))

    return pl.pallas_call(body, name=name, grid=(T // tT,), in_specs=[_lrow(tT), _lvec()], out_specs=_lrow(tT),
                          out_shape=jax.ShapeDtypeStruct((T, LANES), F32), compiler_params=_cp("parallel"))(x, b)


def _logsig_bwd(dy, x, b, name):
    T = x.shape[0]
    tT = _tile(T, HEAD_ROW_TILES)

    def body(dy_ref, x_ref, b_ref, dx_ref, db_ref):
        @pl.when(pl.program_id(0) == 0)
        def _():
            db_ref[...] = jnp.zeros_like(db_ref)

        dx = dy_ref[...] * _sigmoid(-(x_ref[...] + b_ref[...]))
        dx_ref[...] = dx.astype(dx_ref.dtype)
        db_ref[...] += jnp.sum(dx, axis=0, keepdims=True)

    return pl.pallas_call(
        body, name=name, grid=(T // tT,), in_specs=[_lrow(tT), _lrow(tT), _lvec()], out_specs=[_lrow(tT), _lvec()],
        out_shape=[jax.ShapeDtypeStruct((T, LANES), BF16), jax.ShapeDtypeStruct((1, LANES), F32)],
        compiler_params=_cp("arbitrary"))(dy, x, b)


def _gdn_gates(ab, alog, dtb, name):
    T = ab.shape[0]
    tT = _tile(T, HEAD_ROW_TILES)

    def body(ab_ref, al_ref, dt_ref, g_ref, be_ref):
        v = ab_ref[...]
        g_ref[...] = -jnp.exp(al_ref[...]) * _softplus(v + dt_ref[...])
        be_ref[...] = _sigmoid(v)

    return pl.pallas_call(
        body, name=name, grid=(T // tT,), in_specs=[_lrow(tT), _lvec(), _lvec()], out_specs=[_lrow(tT)] * 2,
        out_shape=[jax.ShapeDtypeStruct((T, LANES), F32)] * 2, compiler_params=_cp("parallel"))(ab, alog, dtb)


def _gdn_gates_bwd(dg, dbeta, ab, alog, dtb, name):
    T = ab.shape[0]
    tT = _tile(T, HEAD_ROW_TILES)

    def body(dg_ref, dbe_ref, ab_ref, al_ref, dt_ref, dab_ref, dal_ref, ddt_ref):
        @pl.when(pl.program_id(0) == 0)
        def _():
            dal_ref[...] = jnp.zeros_like(dal_ref)
            ddt_ref[...] = jnp.zeros_like(ddt_ref)

        v, dgv = ab_ref[...], dg_ref[...]
        ea = jnp.exp(al_ref[...])
        z = v + dt_ref[...]
        da = dgv * (-ea * _sigmoid(z))
        sb = _sigmoid(v)
        dab_ref[...] = (da + dbe_ref[...] * sb * (1.0 - sb)).astype(dab_ref.dtype)
        dal_ref[...] += jnp.sum(dgv * (-ea * _softplus(z)), axis=0, keepdims=True)
        ddt_ref[...] += jnp.sum(da, axis=0, keepdims=True)

    return pl.pallas_call(
        body, name=name, grid=(T // tT,), in_specs=[_lrow(tT), _lrow(tT), _lrow(tT), _lvec(), _lvec()],
        out_specs=[_lrow(tT), _lvec(), _lvec()],
        out_shape=[jax.ShapeDtypeStruct((T, LANES), BF16), jax.ShapeDtypeStruct((1, LANES), F32),
                   jax.ShapeDtypeStruct((1, LANES), F32)],
        compiler_params=_cp("arbitrary"))(dg, dbeta, ab, alog, dtb)


def _cumsum(x, seg, reverse, name):
    T = x.shape[0]
    tb = _tile(T, (256, 128, 64))
    nb = T // tb
    carry = seg is None

    def body(x_ref, o_ref, c_ref):
        @pl.when(pl.program_id(0) == 0)
        def _():
            c_ref[...] = jnp.zeros_like(c_ref)

        ri = lax.broadcasted_iota(jnp.int32, (tb, tb), 0)
        ci = lax.broadcasted_iota(jnp.int32, (tb, tb), 1)
        keep = (ci >= ri) if reverse else (ci <= ri)
        if seg is not None:
            keep = keep & ((ri // seg) == (ci // seg))
        y = _hdot(keep.astype(F32), x_ref[...])
        if carry:
            y = y + c_ref[...]
            c_ref[...] = y[0:1, :] if reverse else y[tb - 1:tb, :]
        o_ref[...] = y

    imap = (lambda i: (nb - 1 - i, 0)) if reverse else (lambda i: (i, 0))
    return pl.pallas_call(
        body, name=name, grid=(nb,), in_specs=[pl.BlockSpec((tb, LANES), imap)],
        out_specs=pl.BlockSpec((tb, LANES), imap), out_shape=jax.ShapeDtypeStruct((T, LANES), F32),
        scratch_shapes=[pltpu.VMEM((1, LANES), F32)], compiler_params=_cp("arbitrary"))(x)


ATT_TILES = (1024, 512, 256, 128)


def _att_scores(q, k, f0, fk):
    return _dot(q, k, 'nt') - (fk - f0)


def _diag_keep(tq):
    return lax.broadcasted_iota(jnp.int32, (tq, tq), 1) <= lax.broadcasted_iota(jnp.int32, (tq, tq), 0)


def _tri_pairs(nq, by_key):
    if by_key:
        pairs = [(qi, ki) for ki in range(nq) for qi in range(ki, nq)]
    else:
        pairs = [(qi, ki) for qi in range(nq) for ki in range(qi + 1)]
    return jnp.asarray([p[0] for p in pairs], jnp.int32), jnp.asarray([p[1] for p in pairs], jnp.int32)


def _att_specs(tq):
    qspec = pl.BlockSpec((tq, HD), lambda h, p, qt, kt: (qt[p], h))
    kspec = pl.BlockSpec((tq, HD), lambda h, p, qt, kt: (kt[p], h))
    f0spec = pl.BlockSpec((None, None, 1, 1), lambda h, p, qt, kt: (h, qt[p], 0, 0))
    fkspec = pl.BlockSpec((None, 1, tq), lambda h, p, qt, kt: (h, 0, kt[p]))
    return qspec, kspec, f0spec, fkspec


def _flash_fwd(qn, kn, vb, f0, fkr, H, name, carry=()):
    T = qn.shape[0]
    tq = _tile(T, ATT_TILES)
    nq = T // tq
    qt, kt = _tri_pairs(nq, by_key=False)
    n_pairs, n_carry = qt.shape[0], len(carry)

    def body(qt_ref, kt_ref, q_ref, k_ref, v_ref, f0_ref, fk_ref, *rest):
        w_refs, (o_ref, lse_ref), g_refs = rest[:n_carry], rest[n_carry:n_carry + 2], rest[n_carry + 2:2 * n_carry + 2]
        m_s, l_s, acc_s = rest[2 * n_carry + 2:2 * n_carry + 5]
        jobs = [_gather_protocol(w_refs[i], g_refs[i], *rest[2 * n_carry + 5:], GATHER_SEMS * i) for i in range(n_carry)]
        qi, ki = qt_ref[pl.program_id(1)], kt_ref[pl.program_id(1)]

        if jobs:
            @pl.when((pl.program_id(0) == 0) & (pl.program_id(1) == 0))
            def _():
                for start, _ in jobs:
                    start()

        @pl.when(ki == 0)
        def _():
            m_s[...] = jnp.full_like(m_s, NEG)
            l_s[...] = jnp.zeros_like(l_s)
            acc_s[...] = jnp.zeros_like(acc_s)

        def step(diagonal):
            s = _att_scores(q_ref[...], k_ref[...], f0_ref[...], fk_ref[...])
            if diagonal:
                s = jnp.where(_diag_keep(tq), s, NEG)
            m_prev = m_s[...]
            m_new = jnp.maximum(m_prev, jnp.max(s, axis=1, keepdims=True))
            alpha = jnp.exp(m_prev - m_new)
            p = jnp.exp(s - m_new[:, :1])
            l_s[...] = alpha * l_s[...] + jnp.sum(p, axis=1, keepdims=True)
            acc_s[...] = acc_s[...] * alpha + _bdot(p, v_ref[...])
            m_s[...] = m_new

        @pl.when(ki < qi)
        def _():
            step(False)

        @pl.when(ki == qi)
        def _():
            step(True)
            o_ref[...] = acc_s[...] / l_s[...]
            lse_ref[...] = m_s[...] + jnp.log(l_s[...])

        if jobs:
            @pl.when((pl.program_id(0) == H - 1) & (pl.program_id(1) == n_pairs - 1))
            def _():
                for _, finish in jobs:
                    finish()

    qspec, kspec, f0spec, fkspec = _att_specs(tq)
    sems = [pltpu.SemaphoreType.DMA((GATHER_SEMS * n_carry,))] * 2 if n_carry else []
    grid_spec = pltpu.PrefetchScalarGridSpec(
        num_scalar_prefetch=2, grid=(H, n_pairs),
        in_specs=[qspec, kspec, kspec, f0spec, fkspec] + [HBM_SPEC] * n_carry,
        out_specs=[qspec, qspec] + [HBM_SPEC] * n_carry,
        scratch_shapes=[pltpu.VMEM((tq, HD), F32)] * 3 + sems)
    out = pl.pallas_call(
        body, name=name, grid_spec=grid_spec,
        out_shape=[jax.ShapeDtypeStruct((T, H * HD), F32)] * 2
        + [jax.ShapeDtypeStruct((N_CHIPS,) + a.shape, a.dtype) for a in carry],
        compiler_params=_cp("arbitrary", "arbitrary") if n_carry else _cp("parallel", "arbitrary"),
    )(qt, kt, qn, kn, vb, f0, fkr, *carry)
    return out[0], out[1], list(out[2:])


def _flash_bwd(qn, kn, vb, f0, fkr, dao, lse, delta, H, name):
    T = qn.shape[0]
    tq = _tile(T, ATT_TILES)
    nq = T // tq
    qt, kt = _tri_pairs(nq, by_key=True)

    def body(qt_ref, kt_ref, q_ref, k_ref, v_ref, f0_ref, fk_ref, do_ref, lse_ref, dl_ref,
             dq_ref, dfq_ref, dk_ref, dv_ref, dfk_ref, dk_s, dv_s, dfk_s):
        qi, ki = qt_ref[pl.program_id(1)], kt_ref[pl.program_id(1)]

        @pl.when(pl.program_id(1) == 0)
        def _():
            dq_ref[...] = jnp.zeros_like(dq_ref)
            dfq_ref[...] = jnp.zeros_like(dfq_ref)

        @pl.when(qi == ki)
        def _():
            dk_s[...] = jnp.zeros_like(dk_s)
            dv_s[...] = jnp.zeros_like(dv_s)
            dfk_s[...] = jnp.zeros_like(dfk_s)

        def step(diagonal):
            s = _att_scores(q_ref[...], k_ref[...], f0_ref[...], fk_ref[...])
            p = jnp.exp(s - lse_ref[...][:, :1])
            if diagonal:
                p = jnp.where(_diag_keep(tq), p, 0.0)
            dp = _dot(do_ref[...], v_ref[...], 'nt')
            ds = p * (dp - dl_ref[...][:, :1])
            dv_s[...] += _bdot(p, do_ref[...], 'tn')
            dk_s[...] += _bdot(ds, q_ref[...], 'tn')
            dfk_s[...] -= jnp.sum(ds, axis=0, keepdims=True)
            rows = pl.ds(pl.multiple_of(qi * tq, tq), tq)
            dq_ref[rows, :] += _bdot(ds, k_ref[...])
            dfq_ref[rows, :] += jnp.broadcast_to(jnp.sum(ds, axis=1, keepdims=True), (tq, HD))

        @pl.when(qi > ki)
        def _():
            step(False)

        @pl.when(qi == ki)
        def _():
            step(True)

        @pl.when(qi == nq - 1)
        def _():
            dk_ref[...] = dk_s[...]
            dv_ref[...] = dv_s[...].astype(dv_ref.dtype)
            dfk_ref[...] = dfk_s[...]

    qspec, kspec, f0spec, fkspec = _att_specs(tq)
    head = pl.BlockSpec((T, HD), lambda h, p, qt, kt: (0, h))
    grid_spec = pltpu.PrefetchScalarGridSpec(
        num_scalar_prefetch=2, grid=(H, qt.shape[0]),
        in_specs=[qspec, kspec, kspec, f0spec, fkspec, qspec, qspec, qspec],
        out_specs=[head, head, kspec, kspec, fkspec],
        scratch_shapes=[pltpu.VMEM((tq, HD), F32), pltpu.VMEM((tq, HD), F32), pltpu.VMEM((1, tq), F32)])
    return pl.pallas_call(
        body, name=name, grid_spec=grid_spec,
        out_shape=[jax.ShapeDtypeStruct((T, H * HD), F32)] * 3
        + [jax.ShapeDtypeStruct((T, H * HD), BF16), jax.ShapeDtypeStruct((H, 1, T), F32)],
        compiler_params=_cp("parallel", "arbitrary"),
    )(qt, kt, qn, kn, vb, f0, fkr, dao, lse, delta)


CONV_TILES = (512, 256, 128, 64)
HALO = SUBLANES


def _dwconv(x, xoff, W, w, b, act, voff, out_dtype, name):
    T = x.shape[0]
    K = w.shape[0]
    tT, tC = _tile(T, CONV_TILES), _tile(W, CONV_TILES)
    xb, hb = xoff // tC, tT // HALO
    glu = act == 'glu'

    def body(*refs):
        if glu:
            x_ref, xp_ref, w_ref, b_ref, v_ref, o_ref, buf = refs
        else:
            x_ref, xp_ref, w_ref, o_ref, buf = refs
        i = pl.program_id(0)
        buf[0:HALO, :] = jnp.where(i > 0, xp_ref[...], 0.0)
        buf[HALO:, :] = x_ref[...]
        conv = w_ref[0:1, :] * buf[pl.ds(HALO - (K - 1), tT), :]
        for k in range(1, K):
            conv = conv + w_ref[k:k + 1, :] * buf[pl.ds(HALO - (K - 1) + k, tT), :]
        if glu:
            o_ref[...] = (_gelu(conv + b_ref[...]) * v_ref[...]).astype(o_ref.dtype)
        else:
            o_ref[...] = _silu(conv).astype(o_ref.dtype)

    cur = pl.BlockSpec((tT, tC), lambda i, j: (i, xb + j))
    prev = pl.BlockSpec((HALO, tC), lambda i, j: (jnp.maximum(i * hb - 1, 0), xb + j))
    wspec = pl.BlockSpec((K, tC), lambda i, j: (0, j))
    in_specs, args = [cur, prev, wspec], [x, x, w]
    if glu:
        vb = voff // tC
        in_specs += [pl.BlockSpec((1, tC), lambda i, j: (0, j)), pl.BlockSpec((tT, tC), lambda i, j: (i, vb + j))]
        args += [b, x]
    return pl.pallas_call(
        body, name=name, grid=(T // tT, W // tC), in_specs=in_specs,
        out_specs=pl.BlockSpec((tT, tC), lambda i, j: (i, j)), out_shape=jax.ShapeDtypeStruct((T, W), out_dtype),
        scratch_shapes=[pltpu.VMEM((tT + HALO, tC), F32)], compiler_params=_cp("parallel", "parallel"),
    )(*args)


def _dwconv_bwd(x, xoff, W, w, woff, b, act, voff, dy, name):
    T = x.shape[0]
    K = w.shape[0]
    tT, tC = _tile(T, CONV_TILES), _tile(W, CONV_TILES)
    xb, wb, hb, nT = xoff // tC, woff // tC, tT // HALO, T // tT
    last_halo = T // HALO - 1
    glu = act == 'glu'

    def body(*refs):
        if glu:
            (x_ref, xp_ref, xn_ref, dy_ref, dyn_ref, w_ref, b_ref, v_ref, vn_ref,
             dx_ref, dv_ref, dw_ref, db_ref, xbuf, dybuf, dbuf, vbuf) = refs
        else:
            x_ref, xp_ref, xn_ref, dy_ref, dyn_ref, w_ref, dx_ref, dw_ref, xbuf, dybuf, dbuf = refs
        i = pl.program_id(1)

        @pl.when(i == 0)
        def _():
            dw_ref[...] = jnp.zeros_like(dw_ref)
            if glu:
                db_ref[...] = jnp.zeros_like(db_ref)

        ext = tT + HALO
        xbuf[0:HALO, :] = jnp.where(i > 0, xp_ref[...], 0.0)
        xbuf[HALO:HALO + tT, :] = x_ref[...]
        xbuf[HALO + tT:, :] = xn_ref[...]
        dybuf[0:tT, :] = dy_ref[...].astype(F32)
        dybuf[tT:, :] = jnp.where(i < nT - 1, dyn_ref[...].astype(F32), 0.0)
        conv = w_ref[0:1, :] * xbuf[pl.ds(HALO - (K - 1), ext), :]
        for k in range(1, K):
            conv = conv + w_ref[k:k + 1, :] * xbuf[pl.ds(HALO - (K - 1) + k, ext), :]
        dyv = dybuf[...]
        if glu:
            vbuf[0:tT, :] = v_ref[...]
            vbuf[tT:, :] = vn_ref[...]
            z = conv + b_ref[...]
            cdf, pdf = _normal_cdf_pdf(z)
            dconv = dyv * vbuf[...] * (cdf + z * pdf)
            dv_ref[...] = (dyv[0:tT, :] * (z[0:tT, :] * cdf[0:tT, :])).astype(dv_ref.dtype)
        else:
            sg = _sigmoid(conv)
            dconv = dyv * (sg * (1.0 + conv * (1.0 - sg)))
        dbuf[...] = dconv
        dx = w_ref[0:1, :] * dbuf[pl.ds(K - 1, tT), :]
        for k in range(1, K):
            dx = dx + w_ref[k:k + 1, :] * dbuf[pl.ds(K - 1 - k, tT), :]
        dx_ref[...] = dx.astype(dx_ref.dtype)
        dc = dconv[0:tT, :]
        for k in range(K):
            dw_ref[k:k + 1, :] += jnp.sum(dc * xbuf[pl.ds(HALO - (K - 1) + k, tT), :], axis=0, keepdims=True)
        if glu:
            db_ref[...] += jnp.sum(dc, axis=0, keepdims=True)

    def cur(off):
        return pl.BlockSpec((tT, tC), lambda j, i: (i, off + j))

    def nxt(off):
        return pl.BlockSpec((HALO, tC), lambda j, i: (jnp.minimum((i + 1) * hb, last_halo), off + j))

    prev = pl.BlockSpec((HALO, tC), lambda j, i: (jnp.maximum(i * hb - 1, 0), xb + j))
    wspec = pl.BlockSpec((K, tC), lambda j, i: (0, wb + j))
    acc_w = pl.BlockSpec((K, tC), lambda j, i: (0, j))
    acc_b = pl.BlockSpec((1, tC), lambda j, i: (0, j))
    in_specs = [cur(xb), prev, nxt(xb), cur(0), nxt(0), wspec]
    args = [x, x, x, dy, dy, w]
    out_specs = [cur(0)]
    out_shape = [jax.ShapeDtypeStruct((T, W), BF16)]
    scratch = [pltpu.VMEM((tT + 2 * HALO, tC), F32), pltpu.VMEM((tT + HALO, tC), F32), pltpu.VMEM((tT + HALO, tC), F32)]
    if glu:
        vb = voff // tC
        in_specs += [pl.BlockSpec((1, tC), lambda j, i: (0, wb + j)), cur(vb), nxt(vb)]
        args += [b, x, x]
        out_specs += [cur(0), acc_w, acc_b]
        out_shape += [jax.ShapeDtypeStruct((T, W), BF16), jax.ShapeDtypeStruct((K, W), F32),
                      jax.ShapeDtypeStruct((1, W), F32)]
        scratch += [pltpu.VMEM((tT + HALO, tC), F32)]
    else:
        out_specs += [acc_w]
        out_shape += [jax.ShapeDtypeStruct((K, W), F32)]
    return pl.pallas_call(
        body, name=name, grid=(W // tC, nT), in_specs=in_specs, out_specs=out_specs, out_shape=out_shape,
        scratch_shapes=scratch, compiler_params=_cp("parallel", "arbitrary"),
    )(*args)


V_PER_K = 2


GDN_PREP_CHUNKS = 16
GDN_SCAN_CHUNKS = 4


def _b3(a, b, mode='nn', precision=None):
    c = {'nn': ((2,), (1,)), 'nt': ((2,), (2,)), 'tn': ((1,), (1,))}[mode]
    return lax.dot_general(a, b, (c, ((0,), (0,))), precision=precision, preferred_element_type=F32)


def _bb3(a, b, mode='nn'):
    return _b3(a.astype(BF16), b.astype(BF16), mode)


def _hb3(a, b, mode='nn'):
    return _b3(a, b, mode, precision=HIGHEST)


def _split_bf16(a):
    hi = a.astype(BF16)
    return hi, (a - hi.astype(F32)).astype(BF16)


def _nb3(a, b, mode='nn'):
    ah, al = _split_bf16(a)
    bh, bl = _split_bf16(b)
    return _b3(ah, bh, mode) + _b3(ah, bl, mode) + _b3(al, bh, mode)


def _to_batch(x, nc):
    C = GDN_CHUNK
    return jnp.concatenate([x[:, j * HD:(j + 1) * HD].reshape(nc, C, HD) for j in range(V_PER_K)], axis=0)


def _from_batch(x, nc):
    C = GDN_CHUNK
    return jnp.concatenate([x[j * nc:(j + 1) * nc].reshape(nc * C, HD) for j in range(V_PER_K)], axis=1)


def _both_heads(x, nc):
    xc = x.reshape(nc, GDN_CHUNK, HD)
    return jnp.concatenate([xc] * V_PER_K, axis=0)


def _gdn_local(q2, k2, gb, bb):
    B, C, _ = k2.shape
    ri = lax.broadcasted_iota(jnp.int32, (B, C, C), 1)
    ci = lax.broadcasted_iota(jnp.int32, (B, C, C), 2)
    lower, strict = ri >= ci, ri > ci
    pick0 = (lax.broadcasted_iota(jnp.int32, (B, C, HD), 2) == 0).astype(F32)
    g_cols = _hb3(pick0, gb, 'nt')
    dm = jnp.exp(jnp.where(lower, gb[:, :, :C] - g_cols, NEG))
    kk = _bb3(k2, k2, 'nt')
    a = jnp.where(strict, kk * dm * bb[:, :, :C], 0.0)
    eg = jnp.exp(gb)
    gl = gb[:, C - 1:C, :]
    return dict(lower=lower, strict=strict, eye=(ri == ci).astype(F32), dm=dm, kk=kk, a=a, eg=eg, gl=gl,
                qd=q2 * eg, kd=k2 * jnp.exp(gl - gb))


def _gdn_specs(T, Hk, voff, nc, rev=False):
    C = GDN_CHUNK
    nb = T // (nc * C)
    vb = voff // (V_PER_K * HD)
    ix = (lambda i: nb - 1 - i) if rev else (lambda i: i)
    kspec = pl.BlockSpec((nc * C, HD), lambda h, i: (ix(i), h))
    pair = pl.BlockSpec((nc * C, V_PER_K * HD), lambda h, i: (ix(i), h))
    vspec = pl.BlockSpec((nc * C, V_PER_K * HD), lambda h, i: (ix(i), vb + h))
    cc = pl.BlockSpec((V_PER_K, nc, C, C), lambda h, i: (h, ix(i), 0, 0))
    state = pl.BlockSpec((V_PER_K, nc, HD, HD), lambda h, i: (h, ix(i), 0, 0))
    scal = pl.BlockSpec((V_PER_K, nc, SUBLANES, HD), lambda h, i: (h, ix(i), 0, 0))
    return nb, kspec, pair, vspec, cc, state, scal


def _gdn_prep(qn, kn, qkvc, voff, gcb, betab, Hk, name):
    T = qn.shape[0]
    C, nc = GDN_CHUNK, GDN_PREP_CHUNKS
    Hv, N = Hk * V_PER_K, T // C
    nb, kspec, pair, vspec, cc, _, _ = _gdn_specs(T, Hk, voff, nc)

    def body(q_ref, k_ref, v_ref, g_ref, b_ref, u_ref, w_ref, tm_ref, qkm_ref):
        q2, k2 = _both_heads(q_ref[...], nc), _both_heads(k_ref[...], nc)
        v2, gb, bb = _to_batch(v_ref[...], nc), _to_batch(g_ref[...], nc), _to_batch(b_ref[...], nc)
        lc = _gdn_local(q2, k2, gb, bb)
        p = -lc['a']
        tm = lc['eye'] + p
        for _ in range(5):
            p = _nb3(p, p)
            tm = tm + _nb3(tm, p)
        u_ref[...] = _from_batch(_nb3(tm, v2 * bb), nc)
        w_ref[...] = _from_batch(_nb3(tm, k2 * (bb * lc['eg'])), nc)
        tm_ref[...] = tm.reshape(V_PER_K, nc, C, C)
        qkm_ref[...] = jnp.where(lc['lower'], _bb3(q2, k2, 'nt') * lc['dm'], 0.0).reshape(V_PER_K, nc, C, C)

    return pl.pallas_call(
        body, name=name, grid=(Hk, nb), in_specs=[kspec, kspec, vspec, pair, pair], out_specs=[pair, pair, cc, cc],
        out_shape=[jax.ShapeDtypeStruct((T, Hv * HD), F32)] * 2 + [jax.ShapeDtypeStruct((Hv, N, C, C), F32)] * 2,
        compiler_params=_cp("parallel", "parallel"),
    )(qn, kn, qkvc, gcb, betab)


SCAN_K_HEADS = 4
SCAN_V_HEADS = SCAN_K_HEADS * V_PER_K


def _heads(ref, rows, per=1):
    return jnp.stack([ref[rows, (j // per) * HD:(j // per + 1) * HD] for j in range(SCAN_V_HEADS)])


def _put_heads(ref, rows, x):
    for j in range(SCAN_V_HEADS):
        ref[rows, j * HD:(j + 1) * HD] = x[j]


def _scan_chunk(q_ref, k_ref, g_ref, rows):
    C = GDN_CHUNK
    gb = _heads(g_ref, rows)
    gl = gb[:, C - 1:C, :]
    return _heads(q_ref, rows, V_PER_K) * jnp.exp(gb), _heads(k_ref, rows, V_PER_K) * jnp.exp(gl - gb), jnp.exp(gl)


def _scan_specs(T, ns, rev=False):
    C = GDN_CHUNK
    nb = T // (ns * C)
    ix = (lambda i: nb - 1 - i) if rev else (lambda i: i)
    kspec = pl.BlockSpec((ns * C, SCAN_K_HEADS * HD), lambda h, i: (ix(i), h))
    vspec = pl.BlockSpec((ns * C, SCAN_V_HEADS * HD), lambda h, i: (ix(i), h))
    per_chunk = lambda *tail: pl.BlockSpec((SCAN_V_HEADS, ns) + tail, lambda h, i: (h, ix(i), 0, 0))
    return nb, kspec, vspec, per_chunk(C, C), per_chunk(HD, HD), per_chunk(SUBLANES, HD)


def _gdn_scan(qn, kn, gcb, u, w, qkm, Hk, name):
    T = qn.shape[0]
    C, ns = GDN_CHUNK, GDN_SCAN_CHUNKS
    Hv, N = Hk * V_PER_K, T // C
    nb, kspec, pair, cc, state, _ = _scan_specs(T, ns)

    def body(q_ref, k_ref, g_ref, u_ref, w_ref, qkm_ref, o_ref, sp_ref, s_s):
        @pl.when(pl.program_id(1) == 0)
        def _():
            s_s[...] = jnp.zeros_like(s_s)

        s = s_s[...]
        for t in range(ns):
            rows = slice(t * C, (t + 1) * C)
            qd, kd, egl = _scan_chunk(q_ref, k_ref, g_ref, rows)
            sp_ref[:, t] = s
            vn = _heads(u_ref, rows) - _bb3(_heads(w_ref, rows), s)
            _put_heads(o_ref, rows, _bb3(qd, s) + _bb3(qkm_ref[:, t], vn))
            s = s * egl + _bb3(kd, vn, 'tn')
        s_s[...] = s

    return pl.pallas_call(
        body, name=name, grid=(Hk // SCAN_K_HEADS, nb), in_specs=[kspec, kspec, pair, pair, pair, cc],
        out_specs=[pair, state],
        out_shape=[jax.ShapeDtypeStruct((T, Hv * HD), F32), jax.ShapeDtypeStruct((Hv, N, HD, HD), F32)],
        scratch_shapes=[pltpu.VMEM((SCAN_V_HEADS, HD, HD), F32)], compiler_params=_cp("parallel", "arbitrary"),
    )(qn, kn, gcb, u, w, qkm)


def _gdn_scan_bwd(qn, kn, gcb, u, w, qkm, sprev, do, Hk, name):
    T = qn.shape[0]
    C, ns = GDN_CHUNK, GDN_SCAN_CHUNKS
    Hv, N = Hk * V_PER_K, T // C
    nb, kspec, pair, cc, state, scal = _scan_specs(T, ns, rev=True)
    nv = SCAN_V_HEADS

    def body(q_ref, k_ref, g_ref, u_ref, w_ref, qkm_ref, sp_ref, do_ref,
             dqd_ref, dkd_ref, du_ref, dw_ref, dqkm_ref, dgl_ref, ds_s):
        @pl.when(pl.program_id(1) == 0)
        def _():
            ds_s[...] = jnp.zeros_like(ds_s)

        lower = lax.broadcasted_iota(jnp.int32, (nv, C, C), 1) >= lax.broadcasted_iota(jnp.int32, (nv, C, C), 2)
        ds = ds_s[...]
        for t in reversed(range(ns)):
            rows = slice(t * C, (t + 1) * C)
            qd, kd, egl = _scan_chunk(q_ref, k_ref, g_ref, rows)
            s, w_, qkm_, dov = sp_ref[:, t], _heads(w_ref, rows), qkm_ref[:, t], _heads(do_ref, rows)
            vn = _heads(u_ref, rows) - _bb3(w_, s)
            _put_heads(dqd_ref, rows, _bb3(dov, s, 'nt'))
            dqkm_ref[:, t] = jnp.where(lower, _bb3(dov, vn, 'nt'), 0.0)
            dvn = _bb3(qkm_, dov, 'tn') + _bb3(kd, ds)
            _put_heads(dkd_ref, rows, _bb3(vn, ds, 'nt'))
            dgl = jnp.sum(jnp.sum(ds * s, axis=1, keepdims=True) * egl, axis=2, keepdims=True)
            dgl_ref[:, t] = jnp.broadcast_to(dgl, (nv, SUBLANES, HD))
            _put_heads(du_ref, rows, dvn)
            _put_heads(dw_ref, rows, -_bb3(dvn, s, 'nt'))
            ds = ds * egl + _bb3(qd, dov, 'tn') - _bb3(w_, dvn, 'tn')
        ds_s[...] = ds

    return pl.pallas_call(
        body, name=name, grid=(Hk // SCAN_K_HEADS, nb), in_specs=[kspec, kspec, pair, pair, pair, cc, state, pair],
        out_specs=[pair] * 4 + [cc, scal],
        out_shape=[jax.ShapeDtypeStruct((T, Hv * HD), F32)] * 4
        + [jax.ShapeDtypeStruct((Hv, N, C, C), F32), jax.ShapeDtypeStruct((Hv, N, SUBLANES, HD), F32)],
        scratch_shapes=[pltpu.VMEM((nv, HD, HD), F32)], compiler_params=_cp("parallel", "arbitrary"),
    )(qn, kn, gcb, u, w, qkm, sprev, do)


def _gdn_prep_bwd(qn, kn, qkvc, voff, gcb, betab, tm, u, w, qkm, dqd, dkd, du, dw, dqkm, dgl, Hk, name):
    T = qn.shape[0]
    C, nc = GDN_CHUNK, GDN_PREP_CHUNKS
    Hv = Hk * V_PER_K
    B = V_PER_K * nc
    nb, kspec, pair, vspec, cc, _, scal = _gdn_specs(T, Hk, voff, nc)

    def body(q_ref, k_ref, v_ref, g_ref, b_ref, tm_ref, u_ref, w_ref, qkm_ref, dqd_ref, dkd_ref, du_ref, dw_ref,
             dqkm_ref, dgl_ref, dq_ref, dk_ref, dv_ref, dg_ref, dbe_ref):
        q2, k2 = _both_heads(q_ref[...], nc), _both_heads(k_ref[...], nc)
        v2, gb, bb = _to_batch(v_ref[...], nc), _to_batch(g_ref[...], nc), _to_batch(b_ref[...], nc)
        lc = _gdn_local(q2, k2, gb, bb)
        dm, eg, gl = lc['dm'], lc['eg'], lc['gl']
        tm_, qkm_, dqkm_ = (r[...].reshape(B, C, C) for r in (tm_ref, qkm_ref, dqkm_ref))
        u_, w_, dqd_, dkd_, du_, dw_ = (_to_batch(r[...], nc) for r in (u_ref, w_ref, dqd_ref, dkd_ref, du_ref, dw_ref))
        dgl_ = dgl_ref[...].reshape(B, SUBLANES, HD)[:, :1, :1]
        rowsum = lambda x: jnp.sum(x, axis=-1, keepdims=True)
        dbv = _nb3(tm_, du_, 'tn')
        dbk = _nb3(tm_, dw_, 'tn')
        da = jnp.where(lc['strict'], -(_bb3(dbv, u_, 'nt') + _bb3(dbk, w_, 'nt')), 0.0)
        rk = rowsum(dbk * k2)
        dbeta = rowsum(dbv * v2) + rk * eg[:, :, :1] + rowsum(da * lc['kk'] * dm)
        dkk = da * dm * bb[:, :, :C]
        dqkr = dqkm_ * dm
        dk = dbk * (bb * eg) + _bb3(dkk, k2) + _bb3(dkk, k2, 'tn') + _bb3(dqkr, q2, 'tn') + dkd_ * jnp.exp(gl - gb)
        dq = _bb3(dqkr, k2) + dqd_ * eg
        de = da * lc['a'] + dqkm_ * qkm_
        sk = rowsum(dkd_ * lc['kd'])
        dg = rk * (bb[:, :, :1] * eg[:, :, :1]) + rowsum(de) - _hb3(de, jnp.ones((B, C, HD), F32), 'tn')[:, :, :1] \
            + rowsum(dqd_ * lc['qd']) - sk
        last = (lax.broadcasted_iota(jnp.int32, (B, C, HD), 1) == C - 1).astype(F32)
        dgb = jnp.broadcast_to(dg, (B, C, HD)) + last * (dgl_ + jnp.sum(sk, axis=1, keepdims=True))
        suffix = (lax.broadcasted_iota(jnp.int32, (B, C, C), 2) >= lax.broadcasted_iota(jnp.int32, (B, C, C), 1)).astype(F32)
        dq_ref[...] = _from_batch(dq, nc)
        dk_ref[...] = _from_batch(dk, nc)
        dv_ref[...] = _from_batch(dbv * bb, nc)
        dg_ref[...] = _from_batch(_hb3(suffix, dgb), nc)
        dbe_ref[...] = _from_batch(jnp.broadcast_to(dbeta, (B, C, HD)), nc)

    return pl.pallas_call(
        body, name=name, grid=(Hk, nb),
        in_specs=[kspec, kspec, vspec, pair, pair, cc, pair, pair, cc, pair, pair, pair, pair, cc, scal],
        out_specs=[pair] * 5, out_shape=[jax.ShapeDtypeStruct((T, Hv * HD), F32)] * 5,
        compiler_params=_cp("parallel", "parallel"),
    )(qn, kn, qkvc, gcb, betab, tm, u, w, qkm, dqd, dkd, du, dw, dqkm, dgl)


def _adamw_math(w, g, m, v):
    m = ADAM_B1 * m + (1.0 - ADAM_B1) * g
    v = ADAM_B2 * v + (1.0 - ADAM_B2) * jnp.square(g)
    m_hat = m / (1.0 - ADAM_B1 ** ADAM_STEP)
    v_hat = v / (1.0 - ADAM_B2 ** ADAM_STEP)
    delta = -ADAM_LR * (m_hat / (jnp.sqrt(v_hat) + ADAM_EPS) + ADAM_WD * w)
    return delta, m, v


STREAM_BLOCK_BYTES = 2 << 20


def _stream_rows(R, C, mult=SUBLANES):
    for tr in (512, 256, 128, 64, 32, 16, 8):
        if R % tr == 0 and tr % mult == 0 and tr * C * 4 <= STREAM_BLOCK_BYTES:
            return tr
    return R


def _adamw(w, m, v, gs, name):
    R, C = w.shape
    tr = _stream_rows(R, C)
    n = len(gs)

    def body(*refs):
        w_ref, m_ref, v_ref = refs[:3]
        g_refs = refs[3:3 + n]
        g_out, d_out, m_out, v_out = refs[3 + n:]
        g = g_refs[0][...]
        for r_ in g_refs[1:]:
            g = g + r_[...]
        g_out[...] = g
        d_out[...], m_out[...], v_out[...] = _adamw_math(w_ref[...], g, m_ref[...], v_ref[...])

    spec = pl.BlockSpec((tr, C), lambda i: (i, 0))
    return pl.pallas_call(
        body, name=name, grid=(R // tr,), in_specs=[spec] * (3 + n), out_specs=[spec] * 4,
        out_shape=[jax.ShapeDtypeStruct((R, C), F32)] * 4, compiler_params=_cp("parallel"),
    )(w, m, v, *gs)


def _sum_devices(g8, name):
    _, M, C = g8.shape
    tr = _tile(M, (512, 256, 128, 64, 32, 16, 8))

    def body(g_ref, o_ref):
        acc = g_ref[0]
        for d in range(1, N_DEV):
            acc = acc + g_ref[d]
        o_ref[...] = acc

    return pl.pallas_call(
        body, name=name, grid=(M // tr,), in_specs=[pl.BlockSpec((N_DEV, tr, C), lambda i: (0, i, 0))],
        out_specs=pl.BlockSpec((tr, C), lambda i: (i, 0)), out_shape=jax.ShapeDtypeStruct((M, C), F32),
        compiler_params=_cp("parallel"),
    )(g8)


def _ada_w_update(c_all, dm, w, m, v, name):
    n_mod, D, Ns = w.shape
    tr = _tile(D, (256, 128))

    def body(c_ref, dm_ref, w_ref, m_ref, v_ref, g_out, d_out, m_out, v_out):
        g = _hdot(_silu(c_ref[...]), dm_ref[...], 'tn')
        g_out[...] = g
        d_out[...], m_out[...], v_out[...] = _adamw_math(w_ref[...], g, m_ref[...], v_ref[...])

    wspec = pl.BlockSpec((None, tr, Ns), lambda i, r: (i, r, 0))
    return pl.pallas_call(
        body, name=name, grid=(n_mod, D // tr),
        in_specs=[pl.BlockSpec((N_DEV, tr), lambda i, r: (0, r)), pl.BlockSpec((None, N_DEV, Ns), lambda i, r: (i, 0, 0)),
                  wspec, wspec, wspec],
        out_specs=[wspec] * 4, out_shape=[jax.ShapeDtypeStruct((n_mod, D, Ns), F32)] * 4,
        compiler_params=_cp("parallel", "parallel"),
    )(c_all, dm, w, m, v)


def _place():
    return lax.axis_index("x"), lax.axis_index("y"), lax.axis_index("c")


def _allgather8(x_shard, name):
    m_per, n = x_shard.shape

    def body(x_ref, out_ref, send_sems, recv_sems, local_sem):
        x, y, c = _place()
        me, sibling = (x, y, c), (x, y, 1 - c)
        chips = [(1 - x, y), (x, 1 - y), (1 - x, 1 - y)]

        def rows(px, py, pc):
            return out_ref.at[pl.ds((4 * px + 2 * py + pc) * m_per, m_per), :]

        def copy(k, block, to, src=None):
            return pltpu.make_async_remote_copy(
                src_ref=rows(*block) if src is None else src, dst_ref=rows(*block),
                send_sem=send_sems.at[k], recv_sem=recv_sems.at[k], device_id=to, device_id_type=MESH)

        mine = pltpu.make_async_copy(x_ref, rows(*me), local_sem)
        mine.start()
        first = [copy(0, me, sibling, src=x_ref)]
        first += [copy(1 + j, me, (*chip, c), src=x_ref) for j, chip in enumerate(chips)]
        for cp in first:
            cp.start()
        passed = [copy(4 + j, (*chip, c), sibling) for j, chip in enumerate(chips)]
        for j, chip in enumerate(chips):
            copy(1 + j, (*chip, c), me).wait_recv()
            passed[j].start()
        copy(0, sibling, me).wait_recv()
        for j, chip in enumerate(chips):
            copy(4 + j, (*chip, 1 - c), me).wait_recv()
        for cp in first + passed:
            cp.wait_send()
        mine.wait()

    return pl.pallas_call(
        body, name=name, out_shape=jax.ShapeDtypeStruct((N_DEV * m_per, n), x_shard.dtype),
        in_specs=[pl.BlockSpec(memory_space=pltpu.VMEM)], out_specs=pl.BlockSpec(memory_space=pltpu.VMEM),
        scratch_shapes=[pltpu.SemaphoreType.DMA((7,)), pltpu.SemaphoreType.DMA((7,)), pltpu.SemaphoreType.DMA],
        compiler_params=pltpu.CompilerParams(vmem_limit_bytes=VMEM_LIMIT),
    )(x_shard)


HBM_SPEC = pl.BlockSpec(memory_space=pltpu.HBM)


GATHER_SEMS = 6


def _gather_protocol(w_ref, out_ref, send_sems, recv_sems, k0):
    half = w_ref.shape[0] // 2
    x, y, c = _place()
    me, sibling = (x, y, c), (x, y, 1 - c)
    chips = [(1 - x, y), (x, 1 - y), (1 - x, 1 - y)]

    def part(cx, cy, hc):
        return out_ref.at[2 * cx + cy, pl.ds(hc * half, half), :]

    def copy(k, block, to, src=None):
        return pltpu.make_async_remote_copy(
            src_ref=part(*block) if src is None else src, dst_ref=part(*block),
            send_sem=send_sems.at[k0 + k], recv_sem=recv_sems.at[k0 + k], device_id=to, device_id_type=MESH)

    def first():
        return [copy(j, me, (*chip, c), src=w_ref.at[pl.ds(c * half, half), :]) for j, chip in enumerate(chips)]

    def start():
        for cp in first():
            cp.start()

    def finish():
        passed = [copy(3 + j, (*chip, c), sibling) for j, chip in enumerate(chips)]
        for j, chip in enumerate(chips):
            copy(j, (*chip, c), me).wait_recv()
            passed[j].start()
        for j, chip in enumerate(chips):
            copy(3 + j, (*chip, 1 - c), me).wait_recv()
        for cp in first() + passed:
            cp.wait_send()

    return start, finish


def _gather_weights(w_flat, name):
    R, C = w_flat.shape

    def body(w_ref, out_ref, send_sems, recv_sems):
        start, finish = _gather_protocol(w_ref, out_ref, send_sems, recv_sems, 0)
        start()
        finish()

    return pl.pallas_call(
        body, name=name, out_shape=jax.ShapeDtypeStruct((N_CHIPS, R, C), w_flat.dtype),
        in_specs=[HBM_SPEC], out_specs=HBM_SPEC,
        scratch_shapes=[pltpu.SemaphoreType.DMA((GATHER_SEMS,)), pltpu.SemaphoreType.DMA((GATHER_SEMS,))],
    )(w_flat)


def _swap_halves(g, name):
    n, R, C = g.shape
    half = R // 2

    def body(g_ref, got_ref, send_sem, recv_sem):
        x, y, c = _place()
        cp = pltpu.make_async_remote_copy(
            src_ref=g_ref.at[:, pl.ds((1 - c) * half, half), :], dst_ref=got_ref,
            send_sem=send_sem, recv_sem=recv_sem, device_id=(x, y, 1 - c), device_id_type=MESH)
        cp.start()
        cp.wait()

    return pl.pallas_call(
        body, name=name, out_shape=jax.ShapeDtypeStruct((n, half, C), g.dtype),
        in_specs=[HBM_SPEC], out_specs=HBM_SPEC,
        scratch_shapes=[pltpu.SemaphoreType.DMA, pltpu.SemaphoreType.DMA],
    )(g)


SCATTER_SEMS = 3


def _scatter_protocol(q_ref, got_ref, send_sems, recv_sems, k0):
    x, y, c = _place()

    def copies():
        return [pltpu.make_async_remote_copy(
            src_ref=q_ref.at[2 * cx + cy], dst_ref=got_ref.at[j], send_sem=send_sems.at[k0 + j],
            recv_sem=recv_sems.at[k0 + j], device_id=(cx, cy, c), device_id_type=MESH)
            for j, (cx, cy) in enumerate([(1 - x, y), (x, 1 - y), (1 - x, 1 - y)])]

    def start():
        for cp in copies():
            cp.start()

    def finish():
        for cp in copies():
            cp.wait()

    return start, finish


def _scatter_shapes(qs):
    return [jax.ShapeDtypeStruct((N_CHIPS - 1,) + q.shape[1:], q.dtype) for q in qs]


def _join_halves(h, name):
    R, C = h.shape
    R2 = R // 2

    def body(h_ref, out_ref, send_sem, recv_sem):
        x, y, c = _place()
        cp = pltpu.make_async_remote_copy(
            src_ref=h_ref.at[pl.ds(c * R2, R2), :], dst_ref=out_ref.at[pl.ds(c * R2, R2), :],
            send_sem=send_sem, recv_sem=recv_sem, device_id=(x, y, 1 - c), device_id_type=MESH)
        cp.start()
        cp.wait()

    return pl.pallas_call(
        body, name=name, out_shape=jax.ShapeDtypeStruct((R, C), h.dtype),
        in_specs=[HBM_SPEC], out_specs=HBM_SPEC, input_output_aliases={0: 0},
        scratch_shapes=[pltpu.SemaphoreType.DMA, pltpu.SemaphoreType.DMA],
    )(h)


def _add_halves(g, got, c_idx, name):
    n, R, C = g.shape
    half = R // 2
    tr = _stream_rows(half, C, 2 * SUBLANES)
    nb = half // tr

    def body(c_ref, g_ref, got_ref, o_ref):
        o_ref[...] = (g_ref[...] + got_ref[...]).astype(o_ref.dtype)

    grid_spec = pltpu.PrefetchScalarGridSpec(
        num_scalar_prefetch=1, grid=(n, nb),
        in_specs=[pl.BlockSpec((None, tr, C), lambda s, i, c_ref: (s, c_ref[0] * nb + i, 0)),
                  pl.BlockSpec((None, tr, C), lambda s, i, c_ref: (s, i, 0))],
        out_specs=pl.BlockSpec((None, tr, C), lambda s, i, c_ref: (s, i, 0)))
    return pl.pallas_call(
        body, name=name, grid_spec=grid_spec, out_shape=jax.ShapeDtypeStruct((n, half, C), BF16),
        compiler_params=_cp("parallel", "parallel"),
    )(c_idx, g, got)


def _add_chips(q, got, sc_idx, name):
    n, R2, C = q.shape
    tr = _stream_rows(R2, C, 2 * SUBLANES)
    nb = R2 // tr

    def body(s_ref, q_ref, g0_ref, g1_ref, g2_ref, o_ref):
        o_ref[...] = ((q_ref[...].astype(F32) + g0_ref[...].astype(F32)) + g1_ref[...].astype(F32)) \
            + g2_ref[...].astype(F32)

    def got_spec(j):
        return pl.BlockSpec((None, tr, C), lambda i, s_ref: (j, i, 0))

    grid_spec = pltpu.PrefetchScalarGridSpec(
        num_scalar_prefetch=1, grid=(nb,),
        in_specs=[pl.BlockSpec((None, tr, C), lambda i, s_ref: (s_ref[0], i, 0)), got_spec(0), got_spec(1), got_spec(2)],
        out_specs=pl.BlockSpec((tr, C), lambda i, s_ref: (s_ref[1] * nb + i, 0)))
    return pl.pallas_call(
        body, name=name, grid_spec=grid_spec, out_shape=jax.ShapeDtypeStruct((2 * R2, C), F32),
        compiler_params=_cp("parallel"),
    )(sc_idx, q, got, got, got)


def _pack_lanes(arrs):
    rows = []
    for a in arrs:
        f = a.reshape(-1)
        n = -(-f.shape[0] // LANES) * LANES
        rows.append(jnp.pad(f, (0, n - f.shape[0])).reshape(-1, LANES))
    out = jnp.concatenate(rows, axis=0)
    pad = -out.shape[0] % SUBLANES
    return jnp.pad(out, ((0, pad), (0, 0)))


def _unpack_lanes(packed, shapes):
    out, r = [], 0
    for shp in shapes:
        n = math.prod(shp)
        nr = -(-n // LANES)
        out.append(packed[r:r + nr].reshape(-1)[:n].reshape(shp))
        r += nr
    return out


def _shards_to_full(sh, axis):
    return jnp.concatenate([sh[i] for i in range(N_CHIPS)], axis=axis)


def _full_to_shards(full, axis):
    return jnp.stack(jnp.split(full, N_CHIPS, axis=axis), axis=0)


def _pad_cols(a, n):
    return jnp.pad(a, ((0, 0), (0, n - a.shape[1])))


def _lane_bcast(a):
    return jnp.repeat(a, HD, axis=1)


def _split_mod(mod):
    D = mod.shape[0] // 3
    return mod[None, :D], mod[None, D:2 * D], mod[None, 2 * D:]


def _fox_fwd(h, w, tag, late=None):
    T, D = h.shape
    H = D // HD
    proj = _mm(h, w['cat'], 'nn', F32, tag + '_proj')
    flog = _mm(h, w['f'], 'nn', F32, tag + '_flog')
    qn = _headnorm(proj, 0, H, w['q_norm'], 1.0 / HD, HD ** -0.5, BF16, tag + '_qnorm')
    kn = _headnorm(proj, H, H, w['k_norm'], 1.0 / HD, 1.0, BF16, tag + '_knorm')
    vb = proj[:, 2 * D:3 * D].astype(BF16)
    fcum = _cumsum(_logsig(flog, w['f_bias'], tag + '_logf'), None, False, tag + '_fcum')
    tq = _tile(T, ATT_TILES)
    f0 = fcum[::tq, :H].T.reshape(H, T // tq, 1, 1)
    fkr = fcum[:, :H].T.reshape(H, 1, T)
    ao, lse, carried = _flash_fwd(qn, kn, vb, f0, fkr, H, tag + '_att', late[0] if late else ())
    if late:
        w['o'], rest = late[1](carried)
    gated = _fox_gate(ao, proj, 3 * H, H, tag + '_ogate')
    y = _mm(gated, w['o'], 'nn', F32, tag + '_out')
    sv = dict(h=h, proj=proj, flog=flog, qn=qn, kn=kn, vb=vb, f0=f0, fkr=fkr, ao=ao, lse=lse, gated=gated)
    return (y, sv, rest) if late else (y, sv)


def _mm_carry(a, b, mode, out_dtype, name, reduce_pairs, tag, grads):
    if reduce_pairs is None:
        return _mm(a, b, mode, out_dtype, name), {}
    pairs = reduce_pairs(tag, grads)
    out, got = _mm(a, b, mode, out_dtype, name, scatter=tuple(q for _, q in pairs))
    return out, {key: (q, r) for (key, q), r in zip(pairs, got)}


def _fox_bwd(dy, w, sv, tag, reduce_pairs=None):
    h, proj = sv['h'], sv['proj']
    T, D = h.shape
    H = D // HD
    g = {}
    g['o'] = _mm(sv['gated'], dy, 'tn', F32, tag + '_dwo')
    dgated = _mm(dy, w['o'], 'nt', F32, tag + '_dgated')
    dao, dog, delta = _fox_gate_bwd(dgated, sv['ao'], proj, 3 * H, H, tag + '_ogate_bwd')
    dq, dfq, dk, dv, dfk = _flash_bwd(sv['qn'], sv['kn'], sv['vb'], sv['f0'], sv['fkr'], dao, sv['lse'], delta, H,
                                      tag + '_att_bwd')
    dfcum = _pad_cols(dfq[:, ::HD] + dfk.reshape(H, T).T, LANES)
    dlogf = _cumsum(dfcum, None, True, tag + '_fcum_bwd')
    dflog, g['f_bias'] = _logsig_bwd(dlogf, sv['flog'], w['f_bias'], tag + '_logf_bwd')
    dqr, g['q_norm'] = _headnorm_bwd([(dq, 1, 0)], proj, 0, H, w['q_norm'], 1.0 / HD, HD ** -0.5, BF16,
                                     tag + '_qnorm_bwd')
    dkr, g['k_norm'] = _headnorm_bwd([(dk, 1, 0)], proj, H, H, w['k_norm'], 1.0 / HD, 1.0, BF16, tag + '_knorm_bwd')
    dproj = jnp.concatenate([dqr, dkr, dv, dog], axis=1)
    g['cat'] = _mm(h, dproj, 'tn', F32, tag + '_dwcat')
    g['f'] = _mm(h, dflog, 'tn', F32, tag + '_dwf')
    dh, exchanged = _mm_carry(dproj, w['cat'], 'nt', F32, tag + '_dh', reduce_pairs, tag, g)
    return [dh, _mm(dflog, w['f'], 'nt', F32, tag + '_dh_f')], g, exchanged


def _gdn_fwd(h, w, tag):
    T, D = h.shape
    Hk = D // HD
    Hv = V_PER_K * Hk
    proj = _mm(h, w['cat'], 'nn', F32, tag + '_proj')
    ab = _mm(h, w['ab'], 'nn', F32, tag + '_ab')
    qkvc = _dwconv(proj, 0, 4 * D, w['conv'], None, 'silu', 0, F32, tag + '_conv')
    qn = _headnorm(qkvc, 0, Hk, None, 1.0, HD ** -0.5, F32, tag + '_qnorm')
    kn = _headnorm(qkvc, Hk, Hk, None, 1.0, 1.0, F32, tag + '_knorm')
    graw, beta = _gdn_gates(ab, w['a_log'], w['dt_bias'], tag + '_gates')
    gc = _cumsum(graw, GDN_CHUNK, False, tag + '_gcum')
    gcb = _lane_bcast(gc[:, :Hv])
    betab = _lane_bcast(beta[:, Hv:2 * Hv])
    u, wk, tm, qkm = _gdn_prep(qn, kn, qkvc, 2 * D, gcb, betab, Hk, tag + '_prep')
    o, sprev = _gdn_scan(qn, kn, gcb, u, wk, qkm, Hk, tag + '_scan')
    go = _gdn_out(o, proj, 4 * Hk, Hv, w['out_norm'], tag + '_onorm')
    y = _mm(go, w['o'], 'nn', F32, tag + '_out')
    return y, dict(h=h, proj=proj, ab=ab, qkvc=qkvc, qn=qn, kn=kn, gcb=gcb, betab=betab, o=o, sprev=sprev, go=go,
                   u=u, wk=wk, tm=tm, qkm=qkm)


def _gdn_bwd(dy, w, sv, tag, reduce_pairs=None):
    h, proj, qkvc = sv['h'], sv['proj'], sv['qkvc']
    T, D = h.shape
    Hk = D // HD
    Hv = V_PER_K * Hk
    g = {}
    g['o'] = _mm(sv['go'], dy, 'tn', F32, tag + '_dwo')
    dgo = _mm(dy, w['o'], 'nt', F32, tag + '_dgo')
    do, dz, g['out_norm'] = _gdn_out_bwd(dgo, sv['o'], proj, 4 * Hk, Hv, w['out_norm'], tag + '_onorm_bwd')
    local = (sv['u'], sv['wk'], sv['qkm'])
    dqd, dkd, du, dwk, dqkm, dgl = _gdn_scan_bwd(sv['qn'], sv['kn'], sv['gcb'], *local, sv['sprev'], do, Hk,
                                                 tag + '_scan_bwd')
    dqp, dkp, dv, dgb, dbetab = _gdn_prep_bwd(sv['qn'], sv['kn'], qkvc, 2 * D, sv['gcb'], sv['betab'], sv['tm'], *local,
                                              dqd, dkd, du, dwk, dqkm, dgl, Hk, tag + '_prep_bwd')
    pairs = lambda a: [(a, V_PER_K, j) for j in range(V_PER_K)]
    dqc, _ = _headnorm_bwd(pairs(dqp), qkvc, 0, Hk, None, 1.0, HD ** -0.5, F32, tag + '_qnorm_bwd')
    dkc, _ = _headnorm_bwd(pairs(dkp), qkvc, Hk, Hk, None, 1.0, 1.0, F32, tag + '_knorm_bwd')
    zeros = jnp.zeros((T, Hv), F32)
    dg_pad = _pad_cols(dgb[:, ::HD], LANES)
    dbeta_pad = _pad_cols(jnp.concatenate([zeros, dbetab[:, ::HD]], axis=1), LANES)
    dab, g['a_log'], g['dt_bias'] = _gdn_gates_bwd(dg_pad, dbeta_pad, sv['ab'], w['a_log'], w['dt_bias'], tag + '_gates_bwd')
    dpq, dwq = _dwconv_bwd(proj, 0, D, w['conv'], 0, None, 'silu', 0, dqc, tag + '_conv_bwd_q')
    dpk, dwk = _dwconv_bwd(proj, D, D, w['conv'], D, None, 'silu', 0, dkc, tag + '_conv_bwd_k')
    dpv, dwv = _dwconv_bwd(proj, 2 * D, 2 * D, w['conv'], 2 * D, None, 'silu', 0, dv, tag + '_conv_bwd_v')
    g['conv'] = jnp.concatenate([dwq, dwk, dwv], axis=1)
    dproj = jnp.concatenate([dpq, dpk, dpv, dz], axis=1)
    g['cat'] = _mm(h, dproj, 'tn', F32, tag + '_dwcat')
    g['ab'] = _mm(h, dab, 'tn', F32, tag + '_dwab')
    dh, exchanged = _mm_carry(dproj, w['cat'], 'nt', F32, tag + '_dh', reduce_pairs, tag, g)
    return [dh, _mm(dab, w['ab'], 'nt', F32, tag + '_dh_ab')], g, exchanged


def _ffn_fwd(h, w, tag):
    dff = w['down'].shape[0]
    up = _mm(h, w['up'], 'nn', F32, tag + '_up')
    act = _dwconv(up, 0, dff, w['conv'], w['conv_b'], 'glu', dff, BF16, tag + '_conv')
    y = _mm(act, w['down'], 'nn', F32, tag + '_down')
    return y, dict(h=h, up=up, act=act)


def _ffn_bwd(dy, w, sv, tag, reduce_pairs=None):
    h, up = sv['h'], sv['up']
    dff = w['down'].shape[0]
    g = {}
    g['down'] = _mm(sv['act'], dy, 'tn', F32, tag + '_dwdown')
    dact = _mm(dy, w['down'], 'nt', F32, tag + '_dact')
    dgate, dval, g['conv'], g['conv_b'] = _dwconv_bwd(up, 0, dff, w['conv'], 0, w['conv_b'], 'glu', dff, dact,
                                                      tag + '_conv_bwd')
    dup = jnp.concatenate([dgate, dval], axis=1)
    g['up'] = _mm(h, dup, 'tn', F32, tag + '_dwup')
    dh, exchanged = _mm_carry(dup, w['up'], 'nt', F32, tag + '_dh', reduce_pairs, tag, g)
    return [dh], g, exchanged


def _local_step(x, target, mods, norm_g, wf, wg, wffn, late=None, reduce_pairs=None):
    tape = []
    for i in range(2):
        for sub in range(2):
            if sub == 0:
                fwd, bwd, w, tag = [(_fox_fwd, _fox_bwd, wf, 'fox'), (_gdn_fwd, _gdn_bwd, wg, 'gdn')][i]
            else:
                fwd, bwd, w, tag = _ffn_fwd, _ffn_bwd, wffn[i], 'ffn%d' % i
            shift, scale, gate = _split_mod(mods[i, sub])
            g_pre, g_post = norm_g[i, 2 * sub][None], norm_g[i, 2 * sub + 1][None]
            h = _pre_norm(x, g_pre, scale, shift, tag + '_prenorm')
            if late is not None and (i, sub) == (0, 0):
                y, sv, (wg, wffn) = fwd(h, w, tag, late)
            else:
                y, sv = fwd(h, w, tag)
            x_out = _post_res(x, y, gate, g_post, tag + '_postnorm')
            tape.append((bwd, w, tag, sv, x, y, g_pre, g_post, scale, gate))
            x = x_out
    dx, lsum = _loss_head(x, target, 'loss_head')
    loss = lsum[0, 0]
    dmods = [[None, None], [None, None]]
    dnorm = [[None] * 4, [None] * 4]
    wgrads = {}
    exchanged = {}
    for idx in reversed(range(4)):
        i, sub = divmod(idx, 2)
        bwd, w, tag, sv, x_in, y, g_pre, g_post, scale, gate = tape[idx]
        dy, dgate, dgpost = _post_res_bwd(dx, y, gate, g_post, tag + '_postnorm_bwd')
        dh, wgrads[tag], ex = bwd(dy, w, sv, tag, reduce_pairs)
        exchanged.update(ex)
        dx, dshift, dscale, dgpre = _pre_norm_bwd(dh, x_in, g_pre, scale, dx, tag + '_prenorm_bwd')
        dmods[i][sub] = jnp.concatenate([dshift[0], dscale[0], dgate[0]])
        dnorm[i][2 * sub], dnorm[i][2 * sub + 1] = dgpre[0], dgpost[0]
    dmods = jnp.stack([jnp.stack(r) for r in dmods])
    dnorm = jnp.stack([jnp.stack(r) for r in dnorm])
    return loss, dx, dmods, dnorm, wgrads, exchanged


def _unpack_lanes_dev(packed, shapes):
    n = packed.shape[0]
    out, r = [], 0
    for shp in shapes:
        k = math.prod(shp)
        nr = -(-k // LANES)
        out.append(packed[:, r:r + nr].reshape(n, -1)[:, :k].reshape((n,) + tuple(shp)))
        r += nr
    return out


def _gather_lanes(arrs, name):
    packed = _pack_lanes(arrs)
    got = _allgather8(packed, name).reshape(N_DEV, packed.shape[0], LANES)
    return _unpack_lanes_dev(got, [a.shape for a in arrs]), got


def kernel(x, c, ada_w, ada_b, norm_g, fox_w_in, fox_f_bias, fox_q_norm, fox_k_norm, fox_w_o, gdn_w_in, gdn_conv_w, gdn_a_log, gdn_dt_bias, gdn_out_norm, gdn_w_o, ffn_w_up, ffn_conv_w, ffn_conv_b, ffn_w_down, loss_target, m_ada_w, m_ada_b, m_norm_g, m_fox_w_in, m_fox_f_bias, m_fox_q_norm, m_fox_k_norm, m_fox_w_o, m_gdn_w_in, m_gdn_conv_w, m_gdn_a_log, m_gdn_dt_bias, m_gdn_out_norm, m_gdn_w_o, m_ffn_w_up, m_ffn_conv_w, m_ffn_conv_b, m_ffn_w_down, v_ada_w, v_ada_b, v_norm_g, v_fox_w_in, v_fox_f_bias, v_fox_q_norm, v_fox_k_norm, v_fox_w_o, v_gdn_w_in, v_gdn_conv_w, v_gdn_a_log, v_gdn_dt_bias, v_gdn_out_norm, v_gdn_w_o, v_ffn_w_up, v_ffn_conv_w, v_ffn_conv_b, v_ffn_w_down):
    args = locals()
    w = {n: args[n] for n in WEIGHTS}
    mom = {n: args['m_' + n] for n in WEIGHTS}
    var = {n: args['v_' + n] for n in WEIGHTS}
    _, T, D = x.shape
    H = D // HD
    Hv = V_PER_K * H
    xi, yi, ci = _place()
    s_idx = 2 * xi + yi
    b_idx = 4 * xi + 2 * yi + ci
    sc_arr = jnp.stack([s_idx, ci]).astype(jnp.int32)
    c_arr = jnp.reshape(ci, (1,)).astype(jnp.int32)

    (c_all, ab_all, ng_all, gcw_all, fcw_all), _ = _gather_lanes(
        [jnp.tile(c, (SUBLANES, 1)), ada_b, norm_g, gdn_conv_w, ffn_conv_w], 'gather_small')
    c_all = c_all[:, 0, :]
    chips = lambda a: jnp.concatenate([a[2 * s] for s in range(N_CHIPS)], axis=-1)
    norm_g_full, gdn_conv_full, ffn_conv_full = chips(ng_all), chips(gcw_all), chips(fcw_all)

    Ns = ada_w.shape[-1]
    ada_w4 = ada_w.reshape(4, D, Ns)
    part = jnp.stack([_mm(c_all, ada_w4[i], 'nn', F32, 'ada_proj%d' % i, a_act='silu') for i in range(4)])
    part = part + ada_b.reshape(4, 1, Ns)
    (part_all,), _ = _gather_lanes([part], 'gather_mods')
    mine = lax.dynamic_index_in_dim(part_all[0::2], b_idx, axis=2, keepdims=False)
    mods = mine.transpose(1, 0, 2).reshape(2, 2, N_CHIPS * Ns)

    as2d = lambda a: a.reshape(-1, a.shape[-1])
    own = {n: as2d(w[n]).astype(BF16) for n in BIG}

    def whole(n, gathered):
        shards = lax.dynamic_update_index_in_dim(gathered, own[n], s_idx, 0)
        return _shards_to_full(shards.reshape((N_CHIPS,) + w[n].shape), BIG_SHARD_AXIS[n])

    fw = whole('fox_w_in', _gather_weights(own['fox_w_in'], 'gather_fox_w_in'))[0]
    wf = dict(cat=jnp.concatenate([fw[:, :3 * D], fw[:, 3 * D + H:]], axis=1), f=_pad_cols(fw[:, 3 * D:3 * D + H], LANES),
              f_bias=_pad_cols(fox_f_bias, LANES), q_norm=fox_q_norm, k_norm=fox_k_norm)
    later = [n for n in BIG if n != 'fox_w_in']

    def later_weights(gathered):
        full = {n: whole(n, a) for n, a in zip(later, gathered)}
        gw = full['gdn_w_in'][0]
        wg = dict(cat=gw[:, :6 * D], ab=_pad_cols(gw[:, 6 * D:], LANES), conv=gdn_conv_full[0],
                  a_log=_pad_cols(gdn_a_log, LANES), dt_bias=_pad_cols(gdn_dt_bias, LANES), out_norm=gdn_out_norm,
                  o=full['gdn_w_o'][0])
        wffn = [dict(up=full['ffn_w_up'][i], conv=ffn_conv_full[i], conv_b=ffn_conv_b[i][None],
                     down=full['ffn_w_down'][i]) for i in range(2)]
        return full['fox_w_o'][0], (wg, wffn)

    def reduce_pairs(tag, gr):
        if tag == 'fox':
            full = {('fox_w_in', 0): jnp.concatenate([gr['cat'][:, :3 * D], gr['f'][:, :H], gr['cat'][:, 3 * D:]], axis=1),
                    ('fox_w_o', 0): gr['o']}
        elif tag == 'gdn':
            full = {('gdn_w_in', 0): jnp.concatenate([gr['cat'], gr['ab'][:, :2 * Hv]], axis=1), ('gdn_w_o', 0): gr['o']}
        else:
            layer = int(tag[-1])
            full = {('ffn_w_up', layer): gr['up'], ('ffn_w_down', layer): gr['down']}
        out = []
        for (n, layer), a in full.items():
            shards = _full_to_shards(a, BIG_SHARD_AXIS[n] - 1)
            name = '%s%d' % (n, layer)
            got = _swap_halves(shards, name + '_to_sibling')
            out.append(((n, layer), _add_halves(shards, got, c_arr, name + '_add_sibling')))
        return out

    loss, dx, dmods, dnorm, g, exchanged = _local_step(
        x[0], loss_target[0], mods, norm_g_full, wf, None, None,
        late=([own[n] for n in later], later_weights), reduce_pairs=reduce_pairs)
    loss = lax.psum(loss, ('x', 'y', 'c'))

    gf, gg = g['fox'], g['gdn']
    big_out = {}
    for n in BIG:
        parts = []
        for layer in range(w[n].shape[0]):
            tag = '%s%d' % (n, layer)
            pair_sum, received = exchanged[(n, layer)]
            half_sum = _add_chips(pair_sum, received, sc_arr, tag + '_add_chips')
            parts.append(_join_halves(half_sum, tag + '_join'))
        g_shard = parts[0] if len(parts) == 1 else jnp.concatenate(parts, axis=0)
        big_out[n] = [o.reshape(w[n].shape)
                      for o in _adamw(as2d(w[n]), as2d(mom[n]), as2d(var[n]), [g_shard], 'adamw_' + n)]

    small_part = [dmods, dnorm, gf['f_bias'][:, :H], gf['q_norm'], gf['k_norm'], gg['conv'][None],
                  gg['a_log'][:, :Hv], gg['dt_bias'][:, :Hv], gg['out_norm'],
                  jnp.stack([g['ffn0']['conv'], g['ffn1']['conv']]),
                  jnp.concatenate([g['ffn0']['conv_b'], g['ffn1']['conv_b']], axis=0)]
    (dmods_all, *_), got = _gather_lanes(small_part, 'gather_small_grads')
    tot = _unpack_lanes(_sum_devices(got, 'sum_small_grads'), [a.shape for a in small_part])
    small_full = dict(zip(SMALL, tot))
    small_g = {n: (lax.dynamic_slice_in_dim(small_full[n], s_idx * w[n].shape[-1], w[n].shape[-1], axis=-1)
                   if n in SMALL_SHARDED else small_full[n]) for n in SMALL}
    packs = lambda d: _pack_lanes([d[n] for n in SMALL])
    small_shapes = [w[n].shape for n in SMALL]
    small_out = [_unpack_lanes(o, small_shapes)
                 for o in _adamw(packs(w), packs(mom), packs(var), [packs(small_g)], 'adamw_small')]

    dm = lax.dynamic_slice_in_dim(dmods_all.reshape(N_DEV, 4, N_CHIPS * Ns), s_idx * Ns, Ns, axis=-1).transpose(1, 0, 2)
    ada_out = [o.reshape(ada_w.shape) for o in
               _ada_w_update(c_all, dm, ada_w4, m_ada_w.reshape(4, D, Ns), v_ada_w.reshape(4, D, Ns), 'adamw_ada_w')]

    outs = []
    for k in range(4):
        by_name = {'ada_w': ada_out[k]}
        by_name.update({n: big_out[n][k] for n in BIG})
        by_name.update(zip(SMALL, small_out[k]))
        outs += [by_name[n] for n in WEIGHTS]
    return (loss, dx[None], *outs)
```

```python
import functools
import math

import jax
import jax.numpy as jnp
from jax import lax
from jax.experimental import pallas as pl
from jax.experimental.pallas import tpu as pltpu

F32 = jnp.float32
BF16 = jnp.bfloat16
EPS = 1e-6
HD = 128
GDN_CHUNK = 64
GDN_CONV = 4
FFN_CONV = 3
LANES = 128
SUBLANES = 8
VMEM_LIMIT = 56 * 1024 * 1024
HIGHEST = lax.Precision.HIGHEST
NEG = -1e30

ADAM_LR = 0.001
ADAM_B1 = 0.9
ADAM_B2 = 0.999
ADAM_EPS = 1e-08
ADAM_WD = 0.01
ADAM_STEP = 10

WEIGHTS = ['ada_w', 'ada_b', 'norm_g', 'fox_w_in', 'fox_f_bias', 'fox_q_norm', 'fox_k_norm', 'fox_w_o',
           'gdn_w_in', 'gdn_conv_w', 'gdn_a_log', 'gdn_dt_bias', 'gdn_out_norm', 'gdn_w_o',
           'ffn_w_up', 'ffn_conv_w', 'ffn_conv_b', 'ffn_w_down']
BIG = ['fox_w_in', 'fox_w_o', 'gdn_w_in', 'gdn_w_o', 'ffn_w_up', 'ffn_w_down']
BIG_SHARD_AXIS = {'fox_w_in': 2, 'fox_w_o': 1, 'gdn_w_in': 2, 'gdn_w_o': 1, 'ffn_w_up': 2, 'ffn_w_down': 1}
SMALL = ['ada_b', 'norm_g', 'fox_f_bias', 'fox_q_norm', 'fox_k_norm', 'gdn_conv_w', 'gdn_a_log',
         'gdn_dt_bias', 'gdn_out_norm', 'ffn_conv_w', 'ffn_conv_b']
SMALL_SHARDED = ['ada_b', 'norm_g', 'gdn_conv_w', 'ffn_conv_w']
N_CHIPS = 4
N_DEV = 8
MESH = pl.DeviceIdType.MESH


def _tile(n, cands):
    for c in cands:
        if n % c == 0:
            return c
    return n


def _cp(*sem):
    return pltpu.CompilerParams(dimension_semantics=sem, vmem_limit_bytes=VMEM_LIMIT)


def _dot(a, b, mode='nn', precision=None):
    dims = {'nn': (((1,), (0,)), ((), ())), 'nt': (((1,), (1,)), ((), ())), 'tn': (((0,), (0,)), ((), ()))}[mode]
    return lax.dot_general(a, b, dims, precision=precision, preferred_element_type=F32)


def _bdot(a, b, mode='nn'):
    return _dot(a.astype(BF16), b.astype(BF16), mode)


def _hdot(a, b, mode='nn'):
    return _dot(a, b, mode, precision=HIGHEST)


def _sigmoid(x):
    return 1.0 / (1.0 + jnp.exp(-x))


def _silu(x):
    return x * _sigmoid(x)


def _softplus(x):
    return jnp.maximum(x, 0.0) + jnp.log(1.0 + jnp.exp(-jnp.abs(x)))


def _erf(x):
    return lax.erf(x)


def _gelu(x):
    return 0.5 * x * (1.0 + _erf(x * (2.0 ** -0.5)))


def _normal_cdf_pdf(x):
    cdf = 0.5 * (1.0 + _erf(x * (2.0 ** -0.5)))
    pdf = jnp.exp(-0.5 * x * x) * (1.0 / math.sqrt(2.0 * math.pi))
    return cdf, pdf


MM_K_CAP = 2816


def _k_tile(K, cap):
    for t in range(cap - cap % LANES, 0, -LANES):
        if K % t == 0:
            return t
    return K


def _mm(a, b, mode, out_dtype, name, a_act=None, scatter=()):
    if mode == 'nn':
        (M, K), (_, N) = a.shape, b.shape
    elif mode == 'nt':
        (M, K), (N, _) = a.shape, b.shape
    else:
        (K, M), (_, N) = a.shape, b.shape
    big = (1024, 512, 256, 128)
    narrow = a.dtype.itemsize == 2 and b.dtype.itemsize == 2
    tm, tn, tk = _tile(M, big), _tile(N, big), _k_tile(K, MM_K_CAP if narrow else MM_K_CAP // 2)
    nk, ns = K // tk, len(scatter)
    grid = (M // tm, N // tn, nk)

    def body(a_ref, b_ref, *rest):
        q_refs, o_ref, got_refs = rest[:ns], rest[ns], rest[ns + 1:2 * ns + 1]
        acc = rest[2 * ns + 1:2 * ns + 1 + (nk > 1)]
        jobs = [_scatter_protocol(q_refs[i], got_refs[i], *rest[len(rest) - 2:], SCATTER_SEMS * i) for i in range(ns)]
        step = [pl.program_id(d) for d in range(3)]
        if jobs:
            @pl.when((step[0] == 0) & (step[1] == 0) & (step[2] == 0))
            def _():
                for start, _ in jobs:
                    start()

        av = a_ref[...]
        if a_act == 'silu':
            av = _silu(av.astype(F32))
        part = _bdot(av, b_ref[...], mode)
        if nk == 1:
            o_ref[...] = part.astype(o_ref.dtype)
        else:
            acc_ref, = acc
            k = step[2]

            @pl.when(k == 0)
            def _():
                acc_ref[...] = part

            @pl.when(k > 0)
            def _():
                acc_ref[...] += part

            @pl.when(k == nk - 1)
            def _():
                o_ref[...] = acc_ref[...].astype(o_ref.dtype)

        if jobs:
            @pl.when((step[0] == grid[0] - 1) & (step[1] == grid[1] - 1) & (step[2] == grid[2] - 1))
            def _():
                for _, finish in jobs:
                    finish()

    if mode == 'nn':
        a_spec = pl.BlockSpec((tm, tk), lambda i, j, k: (i, k))
        b_spec = pl.BlockSpec((tk, tn), lambda i, j, k: (k, j))
    elif mode == 'nt':
        a_spec = pl.BlockSpec((tm, tk), lambda i, j, k: (i, k))
        b_spec = pl.BlockSpec((tn, tk), lambda i, j, k: (j, k))
    else:
        a_spec = pl.BlockSpec((tk, tm), lambda i, j, k: (k, i))
        b_spec = pl.BlockSpec((tk, tn), lambda i, j, k: (k, j))
    sems = [pltpu.SemaphoreType.DMA((SCATTER_SEMS * ns,))] * 2 if ns else []
    out = pl.pallas_call(
        body, name=name, grid=grid,
        in_specs=[a_spec, b_spec] + [HBM_SPEC] * ns,
        out_specs=[pl.BlockSpec((tm, tn), lambda i, j, k: (i, j))] + [HBM_SPEC] * ns,
        out_shape=[jax.ShapeDtypeStruct((M, N), out_dtype)] + _scatter_shapes(scatter),
        scratch_shapes=([pltpu.VMEM((tm, tn), F32)] if nk > 1 else []) + sems,
        compiler_params=_cp("arbitrary", "arbitrary", "arbitrary") if ns else _cp("parallel", "parallel", "arbitrary"),
    )(a, b, *scatter)
    return (out[0], list(out[1:])) if ns else out[0]


ROW_TILES = (256, 128, 64, 32, 16, 8)


def _row_spec(tT, D):
    return pl.BlockSpec((tT, D), lambda i: (i, 0))


def _vec_spec(D):
    return pl.BlockSpec((1, D), lambda i: (0, 0))


def _pre_norm(x, g, scale, shift, name):
    T, D = x.shape
    tT = _tile(T, ROW_TILES)

    def body(x_ref, g_ref, sc_ref, sh_ref, h_ref):
        xv = x_ref[...]
        r = lax.rsqrt(jnp.mean(xv * xv, axis=-1, keepdims=True) + EPS)
        h_ref[...] = ((xv * r) * g_ref[...] * (1.0 + sc_ref[...]) + sh_ref[...]).astype(h_ref.dtype)

    return pl.pallas_call(
        body, name=name, grid=(T // tT,),
        in_specs=[_row_spec(tT, D), _vec_spec(D), _vec_spec(D), _vec_spec(D)],
        out_specs=_row_spec(tT, D), out_shape=jax.ShapeDtypeStruct((T, D), BF16),
        compiler_params=_cp("parallel"),
    )(x, g, scale, shift)


def _post_res(x, y, gate, g, name):
    T, D = x.shape
    tT = _tile(T, ROW_TILES)

    def body(x_ref, y_ref, gate_ref, g_ref, o_ref):
        yv = y_ref[...]
        r = lax.rsqrt(jnp.mean(yv * yv, axis=-1, keepdims=True) + EPS)
        o_ref[...] = x_ref[...] + gate_ref[...] * ((yv * r) * g_ref[...])

    return pl.pallas_call(
        body, name=name, grid=(T // tT,),
        in_specs=[_row_spec(tT, D), _row_spec(tT, D), _vec_spec(D), _vec_spec(D)],
        out_specs=_row_spec(tT, D), out_shape=jax.ShapeDtypeStruct((T, D), F32),
        compiler_params=_cp("parallel"),
    )(x, y, gate, g)


def _post_res_bwd(dout, y, gate, g, name):
    T, D = y.shape
    tT = _tile(T, ROW_TILES)

    def body(do_ref, y_ref, gate_ref, g_ref, dy_ref, dgate_ref, dg_ref):
        @pl.when(pl.program_id(0) == 0)
        def _():
            dgate_ref[...] = jnp.zeros_like(dgate_ref)
            dg_ref[...] = jnp.zeros_like(dg_ref)

        yv, dov, gatev, gv = y_ref[...], do_ref[...], gate_ref[...], g_ref[...]
        r = lax.rsqrt(jnp.mean(yv * yv, axis=-1, keepdims=True) + EPS)
        yn = yv * r
        t = dov * yn
        dgate_ref[...] += jnp.sum(t * gv, axis=0, keepdims=True)
        dg_ref[...] += jnp.sum(t * gatev, axis=0, keepdims=True)
        dyn = dov * (gatev * gv)
        dy_ref[...] = (r * (dyn - yn * jnp.mean(dyn * yn, axis=-1, keepdims=True))).astype(dy_ref.dtype)

    return pl.pallas_call(
        body, name=name, grid=(T // tT,),
        in_specs=[_row_spec(tT, D), _row_spec(tT, D), _vec_spec(D), _vec_spec(D)],
        out_specs=[_row_spec(tT, D), _vec_spec(D), _vec_spec(D)],
        out_shape=[jax.ShapeDtypeStruct((T, D), BF16), jax.ShapeDtypeStruct((1, D), F32),
                   jax.ShapeDtypeStruct((1, D), F32)],
        compiler_params=_cp("arbitrary"),
    )(dout, y, gate, g)


def _pre_norm_bwd(dhs, x, g, scale, dres, name):
    T, D = x.shape
    tT = _tile(T, ROW_TILES)
    n = len(dhs)

    def body(*refs):
        dh_refs = refs[:n]
        x_ref, g_ref, sc_ref, dres_ref, dx_ref, dsh_ref, dsc_ref, dg_ref = refs[n:]

        @pl.when(pl.program_id(0) == 0)
        def _():
            dsh_ref[...] = jnp.zeros_like(dsh_ref)
            dsc_ref[...] = jnp.zeros_like(dsc_ref)
            dg_ref[...] = jnp.zeros_like(dg_ref)

        dh = dh_refs[0][...]
        for r_ in dh_refs[1:]:
            dh = dh + r_[...]
        xv, gv, scv = x_ref[...], g_ref[...], sc_ref[...]
        r = lax.rsqrt(jnp.mean(xv * xv, axis=-1, keepdims=True) + EPS)
        xn = xv * r
        t = dh * xn
        dsh_ref[...] += jnp.sum(dh, axis=0, keepdims=True)
        dsc_ref[...] += jnp.sum(t * gv, axis=0, keepdims=True)
        dg_ref[...] += jnp.sum(t * (1.0 + scv), axis=0, keepdims=True)
        dxn = dh * (gv * (1.0 + scv))
        dx_ref[...] = dres_ref[...] + r * (dxn - xn * jnp.mean(dxn * xn, axis=-1, keepdims=True))

    return pl.pallas_call(
        body, name=name, grid=(T // tT,),
        in_specs=[_row_spec(tT, D)] * n + [_row_spec(tT, D), _vec_spec(D), _vec_spec(D), _row_spec(tT, D)],
        out_specs=[_row_spec(tT, D), _vec_spec(D), _vec_spec(D), _vec_spec(D)],
        out_shape=[jax.ShapeDtypeStruct((T, D), F32)] + [jax.ShapeDtypeStruct((1, D), F32)] * 3,
        compiler_params=_cp("arbitrary"),
    )(*dhs, x, g, scale, dres)


def _loss_head(y, target, name):
    T, D = y.shape
    tT = _tile(T, ROW_TILES)

    def body(y_ref, t_ref, dy_ref, l_ref):
        @pl.when(pl.program_id(0) == 0)
        def _():
            l_ref[...] = jnp.zeros_like(l_ref)

        e = y_ref[...] - t_ref[...]
        dy_ref[...] = e * (1.0 / D)
        s = jnp.sum(jnp.mean(e * e, axis=-1, keepdims=True), axis=0, keepdims=True)
        l_ref[...] += 0.5 * s

    return pl.pallas_call(
        body, name=name, grid=(T // tT,),
        in_specs=[_row_spec(tT, D), _row_spec(tT, D)],
        out_specs=[_row_spec(tT, D), pl.BlockSpec((SUBLANES, LANES), lambda i: (0, 0))],
        out_shape=[jax.ShapeDtypeStruct((T, D), F32), jax.ShapeDtypeStruct((SUBLANES, LANES), F32)],
        compiler_params=_cp("arbitrary"),
    )(y, target)


HEAD_ROW_TILES = (2048, 1024, 512, 256, 128, 64)


def _hb(tT, off=0):
    return pl.BlockSpec((tT, HD), lambda i, h: (i, off + h))


def _hvec():
    return pl.BlockSpec((1, HD), lambda i, h: (0, 0))


def _headnorm(x, off, H, g, c1, post, out_dtype, name):
    T = x.shape[0]
    tT = _tile(T, HEAD_ROW_TILES)
    has_g = g is not None

    def body(*refs):
        x_ref = refs[0]
        o_ref = refs[-1]
        xv = x_ref[...]
        yv = xv * lax.rsqrt(c1 * jnp.sum(xv * xv, axis=-1, keepdims=True) + EPS)
        if has_g:
            yv = yv * refs[1][...]
        if post != 1.0:
            yv = yv * post
        o_ref[...] = yv.astype(o_ref.dtype)

    return pl.pallas_call(
        body, name=name, grid=(T // tT, H),
        in_specs=[_hb(tT, off)] + ([_hvec()] if has_g else []),
        out_specs=_hb(tT), out_shape=jax.ShapeDtypeStruct((T, H * HD), out_dtype),
        compiler_params=_cp("parallel", "parallel"),
    )(*([x, g] if has_g else [x]))


def _headnorm_bwd(dys, x, off, H, g, c1, post, out_dtype, name):
    T = x.shape[0]
    tT = _tile(T, HEAD_ROW_TILES)
    n = len(dys)
    has_g = g is not None

    def body(*refs):
        dy_refs = refs[:n]
        x_ref = refs[n]
        g_ref = refs[n + 1] if has_g else None
        dx_ref, dg_ref = refs[-2], refs[-1]

        @pl.when((pl.program_id(0) == 0) & (pl.program_id(1) == 0))
        def _():
            dg_ref[...] = jnp.zeros_like(dg_ref)

        dy = dy_refs[0][...].astype(F32)
        for r_ in dy_refs[1:]:
            dy = dy + r_[...].astype(F32)
        if post != 1.0:
            dy = dy * post
        xv = x_ref[...]
        r = lax.rsqrt(c1 * jnp.sum(xv * xv, axis=-1, keepdims=True) + EPS)
        xn = xv * r
        if has_g:
            dg_ref[...] += jnp.sum(dy * xn, axis=0, keepdims=True)
            dy = dy * g_ref[...]
        dx_ref[...] = (r * (dy - xn * (c1 * jnp.sum(dy * xn, axis=-1, keepdims=True)))).astype(dx_ref.dtype)

    dy_specs = [pl.BlockSpec((tT, HD), lambda i, h, st=st, of=of: (i, st * h + of)) for (_, st, of) in dys]
    return pl.pallas_call(
        body, name=name, grid=(T // tT, H),
        in_specs=dy_specs + [_hb(tT, off)] + ([_hvec()] if has_g else []),
        out_specs=[_hb(tT), _hvec()],
        out_shape=[jax.ShapeDtypeStruct((T, H * HD), out_dtype), jax.ShapeDtypeStruct((1, HD), F32)],
        compiler_params=_cp("arbitrary", "arbitrary"),
    )(*[d[0] for d in dys], x, *([g] if has_g else []))


def _fox_gate(ao, proj, og_off, H, name):
    T = ao.shape[0]
    tT = _tile(T, HEAD_ROW_TILES)

    def body(ao_ref, og_ref, o_ref):
        o_ref[...] = (ao_ref[...] * _sigmoid(og_ref[...])).astype(o_ref.dtype)

    return pl.pallas_call(
        body, name=name, grid=(T // tT, H),
        in_specs=[_hb(tT), _hb(tT, og_off)], out_specs=_hb(tT),
        out_shape=jax.ShapeDtypeStruct((T, H * HD), BF16), compiler_params=_cp("parallel", "parallel"),
    )(ao, proj)


def _fox_gate_bwd(dgated, ao, proj, og_off, H, name):
    T = ao.shape[0]
    tT = _tile(T, HEAD_ROW_TILES)

    def body(dg_ref, ao_ref, og_ref, dao_ref, dog_ref, delta_ref):
        dg, aov = dg_ref[...], ao_ref[...]
        sg = _sigmoid(og_ref[...])
        dao = dg * sg
        dao_ref[...] = dao.astype(dao_ref.dtype)
        dog_ref[...] = (dg * aov * sg * (1.0 - sg)).astype(dog_ref.dtype)
        delta_ref[...] = jnp.broadcast_to(jnp.sum(dao * aov, axis=-1, keepdims=True), delta_ref.shape)

    return pl.pallas_call(
        body, name=name, grid=(T // tT, H),
        in_specs=[_hb(tT), _hb(tT), _hb(tT, og_off)], out_specs=[_hb(tT)] * 3,
        out_shape=[jax.ShapeDtypeStruct((T, H * HD), BF16), jax.ShapeDtypeStruct((T, H * HD), BF16),
                   jax.ShapeDtypeStruct((T, H * HD), F32)],
        compiler_params=_cp("parallel", "parallel"),
    )(dgated, ao, proj)


def _gdn_out(o, proj, z_off, Hv, g, name):
    T = o.shape[0]
    tT = _tile(T, HEAD_ROW_TILES)

    def body(o_ref, z_ref, g_ref, y_ref):
        ov, zv = o_ref[...], z_ref[...]
        r = lax.rsqrt(jnp.mean(ov * ov, axis=-1, keepdims=True) + EPS)
        y_ref[...] = (((ov * r) * g_ref[...]) * _silu(zv)).astype(y_ref.dtype)

    return pl.pallas_call(
        body, name=name, grid=(T // tT, Hv),
        in_specs=[_hb(tT), _hb(tT, z_off), _hvec()], out_specs=_hb(tT),
        out_shape=jax.ShapeDtypeStruct((T, Hv * HD), BF16), compiler_params=_cp("parallel", "parallel"),
    )(o, proj, g)


def _gdn_out_bwd(dy, o, proj, z_off, Hv, g, name):
    T = o.shape[0]
    tT = _tile(T, HEAD_ROW_TILES)

    def body(dy_ref, o_ref, z_ref, g_ref, do_ref, dz_ref, dg_ref):
        @pl.when((pl.program_id(0) == 0) & (pl.program_id(1) == 0))
        def _():
            dg_ref[...] = jnp.zeros_like(dg_ref)

        dyv, ov, zv, gv = dy_ref[...], o_ref[...], z_ref[...], g_ref[...]
        r = lax.rsqrt(jnp.mean(ov * ov, axis=-1, keepdims=True) + EPS)
        on = ov * r
        sg = _sigmoid(zv)
        sz = zv * sg
        dz_ref[...] = (dyv * (on * gv) * (sg * (1.0 + zv * (1.0 - sg)))).astype(dz_ref.dtype)
        t = dyv * sz
        dg_ref[...] += jnp.sum(t * on, axis=0, keepdims=True)
        don = t * gv
        do_ref[...] = r * (don - on * jnp.mean(don * on, axis=-1, keepdims=True))

    return pl.pallas_call(
        body, name=name, grid=(T // tT, Hv),
        in_specs=[_hb(tT), _hb(tT), _hb(tT, z_off), _hvec()], out_specs=[_hb(tT), _hb(tT), _hvec()],
        out_shape=[jax.ShapeDtypeStruct((T, Hv * HD), F32), jax.ShapeDtypeStruct((T, Hv * HD), BF16),
                   jax.ShapeDtypeStruct((1, HD), F32)],
        compiler_params=_cp("arbitrary", "arbitrary"),
    )(dy, o, proj, g)


def _lrow(tT):
    return pl.BlockSpec((tT, LANES), lambda i: (i, 0))


def _lvec():
    return pl.BlockSpec((1, LANES), lambda i: (0, 0))


def _logsig(x, b, name):
    T = x.shape[0]
    tT = _tile(T, HEAD_ROW_TILES)

    def body(x_ref, b_ref, o_ref):
        o_ref[...] = -_softplus(-(x_ref[...] + b_ref[...]))

    return pl.pallas_call(body, name=name, grid=(T // tT,), in_specs=[_lrow(tT), _lvec()], out_specs=_lrow(tT),
                          out_shape=jax.ShapeDtypeStruct((T, LANES), F32), compiler_params=_cp("parallel"))(x, b)


def _logsig_bwd(dy, x, b, name):
    T = x.shape[0]
    tT = _tile(T, HEAD_ROW_TILES)

    def body(dy_ref, x_ref, b_ref, dx_ref, db_ref):
        @pl.when(pl.program_id(0) == 0)
        def _():
            db_ref[...] = jnp.zeros_like(db_ref)

        dx = dy_ref[...] * _sigmoid(-(x_ref[...] + b_ref[...]))
        dx_ref[...] = dx.astype(dx_ref.dtype)
        db_ref[...] += jnp.sum(dx, axis=0, keepdims=True)

    return pl.pallas_call(
        body, name=name, grid=(T // tT,), in_specs=[_lrow(tT), _lrow(tT), _lvec()], out_specs=[_lrow(tT), _lvec()],
        out_shape=[jax.ShapeDtypeStruct((T, LANES), BF16), jax.ShapeDtypeStruct((1, LANES), F32)],
        compiler_params=_cp("arbitrary"))(dy, x, b)


def _gdn_gates(ab, alog, dtb, name):
    T = ab.shape[0]
    tT = _tile(T, HEAD_ROW_TILES)

    def body(ab_ref, al_ref, dt_ref, g_ref, be_ref):
        v = ab_ref[...]
        g_ref[...] = -jnp.exp(al_ref[...]) * _softplus(v + dt_ref[...])
        be_ref[...] = _sigmoid(v)

    return pl.pallas_call(
        body, name=name, grid=(T // tT,), in_specs=[_lrow(tT), _lvec(), _lvec()], out_specs=[_lrow(tT)] * 2,
        out_shape=[jax.ShapeDtypeStruct((T, LANES), F32)] * 2, compiler_params=_cp("parallel"))(ab, alog, dtb)


def _gdn_gates_bwd(dg, dbeta, ab, alog, dtb, name):
    T = ab.shape[0]
    tT = _tile(T, HEAD_ROW_TILES)

    def body(dg_ref, dbe_ref, ab_ref, al_ref, dt_ref, dab_ref, dal_ref, ddt_ref):
        @pl.when(pl.program_id(0) == 0)
        def _():
            dal_ref[...] = jnp.zeros_like(dal_ref)
            ddt_ref[...] = jnp.zeros_like(ddt_ref)

        v, dgv = ab_ref[...], dg_ref[...]
        ea = jnp.exp(al_ref[...])
        z = v + dt_ref[...]
        da = dgv * (-ea * _sigmoid(z))
        sb = _sigmoid(v)
        dab_ref[...] = (da + dbe_ref[...] * sb * (1.0 - sb)).astype(dab_ref.dtype)
        dal_ref[...] += jnp.sum(dgv * (-ea * _softplus(z)), axis=0, keepdims=True)
        ddt_ref[...] += jnp.sum(da, axis=0, keepdims=True)

    return pl.pallas_call(
        body, name=name, grid=(T // tT,), in_specs=[_lrow(tT), _lrow(tT), _lrow(tT), _lvec(), _lvec()],
        out_specs=[_lrow(tT), _lvec(), _lvec()],
        out_shape=[jax.ShapeDtypeStruct((T, LANES), BF16), jax.ShapeDtypeStruct((1, LANES), F32),
                   jax.ShapeDtypeStruct((1, LANES), F32)],
        compiler_params=_cp("arbitrary"))(dg, dbeta, ab, alog, dtb)


def _spread_heads(xs, offs, n_heads, name):
    T = xs[0].shape[0]
    tT = _tile(T, ROW_TILES)
    n = len(xs)

    def body(*refs):
        for x_ref, o_ref, off in zip(refs[:n], refs[n:], offs):
            xv = x_ref[...]
            for j in range(n_heads):
                o_ref[:, j * HD:(j + 1) * HD] = jnp.broadcast_to(xv[:, off + j:off + j + 1], (tT, HD))

    wide = pl.BlockSpec((tT, n_heads * HD), lambda i: (i, 0))
    return pl.pallas_call(
        body, name=name, grid=(T // tT,), in_specs=[_lrow(tT)] * n, out_specs=[wide] * n,
        out_shape=[jax.ShapeDtypeStruct((T, n_heads * HD), F32)] * n, compiler_params=_cp("parallel"))(*xs)


def _cumsum(x, seg, reverse, name):
    T = x.shape[0]
    tb = _tile(T, (256, 128, 64))
    nb = T // tb
    carry = seg is None

    def body(x_ref, o_ref, c_ref):
        @pl.when(pl.program_id(0) == 0)
        def _():
            c_ref[...] = jnp.zeros_like(c_ref)

        ri = lax.broadcasted_iota(jnp.int32, (tb, tb), 0)
        ci = lax.broadcasted_iota(jnp.int32, (tb, tb), 1)
        keep = (ci >= ri) if reverse else (ci <= ri)
        if seg is not None:
            keep = keep & ((ri // seg) == (ci // seg))
        y = _hdot(keep.astype(F32), x_ref[...])
        if carry:
            y = y + c_ref[...]
            c_ref[...] = y[0:1, :] if reverse else y[tb - 1:tb, :]
        o_ref[...] = y

    imap = (lambda i: (nb - 1 - i, 0)) if reverse else (lambda i: (i, 0))
    return pl.pallas_call(
        body, name=name, grid=(nb,), in_specs=[pl.BlockSpec((tb, LANES), imap)],
        out_specs=pl.BlockSpec((tb, LANES), imap), out_shape=jax.ShapeDtypeStruct((T, LANES), F32),
        scratch_shapes=[pltpu.VMEM((1, LANES), F32)], compiler_params=_cp("arbitrary"))(x)


ATT_TILES = (1024, 512, 256, 128)


def _att_scores(q, k, f0, fk):
    return _dot(q, k, 'nt') - (fk - f0)


def _diag_keep(tq):
    return lax.broadcasted_iota(jnp.int32, (tq, tq), 1) <= lax.broadcasted_iota(jnp.int32, (tq, tq), 0)


def _tri_pairs(nq, by_key):
    if by_key:
        pairs = [(qi, ki) for ki in range(nq) for qi in range(ki, nq)]
    else:
        pairs = [(qi, ki) for qi in range(nq) for ki in range(qi + 1)]
    return jnp.asarray([p[0] for p in pairs], jnp.int32), jnp.asarray([p[1] for p in pairs], jnp.int32)


def _att_specs(tq):
    qspec = pl.BlockSpec((tq, HD), lambda h, p, qt, kt: (qt[p], h))
    kspec = pl.BlockSpec((tq, HD), lambda h, p, qt, kt: (kt[p], h))
    f0spec = pl.BlockSpec((None, None, 1, 1), lambda h, p, qt, kt: (h, qt[p], 0, 0))
    fkspec = pl.BlockSpec((None, 1, tq), lambda h, p, qt, kt: (h, 0, kt[p]))
    return qspec, kspec, f0spec, fkspec


def _flash_fwd(qn, kn, vb, f0, fkr, H, name, carry=()):
    T = qn.shape[0]
    tq = _tile(T, ATT_TILES)
    nq = T // tq
    qt, kt = _tri_pairs(nq, by_key=False)
    n_pairs, n_carry = qt.shape[0], len(carry)

    def body(qt_ref, kt_ref, q_ref, k_ref, v_ref, f0_ref, fk_ref, *rest):
        w_refs, (o_ref, lse_ref), g_refs = rest[:n_carry], rest[n_carry:n_carry + 2], rest[n_carry + 2:2 * n_carry + 2]
        m_s, l_s, acc_s = rest[2 * n_carry + 2:2 * n_carry + 5]
        jobs = [_gather_protocol(w_refs[i], g_refs[i], *rest[2 * n_carry + 5:], GATHER_SEMS * i) for i in range(n_carry)]
        qi, ki = qt_ref[pl.program_id(1)], kt_ref[pl.program_id(1)]

        if jobs:
            @pl.when((pl.program_id(0) == 0) & (pl.program_id(1) == 0))
            def _():
                for start, _ in jobs:
                    start()

        @pl.when(ki == 0)
        def _():
            m_s[...] = jnp.full_like(m_s, NEG)
            l_s[...] = jnp.zeros_like(l_s)
            acc_s[...] = jnp.zeros_like(acc_s)

        def step(diagonal):
            s = _att_scores(q_ref[...], k_ref[...], f0_ref[...], fk_ref[...])
            if diagonal:
                s = jnp.where(_diag_keep(tq), s, NEG)
            m_prev = m_s[...]
            m_new = jnp.maximum(m_prev, jnp.max(s, axis=1, keepdims=True))
            alpha = jnp.exp(m_prev - m_new)
            p = jnp.exp(s - m_new[:, :1])
            l_s[...] = alpha * l_s[...] + jnp.sum(p, axis=1, keepdims=True)
            acc_s[...] = acc_s[...] * alpha + _bdot(p, v_ref[...])
            m_s[...] = m_new

        @pl.when(ki < qi)
        def _():
            step(False)

        @pl.when(ki == qi)
        def _():
            step(True)
            o_ref[...] = acc_s[...] / l_s[...]
            lse_ref[...] = m_s[...] + jnp.log(l_s[...])

        if jobs:
            @pl.when((pl.program_id(0) == H - 1) & (pl.program_id(1) == n_pairs - 1))
            def _():
                for _, finish in jobs:
                    finish()

    qspec, kspec, f0spec, fkspec = _att_specs(tq)
    sems = [pltpu.SemaphoreType.DMA((GATHER_SEMS * n_carry,))] * 2 if n_carry else []
    grid_spec = pltpu.PrefetchScalarGridSpec(
        num_scalar_prefetch=2, grid=(H, n_pairs),
        in_specs=[qspec, kspec, kspec, f0spec, fkspec] + [HBM_SPEC] * n_carry,
        out_specs=[qspec, qspec] + [HBM_SPEC] * n_carry,
        scratch_shapes=[pltpu.VMEM((tq, HD), F32)] * 3 + sems)
    out = pl.pallas_call(
        body, name=name, grid_spec=grid_spec,
        out_shape=[jax.ShapeDtypeStruct((T, H * HD), F32)] * 2
        + [jax.ShapeDtypeStruct((N_CHIPS,) + a.shape, a.dtype) for a in carry],
        compiler_params=_cp("arbitrary", "arbitrary") if n_carry else _cp("parallel", "arbitrary"),
    )(qt, kt, qn, kn, vb, f0, fkr, *carry)
    return out[0], out[1], list(out[2:])


def _flash_bwd(qn, kn, vb, f0, fkr, dao, lse, delta, H, name):
    T = qn.shape[0]
    tq = _tile(T, ATT_TILES)
    nq = T // tq
    qt, kt = _tri_pairs(nq, by_key=True)

    def body(qt_ref, kt_ref, q_ref, k_ref, v_ref, f0_ref, fk_ref, do_ref, lse_ref, dl_ref,
             dq_ref, dfq_ref, dk_ref, dv_ref, dfk_ref, dk_s, dv_s, dfk_s):
        qi, ki = qt_ref[pl.program_id(1)], kt_ref[pl.program_id(1)]

        @pl.when(pl.program_id(1) == 0)
        def _():
            dq_ref[...] = jnp.zeros_like(dq_ref)
            dfq_ref[...] = jnp.zeros_like(dfq_ref)

        @pl.when(qi == ki)
        def _():
            dk_s[...] = jnp.zeros_like(dk_s)
            dv_s[...] = jnp.zeros_like(dv_s)
            dfk_s[...] = jnp.zeros_like(dfk_s)

        def step(diagonal):
            s = _att_scores(q_ref[...], k_ref[...], f0_ref[...], fk_ref[...])
            p = jnp.exp(s - lse_ref[...][:, :1])
            if diagonal:
                p = jnp.where(_diag_keep(tq), p, 0.0)
            dp = _dot(do_ref[...], v_ref[...], 'nt')
            ds = p * (dp - dl_ref[...][:, :1])
            dv_s[...] += _bdot(p, do_ref[...], 'tn')
            dk_s[...] += _bdot(ds, q_ref[...], 'tn')
            dfk_s[...] -= jnp.sum(ds, axis=0, keepdims=True)
            rows = pl.ds(pl.multiple_of(qi * tq, tq), tq)
            dq_ref[rows, :] += _bdot(ds, k_ref[...])
            dfq_ref[rows, :] += jnp.broadcast_to(jnp.sum(ds, axis=1, keepdims=True), (tq, HD))

        @pl.when(qi > ki)
        def _():
            step(False)

        @pl.when(qi == ki)
        def _():
            step(True)

        @pl.when(qi == nq - 1)
        def _():
            dk_ref[...] = dk_s[...]
            dv_ref[...] = dv_s[...].astype(dv_ref.dtype)
            dfk_ref[...] = dfk_s[...]

    qspec, kspec, f0spec, fkspec = _att_specs(tq)
    head = pl.BlockSpec((T, HD), lambda h, p, qt, kt: (0, h))
    grid_spec = pltpu.PrefetchScalarGridSpec(
        num_scalar_prefetch=2, grid=(H, qt.shape[0]),
        in_specs=[qspec, kspec, kspec, f0spec, fkspec, qspec, qspec, qspec],
        out_specs=[head, head, kspec, kspec, fkspec],
        scratch_shapes=[pltpu.VMEM((tq, HD), F32), pltpu.VMEM((tq, HD), F32), pltpu.VMEM((1, tq), F32)])
    return pl.pallas_call(
        body, name=name, grid_spec=grid_spec,
        out_shape=[jax.ShapeDtypeStruct((T, H * HD), F32)] * 3
        + [jax.ShapeDtypeStruct((T, H * HD), BF16), jax.ShapeDtypeStruct((H, 1, T), F32)],
        compiler_params=_cp("parallel", "arbitrary"),
    )(qt, kt, qn, kn, vb, f0, fkr, dao, lse, delta)


CONV_TILES = (512, 256, 128, 64)
HALO = SUBLANES


def _dwconv(x, xoff, W, w, b, act, voff, out_dtype, name):
    T = x.shape[0]
    K = w.shape[0]
    tT, tC = _tile(T, CONV_TILES), _tile(W, CONV_TILES)
    xb, hb = xoff // tC, tT // HALO
    glu = act == 'glu'

    def body(*refs):
        if glu:
            x_ref, xp_ref, w_ref, b_ref, v_ref, o_ref, buf = refs
        else:
            x_ref, xp_ref, w_ref, o_ref, buf = refs
        i = pl.program_id(0)
        buf[0:HALO, :] = jnp.where(i > 0, xp_ref[...], 0.0)
        buf[HALO:, :] = x_ref[...]
        conv = w_ref[0:1, :] * buf[pl.ds(HALO - (K - 1), tT), :]
        for k in range(1, K):
            conv = conv + w_ref[k:k + 1, :] * buf[pl.ds(HALO - (K - 1) + k, tT), :]
        if glu:
            o_ref[...] = (_gelu(conv + b_ref[...]) * v_ref[...]).astype(o_ref.dtype)
        else:
            o_ref[...] = _silu(conv).astype(o_ref.dtype)

    cur = pl.BlockSpec((tT, tC), lambda i, j: (i, xb + j))
    prev = pl.BlockSpec((HALO, tC), lambda i, j: (jnp.maximum(i * hb - 1, 0), xb + j))
    wspec = pl.BlockSpec((K, tC), lambda i, j: (0, j))
    in_specs, args = [cur, prev, wspec], [x, x, w]
    if glu:
        vb = voff // tC
        in_specs += [pl.BlockSpec((1, tC), lambda i, j: (0, j)), pl.BlockSpec((tT, tC), lambda i, j: (i, vb + j))]
        args += [b, x]
    return pl.pallas_call(
        body, name=name, grid=(T // tT, W // tC), in_specs=in_specs,
        out_specs=pl.BlockSpec((tT, tC), lambda i, j: (i, j)), out_shape=jax.ShapeDtypeStruct((T, W), out_dtype),
        scratch_shapes=[pltpu.VMEM((tT + HALO, tC), F32)], compiler_params=_cp("parallel", "parallel"),
    )(*args)


def _dwconv_bwd(x, xoff, W, w, woff, b, act, voff, dy, name):
    T = x.shape[0]
    K = w.shape[0]
    tT, tC = _tile(T, CONV_TILES), _tile(W, CONV_TILES)
    xb, wb, hb, nT = xoff // tC, woff // tC, tT // HALO, T // tT
    last_halo = T // HALO - 1
    glu = act == 'glu'

    def body(*refs):
        if glu:
            (x_ref, xp_ref, xn_ref, dy_ref, dyn_ref, w_ref, b_ref, v_ref, vn_ref,
             dx_ref, dv_ref, dw_ref, db_ref, xbuf, dybuf, dbuf, vbuf) = refs
        else:
            x_ref, xp_ref, xn_ref, dy_ref, dyn_ref, w_ref, dx_ref, dw_ref, xbuf, dybuf, dbuf = refs
        i = pl.program_id(1)

        @pl.when(i == 0)
        def _():
            dw_ref[...] = jnp.zeros_like(dw_ref)
            if glu:
                db_ref[...] = jnp.zeros_like(db_ref)

        ext = tT + HALO
        xbuf[0:HALO, :] = jnp.where(i > 0, xp_ref[...], 0.0)
        xbuf[HALO:HALO + tT, :] = x_ref[...]
        xbuf[HALO + tT:, :] = xn_ref[...]
        dybuf[0:tT, :] = dy_ref[...].astype(F32)
        dybuf[tT:, :] = jnp.where(i < nT - 1, dyn_ref[...].astype(F32), 0.0)
        conv = w_ref[0:1, :] * xbuf[pl.ds(HALO - (K - 1), ext), :]
        for k in range(1, K):
            conv = conv + w_ref[k:k + 1, :] * xbuf[pl.ds(HALO - (K - 1) + k, ext), :]
        dyv = dybuf[...]
        if glu:
            vbuf[0:tT, :] = v_ref[...]
            vbuf[tT:, :] = vn_ref[...]
            z = conv + b_ref[...]
            cdf, pdf = _normal_cdf_pdf(z)
            dconv = dyv * vbuf[...] * (cdf + z * pdf)
            dv_ref[...] = (dyv[0:tT, :] * (z[0:tT, :] * cdf[0:tT, :])).astype(dv_ref.dtype)
        else:
            sg = _sigmoid(conv)
            dconv = dyv * (sg * (1.0 + conv * (1.0 - sg)))
        dbuf[...] = dconv
        dx = w_ref[0:1, :] * dbuf[pl.ds(K - 1, tT), :]
        for k in range(1, K):
            dx = dx + w_ref[k:k + 1, :] * dbuf[pl.ds(K - 1 - k, tT), :]
        dx_ref[...] = dx.astype(dx_ref.dtype)
        dc = dconv[0:tT, :]
        for k in range(K):
            dw_ref[k:k + 1, :] += jnp.sum(dc * xbuf[pl.ds(HALO - (K - 1) + k, tT), :], axis=0, keepdims=True)
        if glu:
            db_ref[...] += jnp.sum(dc, axis=0, keepdims=True)

    def cur(off):
        return pl.BlockSpec((tT, tC), lambda j, i: (i, off + j))

    def nxt(off):
        return pl.BlockSpec((HALO, tC), lambda j, i: (jnp.minimum((i + 1) * hb, last_halo), off + j))

    prev = pl.BlockSpec((HALO, tC), lambda j, i: (jnp.maximum(i * hb - 1, 0), xb + j))
    wspec = pl.BlockSpec((K, tC), lambda j, i: (0, wb + j))
    acc_w = pl.BlockSpec((K, tC), lambda j, i: (0, j))
    acc_b = pl.BlockSpec((1, tC), lambda j, i: (0, j))
    in_specs = [cur(xb), prev, nxt(xb), cur(0), nxt(0), wspec]
    args = [x, x, x, dy, dy, w]
    out_specs = [cur(0)]
    out_shape = [jax.ShapeDtypeStruct((T, W), BF16)]
    scratch = [pltpu.VMEM((tT + 2 * HALO, tC), F32), pltpu.VMEM((tT + HALO, tC), F32), pltpu.VMEM((tT + HALO, tC), F32)]
    if glu:
        vb = voff // tC
        in_specs += [pl.BlockSpec((1, tC), lambda j, i: (0, wb + j)), cur(vb), nxt(vb)]
        args += [b, x, x]
        out_specs += [cur(0), acc_w, acc_b]
        out_shape += [jax.ShapeDtypeStruct((T, W), BF16), jax.ShapeDtypeStruct((K, W), F32),
                      jax.ShapeDtypeStruct((1, W), F32)]
        scratch += [pltpu.VMEM((tT + HALO, tC), F32)]
    else:
        out_specs += [acc_w]
        out_shape += [jax.ShapeDtypeStruct((K, W), F32)]
    return pl.pallas_call(
        body, name=name, grid=(W // tC, nT), in_specs=in_specs, out_specs=out_specs, out_shape=out_shape,
        scratch_shapes=scratch, compiler_params=_cp("parallel", "arbitrary"),
    )(*args)


V_PER_K = 2


GDN_PREP_CHUNKS = 16
GDN_SCAN_CHUNKS = 4


def _b3(a, b, mode='nn', precision=None):
    c = {'nn': ((2,), (1,)), 'nt': ((2,), (2,)), 'tn': ((1,), (1,))}[mode]
    return lax.dot_general(a, b, (c, ((0,), (0,))), precision=precision, preferred_element_type=F32)


def _bb3(a, b, mode='nn'):
    return _b3(a.astype(BF16), b.astype(BF16), mode)


def _hb3(a, b, mode='nn'):
    return _b3(a, b, mode, precision=HIGHEST)


def _split_bf16(a):
    hi = a.astype(BF16)
    return hi, (a - hi.astype(F32)).astype(BF16)


def _nb3(a, b, mode='nn'):
    ah, al = _split_bf16(a)
    bh, bl = _split_bf16(b)
    return _b3(ah, bh, mode) + _b3(ah, bl, mode) + _b3(al, bh, mode)


def _to_batch(x, nc):
    C = GDN_CHUNK
    return jnp.concatenate([x[:, j * HD:(j + 1) * HD].reshape(nc, C, HD) for j in range(V_PER_K)], axis=0)


def _from_batch(x, nc):
    C = GDN_CHUNK
    return jnp.concatenate([x[j * nc:(j + 1) * nc].reshape(nc * C, HD) for j in range(V_PER_K)], axis=1)


def _both_heads(x, nc):
    xc = x.reshape(nc, GDN_CHUNK, HD)
    return jnp.concatenate([xc] * V_PER_K, axis=0)


def _gdn_local(q2, k2, gb, bb):
    B, C, _ = k2.shape
    ri = lax.broadcasted_iota(jnp.int32, (B, C, C), 1)
    ci = lax.broadcasted_iota(jnp.int32, (B, C, C), 2)
    lower, strict = ri >= ci, ri > ci
    pick0 = (lax.broadcasted_iota(jnp.int32, (B, C, HD), 2) == 0).astype(F32)
    g_cols = _hb3(pick0, gb, 'nt')
    dm = jnp.exp(jnp.where(lower, gb[:, :, :C] - g_cols, NEG))
    kk = _bb3(k2, k2, 'nt')
    a = jnp.where(strict, kk * dm * bb[:, :, :C], 0.0)
    eg = jnp.exp(gb)
    gl = gb[:, C - 1:C, :]
    return dict(lower=lower, strict=strict, eye=(ri == ci).astype(F32), dm=dm, kk=kk, a=a, eg=eg, gl=gl,
                qd=q2 * eg, kd=k2 * jnp.exp(gl - gb))


def _gdn_specs(T, Hk, voff, nc, rev=False):
    C = GDN_CHUNK
    nb = T // (nc * C)
    vb = voff // (V_PER_K * HD)
    ix = (lambda i: nb - 1 - i) if rev else (lambda i: i)
    kspec = pl.BlockSpec((nc * C, HD), lambda h, i: (ix(i), h))
    pair = pl.BlockSpec((nc * C, V_PER_K * HD), lambda h, i: (ix(i), h))
    vspec = pl.BlockSpec((nc * C, V_PER_K * HD), lambda h, i: (ix(i), vb + h))
    cc = pl.BlockSpec((V_PER_K, nc, C, C), lambda h, i: (h, ix(i), 0, 0))
    state = pl.BlockSpec((V_PER_K, nc, HD, HD), lambda h, i: (h, ix(i), 0, 0))
    scal = pl.BlockSpec((V_PER_K, nc, SUBLANES, HD), lambda h, i: (h, ix(i), 0, 0))
    return nb, kspec, pair, vspec, cc, state, scal


def _gdn_prep(qn, kn, qkvc, voff, gcb, betab, Hk, name):
    T = qn.shape[0]
    C, nc = GDN_CHUNK, GDN_PREP_CHUNKS
    Hv, N = Hk * V_PER_K, T // C
    nb, kspec, pair, vspec, cc, _, _ = _gdn_specs(T, Hk, voff, nc)

    def body(q_ref, k_ref, v_ref, g_ref, b_ref, u_ref, w_ref, tm_ref, qkm_ref):
        q2, k2 = _both_heads(q_ref[...], nc), _both_heads(k_ref[...], nc)
        v2, gb, bb = _to_batch(v_ref[...], nc), _to_batch(g_ref[...], nc), _to_batch(b_ref[...], nc)
        lc = _gdn_local(q2, k2, gb, bb)
        p = -lc['a']
        tm = lc['eye'] + p
        for _ in range(5):
            p = _nb3(p, p)
            tm = tm + _nb3(tm, p)
        u_ref[...] = _from_batch(_nb3(tm, v2 * bb), nc)
        w_ref[...] = _from_batch(_nb3(tm, k2 * (bb * lc['eg'])), nc)
        tm_ref[...] = tm.reshape(V_PER_K, nc, C, C)
        qkm_ref[...] = jnp.where(lc['lower'], _bb3(q2, k2, 'nt') * lc['dm'], 0.0).reshape(V_PER_K, nc, C, C)

    return pl.pallas_call(
        body, name=name, grid=(Hk, nb), in_specs=[kspec, kspec, vspec, pair, pair], out_specs=[pair, pair, cc, cc],
        out_shape=[jax.ShapeDtypeStruct((T, Hv * HD), F32)] * 2 + [jax.ShapeDtypeStruct((Hv, N, C, C), F32)] * 2,
        compiler_params=_cp("parallel", "parallel"),
    )(qn, kn, qkvc, gcb, betab)


SCAN_K_HEADS = 4
SCAN_V_HEADS = SCAN_K_HEADS * V_PER_K


def _heads(ref, rows, per=1):
    return jnp.stack([ref[rows, (j // per) * HD:(j // per + 1) * HD] for j in range(SCAN_V_HEADS)])


def _put_heads(ref, rows, x):
    for j in range(SCAN_V_HEADS):
        ref[rows, j * HD:(j + 1) * HD] = x[j]


def _scan_chunk(q_ref, k_ref, g_ref, rows):
    C = GDN_CHUNK
    gb = _heads(g_ref, rows)
    gl = gb[:, C - 1:C, :]
    return _heads(q_ref, rows, V_PER_K) * jnp.exp(gb), _heads(k_ref, rows, V_PER_K) * jnp.exp(gl - gb), jnp.exp(gl)


def _scan_specs(T, ns, rev=False):
    C = GDN_CHUNK
    nb = T // (ns * C)
    ix = (lambda i: nb - 1 - i) if rev else (lambda i: i)
    kspec = pl.BlockSpec((ns * C, SCAN_K_HEADS * HD), lambda h, i: (ix(i), h))
    vspec = pl.BlockSpec((ns * C, SCAN_V_HEADS * HD), lambda h, i: (ix(i), h))
    per_chunk = lambda *tail: pl.BlockSpec((SCAN_V_HEADS, ns) + tail, lambda h, i: (h, ix(i), 0, 0))
    return nb, kspec, vspec, per_chunk(C, C), per_chunk(HD, HD), per_chunk(SUBLANES, HD)


def _gdn_scan(qn, kn, gcb, u, w, qkm, Hk, name):
    T = qn.shape[0]
    C, ns = GDN_CHUNK, GDN_SCAN_CHUNKS
    Hv, N = Hk * V_PER_K, T // C
    nb, kspec, pair, cc, state, _ = _scan_specs(T, ns)

    def body(q_ref, k_ref, g_ref, u_ref, w_ref, qkm_ref, o_ref, sp_ref, s_s):
        @pl.when(pl.program_id(1) == 0)
        def _():
            s_s[...] = jnp.zeros_like(s_s)

        s = s_s[...]
        for t in range(ns):
            rows = slice(t * C, (t + 1) * C)
            qd, kd, egl = _scan_chunk(q_ref, k_ref, g_ref, rows)
            sp_ref[:, t] = s
            vn = _heads(u_ref, rows) - _bb3(_heads(w_ref, rows), s)
            _put_heads(o_ref, rows, _bb3(qd, s) + _bb3(qkm_ref[:, t], vn))
            s = s * egl + _bb3(kd, vn, 'tn')
        s_s[...] = s

    return pl.pallas_call(
        body, name=name, grid=(Hk // SCAN_K_HEADS, nb), in_specs=[kspec, kspec, pair, pair, pair, cc],
        out_specs=[pair, state],
        out_shape=[jax.ShapeDtypeStruct((T, Hv * HD), F32), jax.ShapeDtypeStruct((Hv, N, HD, HD), F32)],
        scratch_shapes=[pltpu.VMEM((SCAN_V_HEADS, HD, HD), F32)], compiler_params=_cp("parallel", "arbitrary"),
    )(qn, kn, gcb, u, w, qkm)


def _gdn_scan_bwd(qn, kn, gcb, u, w, qkm, sprev, do, Hk, name):
    T = qn.shape[0]
    C, ns = GDN_CHUNK, GDN_SCAN_CHUNKS
    Hv, N = Hk * V_PER_K, T // C
    nb, kspec, pair, cc, state, scal = _scan_specs(T, ns, rev=True)
    nv = SCAN_V_HEADS

    def body(q_ref, k_ref, g_ref, u_ref, w_ref, qkm_ref, sp_ref, do_ref,
             dqd_ref, dkd_ref, du_ref, dw_ref, dqkm_ref, dgl_ref, ds_s):
        @pl.when(pl.program_id(1) == 0)
        def _():
            ds_s[...] = jnp.zeros_like(ds_s)

        lower = lax.broadcasted_iota(jnp.int32, (nv, C, C), 1) >= lax.broadcasted_iota(jnp.int32, (nv, C, C), 2)
        ds = ds_s[...]
        for t in reversed(range(ns)):
            rows = slice(t * C, (t + 1) * C)
            qd, kd, egl = _scan_chunk(q_ref, k_ref, g_ref, rows)
            s, w_, qkm_, dov = sp_ref[:, t], _heads(w_ref, rows), qkm_ref[:, t], _heads(do_ref, rows)
            vn = _heads(u_ref, rows) - _bb3(w_, s)
            _put_heads(dqd_ref, rows, _bb3(dov, s, 'nt'))
            dqkm_ref[:, t] = jnp.where(lower, _bb3(dov, vn, 'nt'), 0.0)
            dvn = _bb3(qkm_, dov, 'tn') + _bb3(kd, ds)
            _put_heads(dkd_ref, rows, _bb3(vn, ds, 'nt'))
            dgl = jnp.sum(jnp.sum(ds * s, axis=1, keepdims=True) * egl, axis=2, keepdims=True)
            dgl_ref[:, t] = jnp.broadcast_to(dgl, (nv, SUBLANES, HD))
            _put_heads(du_ref, rows, dvn)
            _put_heads(dw_ref, rows, -_bb3(dvn, s, 'nt'))
            ds = ds * egl + _bb3(qd, dov, 'tn') - _bb3(w_, dvn, 'tn')
        ds_s[...] = ds

    return pl.pallas_call(
        body, name=name, grid=(Hk // SCAN_K_HEADS, nb), in_specs=[kspec, kspec, pair, pair, pair, cc, state, pair],
        out_specs=[pair] * 4 + [cc, scal],
        out_shape=[jax.ShapeDtypeStruct((T, Hv * HD), F32)] * 4
        + [jax.ShapeDtypeStruct((Hv, N, C, C), F32), jax.ShapeDtypeStruct((Hv, N, SUBLANES, HD), F32)],
        scratch_shapes=[pltpu.VMEM((nv, HD, HD), F32)], compiler_params=_cp("parallel", "arbitrary"),
    )(qn, kn, gcb, u, w, qkm, sprev, do)


def _gdn_prep_bwd(qn, kn, qkvc, voff, gcb, betab, tm, u, w, qkm, dqd, dkd, du, dw, dqkm, dgl, Hk, name):
    T = qn.shape[0]
    C, nc = GDN_CHUNK, GDN_PREP_CHUNKS
    Hv = Hk * V_PER_K
    B = V_PER_K * nc
    nb, kspec, pair, vspec, cc, _, scal = _gdn_specs(T, Hk, voff, nc)

    def body(q_ref, k_ref, v_ref, g_ref, b_ref, tm_ref, u_ref, w_ref, qkm_ref, dqd_ref, dkd_ref, du_ref, dw_ref,
             dqkm_ref, dgl_ref, dq_ref, dk_ref, dv_ref, dg_ref, dbe_ref):
        q2, k2 = _both_heads(q_ref[...], nc), _both_heads(k_ref[...], nc)
        v2, gb, bb = _to_batch(v_ref[...], nc), _to_batch(g_ref[...], nc), _to_batch(b_ref[...], nc)
        lc = _gdn_local(q2, k2, gb, bb)
        dm, eg, gl = lc['dm'], lc['eg'], lc['gl']
        tm_, qkm_, dqkm_ = (r[...].reshape(B, C, C) for r in (tm_ref, qkm_ref, dqkm_ref))
        u_, w_, dqd_, dkd_, du_, dw_ = (_to_batch(r[...], nc) for r in (u_ref, w_ref, dqd_ref, dkd_ref, du_ref, dw_ref))
        dgl_ = dgl_ref[...].reshape(B, SUBLANES, HD)[:, :1, :1]
        rowsum = lambda x: jnp.sum(x, axis=-1, keepdims=True)
        dbv = _nb3(tm_, du_, 'tn')
        dbk = _nb3(tm_, dw_, 'tn')
        da = jnp.where(lc['strict'], -(_bb3(dbv, u_, 'nt') + _bb3(dbk, w_, 'nt')), 0.0)
        rk = rowsum(dbk * k2)
        dbeta = rowsum(dbv * v2) + rk * eg[:, :, :1] + rowsum(da * lc['kk'] * dm)
        dkk = da * dm * bb[:, :, :C]
        dqkr = dqkm_ * dm
        dk = dbk * (bb * eg) + _bb3(dkk, k2) + _bb3(dkk, k2, 'tn') + _bb3(dqkr, q2, 'tn') + dkd_ * jnp.exp(gl - gb)
        dq = _bb3(dqkr, k2) + dqd_ * eg
        de = da * lc['a'] + dqkm_ * qkm_
        sk = rowsum(dkd_ * lc['kd'])
        dg = rk * (bb[:, :, :1] * eg[:, :, :1]) + rowsum(de) - _hb3(de, jnp.ones((B, C, HD), F32), 'tn')[:, :, :1] \
            + rowsum(dqd_ * lc['qd']) - sk
        last = (lax.broadcasted_iota(jnp.int32, (B, C, HD), 1) == C - 1).astype(F32)
        dgb = jnp.broadcast_to(dg, (B, C, HD)) + last * (dgl_ + jnp.sum(sk, axis=1, keepdims=True))
        suffix = (lax.broadcasted_iota(jnp.int32, (B, C, C), 2) >= lax.broadcasted_iota(jnp.int32, (B, C, C), 1)).astype(F32)
        dq_ref[...] = _from_batch(dq, nc)
        dk_ref[...] = _from_batch(dk, nc)
        dv_ref[...] = _from_batch(dbv * bb, nc)
        dg_ref[...] = _from_batch(_hb3(suffix, dgb), nc)
        dbe_ref[...] = _from_batch(jnp.broadcast_to(dbeta, (B, C, HD)), nc)

    return pl.pallas_call(
        body, name=name, grid=(Hk, nb),
        in_specs=[kspec, kspec, vspec, pair, pair, cc, pair, pair, cc, pair, pair, pair, pair, cc, scal],
        out_specs=[pair] * 5, out_shape=[jax.ShapeDtypeStruct((T, Hv * HD), F32)] * 5,
        compiler_params=_cp("parallel", "parallel"),
    )(qn, kn, qkvc, gcb, betab, tm, u, w, qkm, dqd, dkd, du, dw, dqkm, dgl)


def _adamw_math(w, g, m, v):
    m = ADAM_B1 * m + (1.0 - ADAM_B1) * g
    v = ADAM_B2 * v + (1.0 - ADAM_B2) * jnp.square(g)
    m_hat = m / (1.0 - ADAM_B1 ** ADAM_STEP)
    v_hat = v / (1.0 - ADAM_B2 ** ADAM_STEP)
    delta = -ADAM_LR * (m_hat / (jnp.sqrt(v_hat) + ADAM_EPS) + ADAM_WD * w)
    return delta, m, v


STREAM_BLOCK_BYTES = 2 << 20


def _stream_rows(R, C, mult=SUBLANES):
    for tr in (512, 256, 128, 64, 32, 16, 8):
        if R % tr == 0 and tr % mult == 0 and tr * C * 4 <= STREAM_BLOCK_BYTES:
            return tr
    return R


def _adamw(w, m, v, gs, name):
    R, C = w.shape
    tr = _stream_rows(R, C)
    n = len(gs)

    def body(*refs):
        w_ref, m_ref, v_ref = refs[:3]
        g_refs = refs[3:3 + n]
        g_out, d_out, m_out, v_out = refs[3 + n:]
        g = g_refs[0][...]
        for r_ in g_refs[1:]:
            g = g + r_[...]
        g_out[...] = g
        d_out[...], m_out[...], v_out[...] = _adamw_math(w_ref[...], g, m_ref[...], v_ref[...])

    spec = pl.BlockSpec((tr, C), lambda i: (i, 0))
    return pl.pallas_call(
        body, name=name, grid=(R // tr,), in_specs=[spec] * (3 + n), out_specs=[spec] * 4,
        out_shape=[jax.ShapeDtypeStruct((R, C), F32)] * 4, compiler_params=_cp("parallel"),
    )(w, m, v, *gs)


def _sum_devices(g8, name):
    _, M, C = g8.shape
    tr = _tile(M, (512, 256, 128, 64, 32, 16, 8))

    def body(g_ref, o_ref):
        acc = g_ref[0]
        for d in range(1, N_DEV):
            acc = acc + g_ref[d]
        o_ref[...] = acc

    return pl.pallas_call(
        body, name=name, grid=(M // tr,), in_specs=[pl.BlockSpec((N_DEV, tr, C), lambda i: (0, i, 0))],
        out_specs=pl.BlockSpec((tr, C), lambda i: (i, 0)), out_shape=jax.ShapeDtypeStruct((M, C), F32),
        compiler_params=_cp("parallel"),
    )(g8)


def _ada_w_update(c_all, dm, w, m, v, name):
    n_mod, D, Ns = w.shape
    tr = _tile(D, (256, 128))

    def body(c_ref, dm_ref, w_ref, m_ref, v_ref, g_out, d_out, m_out, v_out):
        g = _hdot(_silu(c_ref[...]), dm_ref[...], 'tn')
        g_out[...] = g
        d_out[...], m_out[...], v_out[...] = _adamw_math(w_ref[...], g, m_ref[...], v_ref[...])

    wspec = pl.BlockSpec((None, tr, Ns), lambda i, r: (i, r, 0))
    return pl.pallas_call(
        body, name=name, grid=(n_mod, D // tr),
        in_specs=[pl.BlockSpec((N_DEV, tr), lambda i, r: (0, r)), pl.BlockSpec((None, N_DEV, Ns), lambda i, r: (i, 0, 0)),
                  wspec, wspec, wspec],
        out_specs=[wspec] * 4, out_shape=[jax.ShapeDtypeStruct((n_mod, D, Ns), F32)] * 4,
        compiler_params=_cp("parallel", "parallel"),
    )(c_all, dm, w, m, v)


def _place():
    return lax.axis_index("x"), lax.axis_index("y"), lax.axis_index("c")


def _allgather8(x_shard, name):
    m_per, n = x_shard.shape

    def body(x_ref, out_ref, send_sems, recv_sems, local_sem):
        x, y, c = _place()
        me, sibling = (x, y, c), (x, y, 1 - c)
        chips = [(1 - x, y), (x, 1 - y), (1 - x, 1 - y)]

        def rows(px, py, pc):
            return out_ref.at[pl.ds((4 * px + 2 * py + pc) * m_per, m_per), :]

        def copy(k, block, to, src=None):
            return pltpu.make_async_remote_copy(
                src_ref=rows(*block) if src is None else src, dst_ref=rows(*block),
                send_sem=send_sems.at[k], recv_sem=recv_sems.at[k], device_id=to, device_id_type=MESH)

        mine = pltpu.make_async_copy(x_ref, rows(*me), local_sem)
        mine.start()
        first = [copy(0, me, sibling, src=x_ref)]
        first += [copy(1 + j, me, (*chip, c), src=x_ref) for j, chip in enumerate(chips)]
        for cp in first:
            cp.start()
        passed = [copy(4 + j, (*chip, c), sibling) for j, chip in enumerate(chips)]
        for j, chip in enumerate(chips):
            copy(1 + j, (*chip, c), me).wait_recv()
            passed[j].start()
        copy(0, sibling, me).wait_recv()
        for j, chip in enumerate(chips):
            copy(4 + j, (*chip, 1 - c), me).wait_recv()
        for cp in first + passed:
            cp.wait_send()
        mine.wait()

    return pl.pallas_call(
        body, name=name, out_shape=jax.ShapeDtypeStruct((N_DEV * m_per, n), x_shard.dtype),
        in_specs=[pl.BlockSpec(memory_space=pltpu.VMEM)], out_specs=pl.BlockSpec(memory_space=pltpu.VMEM),
        scratch_shapes=[pltpu.SemaphoreType.DMA((7,)), pltpu.SemaphoreType.DMA((7,)), pltpu.SemaphoreType.DMA],
        compiler_params=pltpu.CompilerParams(vmem_limit_bytes=VMEM_LIMIT),
    )(x_shard)


HBM_SPEC = pl.BlockSpec(memory_space=pltpu.HBM)


GATHER_SEMS = 6


def _gather_protocol(w_ref, out_ref, send_sems, recv_sems, k0):
    half = w_ref.shape[0] // 2
    x, y, c = _place()
    me, sibling = (x, y, c), (x, y, 1 - c)
    chips = [(1 - x, y), (x, 1 - y), (1 - x, 1 - y)]

    def part(cx, cy, hc):
        return out_ref.at[2 * cx + cy, pl.ds(hc * half, half), :]

    def copy(k, block, to, src=None):
        return pltpu.make_async_remote_copy(
            src_ref=part(*block) if src is None else src, dst_ref=part(*block),
            send_sem=send_sems.at[k0 + k], recv_sem=recv_sems.at[k0 + k], device_id=to, device_id_type=MESH)

    def first():
        return [copy(j, me, (*chip, c), src=w_ref.at[pl.ds(c * half, half), :]) for j, chip in enumerate(chips)]

    def start():
        for cp in first():
            cp.start()

    def finish():
        passed = [copy(3 + j, (*chip, c), sibling) for j, chip in enumerate(chips)]
        for j, chip in enumerate(chips):
            copy(j, (*chip, c), me).wait_recv()
            passed[j].start()
        for j, chip in enumerate(chips):
            copy(3 + j, (*chip, 1 - c), me).wait_recv()
        for cp in first() + passed:
            cp.wait_send()

    return start, finish


def _gather_weights(w_flat, name):
    R, C = w_flat.shape

    def body(w_ref, out_ref, send_sems, recv_sems):
        start, finish = _gather_protocol(w_ref, out_ref, send_sems, recv_sems, 0)
        start()
        finish()

    return pl.pallas_call(
        body, name=name, out_shape=jax.ShapeDtypeStruct((N_CHIPS, R, C), w_flat.dtype),
        in_specs=[HBM_SPEC], out_specs=HBM_SPEC,
        scratch_shapes=[pltpu.SemaphoreType.DMA((GATHER_SEMS,)), pltpu.SemaphoreType.DMA((GATHER_SEMS,))],
    )(w_flat)


def _swap_halves(g, name):
    n, R, C = g.shape
    half = R // 2

    def body(g_ref, got_ref, send_sem, recv_sem):
        x, y, c = _place()
        cp = pltpu.make_async_remote_copy(
            src_ref=g_ref.at[:, pl.ds((1 - c) * half, half), :], dst_ref=got_ref,
            send_sem=send_sem, recv_sem=recv_sem, device_id=(x, y, 1 - c), device_id_type=MESH)
        cp.start()
        cp.wait()

    return pl.pallas_call(
        body, name=name, out_shape=jax.ShapeDtypeStruct((n, half, C), g.dtype),
        in_specs=[HBM_SPEC], out_specs=HBM_SPEC,
        scratch_shapes=[pltpu.SemaphoreType.DMA, pltpu.SemaphoreType.DMA],
    )(g)


SCATTER_SEMS = 3


def _scatter_protocol(q_ref, got_ref, send_sems, recv_sems, k0):
    x, y, c = _place()

    def copies():
        return [pltpu.make_async_remote_copy(
            src_ref=q_ref.at[2 * cx + cy], dst_ref=got_ref.at[j], send_sem=send_sems.at[k0 + j],
            recv_sem=recv_sems.at[k0 + j], device_id=(cx, cy, c), device_id_type=MESH)
            for j, (cx, cy) in enumerate([(1 - x, y), (x, 1 - y), (1 - x, 1 - y)])]

    def start():
        for cp in copies():
            cp.start()

    def finish():
        for cp in copies():
            cp.wait()

    return start, finish


def _scatter_shapes(qs):
    return [jax.ShapeDtypeStruct((N_CHIPS - 1,) + q.shape[1:], q.dtype) for q in qs]


def _join_halves(h, name):
    R, C = h.shape
    R2 = R // 2

    def body(h_ref, out_ref, send_sem, recv_sem):
        x, y, c = _place()
        cp = pltpu.make_async_remote_copy(
            src_ref=h_ref.at[pl.ds(c * R2, R2), :], dst_ref=out_ref.at[pl.ds(c * R2, R2), :],
            send_sem=send_sem, recv_sem=recv_sem, device_id=(x, y, 1 - c), device_id_type=MESH)
        cp.start()
        cp.wait()

    return pl.pallas_call(
        body, name=name, out_shape=jax.ShapeDtypeStruct((R, C), h.dtype),
        in_specs=[HBM_SPEC], out_specs=HBM_SPEC, input_output_aliases={0: 0},
        scratch_shapes=[pltpu.SemaphoreType.DMA, pltpu.SemaphoreType.DMA],
    )(h)


def _add_halves(g, got, c_idx, name):
    n, R, C = g.shape
    half = R // 2
    tr = _stream_rows(half, C, 2 * SUBLANES)
    nb = half // tr

    def body(c_ref, g_ref, got_ref, o_ref):
        o_ref[...] = (g_ref[...] + got_ref[...]).astype(o_ref.dtype)

    grid_spec = pltpu.PrefetchScalarGridSpec(
        num_scalar_prefetch=1, grid=(n, nb),
        in_specs=[pl.BlockSpec((None, tr, C), lambda s, i, c_ref: (s, c_ref[0] * nb + i, 0)),
                  pl.BlockSpec((None, tr, C), lambda s, i, c_ref: (s, i, 0))],
        out_specs=pl.BlockSpec((None, tr, C), lambda s, i, c_ref: (s, i, 0)))
    return pl.pallas_call(
        body, name=name, grid_spec=grid_spec, out_shape=jax.ShapeDtypeStruct((n, half, C), BF16),
        compiler_params=_cp("parallel", "parallel"),
    )(c_idx, g, got)


def _add_chips(q, got, sc_idx, name):
    n, R2, C = q.shape
    tr = _stream_rows(R2, C, 2 * SUBLANES)
    nb = R2 // tr

    def body(s_ref, q_ref, g0_ref, g1_ref, g2_ref, o_ref):
        o_ref[...] = ((q_ref[...].astype(F32) + g0_ref[...].astype(F32)) + g1_ref[...].astype(F32)) \
            + g2_ref[...].astype(F32)

    def got_spec(j):
        return pl.BlockSpec((None, tr, C), lambda i, s_ref: (j, i, 0))

    grid_spec = pltpu.PrefetchScalarGridSpec(
        num_scalar_prefetch=1, grid=(nb,),
        in_specs=[pl.BlockSpec((None, tr, C), lambda i, s_ref: (s_ref[0], i, 0)), got_spec(0), got_spec(1), got_spec(2)],
        out_specs=pl.BlockSpec((tr, C), lambda i, s_ref: (s_ref[1] * nb + i, 0)))
    return pl.pallas_call(
        body, name=name, grid_spec=grid_spec, out_shape=jax.ShapeDtypeStruct((2 * R2, C), F32),
        compiler_params=_cp("parallel"),
    )(sc_idx, q, got, got, got)


def _pack_lanes(arrs):
    rows = []
    for a in arrs:
        f = a.reshape(-1)
        n = -(-f.shape[0] // LANES) * LANES
        rows.append(jnp.pad(f, (0, n - f.shape[0])).reshape(-1, LANES))
    out = jnp.concatenate(rows, axis=0)
    pad = -out.shape[0] % SUBLANES
    return jnp.pad(out, ((0, pad), (0, 0)))


def _unpack_lanes(packed, shapes):
    out, r = [], 0
    for shp in shapes:
        n = math.prod(shp)
        nr = -(-n // LANES)
        out.append(packed[r:r + nr].reshape(-1)[:n].reshape(shp))
        r += nr
    return out


def _shards_to_full(sh, axis):
    return jnp.concatenate([sh[i] for i in range(N_CHIPS)], axis=axis)


def _full_to_shards(full, axis):
    return jnp.stack(jnp.split(full, N_CHIPS, axis=axis), axis=0)


def _pad_cols(a, n):
    return jnp.pad(a, ((0, 0), (0, n - a.shape[1])))


def _split_mod(mod):
    D = mod.shape[0] // 3
    return mod[None, :D], mod[None, D:2 * D], mod[None, 2 * D:]


def _fox_fwd(h, w, tag, late=None):
    T, D = h.shape
    H = D // HD
    proj = _mm(h, w['cat'], 'nn', F32, tag + '_proj')
    flog = _mm(h, w['f'], 'nn', F32, tag + '_flog')
    qn = _headnorm(proj, 0, H, w['q_norm'], 1.0 / HD, HD ** -0.5, BF16, tag + '_qnorm')
    kn = _headnorm(proj, H, H, w['k_norm'], 1.0 / HD, 1.0, BF16, tag + '_knorm')
    vb = proj[:, 2 * D:3 * D].astype(BF16)
    fcum = _cumsum(_logsig(flog, w['f_bias'], tag + '_logf'), None, False, tag + '_fcum')
    tq = _tile(T, ATT_TILES)
    f0 = fcum[::tq, :H].T.reshape(H, T // tq, 1, 1)
    fkr = fcum[:, :H].T.reshape(H, 1, T)
    ao, lse, carried = _flash_fwd(qn, kn, vb, f0, fkr, H, tag + '_att', late[0] if late else ())
    if late:
        w['o'], rest = late[1](carried)
    gated = _fox_gate(ao, proj, 3 * H, H, tag + '_ogate')
    y = _mm(gated, w['o'], 'nn', F32, tag + '_out')
    sv = dict(h=h, proj=proj, flog=flog, qn=qn, kn=kn, vb=vb, f0=f0, fkr=fkr, ao=ao, lse=lse, gated=gated)
    return (y, sv, rest) if late else (y, sv)


def _mm_carry(a, b, mode, out_dtype, name, reduce_pairs, tag, grads):
    if reduce_pairs is None:
        return _mm(a, b, mode, out_dtype, name), {}
    pairs = reduce_pairs(tag, grads)
    out, got = _mm(a, b, mode, out_dtype, name, scatter=tuple(q for _, q in pairs))
    return out, {key: (q, r) for (key, q), r in zip(pairs, got)}


def _fox_bwd(dy, w, sv, tag, reduce_pairs=None):
    h, proj = sv['h'], sv['proj']
    T, D = h.shape
    H = D // HD
    g = {}
    g['o'] = _mm(sv['gated'], dy, 'tn', F32, tag + '_dwo')
    dgated = _mm(dy, w['o'], 'nt', F32, tag + '_dgated')
    dao, dog, delta = _fox_gate_bwd(dgated, sv['ao'], proj, 3 * H, H, tag + '_ogate_bwd')
    dq, dfq, dk, dv, dfk = _flash_bwd(sv['qn'], sv['kn'], sv['vb'], sv['f0'], sv['fkr'], dao, sv['lse'], delta, H,
                                      tag + '_att_bwd')
    dfcum = _pad_cols(dfq[:, ::HD] + dfk.reshape(H, T).T, LANES)
    dlogf = _cumsum(dfcum, None, True, tag + '_fcum_bwd')
    dflog, g['f_bias'] = _logsig_bwd(dlogf, sv['flog'], w['f_bias'], tag + '_logf_bwd')
    dqr, g['q_norm'] = _headnorm_bwd([(dq, 1, 0)], proj, 0, H, w['q_norm'], 1.0 / HD, HD ** -0.5, BF16,
                                     tag + '_qnorm_bwd')
    dkr, g['k_norm'] = _headnorm_bwd([(dk, 1, 0)], proj, H, H, w['k_norm'], 1.0 / HD, 1.0, BF16, tag + '_knorm_bwd')
    dproj = jnp.concatenate([dqr, dkr, dv, dog], axis=1)
    g['cat'] = _mm(h, dproj, 'tn', F32, tag + '_dwcat')
    g['f'] = _mm(h, dflog, 'tn', F32, tag + '_dwf')
    dh, exchanged = _mm_carry(dproj, w['cat'], 'nt', F32, tag + '_dh', reduce_pairs, tag, g)
    return [dh, _mm(dflog, w['f'], 'nt', F32, tag + '_dh_f')], g, exchanged


def _gdn_fwd(h, w, tag):
    T, D = h.shape
    Hk = D // HD
    Hv = V_PER_K * Hk
    proj = _mm(h, w['cat'], 'nn', F32, tag + '_proj')
    ab = _mm(h, w['ab'], 'nn', F32, tag + '_ab')
    qkvc = _dwconv(proj, 0, 4 * D, w['conv'], None, 'silu', 0, F32, tag + '_conv')
    qn = _headnorm(qkvc, 0, Hk, None, 1.0, HD ** -0.5, F32, tag + '_qnorm')
    kn = _headnorm(qkvc, Hk, Hk, None, 1.0, 1.0, F32, tag + '_knorm')
    graw, beta = _gdn_gates(ab, w['a_log'], w['dt_bias'], tag + '_gates')
    gc = _cumsum(graw, GDN_CHUNK, False, tag + '_gcum')
    gcb, betab = _spread_heads([gc, beta], [0, Hv], Hv, tag + '_spread')
    u, wk, tm, qkm = _gdn_prep(qn, kn, qkvc, 2 * D, gcb, betab, Hk, tag + '_prep')
    o, sprev = _gdn_scan(qn, kn, gcb, u, wk, qkm, Hk, tag + '_scan')
    go = _gdn_out(o, proj, 4 * Hk, Hv, w['out_norm'], tag + '_onorm')
    y = _mm(go, w['o'], 'nn', F32, tag + '_out')
    return y, dict(h=h, proj=proj, ab=ab, qkvc=qkvc, qn=qn, kn=kn, gcb=gcb, betab=betab, o=o, sprev=sprev, go=go,
                   u=u, wk=wk, tm=tm, qkm=qkm)


def _gdn_bwd(dy, w, sv, tag, reduce_pairs=None):
    h, proj, qkvc = sv['h'], sv['proj'], sv['qkvc']
    T, D = h.shape
    Hk = D // HD
    Hv = V_PER_K * Hk
    g = {}
    g['o'] = _mm(sv['go'], dy, 'tn', F32, tag + '_dwo')
    dgo = _mm(dy, w['o'], 'nt', F32, tag + '_dgo')
    do, dz, g['out_norm'] = _gdn_out_bwd(dgo, sv['o'], proj, 4 * Hk, Hv, w['out_norm'], tag + '_onorm_bwd')
    local = (sv['u'], sv['wk'], sv['qkm'])
    dqd, dkd, du, dwk, dqkm, dgl = _gdn_scan_bwd(sv['qn'], sv['kn'], sv['gcb'], *local, sv['sprev'], do, Hk,
                                                 tag + '_scan_bwd')
    dqp, dkp, dv, dgb, dbetab = _gdn_prep_bwd(sv['qn'], sv['kn'], qkvc, 2 * D, sv['gcb'], sv['betab'], sv['tm'], *local,
                                              dqd, dkd, du, dwk, dqkm, dgl, Hk, tag + '_prep_bwd')
    pairs = lambda a: [(a, V_PER_K, j) for j in range(V_PER_K)]
    dqc, _ = _headnorm_bwd(pairs(dqp), qkvc, 0, Hk, None, 1.0, HD ** -0.5, F32, tag + '_qnorm_bwd')
    dkc, _ = _headnorm_bwd(pairs(dkp), qkvc, Hk, Hk, None, 1.0, 1.0, F32, tag + '_knorm_bwd')
    zeros = jnp.zeros((T, Hv), F32)
    dg_pad = _pad_cols(dgb[:, ::HD], LANES)
    dbeta_pad = _pad_cols(jnp.concatenate([zeros, dbetab[:, ::HD]], axis=1), LANES)
    dab, g['a_log'], g['dt_bias'] = _gdn_gates_bwd(dg_pad, dbeta_pad, sv['ab'], w['a_log'], w['dt_bias'], tag + '_gates_bwd')
    dpq, dwq = _dwconv_bwd(proj, 0, D, w['conv'], 0, None, 'silu', 0, dqc, tag + '_conv_bwd_q')
    dpk, dwk = _dwconv_bwd(proj, D, D, w['conv'], D, None, 'silu', 0, dkc, tag + '_conv_bwd_k')
    dpv, dwv = _dwconv_bwd(proj, 2 * D, 2 * D, w['conv'], 2 * D, None, 'silu', 0, dv, tag + '_conv_bwd_v')
    g['conv'] = jnp.concatenate([dwq, dwk, dwv], axis=1)
    dproj = jnp.concatenate([dpq, dpk, dpv, dz], axis=1)
    g['cat'] = _mm(h, dproj, 'tn', F32, tag + '_dwcat')
    g['ab'] = _mm(h, dab, 'tn', F32, tag + '_dwab')
    dh, exchanged = _mm_carry(dproj, w['cat'], 'nt', F32, tag + '_dh', reduce_pairs, tag, g)
    return [dh, _mm(dab, w['ab'], 'nt', F32, tag + '_dh_ab')], g, exchanged


def _ffn_fwd(h, w, tag):
    dff = w['down'].shape[0]
    up = _mm(h, w['up'], 'nn', F32, tag + '_up')
    act = _dwconv(up, 0, dff, w['conv'], w['conv_b'], 'glu', dff, BF16, tag + '_conv')
    y = _mm(act, w['down'], 'nn', F32, tag + '_down')
    return y, dict(h=h, up=up, act=act)


def _ffn_bwd(dy, w, sv, tag, reduce_pairs=None):
    h, up = sv['h'], sv['up']
    dff = w['down'].shape[0]
    g = {}
    g['down'] = _mm(sv['act'], dy, 'tn', F32, tag + '_dwdown')
    dact = _mm(dy, w['down'], 'nt', F32, tag + '_dact')
    dgate, dval, g['conv'], g['conv_b'] = _dwconv_bwd(up, 0, dff, w['conv'], 0, w['conv_b'], 'glu', dff, dact,
                                                      tag + '_conv_bwd')
    dup = jnp.concatenate([dgate, dval], axis=1)
    g['up'] = _mm(h, dup, 'tn', F32, tag + '_dwup')
    dh, exchanged = _mm_carry(dup, w['up'], 'nt', F32, tag + '_dh', reduce_pairs, tag, g)
    return [dh], g, exchanged


def _local_step(x, target, mods, norm_g, wf, wg, wffn, late=None, reduce_pairs=None):
    tape = []
    for i in range(2):
        for sub in range(2):
            if sub == 0:
                fwd, bwd, w, tag = [(_fox_fwd, _fox_bwd, wf, 'fox'), (_gdn_fwd, _gdn_bwd, wg, 'gdn')][i]
            else:
                fwd, bwd, w, tag = _ffn_fwd, _ffn_bwd, wffn[i], 'ffn%d' % i
            shift, scale, gate = _split_mod(mods[i, sub])
            g_pre, g_post = norm_g[i, 2 * sub][None], norm_g[i, 2 * sub + 1][None]
            h = _pre_norm(x, g_pre, scale, shift, tag + '_prenorm')
            if late is not None and (i, sub) == (0, 0):
                y, sv, (wg, wffn) = fwd(h, w, tag, late)
            else:
                y, sv = fwd(h, w, tag)
            x_out = _post_res(x, y, gate, g_post, tag + '_postnorm')
            tape.append((bwd, w, tag, sv, x, y, g_pre, g_post, scale, gate))
            x = x_out
    dx, lsum = _loss_head(x, target, 'loss_head')
    loss = lsum[0, 0]
    dmods = [[None, None], [None, None]]
    dnorm = [[None] * 4, [None] * 4]
    wgrads = {}
    exchanged = {}
    for idx in reversed(range(4)):
        i, sub = divmod(idx, 2)
        bwd, w, tag, sv, x_in, y, g_pre, g_post, scale, gate = tape[idx]
        dy, dgate, dgpost = _post_res_bwd(dx, y, gate, g_post, tag + '_postnorm_bwd')
        dh, wgrads[tag], ex = bwd(dy, w, sv, tag, reduce_pairs)
        exchanged.update(ex)
        dx, dshift, dscale, dgpre = _pre_norm_bwd(dh, x_in, g_pre, scale, dx, tag + '_prenorm_bwd')
        dmods[i][sub] = jnp.concatenate([dshift[0], dscale[0], dgate[0]])
        dnorm[i][2 * sub], dnorm[i][2 * sub + 1] = dgpre[0], dgpost[0]
    dmods = jnp.stack([jnp.stack(r) for r in dmods])
    dnorm = jnp.stack([jnp.stack(r) for r in dnorm])
    return loss, dx, dmods, dnorm, wgrads, exchanged


def _unpack_lanes_dev(packed, shapes):
    n = packed.shape[0]
    out, r = [], 0
    for shp in shapes:
        k = math.prod(shp)
        nr = -(-k // LANES)
        out.append(packed[:, r:r + nr].reshape(n, -1)[:, :k].reshape((n,) + tuple(shp)))
        r += nr
    return out


def _gather_lanes(arrs, name):
    packed = _pack_lanes(arrs)
    got = _allgather8(packed, name).reshape(N_DEV, packed.shape[0], LANES)
    return _unpack_lanes_dev(got, [a.shape for a in arrs]), got


def kernel(x, c, ada_w, ada_b, norm_g, fox_w_in, fox_f_bias, fox_q_norm, fox_k_norm, fox_w_o, gdn_w_in, gdn_conv_w, gdn_a_log, gdn_dt_bias, gdn_out_norm, gdn_w_o, ffn_w_up, ffn_conv_w, ffn_conv_b, ffn_w_down, loss_target, m_ada_w, m_ada_b, m_norm_g, m_fox_w_in, m_fox_f_bias, m_fox_q_norm, m_fox_k_norm, m_fox_w_o, m_gdn_w_in, m_gdn_conv_w, m_gdn_a_log, m_gdn_dt_bias, m_gdn_out_norm, m_gdn_w_o, m_ffn_w_up, m_ffn_conv_w, m_ffn_conv_b, m_ffn_w_down, v_ada_w, v_ada_b, v_norm_g, v_fox_w_in, v_fox_f_bias, v_fox_q_norm, v_fox_k_norm, v_fox_w_o, v_gdn_w_in, v_gdn_conv_w, v_gdn_a_log, v_gdn_dt_bias, v_gdn_out_norm, v_gdn_w_o, v_ffn_w_up, v_ffn_conv_w, v_ffn_conv_b, v_ffn_w_down):
    args = locals()
    w = {n: args[n] for n in WEIGHTS}
    mom = {n: args['m_' + n] for n in WEIGHTS}
    var = {n: args['v_' + n] for n in WEIGHTS}
    _, T, D = x.shape
    H = D // HD
    Hv = V_PER_K * H
    xi, yi, ci = _place()
    s_idx = 2 * xi + yi
    b_idx = 4 * xi + 2 * yi + ci
    sc_arr = jnp.stack([s_idx, ci]).astype(jnp.int32)
    c_arr = jnp.reshape(ci, (1,)).astype(jnp.int32)

    (c_all, ab_all, ng_all, gcw_all, fcw_all), _ = _gather_lanes(
        [jnp.tile(c, (SUBLANES, 1)), ada_b, norm_g, gdn_conv_w, ffn_conv_w], 'gather_small')
    c_all = c_all[:, 0, :]
    chips = lambda a: jnp.concatenate([a[2 * s] for s in range(N_CHIPS)], axis=-1)
    norm_g_full, gdn_conv_full, ffn_conv_full = chips(ng_all), chips(gcw_all), chips(fcw_all)

    Ns = ada_w.shape[-1]
    ada_w4 = ada_w.reshape(4, D, Ns)
    part = jnp.stack([_mm(c_all, ada_w4[i], 'nn', F32, 'ada_proj%d' % i, a_act='silu') for i in range(4)])
    part = part + ada_b.reshape(4, 1, Ns)
    (part_all,), _ = _gather_lanes([part], 'gather_mods')
    mine = lax.dynamic_index_in_dim(part_all[0::2], b_idx, axis=2, keepdims=False)
    mods = mine.transpose(1, 0, 2).reshape(2, 2, N_CHIPS * Ns)

    as2d = lambda a: a.reshape(-1, a.shape[-1])
    own = {n: as2d(w[n]).astype(BF16) for n in BIG}

    def whole(n, gathered):
        shards = lax.dynamic_update_index_in_dim(gathered, own[n], s_idx, 0)
        return _shards_to_full(shards.reshape((N_CHIPS,) + w[n].shape), BIG_SHARD_AXIS[n])

    fw = whole('fox_w_in', _gather_weights(own['fox_w_in'], 'gather_fox_w_in'))[0]
    wf = dict(cat=jnp.concatenate([fw[:, :3 * D], fw[:, 3 * D + H:]], axis=1), f=_pad_cols(fw[:, 3 * D:3 * D + H], LANES),
              f_bias=_pad_cols(fox_f_bias, LANES), q_norm=fox_q_norm, k_norm=fox_k_norm)
    later = [n for n in BIG if n != 'fox_w_in']

    def later_weights(gathered):
        full = {n: whole(n, a) for n, a in zip(later, gathered)}
        gw = full['gdn_w_in'][0]
        wg = dict(cat=gw[:, :6 * D], ab=_pad_cols(gw[:, 6 * D:], LANES), conv=gdn_conv_full[0],
                  a_log=_pad_cols(gdn_a_log, LANES), dt_bias=_pad_cols(gdn_dt_bias, LANES), out_norm=gdn_out_norm,
                  o=full['gdn_w_o'][0])
        wffn = [dict(up=full['ffn_w_up'][i], conv=ffn_conv_full[i], conv_b=ffn_conv_b[i][None],
                     down=full['ffn_w_down'][i]) for i in range(2)]
        return full['fox_w_o'][0], (wg, wffn)

    def reduce_pairs(tag, gr):
        if tag == 'fox':
            full = {('fox_w_in', 0): jnp.concatenate([gr['cat'][:, :3 * D], gr['f'][:, :H], gr['cat'][:, 3 * D:]], axis=1),
                    ('fox_w_o', 0): gr['o']}
        elif tag == 'gdn':
            full = {('gdn_w_in', 0): jnp.concatenate([gr['cat'], gr['ab'][:, :2 * Hv]], axis=1), ('gdn_w_o', 0): gr['o']}
        else:
            layer = int(tag[-1])
            full = {('ffn_w_up', layer): gr['up'], ('ffn_w_down', layer): gr['down']}
        out = []
        for (n, layer), a in full.items():
            shards = _full_to_shards(a, BIG_SHARD_AXIS[n] - 1)
            name = '%s%d' % (n, layer)
            got = _swap_halves(shards, name + '_to_sibling')
            out.append(((n, layer), _add_halves(shards, got, c_arr, name + '_add_sibling')))
        return out

    loss, dx, dmods, dnorm, g, exchanged = _local_step(
        x[0], loss_target[0], mods, norm_g_full, wf, None, None,
        late=([own[n] for n in later], later_weights), reduce_pairs=reduce_pairs)
    loss = lax.psum(loss, ('x', 'y', 'c'))

    gf, gg = g['fox'], g['gdn']
    big_out = {}
    for n in BIG:
        parts = []
        for layer in range(w[n].shape[0]):
            tag = '%s%d' % (n, layer)
            pair_sum, received = exchanged[(n, layer)]
            half_sum = _add_chips(pair_sum, received, sc_arr, tag + '_add_chips')
            parts.append(_join_halves(half_sum, tag + '_join'))
        g_shard = parts[0] if len(parts) == 1 else jnp.concatenate(parts, axis=0)
        big_out[n] = [o.reshape(w[n].shape)
                      for o in _adamw(as2d(w[n]), as2d(mom[n]), as2d(var[n]), [g_shard], 'adamw_' + n)]

    small_part = [dmods, dnorm, gf['f_bias'][:, :H], gf['q_norm'], gf['k_norm'], gg['conv'][None],
                  gg['a_log'][:, :Hv], gg['dt_bias'][:, :Hv], gg['out_norm'],
                  jnp.stack([g['ffn0']['conv'], g['ffn1']['conv']]),
                  jnp.concatenate([g['ffn0']['conv_b'], g['ffn1']['conv_b']], axis=0)]
    (dmods_all, *_), got = _gather_lanes(small_part, 'gather_small_grads')
    tot = _unpack_lanes(_sum_devices(got, 'sum_small_grads'), [a.shape for a in small_part])
    small_full = dict(zip(SMALL, tot))
    small_g = {n: (lax.dynamic_slice_in_dim(small_full[n], s_idx * w[n].shape[-1], w[n].shape[-1], axis=-1)
                   if n in SMALL_SHARDED else small_full[n]) for n in SMALL}
    packs = lambda d: _pack_lanes([d[n] for n in SMALL])
    small_shapes = [w[n].shape for n in SMALL]
    small_out = [_unpack_lanes(o, small_shapes)
                 for o in _adamw(packs(w), packs(mom), packs(var), [packs(small_g)], 'adamw_small')]

    dm = lax.dynamic_slice_in_dim(dmods_all.reshape(N_DEV, 4, N_CHIPS * Ns), s_idx * Ns, Ns, axis=-1).transpose(1, 0, 2)
    ada_out = [o.reshape(ada_w.shape) for o in
               _ada_w_update(c_all, dm, ada_w4, m_ada_w.reshape(4, D, Ns), v_ada_w.reshape(4, D, Ns), 'adamw_ada_w')]

    outs = []
    for k in range(4):
        by_name = {'ada_w': ada_out[k]}
        by_name.update({n: big_out[n][k] for n in BIG})
        by_name.update(zip(SMALL, small_out[k]))
        outs += [by_name[n] for n in WEIGHTS]
    return (loss, dx[None], *outs)
```

```python
import functools
import math

import jax
import jax.numpy as jnp
from jax import lax
from jax.experimental import pallas as pl
from jax.experimental.pallas import tpu as pltpu

F32 = jnp.float32
BF16 = jnp.bfloat16
EPS = 1e-6
HD = 128
GDN_CHUNK = 64
GDN_CONV = 4
FFN_CONV = 3
LANES = 128
SUBLANES = 8
VMEM_LIMIT = 56 * 1024 * 1024
HIGHEST = lax.Precision.HIGHEST
NEG = -1e30

ADAM_LR = 0.001
ADAM_B1 = 0.9
ADAM_B2 = 0.999
ADAM_EPS = 1e-08
ADAM_WD = 0.01
ADAM_STEP = 10

WEIGHTS = ['ada_w', 'ada_b', 'norm_g', 'fox_w_in', 'fox_f_bias', 'fox_q_norm', 'fox_k_norm', 'fox_w_o',
           'gdn_w_in', 'gdn_conv_w', 'gdn_a_log', 'gdn_dt_bias', 'gdn_out_norm', 'gdn_w_o',
           'ffn_w_up', 'ffn_conv_w', 'ffn_conv_b', 'ffn_w_down']
BIG = ['fox_w_in', 'fox_w_o', 'gdn_w_in', 'gdn_w_o', 'ffn_w_up', 'ffn_w_down']
BIG_SHARD_AXIS = {'fox_w_in': 2, 'fox_w_o': 1, 'gdn_w_in': 2, 'gdn_w_o': 1, 'ffn_w_up': 2, 'ffn_w_down': 1}
SMALL = ['ada_b', 'norm_g', 'fox_f_bias', 'fox_q_norm', 'fox_k_norm', 'gdn_conv_w', 'gdn_a_log',
         'gdn_dt_bias', 'gdn_out_norm', 'ffn_conv_w', 'ffn_conv_b']
SMALL_SHARDED = ['ada_b', 'norm_g', 'gdn_conv_w', 'ffn_conv_w']
N_CHIPS = 4
N_DEV = 8
MESH = pl.DeviceIdType.MESH


def _tile(n, cands):
    for c in cands:
        if n % c == 0:
            return c
    return n


def _cp(*sem):
    return pltpu.CompilerParams(dimension_semantics=sem, vmem_limit_bytes=VMEM_LIMIT)


def _dot(a, b, mode='nn', precision=None):
    dims = {'nn': (((1,), (0,)), ((), ())), 'nt': (((1,), (1,)), ((), ())), 'tn': (((0,), (0,)), ((), ()))}[mode]
    return lax.dot_general(a, b, dims, precision=precision, preferred_element_type=F32)


def _bdot(a, b, mode='nn'):
    return _dot(a.astype(BF16), b.astype(BF16), mode)


def _hdot(a, b, mode='nn'):
    return _dot(a, b, mode, precision=HIGHEST)


def _sigmoid(x):
    return 1.0 / (1.0 + jnp.exp(-x))


def _silu(x):
    return x * _sigmoid(x)


def _softplus(x):
    return jnp.maximum(x, 0.0) + jnp.log(1.0 + jnp.exp(-jnp.abs(x)))


def _erf(x):
    return lax.erf(x)


def _gelu(x):
    return 0.5 * x * (1.0 + _erf(x * (2.0 ** -0.5)))


def _normal_cdf_pdf(x):
    cdf = 0.5 * (1.0 + _erf(x * (2.0 ** -0.5)))
    pdf = jnp.exp(-0.5 * x * x) * (1.0 / math.sqrt(2.0 * math.pi))
    return cdf, pdf


MM_K_CAP = 2816


def _k_tile(K, cap):
    for t in range(cap - cap % LANES, 0, -LANES):
        if K % t == 0:
            return t
    return K


def _mm(a, b, mode, out_dtype, name, a_act=None, scatter=()):
    if mode == 'nn':
        (M, K), (_, N) = a.shape, b.shape
    elif mode == 'nt':
        (M, K), (N, _) = a.shape, b.shape
    else:
        (K, M), (_, N) = a.shape, b.shape
    big = (1024, 512, 256, 128)
    narrow = a.dtype.itemsize == 2 and b.dtype.itemsize == 2
    tm, tn, tk = _tile(M, big), _tile(N, big), _k_tile(K, MM_K_CAP if narrow else MM_K_CAP // 2)
    nk, ns = K // tk, len(scatter)
    grid = (M // tm, N // tn, nk)

    def body(a_ref, b_ref, *rest):
        q_refs, o_ref, got_refs = rest[:ns], rest[ns], rest[ns + 1:2 * ns + 1]
        acc = rest[2 * ns + 1:2 * ns + 1 + (nk > 1)]
        jobs = [_scatter_protocol(q_refs[i], got_refs[i], *rest[len(rest) - 2:], SCATTER_SEMS * i) for i in range(ns)]
        step = [pl.program_id(d) for d in range(3)]
        if jobs:
            @pl.when((step[0] == 0) & (step[1] == 0) & (step[2] == 0))
            def _():
                for start, _ in jobs:
                    start()

        av = a_ref[...]
        if a_act == 'silu':
            av = _silu(av.astype(F32))
        part = _bdot(av, b_ref[...], mode)
        if nk == 1:
            o_ref[...] = part.astype(o_ref.dtype)
        else:
            acc_ref, = acc
            k = step[2]

            @pl.when(k == 0)
            def _():
                acc_ref[...] = part

            @pl.when(k > 0)
            def _():
                acc_ref[...] += part

            @pl.when(k == nk - 1)
            def _():
                o_ref[...] = acc_ref[...].astype(o_ref.dtype)

        if jobs:
            @pl.when((step[0] == grid[0] - 1) & (step[1] == grid[1] - 1) & (step[2] == grid[2] - 1))
            def _():
                for _, finish in jobs:
                    finish()

    if mode == 'nn':
        a_spec = pl.BlockSpec((tm, tk), lambda i, j, k: (i, k))
        b_spec = pl.BlockSpec((tk, tn), lambda i, j, k: (k, j))
    elif mode == 'nt':
        a_spec = pl.BlockSpec((tm, tk), lambda i, j, k: (i, k))
        b_spec = pl.BlockSpec((tn, tk), lambda i, j, k: (j, k))
    else:
        a_spec = pl.BlockSpec((tk, tm), lambda i, j, k: (k, i))
        b_spec = pl.BlockSpec((tk, tn), lambda i, j, k: (k, j))
    sems = [pltpu.SemaphoreType.DMA((SCATTER_SEMS * ns,))] * 2 if ns else []
    out = pl.pallas_call(
        body, name=name, grid=grid,
        in_specs=[a_spec, b_spec] + [HBM_SPEC] * ns,
        out_specs=[pl.BlockSpec((tm, tn), lambda i, j, k: (i, j))] + [HBM_SPEC] * ns,
        out_shape=[jax.ShapeDtypeStruct((M, N), out_dtype)] + _scatter_shapes(scatter),
        scratch_shapes=([pltpu.VMEM((tm, tn), F32)] if nk > 1 else []) + sems,
        compiler_params=_cp("arbitrary", "arbitrary", "arbitrary") if ns else _cp("parallel", "parallel", "arbitrary"),
    )(a, b, *scatter)
    return (out[0], list(out[1:])) if ns else out[0]


ROW_TILES = (256, 128, 64, 32, 16, 8)


def _row_spec(tT, D):
    return pl.BlockSpec((tT, D), lambda i: (i, 0))


def _vec_spec(D):
    return pl.BlockSpec((1, D), lambda i: (0, 0))


def _pre_norm(x, g, scale, shift, name):
    T, D = x.shape
    tT = _tile(T, ROW_TILES)

    def body(x_ref, g_ref, sc_ref, sh_ref, h_ref):
        xv = x_ref[...]
        r = lax.rsqrt(jnp.mean(xv * xv, axis=-1, keepdims=True) + EPS)
        h_ref[...] = ((xv * r) * g_ref[...] * (1.0 + sc_ref[...]) + sh_ref[...]).astype(h_ref.dtype)

    return pl.pallas_call(
        body, name=name, grid=(T // tT,),
        in_specs=[_row_spec(tT, D), _vec_spec(D), _vec_spec(D), _vec_spec(D)],
        out_specs=_row_spec(tT, D), out_shape=jax.ShapeDtypeStruct((T, D), BF16),
        compiler_params=_cp("parallel"),
    )(x, g, scale, shift)


def _post_res(x, y, gate, g, name):
    T, D = x.shape
    tT = _tile(T, ROW_TILES)

    def body(x_ref, y_ref, gate_ref, g_ref, o_ref):
        yv = y_ref[...]
        r = lax.rsqrt(jnp.mean(yv * yv, axis=-1, keepdims=True) + EPS)
        o_ref[...] = x_ref[...] + gate_ref[...] * ((yv * r) * g_ref[...])

    return pl.pallas_call(
        body, name=name, grid=(T // tT,),
        in_specs=[_row_spec(tT, D), _row_spec(tT, D), _vec_spec(D), _vec_spec(D)],
        out_specs=_row_spec(tT, D), out_shape=jax.ShapeDtypeStruct((T, D), F32),
        compiler_params=_cp("parallel"),
    )(x, y, gate, g)


def _post_res_bwd(dout, y, gate, g, name):
    T, D = y.shape
    tT = _tile(T, ROW_TILES)

    def body(do_ref, y_ref, gate_ref, g_ref, dy_ref, dgate_ref, dg_ref):
        @pl.when(pl.program_id(0) == 0)
        def _():
            dgate_ref[...] = jnp.zeros_like(dgate_ref)
            dg_ref[...] = jnp.zeros_like(dg_ref)

        yv, dov, gatev, gv = y_ref[...], do_ref[...], gate_ref[...], g_ref[...]
        r = lax.rsqrt(jnp.mean(yv * yv, axis=-1, keepdims=True) + EPS)
        yn = yv * r
        t = dov * yn
        dgate_ref[...] += jnp.sum(t * gv, axis=0, keepdims=True)
        dg_ref[...] += jnp.sum(t * gatev, axis=0, keepdims=True)
        dyn = dov * (gatev * gv)
        dy_ref[...] = (r * (dyn - yn * jnp.mean(dyn * yn, axis=-1, keepdims=True))).astype(dy_ref.dtype)

    return pl.pallas_call(
        body, name=name, grid=(T // tT,),
        in_specs=[_row_spec(tT, D), _row_spec(tT, D), _vec_spec(D), _vec_spec(D)],
        out_specs=[_row_spec(tT, D), _vec_spec(D), _vec_spec(D)],
        out_shape=[jax.ShapeDtypeStruct((T, D), BF16), jax.ShapeDtypeStruct((1, D), F32),
                   jax.ShapeDtypeStruct((1, D), F32)],
        compiler_params=_cp("arbitrary"),
    )(dout, y, gate, g)


def _pre_norm_bwd(dhs, x, g, scale, dres, name):
    T, D = x.shape
    tT = _tile(T, ROW_TILES)
    n = len(dhs)

    def body(*refs):
        dh_refs = refs[:n]
        x_ref, g_ref, sc_ref, dres_ref, dx_ref, dsh_ref, dsc_ref, dg_ref = refs[n:]

        @pl.when(pl.program_id(0) == 0)
        def _():
            dsh_ref[...] = jnp.zeros_like(dsh_ref)
            dsc_ref[...] = jnp.zeros_like(dsc_ref)
            dg_ref[...] = jnp.zeros_like(dg_ref)

        dh = dh_refs[0][...]
        for r_ in dh_refs[1:]:
            dh = dh + r_[...]
        xv, gv, scv = x_ref[...], g_ref[...], sc_ref[...]
        r = lax.rsqrt(jnp.mean(xv * xv, axis=-1, keepdims=True) + EPS)
        xn = xv * r
        t = dh * xn
        dsh_ref[...] += jnp.sum(dh, axis=0, keepdims=True)
        dsc_ref[...] += jnp.sum(t * gv, axis=0, keepdims=True)
        dg_ref[...] += jnp.sum(t * (1.0 + scv), axis=0, keepdims=True)
        dxn = dh * (gv * (1.0 + scv))
        dx_ref[...] = dres_ref[...] + r * (dxn - xn * jnp.mean(dxn * xn, axis=-1, keepdims=True))

    return pl.pallas_call(
        body, name=name, grid=(T // tT,),
        in_specs=[_row_spec(tT, D)] * n + [_row_spec(tT, D), _vec_spec(D), _vec_spec(D), _row_spec(tT, D)],
        out_specs=[_row_spec(tT, D), _vec_spec(D), _vec_spec(D), _vec_spec(D)],
        out_shape=[jax.ShapeDtypeStruct((T, D), F32)] + [jax.ShapeDtypeStruct((1, D), F32)] * 3,
        compiler_params=_cp("arbitrary"),
    )(*dhs, x, g, scale, dres)


def _loss_head(y, target, name):
    T, D = y.shape
    tT = _tile(T, ROW_TILES)

    def body(y_ref, t_ref, dy_ref, l_ref):
        @pl.when(pl.program_id(0) == 0)
        def _():
            l_ref[...] = jnp.zeros_like(l_ref)

        e = y_ref[...] - t_ref[...]
        dy_ref[...] = e * (1.0 / D)
        s = jnp.sum(jnp.mean(e * e, axis=-1, keepdims=True), axis=0, keepdims=True)
        l_ref[...] += 0.5 * s

    return pl.pallas_call(
        body, name=name, grid=(T // tT,),
        in_specs=[_row_spec(tT, D), _row_spec(tT, D)],
        out_specs=[_row_spec(tT, D), pl.BlockSpec((SUBLANES, LANES), lambda i: (0, 0))],
        out_shape=[jax.ShapeDtypeStruct((T, D), F32), jax.ShapeDtypeStruct((SUBLANES, LANES), F32)],
        compiler_params=_cp("arbitrary"),
    )(y, target)


HEAD_ROW_TILES = (2048, 1024, 512, 256, 128, 64)


def _hb(tT, off=0):
    return pl.BlockSpec((tT, HD), lambda i, h: (i, off + h))


def _hvec():
    return pl.BlockSpec((1, HD), lambda i, h: (0, 0))


def _headnorm(x, off, H, g, c1, post, out_dtype, name):
    T = x.shape[0]
    tT = _tile(T, HEAD_ROW_TILES)
    has_g = g is not None

    def body(*refs):
        x_ref = refs[0]
        o_ref = refs[-1]
        xv = x_ref[...]
        yv = xv * lax.rsqrt(c1 * jnp.sum(xv * xv, axis=-1, keepdims=True) + EPS)
        if has_g:
            yv = yv * refs[1][...]
        if post != 1.0:
            yv = yv * post
        o_ref[...] = yv.astype(o_ref.dtype)

    return pl.pallas_call(
        body, name=name, grid=(T // tT, H),
        in_specs=[_hb(tT, off)] + ([_hvec()] if has_g else []),
        out_specs=_hb(tT), out_shape=jax.ShapeDtypeStruct((T, H * HD), out_dtype),
        compiler_params=_cp("parallel", "parallel"),
    )(*([x, g] if has_g else [x]))


def _headnorm_bwd(dys, x, off, H, g, c1, post, out_dtype, name):
    T = x.shape[0]
    tT = _tile(T, HEAD_ROW_TILES)
    n = len(dys)
    has_g = g is not None

    def body(*refs):
        dy_refs = refs[:n]
        x_ref = refs[n]
        g_ref = refs[n + 1] if has_g else None
        dx_ref, dg_ref = refs[-2], refs[-1]

        @pl.when((pl.program_id(0) == 0) & (pl.program_id(1) == 0))
        def _():
            dg_ref[...] = jnp.zeros_like(dg_ref)

        dy = dy_refs[0][...].astype(F32)
        for r_ in dy_refs[1:]:
            dy = dy + r_[...].astype(F32)
        if post != 1.0:
            dy = dy * post
        xv = x_ref[...]
        r = lax.rsqrt(c1 * jnp.sum(xv * xv, axis=-1, keepdims=True) + EPS)
        xn = xv * r
        if has_g:
            dg_ref[...] += jnp.sum(dy * xn, axis=0, keepdims=True)
            dy = dy * g_ref[...]
        dx_ref[...] = (r * (dy - xn * (c1 * jnp.sum(dy * xn, axis=-1, keepdims=True)))).astype(dx_ref.dtype)

    dy_specs = [pl.BlockSpec((tT, HD), lambda i, h, st=st, of=of: (i, st * h + of)) for (_, st, of) in dys]
    return pl.pallas_call(
        body, name=name, grid=(T // tT, H),
        in_specs=dy_specs + [_hb(tT, off)] + ([_hvec()] if has_g else []),
        out_specs=[_hb(tT), _hvec()],
        out_shape=[jax.ShapeDtypeStruct((T, H * HD), out_dtype), jax.ShapeDtypeStruct((1, HD), F32)],
        compiler_params=_cp("arbitrary", "arbitrary"),
    )(*[d[0] for d in dys], x, *([g] if has_g else []))


def _fox_gate(ao, proj, og_off, H, name):
    T = ao.shape[0]
    tT = _tile(T, HEAD_ROW_TILES)

    def body(ao_ref, og_ref, o_ref):
        o_ref[...] = (ao_ref[...] * _sigmoid(og_ref[...])).astype(o_ref.dtype)

    return pl.pallas_call(
        body, name=name, grid=(T // tT, H),
        in_specs=[_hb(tT), _hb(tT, og_off)], out_specs=_hb(tT),
        out_shape=jax.ShapeDtypeStruct((T, H * HD), BF16), compiler_params=_cp("parallel", "parallel"),
    )(ao, proj)


def _fox_gate_bwd(dgated, ao, proj, og_off, H, name):
    T = ao.shape[0]
    tT = _tile(T, HEAD_ROW_TILES)

    def body(dg_ref, ao_ref, og_ref, dao_ref, dog_ref, delta_ref):
        dg, aov = dg_ref[...], ao_ref[...]
        sg = _sigmoid(og_ref[...])
        dao = dg * sg
        dao_ref[...] = dao.astype(dao_ref.dtype)
        dog_ref[...] = (dg * aov * sg * (1.0 - sg)).astype(dog_ref.dtype)
        delta_ref[...] = jnp.broadcast_to(jnp.sum(dao * aov, axis=-1, keepdims=True), delta_ref.shape)

    return pl.pallas_call(
        body, name=name, grid=(T // tT, H),
        in_specs=[_hb(tT), _hb(tT), _hb(tT, og_off)], out_specs=[_hb(tT)] * 3,
        out_shape=[jax.ShapeDtypeStruct((T, H * HD), BF16), jax.ShapeDtypeStruct((T, H * HD), BF16),
                   jax.ShapeDtypeStruct((T, H * HD), F32)],
        compiler_params=_cp("parallel", "parallel"),
    )(dgated, ao, proj)


def _gdn_out(o, proj, z_off, Hv, g, name):
    T = o.shape[0]
    tT = _tile(T, HEAD_ROW_TILES)

    def body(o_ref, z_ref, g_ref, y_ref):
        ov, zv = o_ref[...], z_ref[...]
        r = lax.rsqrt(jnp.mean(ov * ov, axis=-1, keepdims=True) + EPS)
        y_ref[...] = (((ov * r) * g_ref[...]) * _silu(zv)).astype(y_ref.dtype)

    return pl.pallas_call(
        body, name=name, grid=(T // tT, Hv),
        in_specs=[_hb(tT), _hb(tT, z_off), _hvec()], out_specs=_hb(tT),
        out_shape=jax.ShapeDtypeStruct((T, Hv * HD), BF16), compiler_params=_cp("parallel", "parallel"),
    )(o, proj, g)


def _gdn_out_bwd(dy, o, proj, z_off, Hv, g, name):
    T = o.shape[0]
    tT = _tile(T, HEAD_ROW_TILES)

    def body(dy_ref, o_ref, z_ref, g_ref, do_ref, dz_ref, dg_ref):
        @pl.when((pl.program_id(0) == 0) & (pl.program_id(1) == 0))
        def _():
            dg_ref[...] = jnp.zeros_like(dg_ref)

        dyv, ov, zv, gv = dy_ref[...], o_ref[...], z_ref[...], g_ref[...]
        r = lax.rsqrt(jnp.mean(ov * ov, axis=-1, keepdims=True) + EPS)
        on = ov * r
        sg = _sigmoid(zv)
        sz = zv * sg
        dz_ref[...] = (dyv * (on * gv) * (sg * (1.0 + zv * (1.0 - sg)))).astype(dz_ref.dtype)
        t = dyv * sz
        dg_ref[...] += jnp.sum(t * on, axis=0, keepdims=True)
        don = t * gv
        do_ref[...] = r * (don - on * jnp.mean(don * on, axis=-1, keepdims=True))

    return pl.pallas_call(
        body, name=name, grid=(T // tT, Hv),
        in_specs=[_hb(tT), _hb(tT), _hb(tT, z_off), _hvec()], out_specs=[_hb(tT), _hb(tT), _hvec()],
        out_shape=[jax.ShapeDtypeStruct((T, Hv * HD), F32), jax.ShapeDtypeStruct((T, Hv * HD), BF16),
                   jax.ShapeDtypeStruct((1, HD), F32)],
        compiler_params=_cp("arbitrary", "arbitrary"),
    )(dy, o, proj, g)


def _lrow(tT):
    return pl.BlockSpec((tT, LANES), lambda i: (i, 0))


def _lvec():
    return pl.BlockSpec((1, LANES), lambda i: (0, 0))


def _logsig(x, b, name):
    T = x.shape[0]
    tT = _tile(T, HEAD_ROW_TILES)

    def body(x_ref, b_ref, o_ref):
        o_ref[...] = -_softplus(-(x_ref[...] + b_ref[...]))

    return pl.pallas_call(body, name=name, grid=(T // tT,), in_specs=[_lrow(tT), _lvec()], out_specs=_lrow(tT),
                          out_shape=jax.ShapeDtypeStruct((T, LANES), F32), compiler_params=_cp("parallel"))(x, b)


def _logsig_bwd(dy, x, b, name):
    T = x.shape[0]
    tT = _tile(T, HEAD_ROW_TILES)

    def body(dy_ref, x_ref, b_ref, dx_ref, db_ref):
        @pl.when(pl.program_id(0) == 0)
        def _():
            db_ref[...] = jnp.zeros_like(db_ref)

        dx = dy_ref[...] * _sigmoid(-(x_ref[...] + b_ref[...]))
        dx_ref[...] = dx.astype(dx_ref.dtype)
        db_ref[...] += jnp.sum(dx, axis=0, keepdims=True)

    return pl.pallas_call(
        body, name=name, grid=(T // tT,), in_specs=[_lrow(tT), _lrow(tT), _lvec()], out_specs=[_lrow(tT), _lvec()],
        out_shape=[jax.ShapeDtypeStruct((T, LANES), BF16), jax.ShapeDtypeStruct((1, LANES), F32)],
        compiler_params=_cp("arbitrary"))(dy, x, b)


def _gdn_gates(ab, alog, dtb, name):
    T = ab.shape[0]
    tT = _tile(T, HEAD_ROW_TILES)

    def body(ab_ref, al_ref, dt_ref, g_ref, be_ref):
        v = ab_ref[...]
        g_ref[...] = -jnp.exp(al_ref[...]) * _softplus(v + dt_ref[...])
        be_ref[...] = _sigmoid(v)

    return pl.pallas_call(
        body, name=name, grid=(T // tT,), in_specs=[_lrow(tT), _lvec(), _lvec()], out_specs=[_lrow(tT)] * 2,
        out_shape=[jax.ShapeDtypeStruct((T, LANES), F32)] * 2, compiler_params=_cp("parallel"))(ab, alog, dtb)


def _gdn_gates_bwd(dg, dbeta, ab, alog, dtb, name):
    T = ab.shape[0]
    tT = _tile(T, HEAD_ROW_TILES)

    def body(dg_ref, dbe_ref, ab_ref, al_ref, dt_ref, dab_ref, dal_ref, ddt_ref):
        @pl.when(pl.program_id(0) == 0)
        def _():
            dal_ref[...] = jnp.zeros_like(dal_ref)
            ddt_ref[...] = jnp.zeros_like(ddt_ref)

        v, dgv = ab_ref[...], dg_ref[...]
        ea = jnp.exp(al_ref[...])
        z = v + dt_ref[...]
        da = dgv * (-ea * _sigmoid(z))
        sb = _sigmoid(v)
        dab_ref[...] = (da + dbe_ref[...] * sb * (1.0 - sb)).astype(dab_ref.dtype)
        dal_ref[...] += jnp.sum(dgv * (-ea * _softplus(z)), axis=0, keepdims=True)
        ddt_ref[...] += jnp.sum(da, axis=0, keepdims=True)

    return pl.pallas_call(
        body, name=name, grid=(T // tT,), in_specs=[_lrow(tT), _lrow(tT), _lrow(tT), _lvec(), _lvec()],
        out_specs=[_lrow(tT), _lvec(), _lvec()],
        out_shape=[jax.ShapeDtypeStruct((T, LANES), BF16), jax.ShapeDtypeStruct((1, LANES), F32),
                   jax.ShapeDtypeStruct((1, LANES), F32)],
        compiler_params=_cp("arbitrary"))(dg, dbeta, ab, alog, dtb)


def _spread_heads(xs, offs, n_heads, name):
    T = xs[0].shape[0]
    tT = _tile(T, ROW_TILES)
    n = len(xs)

    def body(*refs):
        for x_ref, o_ref, off in zip(refs[:n], refs[n:], offs):
            xv = x_ref[...]
            for j in range(n_heads):
                o_ref[:, j * HD:(j + 1) * HD] = jnp.broadcast_to(xv[:, off + j:off + j + 1], (tT, HD))

    wide = pl.BlockSpec((tT, n_heads * HD), lambda i: (i, 0))
    return pl.pallas_call(
        body, name=name, grid=(T // tT,), in_specs=[_lrow(tT)] * n, out_specs=[wide] * n,
        out_shape=[jax.ShapeDtypeStruct((T, n_heads * HD), F32)] * n, compiler_params=_cp("parallel"))(*xs)


def _cumsum(x, seg, reverse, name):
    T = x.shape[0]
    tb = _tile(T, (256, 128, 64))
    nb = T // tb
    carry = seg is None

    def body(x_ref, o_ref, c_ref):
        @pl.when(pl.program_id(0) == 0)
        def _():
            c_ref[...] = jnp.zeros_like(c_ref)

        ri = lax.broadcasted_iota(jnp.int32, (tb, tb), 0)
        ci = lax.broadcasted_iota(jnp.int32, (tb, tb), 1)
        keep = (ci >= ri) if reverse else (ci <= ri)
        if seg is not None:
            keep = keep & ((ri // seg) == (ci // seg))
        y = _hdot(keep.astype(F32), x_ref[...])
        if carry:
            y = y + c_ref[...]
            c_ref[...] = y[0:1, :] if reverse else y[tb - 1:tb, :]
        o_ref[...] = y

    imap = (lambda i: (nb - 1 - i, 0)) if reverse else (lambda i: (i, 0))
    return pl.pallas_call(
        body, name=name, grid=(nb,), in_specs=[pl.BlockSpec((tb, LANES), imap)],
        out_specs=pl.BlockSpec((tb, LANES), imap), out_shape=jax.ShapeDtypeStruct((T, LANES), F32),
        scratch_shapes=[pltpu.VMEM((1, LANES), F32)], compiler_params=_cp("arbitrary"))(x)


ATT_TILES = (1024, 512, 256, 128)


def _att_scores(q, k, f0, fk):
    return _dot(q, k, 'nt') - (fk - f0)


def _diag_keep(tq):
    return lax.broadcasted_iota(jnp.int32, (tq, tq), 1) <= lax.broadcasted_iota(jnp.int32, (tq, tq), 0)


def _tri_pairs(nq, by_key):
    if by_key:
        pairs = [(qi, ki) for ki in range(nq) for qi in range(ki, nq)]
    else:
        pairs = [(qi, ki) for qi in range(nq) for ki in range(qi + 1)]
    return jnp.asarray([p[0] for p in pairs], jnp.int32), jnp.asarray([p[1] for p in pairs], jnp.int32)


def _att_specs(tq):
    qspec = pl.BlockSpec((tq, HD), lambda h, p, qt, kt: (qt[p], h))
    kspec = pl.BlockSpec((tq, HD), lambda h, p, qt, kt: (kt[p], h))
    f0spec = pl.BlockSpec((None, None, 1, 1), lambda h, p, qt, kt: (h, qt[p], 0, 0))
    fkspec = pl.BlockSpec((None, 1, tq), lambda h, p, qt, kt: (h, 0, kt[p]))
    return qspec, kspec, f0spec, fkspec


def _flash_fwd(qn, kn, vb, f0, fkr, proj, og_off, H, name, carry=()):
    T = qn.shape[0]
    tq = _tile(T, ATT_TILES)
    nq = T // tq
    qt, kt = _tri_pairs(nq, by_key=False)
    n_pairs, n_carry = qt.shape[0], len(carry)

    def body(qt_ref, kt_ref, q_ref, k_ref, v_ref, f0_ref, fk_ref, og_ref, *rest):
        w_refs, (o_ref, lse_ref, gated_ref) = rest[:n_carry], rest[n_carry:n_carry + 3]
        g_refs = rest[n_carry + 3:2 * n_carry + 3]
        m_s, l_s, acc_s = rest[2 * n_carry + 3:2 * n_carry + 6]
        jobs = [_gather_protocol(w_refs[i], g_refs[i], *rest[2 * n_carry + 6:], GATHER_SEMS * i) for i in range(n_carry)]
        qi, ki = qt_ref[pl.program_id(1)], kt_ref[pl.program_id(1)]

        if jobs:
            @pl.when((pl.program_id(0) == 0) & (pl.program_id(1) == 0))
            def _():
                for start, _ in jobs:
                    start()

        @pl.when(ki == 0)
        def _():
            m_s[...] = jnp.full_like(m_s, NEG)
            l_s[...] = jnp.zeros_like(l_s)
            acc_s[...] = jnp.zeros_like(acc_s)

        def step(diagonal):
            s = _att_scores(q_ref[...], k_ref[...], f0_ref[...], fk_ref[...])
            if diagonal:
                s = jnp.where(_diag_keep(tq), s, NEG)
            m_prev = m_s[...]
            m_new = jnp.maximum(m_prev, jnp.max(s, axis=1, keepdims=True))
            alpha = jnp.exp(m_prev - m_new)
            p = jnp.exp(s - m_new[:, :1])
            l_s[...] = alpha * l_s[...] + jnp.sum(p, axis=1, keepdims=True)
            acc_s[...] = acc_s[...] * alpha + _bdot(p, v_ref[...])
            m_s[...] = m_new

        @pl.when(ki < qi)
        def _():
            step(False)

        @pl.when(ki == qi)
        def _():
            step(True)
            o = acc_s[...] / l_s[...]
            o_ref[...] = o
            gated_ref[...] = (o * _sigmoid(og_ref[...])).astype(gated_ref.dtype)
            lse_ref[...] = m_s[...] + jnp.log(l_s[...])

        if jobs:
            @pl.when((pl.program_id(0) == H - 1) & (pl.program_id(1) == n_pairs - 1))
            def _():
                for _, finish in jobs:
                    finish()

    qspec, kspec, f0spec, fkspec = _att_specs(tq)
    ogspec = pl.BlockSpec((tq, HD), lambda h, p, qt, kt: (qt[p], og_off + h))
    sems = [pltpu.SemaphoreType.DMA((GATHER_SEMS * n_carry,))] * 2 if n_carry else []
    grid_spec = pltpu.PrefetchScalarGridSpec(
        num_scalar_prefetch=2, grid=(H, n_pairs),
        in_specs=[qspec, kspec, kspec, f0spec, fkspec, ogspec] + [HBM_SPEC] * n_carry,
        out_specs=[qspec, qspec, qspec] + [HBM_SPEC] * n_carry,
        scratch_shapes=[pltpu.VMEM((tq, HD), F32)] * 3 + sems)
    out = pl.pallas_call(
        body, name=name, grid_spec=grid_spec,
        out_shape=[jax.ShapeDtypeStruct((T, H * HD), F32)] * 2 + [jax.ShapeDtypeStruct((T, H * HD), BF16)]
        + [jax.ShapeDtypeStruct((N_CHIPS,) + a.shape, a.dtype) for a in carry],
        compiler_params=_cp("arbitrary", "arbitrary") if n_carry else _cp("parallel", "arbitrary"),
    )(qt, kt, qn, kn, vb, f0, fkr, proj, *carry)
    return out[0], out[1], out[2], list(out[3:])


def _flash_bwd(qn, kn, vb, f0, fkr, dao, lse, delta, H, name):
    T = qn.shape[0]
    tq = _tile(T, ATT_TILES)
    nq = T // tq
    qt, kt = _tri_pairs(nq, by_key=True)

    def body(qt_ref, kt_ref, q_ref, k_ref, v_ref, f0_ref, fk_ref, do_ref, lse_ref, dl_ref,
             dq_ref, dfq_ref, dk_ref, dv_ref, dfk_ref, dk_s, dv_s, dfk_s):
        qi, ki = qt_ref[pl.program_id(1)], kt_ref[pl.program_id(1)]

        @pl.when(pl.program_id(1) == 0)
        def _():
            dq_ref[...] = jnp.zeros_like(dq_ref)
            dfq_ref[...] = jnp.zeros_like(dfq_ref)

        @pl.when(qi == ki)
        def _():
            dk_s[...] = jnp.zeros_like(dk_s)
            dv_s[...] = jnp.zeros_like(dv_s)
            dfk_s[...] = jnp.zeros_like(dfk_s)

        def step(diagonal):
            s = _att_scores(q_ref[...], k_ref[...], f0_ref[...], fk_ref[...])
            p = jnp.exp(s - lse_ref[...][:, :1])
            if diagonal:
                p = jnp.where(_diag_keep(tq), p, 0.0)
            dp = _dot(do_ref[...], v_ref[...], 'nt')
            ds = p * (dp - dl_ref[...][:, :1])
            dv_s[...] += _bdot(p, do_ref[...], 'tn')
            dk_s[...] += _bdot(ds, q_ref[...], 'tn')
            dfk_s[...] -= jnp.sum(ds, axis=0, keepdims=True)
            rows = pl.ds(pl.multiple_of(qi * tq, tq), tq)
            dq_ref[rows, :] += _bdot(ds, k_ref[...])
            dfq_ref[rows, :] += jnp.broadcast_to(jnp.sum(ds, axis=1, keepdims=True), (tq, HD))

        @pl.when(qi > ki)
        def _():
            step(False)

        @pl.when(qi == ki)
        def _():
            step(True)

        @pl.when(qi == nq - 1)
        def _():
            dk_ref[...] = dk_s[...]
            dv_ref[...] = dv_s[...].astype(dv_ref.dtype)
            dfk_ref[...] = dfk_s[...]

    qspec, kspec, f0spec, fkspec = _att_specs(tq)
    head = pl.BlockSpec((T, HD), lambda h, p, qt, kt: (0, h))
    grid_spec = pltpu.PrefetchScalarGridSpec(
        num_scalar_prefetch=2, grid=(H, qt.shape[0]),
        in_specs=[qspec, kspec, kspec, f0spec, fkspec, qspec, qspec, qspec],
        out_specs=[head, head, kspec, kspec, fkspec],
        scratch_shapes=[pltpu.VMEM((tq, HD), F32), pltpu.VMEM((tq, HD), F32), pltpu.VMEM((1, tq), F32)])
    return pl.pallas_call(
        body, name=name, grid_spec=grid_spec,
        out_shape=[jax.ShapeDtypeStruct((T, H * HD), F32)] * 3
        + [jax.ShapeDtypeStruct((T, H * HD), BF16), jax.ShapeDtypeStruct((H, 1, T), F32)],
        compiler_params=_cp("parallel", "arbitrary"),
    )(qt, kt, qn, kn, vb, f0, fkr, dao, lse, delta)


CONV_TILES = (512, 256, 128, 64)
HALO = SUBLANES


def _dwconv(x, xoff, W, w, b, act, voff, out_dtype, name):
    T = x.shape[0]
    K = w.shape[0]
    tT, tC = _tile(T, CONV_TILES), _tile(W, CONV_TILES)
    xb, hb = xoff // tC, tT // HALO
    glu = act == 'glu'

    def body(*refs):
        if glu:
            x_ref, xp_ref, w_ref, b_ref, v_ref, o_ref, buf = refs
        else:
            x_ref, xp_ref, w_ref, o_ref, buf = refs
        i = pl.program_id(0)
        buf[0:HALO, :] = jnp.where(i > 0, xp_ref[...], 0.0)
        buf[HALO:, :] = x_ref[...]
        conv = w_ref[0:1, :] * buf[pl.ds(HALO - (K - 1), tT), :]
        for k in range(1, K):
            conv = conv + w_ref[k:k + 1, :] * buf[pl.ds(HALO - (K - 1) + k, tT), :]
        if glu:
            o_ref[...] = (_gelu(conv + b_ref[...]) * v_ref[...]).astype(o_ref.dtype)
        else:
            o_ref[...] = _silu(conv).astype(o_ref.dtype)

    cur = pl.BlockSpec((tT, tC), lambda i, j: (i, xb + j))
    prev = pl.BlockSpec((HALO, tC), lambda i, j: (jnp.maximum(i * hb - 1, 0), xb + j))
    wspec = pl.BlockSpec((K, tC), lambda i, j: (0, j))
    in_specs, args = [cur, prev, wspec], [x, x, w]
    if glu:
        vb = voff // tC
        in_specs += [pl.BlockSpec((1, tC), lambda i, j: (0, j)), pl.BlockSpec((tT, tC), lambda i, j: (i, vb + j))]
        args += [b, x]
    return pl.pallas_call(
        body, name=name, grid=(T // tT, W // tC), in_specs=in_specs,
        out_specs=pl.BlockSpec((tT, tC), lambda i, j: (i, j)), out_shape=jax.ShapeDtypeStruct((T, W), out_dtype),
        scratch_shapes=[pltpu.VMEM((tT + HALO, tC), F32)], compiler_params=_cp("parallel", "parallel"),
    )(*args)


def _dwconv_bwd(x, xoff, W, w, woff, b, act, voff, dy, name):
    T = x.shape[0]
    K = w.shape[0]
    tT, tC = _tile(T, CONV_TILES), _tile(W, CONV_TILES)
    xb, wb, hb, nT = xoff // tC, woff // tC, tT // HALO, T // tT
    last_halo = T // HALO - 1
    glu = act == 'glu'

    def body(*refs):
        if glu:
            (x_ref, xp_ref, xn_ref, dy_ref, dyn_ref, w_ref, b_ref, v_ref, vn_ref,
             dx_ref, dv_ref, dw_ref, db_ref, xbuf, dybuf, dbuf, vbuf) = refs
        else:
            x_ref, xp_ref, xn_ref, dy_ref, dyn_ref, w_ref, dx_ref, dw_ref, xbuf, dybuf, dbuf = refs
        i = pl.program_id(1)

        @pl.when(i == 0)
        def _():
            dw_ref[...] = jnp.zeros_like(dw_ref)
            if glu:
                db_ref[...] = jnp.zeros_like(db_ref)

        ext = tT + HALO
        xbuf[0:HALO, :] = jnp.where(i > 0, xp_ref[...], 0.0)
        xbuf[HALO:HALO + tT, :] = x_ref[...]
        xbuf[HALO + tT:, :] = xn_ref[...]
        dybuf[0:tT, :] = dy_ref[...].astype(F32)
        dybuf[tT:, :] = jnp.where(i < nT - 1, dyn_ref[...].astype(F32), 0.0)
        conv = w_ref[0:1, :] * xbuf[pl.ds(HALO - (K - 1), ext), :]
        for k in range(1, K):
            conv = conv + w_ref[k:k + 1, :] * xbuf[pl.ds(HALO - (K - 1) + k, ext), :]
        dyv = dybuf[...]
        if glu:
            vbuf[0:tT, :] = v_ref[...]
            vbuf[tT:, :] = vn_ref[...]
            z = conv + b_ref[...]
            cdf, pdf = _normal_cdf_pdf(z)
            dconv = dyv * vbuf[...] * (cdf + z * pdf)
            dv_ref[...] = (dyv[0:tT, :] * (z[0:tT, :] * cdf[0:tT, :])).astype(dv_ref.dtype)
        else:
            sg = _sigmoid(conv)
            dconv = dyv * (sg * (1.0 + conv * (1.0 - sg)))
        dbuf[...] = dconv
        dx = w_ref[0:1, :] * dbuf[pl.ds(K - 1, tT), :]
        for k in range(1, K):
            dx = dx + w_ref[k:k + 1, :] * dbuf[pl.ds(K - 1 - k, tT), :]
        dx_ref[...] = dx.astype(dx_ref.dtype)
        dc = dconv[0:tT, :]
        for k in range(K):
            dw_ref[k:k + 1, :] += jnp.sum(dc * xbuf[pl.ds(HALO - (K - 1) + k, tT), :], axis=0, keepdims=True)
        if glu:
            db_ref[...] += jnp.sum(dc, axis=0, keepdims=True)

    def cur(off):
        return pl.BlockSpec((tT, tC), lambda j, i: (i, off + j))

    def nxt(off):
        return pl.BlockSpec((HALO, tC), lambda j, i: (jnp.minimum((i + 1) * hb, last_halo), off + j))

    prev = pl.BlockSpec((HALO, tC), lambda j, i: (jnp.maximum(i * hb - 1, 0), xb + j))
    wspec = pl.BlockSpec((K, tC), lambda j, i: (0, wb + j))
    acc_w = pl.BlockSpec((K, tC), lambda j, i: (0, j))
    acc_b = pl.BlockSpec((1, tC), lambda j, i: (0, j))
    in_specs = [cur(xb), prev, nxt(xb), cur(0), nxt(0), wspec]
    args = [x, x, x, dy, dy, w]
    out_specs = [cur(0)]
    out_shape = [jax.ShapeDtypeStruct((T, W), BF16)]
    scratch = [pltpu.VMEM((tT + 2 * HALO, tC), F32), pltpu.VMEM((tT + HALO, tC), F32), pltpu.VMEM((tT + HALO, tC), F32)]
    if glu:
        vb = voff // tC
        in_specs += [pl.BlockSpec((1, tC), lambda j, i: (0, wb + j)), cur(vb), nxt(vb)]
        args += [b, x, x]
        out_specs += [cur(0), acc_w, acc_b]
        out_shape += [jax.ShapeDtypeStruct((T, W), BF16), jax.ShapeDtypeStruct((K, W), F32),
                      jax.ShapeDtypeStruct((1, W), F32)]
        scratch += [pltpu.VMEM((tT + HALO, tC), F32)]
    else:
        out_specs += [acc_w]
        out_shape += [jax.ShapeDtypeStruct((K, W), F32)]
    return pl.pallas_call(
        body, name=name, grid=(W // tC, nT), in_specs=in_specs, out_specs=out_specs, out_shape=out_shape,
        scratch_shapes=scratch, compiler_params=_cp("parallel", "arbitrary"),
    )(*args)


V_PER_K = 2


GDN_PREP_CHUNKS = 16
GDN_SCAN_CHUNKS = 4


def _b3(a, b, mode='nn', precision=None):
    c = {'nn': ((2,), (1,)), 'nt': ((2,), (2,)), 'tn': ((1,), (1,))}[mode]
    return lax.dot_general(a, b, (c, ((0,), (0,))), precision=precision, preferred_element_type=F32)


def _bb3(a, b, mode='nn'):
    return _b3(a.astype(BF16), b.astype(BF16), mode)


def _hb3(a, b, mode='nn'):
    return _b3(a, b, mode, precision=HIGHEST)


def _split_bf16(a):
    hi = a.astype(BF16)
    return hi, (a - hi.astype(F32)).astype(BF16)


def _nb3(a, b, mode='nn'):
    ah, al = _split_bf16(a)
    bh, bl = _split_bf16(b)
    return _b3(ah, bh, mode) + _b3(ah, bl, mode) + _b3(al, bh, mode)


def _to_batch(x, nc):
    C = GDN_CHUNK
    return jnp.concatenate([x[:, j * HD:(j + 1) * HD].reshape(nc, C, HD) for j in range(V_PER_K)], axis=0)


def _from_batch(x, nc):
    C = GDN_CHUNK
    return jnp.concatenate([x[j * nc:(j + 1) * nc].reshape(nc * C, HD) for j in range(V_PER_K)], axis=1)


def _both_heads(x, nc):
    xc = x.reshape(nc, GDN_CHUNK, HD)
    return jnp.concatenate([xc] * V_PER_K, axis=0)


def _gdn_local(q2, k2, gb, bb):
    B, C, _ = k2.shape
    ri = lax.broadcasted_iota(jnp.int32, (B, C, C), 1)
    ci = lax.broadcasted_iota(jnp.int32, (B, C, C), 2)
    lower, strict = ri >= ci, ri > ci
    pick0 = (lax.broadcasted_iota(jnp.int32, (B, C, HD), 2) == 0).astype(F32)
    g_cols = _hb3(pick0, gb, 'nt')
    dm = jnp.exp(jnp.where(lower, gb[:, :, :C] - g_cols, NEG))
    kk = _bb3(k2, k2, 'nt')
    a = jnp.where(strict, kk * dm * bb[:, :, :C], 0.0)
    eg = jnp.exp(gb)
    gl = gb[:, C - 1:C, :]
    return dict(lower=lower, strict=strict, eye=(ri == ci).astype(F32), dm=dm, kk=kk, a=a, eg=eg, gl=gl,
                qd=q2 * eg, kd=k2 * jnp.exp(gl - gb))


def _gdn_specs(T, Hk, voff, nc, rev=False):
    C = GDN_CHUNK
    nb = T // (nc * C)
    vb = voff // (V_PER_K * HD)
    ix = (lambda i: nb - 1 - i) if rev else (lambda i: i)
    kspec = pl.BlockSpec((nc * C, HD), lambda h, i: (ix(i), h))
    pair = pl.BlockSpec((nc * C, V_PER_K * HD), lambda h, i: (ix(i), h))
    vspec = pl.BlockSpec((nc * C, V_PER_K * HD), lambda h, i: (ix(i), vb + h))
    cc = pl.BlockSpec((V_PER_K, nc, C, C), lambda h, i: (h, ix(i), 0, 0))
    state = pl.BlockSpec((V_PER_K, nc, HD, HD), lambda h, i: (h, ix(i), 0, 0))
    scal = pl.BlockSpec((V_PER_K, nc, SUBLANES, HD), lambda h, i: (h, ix(i), 0, 0))
    return nb, kspec, pair, vspec, cc, state, scal


def _gdn_prep(qn, kn, qkvc, voff, gcb, betab, Hk, name):
    T = qn.shape[0]
    C, nc = GDN_CHUNK, GDN_PREP_CHUNKS
    Hv, N = Hk * V_PER_K, T // C
    nb, kspec, pair, vspec, cc, _, _ = _gdn_specs(T, Hk, voff, nc)

    def body(q_ref, k_ref, v_ref, g_ref, b_ref, u_ref, w_ref, tm_ref, qkm_ref):
        q2, k2 = _both_heads(q_ref[...], nc), _both_heads(k_ref[...], nc)
        v2, gb, bb = _to_batch(v_ref[...], nc), _to_batch(g_ref[...], nc), _to_batch(b_ref[...], nc)
        lc = _gdn_local(q2, k2, gb, bb)
        p = -lc['a']
        tm = lc['eye'] + p
        for _ in range(5):
            p = _nb3(p, p)
            tm = tm + _nb3(tm, p)
        u_ref[...] = _from_batch(_nb3(tm, v2 * bb), nc)
        w_ref[...] = _from_batch(_nb3(tm, k2 * (bb * lc['eg'])), nc)
        tm_ref[...] = tm.reshape(V_PER_K, nc, C, C)
        qkm_ref[...] = jnp.where(lc['lower'], _bb3(q2, k2, 'nt') * lc['dm'], 0.0).reshape(V_PER_K, nc, C, C)

    return pl.pallas_call(
        body, name=name, grid=(Hk, nb), in_specs=[kspec, kspec, vspec, pair, pair], out_specs=[pair, pair, cc, cc],
        out_shape=[jax.ShapeDtypeStruct((T, Hv * HD), F32)] * 2 + [jax.ShapeDtypeStruct((Hv, N, C, C), F32)] * 2,
        compiler_params=_cp("parallel", "parallel"),
    )(qn, kn, qkvc, gcb, betab)


SCAN_K_HEADS = 4
SCAN_V_HEADS = SCAN_K_HEADS * V_PER_K


def _heads(ref, rows, per=1):
    return jnp.stack([ref[rows, (j // per) * HD:(j // per + 1) * HD] for j in range(SCAN_V_HEADS)])


def _put_heads(ref, rows, x):
    for j in range(SCAN_V_HEADS):
        ref[rows, j * HD:(j + 1) * HD] = x[j]


def _scan_chunk(q_ref, k_ref, g_ref, rows):
    C = GDN_CHUNK
    gb = _heads(g_ref, rows)
    gl = gb[:, C - 1:C, :]
    return _heads(q_ref, rows, V_PER_K) * jnp.exp(gb), _heads(k_ref, rows, V_PER_K) * jnp.exp(gl - gb), jnp.exp(gl)


def _scan_specs(T, ns, rev=False):
    C = GDN_CHUNK
    nb = T // (ns * C)
    ix = (lambda i: nb - 1 - i) if rev else (lambda i: i)
    kspec = pl.BlockSpec((ns * C, SCAN_K_HEADS * HD), lambda h, i: (ix(i), h))
    vspec = pl.BlockSpec((ns * C, SCAN_V_HEADS * HD), lambda h, i: (ix(i), h))
    per_chunk = lambda *tail: pl.BlockSpec((SCAN_V_HEADS, ns) + tail, lambda h, i: (h, ix(i), 0, 0))
    return nb, kspec, vspec, per_chunk(C, C), per_chunk(HD, HD), per_chunk(SUBLANES, HD)


def _gdn_scan(qn, kn, gcb, u, w, qkm, Hk, name):
    T = qn.shape[0]
    C, ns = GDN_CHUNK, GDN_SCAN_CHUNKS
    Hv, N = Hk * V_PER_K, T // C
    nb, kspec, pair, cc, state, _ = _scan_specs(T, ns)

    def body(q_ref, k_ref, g_ref, u_ref, w_ref, qkm_ref, o_ref, sp_ref, s_s):
        @pl.when(pl.program_id(1) == 0)
        def _():
            s_s[...] = jnp.zeros_like(s_s)

        s = s_s[...]
        for t in range(ns):
            rows = slice(t * C, (t + 1) * C)
            qd, kd, egl = _scan_chunk(q_ref, k_ref, g_ref, rows)
            sp_ref[:, t] = s
            vn = _heads(u_ref, rows) - _bb3(_heads(w_ref, rows), s)
            _put_heads(o_ref, rows, _bb3(qd, s) + _bb3(qkm_ref[:, t], vn))
            s = s * egl + _bb3(kd, vn, 'tn')
        s_s[...] = s

    return pl.pallas_call(
        body, name=name, grid=(Hk // SCAN_K_HEADS, nb), in_specs=[kspec, kspec, pair, pair, pair, cc],
        out_specs=[pair, state],
        out_shape=[jax.ShapeDtypeStruct((T, Hv * HD), F32), jax.ShapeDtypeStruct((Hv, N, HD, HD), F32)],
        scratch_shapes=[pltpu.VMEM((SCAN_V_HEADS, HD, HD), F32)], compiler_params=_cp("parallel", "arbitrary"),
    )(qn, kn, gcb, u, w, qkm)


def _gdn_scan_bwd(qn, kn, gcb, u, w, qkm, sprev, do, Hk, name):
    T = qn.shape[0]
    C, ns = GDN_CHUNK, GDN_SCAN_CHUNKS
    Hv, N = Hk * V_PER_K, T // C
    nb, kspec, pair, cc, state, scal = _scan_specs(T, ns, rev=True)
    nv = SCAN_V_HEADS

    def body(q_ref, k_ref, g_ref, u_ref, w_ref, qkm_ref, sp_ref, do_ref,
             dqd_ref, dkd_ref, du_ref, dw_ref, dqkm_ref, dgl_ref, ds_s):
        @pl.when(pl.program_id(1) == 0)
        def _():
            ds_s[...] = jnp.zeros_like(ds_s)

        lower = lax.broadcasted_iota(jnp.int32, (nv, C, C), 1) >= lax.broadcasted_iota(jnp.int32, (nv, C, C), 2)
        ds = ds_s[...]
        for t in reversed(range(ns)):
            rows = slice(t * C, (t + 1) * C)
            qd, kd, egl = _scan_chunk(q_ref, k_ref, g_ref, rows)
            s, w_, qkm_, dov = sp_ref[:, t], _heads(w_ref, rows), qkm_ref[:, t], _heads(do_ref, rows)
            vn = _heads(u_ref, rows) - _bb3(w_, s)
            _put_heads(dqd_ref, rows, _bb3(dov, s, 'nt'))
            dqkm_ref[:, t] = jnp.where(lower, _bb3(dov, vn, 'nt'), 0.0)
            dvn = _bb3(qkm_, dov, 'tn') + _bb3(kd, ds)
            _put_heads(dkd_ref, rows, _bb3(vn, ds, 'nt'))
            dgl = jnp.sum(jnp.sum(ds * s, axis=1, keepdims=True) * egl, axis=2, keepdims=True)
            dgl_ref[:, t] = jnp.broadcast_to(dgl, (nv, SUBLANES, HD))
            _put_heads(du_ref, rows, dvn)
            _put_heads(dw_ref, rows, -_bb3(dvn, s, 'nt'))
            ds = ds * egl + _bb3(qd, dov, 'tn') - _bb3(w_, dvn, 'tn')
        ds_s[...] = ds

    return pl.pallas_call(
        body, name=name, grid=(Hk // SCAN_K_HEADS, nb), in_specs=[kspec, kspec, pair, pair, pair, cc, state, pair],
        out_specs=[pair] * 4 + [cc, scal],
        out_shape=[jax.ShapeDtypeStruct((T, Hv * HD), F32)] * 4
        + [jax.ShapeDtypeStruct((Hv, N, C, C), F32), jax.ShapeDtypeStruct((Hv, N, SUBLANES, HD), F32)],
        scratch_shapes=[pltpu.VMEM((nv, HD, HD), F32)], compiler_params=_cp("parallel", "arbitrary"),
    )(qn, kn, gcb, u, w, qkm, sprev, do)


def _gdn_prep_bwd(qn, kn, qkvc, voff, gcb, betab, tm, u, w, qkm, dqd, dkd, du, dw, dqkm, dgl, Hk, name):
    T = qn.shape[0]
    C, nc = GDN_CHUNK, GDN_PREP_CHUNKS
    Hv = Hk * V_PER_K
    B = V_PER_K * nc
    nb, kspec, pair, vspec, cc, _, scal = _gdn_specs(T, Hk, voff, nc)

    def body(q_ref, k_ref, v_ref, g_ref, b_ref, tm_ref, u_ref, w_ref, qkm_ref, dqd_ref, dkd_ref, du_ref, dw_ref,
             dqkm_ref, dgl_ref, dq_ref, dk_ref, dv_ref, dg_ref, dbe_ref):
        q2, k2 = _both_heads(q_ref[...], nc), _both_heads(k_ref[...], nc)
        v2, gb, bb = _to_batch(v_ref[...], nc), _to_batch(g_ref[...], nc), _to_batch(b_ref[...], nc)
        lc = _gdn_local(q2, k2, gb, bb)
        dm, eg, gl = lc['dm'], lc['eg'], lc['gl']
        tm_, qkm_, dqkm_ = (r[...].reshape(B, C, C) for r in (tm_ref, qkm_ref, dqkm_ref))
        u_, w_, dqd_, dkd_, du_, dw_ = (_to_batch(r[...], nc) for r in (u_ref, w_ref, dqd_ref, dkd_ref, du_ref, dw_ref))
        dgl_ = dgl_ref[...].reshape(B, SUBLANES, HD)[:, :1, :1]
        rowsum = lambda x: jnp.sum(x, axis=-1, keepdims=True)
        dbv = _nb3(tm_, du_, 'tn')
        dbk = _nb3(tm_, dw_, 'tn')
        da = jnp.where(lc['strict'], -(_bb3(dbv, u_, 'nt') + _bb3(dbk, w_, 'nt')), 0.0)
        rk = rowsum(dbk * k2)
        dbeta = rowsum(dbv * v2) + rk * eg[:, :, :1] + rowsum(da * lc['kk'] * dm)
        dkk = da * dm * bb[:, :, :C]
        dqkr = dqkm_ * dm
        dk = dbk * (bb * eg) + _bb3(dkk, k2) + _bb3(dkk, k2, 'tn') + _bb3(dqkr, q2, 'tn') + dkd_ * jnp.exp(gl - gb)
        dq = _bb3(dqkr, k2) + dqd_ * eg
        de = da * lc['a'] + dqkm_ * qkm_
        sk = rowsum(dkd_ * lc['kd'])
        dg = rk * (bb[:, :, :1] * eg[:, :, :1]) + rowsum(de) - _hb3(de, jnp.ones((B, C, HD), F32), 'tn')[:, :, :1] \
            + rowsum(dqd_ * lc['qd']) - sk
        last = (lax.broadcasted_iota(jnp.int32, (B, C, HD), 1) == C - 1).astype(F32)
        dgb = jnp.broadcast_to(dg, (B, C, HD)) + last * (dgl_ + jnp.sum(sk, axis=1, keepdims=True))
        suffix = (lax.broadcasted_iota(jnp.int32, (B, C, C), 2) >= lax.broadcasted_iota(jnp.int32, (B, C, C), 1)).astype(F32)
        dq_ref[...] = _from_batch(dq, nc)
        dk_ref[...] = _from_batch(dk, nc)
        dv_ref[...] = _from_batch(dbv * bb, nc)
        dg_ref[...] = _from_batch(_hb3(suffix, dgb), nc)
        dbe_ref[...] = _from_batch(jnp.broadcast_to(dbeta, (B, C, HD)), nc)

    return pl.pallas_call(
        body, name=name, grid=(Hk, nb),
        in_specs=[kspec, kspec, vspec, pair, pair, cc, pair, pair, cc, pair, pair, pair, pair, cc, scal],
        out_specs=[pair] * 5, out_shape=[jax.ShapeDtypeStruct((T, Hv * HD), F32)] * 5,
        compiler_params=_cp("parallel", "parallel"),
    )(qn, kn, qkvc, gcb, betab, tm, u, w, qkm, dqd, dkd, du, dw, dqkm, dgl)


def _adamw_math(w, g, m, v):
    m = ADAM_B1 * m + (1.0 - ADAM_B1) * g
    v = ADAM_B2 * v + (1.0 - ADAM_B2) * jnp.square(g)
    m_hat = m / (1.0 - ADAM_B1 ** ADAM_STEP)
    v_hat = v / (1.0 - ADAM_B2 ** ADAM_STEP)
    delta = -ADAM_LR * (m_hat / (jnp.sqrt(v_hat) + ADAM_EPS) + ADAM_WD * w)
    return delta, m, v


STREAM_BLOCK_BYTES = 2 << 20


def _stream_rows(R, C, mult=SUBLANES):
    for tr in (512, 256, 128, 64, 32, 16, 8):
        if R % tr == 0 and tr % mult == 0 and tr * C * 4 <= STREAM_BLOCK_BYTES:
            return tr
    return R


def _adamw(w, m, v, gs, name):
    R, C = w.shape
    tr = _stream_rows(R, C)
    n = len(gs)

    def body(*refs):
        w_ref, m_ref, v_ref = refs[:3]
        g_refs = refs[3:3 + n]
        g_out, d_out, m_out, v_out = refs[3 + n:]
        g = g_refs[0][...]
        for r_ in g_refs[1:]:
            g = g + r_[...]
        g_out[...] = g
        d_out[...], m_out[...], v_out[...] = _adamw_math(w_ref[...], g, m_ref[...], v_ref[...])

    spec = pl.BlockSpec((tr, C), lambda i: (i, 0))
    return pl.pallas_call(
        body, name=name, grid=(R // tr,), in_specs=[spec] * (3 + n), out_specs=[spec] * 4,
        out_shape=[jax.ShapeDtypeStruct((R, C), F32)] * 4, compiler_params=_cp("parallel"),
    )(w, m, v, *gs)


def _sum_devices(g8, name):
    _, M, C = g8.shape
    tr = _tile(M, (512, 256, 128, 64, 32, 16, 8))

    def body(g_ref, o_ref):
        acc = g_ref[0]
        for d in range(1, N_DEV):
            acc = acc + g_ref[d]
        o_ref[...] = acc

    return pl.pallas_call(
        body, name=name, grid=(M // tr,), in_specs=[pl.BlockSpec((N_DEV, tr, C), lambda i: (0, i, 0))],
        out_specs=pl.BlockSpec((tr, C), lambda i: (i, 0)), out_shape=jax.ShapeDtypeStruct((M, C), F32),
        compiler_params=_cp("parallel"),
    )(g8)


def _ada_w_update(c_all, dm, w, m, v, name):
    n_mod, D, Ns = w.shape
    tr = _tile(D, (256, 128))

    def body(c_ref, dm_ref, w_ref, m_ref, v_ref, g_out, d_out, m_out, v_out):
        g = _hdot(_silu(c_ref[...]), dm_ref[...], 'tn')
        g_out[...] = g
        d_out[...], m_out[...], v_out[...] = _adamw_math(w_ref[...], g, m_ref[...], v_ref[...])

    wspec = pl.BlockSpec((None, tr, Ns), lambda i, r: (i, r, 0))
    return pl.pallas_call(
        body, name=name, grid=(n_mod, D // tr),
        in_specs=[pl.BlockSpec((N_DEV, tr), lambda i, r: (0, r)), pl.BlockSpec((None, N_DEV, Ns), lambda i, r: (i, 0, 0)),
                  wspec, wspec, wspec],
        out_specs=[wspec] * 4, out_shape=[jax.ShapeDtypeStruct((n_mod, D, Ns), F32)] * 4,
        compiler_params=_cp("parallel", "parallel"),
    )(c_all, dm, w, m, v)


def _place():
    return lax.axis_index("x"), lax.axis_index("y"), lax.axis_index("c")


def _allgather8(x_shard, name):
    m_per, n = x_shard.shape

    def body(x_ref, out_ref, send_sems, recv_sems, local_sem):
        x, y, c = _place()
        me, sibling = (x, y, c), (x, y, 1 - c)
        chips = [(1 - x, y), (x, 1 - y), (1 - x, 1 - y)]

        def rows(px, py, pc):
            return out_ref.at[pl.ds((4 * px + 2 * py + pc) * m_per, m_per), :]

        def copy(k, block, to, src=None):
            return pltpu.make_async_remote_copy(
                src_ref=rows(*block) if src is None else src, dst_ref=rows(*block),
                send_sem=send_sems.at[k], recv_sem=recv_sems.at[k], device_id=to, device_id_type=MESH)

        mine = pltpu.make_async_copy(x_ref, rows(*me), local_sem)
        mine.start()
        first = [copy(0, me, sibling, src=x_ref)]
        first += [copy(1 + j, me, (*chip, c), src=x_ref) for j, chip in enumerate(chips)]
        for cp in first:
            cp.start()
        passed = [copy(4 + j, (*chip, c), sibling) for j, chip in enumerate(chips)]
        for j, chip in enumerate(chips):
            copy(1 + j, (*chip, c), me).wait_recv()
            passed[j].start()
        copy(0, sibling, me).wait_recv()
        for j, chip in enumerate(chips):
            copy(4 + j, (*chip, 1 - c), me).wait_recv()
        for cp in first + passed:
            cp.wait_send()
        mine.wait()

    return pl.pallas_call(
        body, name=name, out_shape=jax.ShapeDtypeStruct((N_DEV * m_per, n), x_shard.dtype),
        in_specs=[pl.BlockSpec(memory_space=pltpu.VMEM)], out_specs=pl.BlockSpec(memory_space=pltpu.VMEM),
        scratch_shapes=[pltpu.SemaphoreType.DMA((7,)), pltpu.SemaphoreType.DMA((7,)), pltpu.SemaphoreType.DMA],
        compiler_params=pltpu.CompilerParams(vmem_limit_bytes=VMEM_LIMIT),
    )(x_shard)


HBM_SPEC = pl.BlockSpec(memory_space=pltpu.HBM)


GATHER_SEMS = 6


def _gather_protocol(w_ref, out_ref, send_sems, recv_sems, k0):
    half = w_ref.shape[0] // 2
    x, y, c = _place()
    me, sibling = (x, y, c), (x, y, 1 - c)
    chips = [(1 - x, y), (x, 1 - y), (1 - x, 1 - y)]

    def part(cx, cy, hc):
        return out_ref.at[2 * cx + cy, pl.ds(hc * half, half), :]

    def copy(k, block, to, src=None):
        return pltpu.make_async_remote_copy(
            src_ref=part(*block) if src is None else src, dst_ref=part(*block),
            send_sem=send_sems.at[k0 + k], recv_sem=recv_sems.at[k0 + k], device_id=to, device_id_type=MESH)

    def first():
        return [copy(j, me, (*chip, c), src=w_ref.at[pl.ds(c * half, half), :]) for j, chip in enumerate(chips)]

    def start():
        for cp in first():
            cp.start()

    def finish():
        passed = [copy(3 + j, (*chip, c), sibling) for j, chip in enumerate(chips)]
        for j, chip in enumerate(chips):
            copy(j, (*chip, c), me).wait_recv()
            passed[j].start()
        for j, chip in enumerate(chips):
            copy(3 + j, (*chip, 1 - c), me).wait_recv()
        for cp in first() + passed:
            cp.wait_send()

    return start, finish


def _gather_weights(w_flat, name):
    R, C = w_flat.shape

    def body(w_ref, out_ref, send_sems, recv_sems):
        start, finish = _gather_protocol(w_ref, out_ref, send_sems, recv_sems, 0)
        start()
        finish()

    return pl.pallas_call(
        body, name=name, out_shape=jax.ShapeDtypeStruct((N_CHIPS, R, C), w_flat.dtype),
        in_specs=[HBM_SPEC], out_specs=HBM_SPEC,
        scratch_shapes=[pltpu.SemaphoreType.DMA((GATHER_SEMS,)), pltpu.SemaphoreType.DMA((GATHER_SEMS,))],
    )(w_flat)


def _swap_halves(g, name):
    n, R, C = g.shape
    half = R // 2

    def body(g_ref, got_ref, send_sem, recv_sem):
        x, y, c = _place()
        cp = pltpu.make_async_remote_copy(
            src_ref=g_ref.at[:, pl.ds((1 - c) * half, half), :], dst_ref=got_ref,
            send_sem=send_sem, recv_sem=recv_sem, device_id=(x, y, 1 - c), device_id_type=MESH)
        cp.start()
        cp.wait()

    return pl.pallas_call(
        body, name=name, out_shape=jax.ShapeDtypeStruct((n, half, C), g.dtype),
        in_specs=[HBM_SPEC], out_specs=HBM_SPEC,
        scratch_shapes=[pltpu.SemaphoreType.DMA, pltpu.SemaphoreType.DMA],
    )(g)


SCATTER_SEMS = 3


def _scatter_protocol(q_ref, got_ref, send_sems, recv_sems, k0):
    x, y, c = _place()

    def copies():
        return [pltpu.make_async_remote_copy(
            src_ref=q_ref.at[2 * cx + cy], dst_ref=got_ref.at[j], send_sem=send_sems.at[k0 + j],
            recv_sem=recv_sems.at[k0 + j], device_id=(cx, cy, c), device_id_type=MESH)
            for j, (cx, cy) in enumerate([(1 - x, y), (x, 1 - y), (1 - x, 1 - y)])]

    def start():
        for cp in copies():
            cp.start()

    def finish():
        for cp in copies():
            cp.wait()

    return start, finish


def _scatter_shapes(qs):
    return [jax.ShapeDtypeStruct((N_CHIPS - 1,) + q.shape[1:], q.dtype) for q in qs]


def _join_halves(h, name):
    R, C = h.shape
    R2 = R // 2

    def body(h_ref, out_ref, send_sem, recv_sem):
        x, y, c = _place()
        cp = pltpu.make_async_remote_copy(
            src_ref=h_ref.at[pl.ds(c * R2, R2), :], dst_ref=out_ref.at[pl.ds(c * R2, R2), :],
            send_sem=send_sem, recv_sem=recv_sem, device_id=(x, y, 1 - c), device_id_type=MESH)
        cp.start()
        cp.wait()

    return pl.pallas_call(
        body, name=name, out_shape=jax.ShapeDtypeStruct((R, C), h.dtype),
        in_specs=[HBM_SPEC], out_specs=HBM_SPEC, input_output_aliases={0: 0},
        scratch_shapes=[pltpu.SemaphoreType.DMA, pltpu.SemaphoreType.DMA],
    )(h)


def _add_halves(g, got, c_idx, name):
    n, R, C = g.shape
    half = R // 2
    tr = _stream_rows(half, C, 2 * SUBLANES)
    nb = half // tr

    def body(c_ref, g_ref, got_ref, o_ref):
        o_ref[...] = (g_ref[...] + got_ref[...]).astype(o_ref.dtype)

    grid_spec = pltpu.PrefetchScalarGridSpec(
        num_scalar_prefetch=1, grid=(n, nb),
        in_specs=[pl.BlockSpec((None, tr, C), lambda s, i, c_ref: (s, c_ref[0] * nb + i, 0)),
                  pl.BlockSpec((None, tr, C), lambda s, i, c_ref: (s, i, 0))],
        out_specs=pl.BlockSpec((None, tr, C), lambda s, i, c_ref: (s, i, 0)))
    return pl.pallas_call(
        body, name=name, grid_spec=grid_spec, out_shape=jax.ShapeDtypeStruct((n, half, C), BF16),
        compiler_params=_cp("parallel", "parallel"),
    )(c_idx, g, got)


def _add_chips(q, got, sc_idx, name):
    n, R2, C = q.shape
    tr = _stream_rows(R2, C, 2 * SUBLANES)
    nb = R2 // tr

    def body(s_ref, q_ref, g0_ref, g1_ref, g2_ref, o_ref):
        o_ref[...] = ((q_ref[...].astype(F32) + g0_ref[...].astype(F32)) + g1_ref[...].astype(F32)) \
            + g2_ref[...].astype(F32)

    def got_spec(j):
        return pl.BlockSpec((None, tr, C), lambda i, s_ref: (j, i, 0))

    grid_spec = pltpu.PrefetchScalarGridSpec(
        num_scalar_prefetch=1, grid=(nb,),
        in_specs=[pl.BlockSpec((None, tr, C), lambda i, s_ref: (s_ref[0], i, 0)), got_spec(0), got_spec(1), got_spec(2)],
        out_specs=pl.BlockSpec((tr, C), lambda i, s_ref: (s_ref[1] * nb + i, 0)))
    return pl.pallas_call(
        body, name=name, grid_spec=grid_spec, out_shape=jax.ShapeDtypeStruct((2 * R2, C), F32),
        compiler_params=_cp("parallel"),
    )(sc_idx, q, got, got, got)


def _pack_lanes(arrs):
    rows = []
    for a in arrs:
        f = a.reshape(-1)
        n = -(-f.shape[0] // LANES) * LANES
        rows.append(jnp.pad(f, (0, n - f.shape[0])).reshape(-1, LANES))
    out = jnp.concatenate(rows, axis=0)
    pad = -out.shape[0] % SUBLANES
    return jnp.pad(out, ((0, pad), (0, 0)))


def _unpack_lanes(packed, shapes):
    out, r = [], 0
    for shp in shapes:
        n = math.prod(shp)
        nr = -(-n // LANES)
        out.append(packed[r:r + nr].reshape(-1)[:n].reshape(shp))
        r += nr
    return out


def _shards_to_full(sh, axis):
    return jnp.concatenate([sh[i] for i in range(N_CHIPS)], axis=axis)


def _full_to_shards(full, axis):
    return jnp.stack(jnp.split(full, N_CHIPS, axis=axis), axis=0)


def _pad_cols(a, n):
    return jnp.pad(a, ((0, 0), (0, n - a.shape[1])))


def _split_mod(mod):
    D = mod.shape[0] // 3
    return mod[None, :D], mod[None, D:2 * D], mod[None, 2 * D:]


def _fox_fwd(h, w, tag, late=None):
    T, D = h.shape
    H = D // HD
    proj = _mm(h, w['cat'], 'nn', F32, tag + '_proj')
    flog = _mm(h, w['f'], 'nn', F32, tag + '_flog')
    qn = _headnorm(proj, 0, H, w['q_norm'], 1.0 / HD, HD ** -0.5, BF16, tag + '_qnorm')
    kn = _headnorm(proj, H, H, w['k_norm'], 1.0 / HD, 1.0, BF16, tag + '_knorm')
    vb = proj[:, 2 * D:3 * D].astype(BF16)
    fcum = _cumsum(_logsig(flog, w['f_bias'], tag + '_logf'), None, False, tag + '_fcum')
    tq = _tile(T, ATT_TILES)
    f0 = fcum[::tq, :H].T.reshape(H, T // tq, 1, 1)
    fkr = fcum[:, :H].T.reshape(H, 1, T)
    ao, lse, gated, carried = _flash_fwd(qn, kn, vb, f0, fkr, proj, 3 * H, H, tag + '_att', late[0] if late else ())
    if late:
        w['o'], rest = late[1](carried)
    y = _mm(gated, w['o'], 'nn', F32, tag + '_out')
    sv = dict(h=h, proj=proj, flog=flog, qn=qn, kn=kn, vb=vb, f0=f0, fkr=fkr, ao=ao, lse=lse, gated=gated)
    return (y, sv, rest) if late else (y, sv)


def _mm_carry(a, b, mode, out_dtype, name, reduce_pairs, tag, grads):
    if reduce_pairs is None:
        return _mm(a, b, mode, out_dtype, name), {}
    pairs = reduce_pairs(tag, grads)
    out, got = _mm(a, b, mode, out_dtype, name, scatter=tuple(q for _, q in pairs))
    return out, {key: (q, r) for (key, q), r in zip(pairs, got)}


def _fox_bwd(dy, w, sv, tag, reduce_pairs=None):
    h, proj = sv['h'], sv['proj']
    T, D = h.shape
    H = D // HD
    g = {}
    g['o'] = _mm(sv['gated'], dy, 'tn', F32, tag + '_dwo')
    dgated = _mm(dy, w['o'], 'nt', F32, tag + '_dgated')
    dao, dog, delta = _fox_gate_bwd(dgated, sv['ao'], proj, 3 * H, H, tag + '_ogate_bwd')
    dq, dfq, dk, dv, dfk = _flash_bwd(sv['qn'], sv['kn'], sv['vb'], sv['f0'], sv['fkr'], dao, sv['lse'], delta, H,
                                      tag + '_att_bwd')
    dfcum = _pad_cols(dfq[:, ::HD] + dfk.reshape(H, T).T, LANES)
    dlogf = _cumsum(dfcum, None, True, tag + '_fcum_bwd')
    dflog, g['f_bias'] = _logsig_bwd(dlogf, sv['flog'], w['f_bias'], tag + '_logf_bwd')
    dqr, g['q_norm'] = _headnorm_bwd([(dq, 1, 0)], proj, 0, H, w['q_norm'], 1.0 / HD, HD ** -0.5, BF16,
                                     tag + '_qnorm_bwd')
    dkr, g['k_norm'] = _headnorm_bwd([(dk, 1, 0)], proj, H, H, w['k_norm'], 1.0 / HD, 1.0, BF16, tag + '_knorm_bwd')
    dproj = jnp.concatenate([dqr, dkr, dv, dog], axis=1)
    g['cat'] = _mm(h, dproj, 'tn', F32, tag + '_dwcat')
    g['f'] = _mm(h, dflog, 'tn', F32, tag + '_dwf')
    dh, exchanged = _mm_carry(dproj, w['cat'], 'nt', F32, tag + '_dh', reduce_pairs, tag, g)
    return [dh, _mm(dflog, w['f'], 'nt', F32, tag + '_dh_f')], g, exchanged


def _gdn_fwd(h, w, tag):
    T, D = h.shape
    Hk = D // HD
    Hv = V_PER_K * Hk
    proj = _mm(h, w['cat'], 'nn', F32, tag + '_proj')
    ab = _mm(h, w['ab'], 'nn', F32, tag + '_ab')
    qkvc = _dwconv(proj, 0, 4 * D, w['conv'], None, 'silu', 0, F32, tag + '_conv')
    qn = _headnorm(qkvc, 0, Hk, None, 1.0, HD ** -0.5, F32, tag + '_qnorm')
    kn = _headnorm(qkvc, Hk, Hk, None, 1.0, 1.0, F32, tag + '_knorm')
    graw, beta = _gdn_gates(ab, w['a_log'], w['dt_bias'], tag + '_gates')
    gc = _cumsum(graw, GDN_CHUNK, False, tag + '_gcum')
    gcb, betab = _spread_heads([gc, beta], [0, Hv], Hv, tag + '_spread')
    u, wk, tm, qkm = _gdn_prep(qn, kn, qkvc, 2 * D, gcb, betab, Hk, tag + '_prep')
    o, sprev = _gdn_scan(qn, kn, gcb, u, wk, qkm, Hk, tag + '_scan')
    go = _gdn_out(o, proj, 4 * Hk, Hv, w['out_norm'], tag + '_onorm')
    y = _mm(go, w['o'], 'nn', F32, tag + '_out')
    return y, dict(h=h, proj=proj, ab=ab, qkvc=qkvc, qn=qn, kn=kn, gcb=gcb, betab=betab, o=o, sprev=sprev, go=go,
                   u=u, wk=wk, tm=tm, qkm=qkm)


def _gdn_bwd(dy, w, sv, tag, reduce_pairs=None):
    h, proj, qkvc = sv['h'], sv['proj'], sv['qkvc']
    T, D = h.shape
    Hk = D // HD
    Hv = V_PER_K * Hk
    g = {}
    g['o'] = _mm(sv['go'], dy, 'tn', F32, tag + '_dwo')
    dgo = _mm(dy, w['o'], 'nt', F32, tag + '_dgo')
    do, dz, g['out_norm'] = _gdn_out_bwd(dgo, sv['o'], proj, 4 * Hk, Hv, w['out_norm'], tag + '_onorm_bwd')
    local = (sv['u'], sv['wk'], sv['qkm'])
    dqd, dkd, du, dwk, dqkm, dgl = _gdn_scan_bwd(sv['qn'], sv['kn'], sv['gcb'], *local, sv['sprev'], do, Hk,
                                                 tag + '_scan_bwd')
    dqp, dkp, dv, dgb, dbetab = _gdn_prep_bwd(sv['qn'], sv['kn'], qkvc, 2 * D, sv['gcb'], sv['betab'], sv['tm'], *local,
                                              dqd, dkd, du, dwk, dqkm, dgl, Hk, tag + '_prep_bwd')
    pairs = lambda a: [(a, V_PER_K, j) for j in range(V_PER_K)]
    dqc, _ = _headnorm_bwd(pairs(dqp), qkvc, 0, Hk, None, 1.0, HD ** -0.5, F32, tag + '_qnorm_bwd')
    dkc, _ = _headnorm_bwd(pairs(dkp), qkvc, Hk, Hk, None, 1.0, 1.0, F32, tag + '_knorm_bwd')
    zeros = jnp.zeros((T, Hv), F32)
    dg_pad = _pad_cols(dgb[:, ::HD], LANES)
    dbeta_pad = _pad_cols(jnp.concatenate([zeros, dbetab[:, ::HD]], axis=1), LANES)
    dab, g['a_log'], g['dt_bias'] = _gdn_gates_bwd(dg_pad, dbeta_pad, sv['ab'], w['a_log'], w['dt_bias'], tag + '_gates_bwd')
    dpq, dwq = _dwconv_bwd(proj, 0, D, w['conv'], 0, None, 'silu', 0, dqc, tag + '_conv_bwd_q')
    dpk, dwk = _dwconv_bwd(proj, D, D, w['conv'], D, None, 'silu', 0, dkc, tag + '_conv_bwd_k')
    dpv, dwv = _dwconv_bwd(proj, 2 * D, 2 * D, w['conv'], 2 * D, None, 'silu', 0, dv, tag + '_conv_bwd_v')
    g['conv'] = jnp.concatenate([dwq, dwk, dwv], axis=1)
    dproj = jnp.concatenate([dpq, dpk, dpv, dz], axis=1)
    g['cat'] = _mm(h, dproj, 'tn', F32, tag + '_dwcat')
    g['ab'] = _mm(h, dab, 'tn', F32, tag + '_dwab')
    dh, exchanged = _mm_carry(dproj, w['cat'], 'nt', F32, tag + '_dh', reduce_pairs, tag, g)
    return [dh, _mm(dab, w['ab'], 'nt', F32, tag + '_dh_ab')], g, exchanged


def _ffn_fwd(h, w, tag):
    dff = w['down'].shape[0]
    up = _mm(h, w['up'], 'nn', F32, tag + '_up')
    act = _dwconv(up, 0, dff, w['conv'], w['conv_b'], 'glu', dff, BF16, tag + '_conv')
    y = _mm(act, w['down'], 'nn', F32, tag + '_down')
    return y, dict(h=h, up=up, act=act)


def _ffn_bwd(dy, w, sv, tag, reduce_pairs=None):
    h, up = sv['h'], sv['up']
    dff = w['down'].shape[0]
    g = {}
    g['down'] = _mm(sv['act'], dy, 'tn', F32, tag + '_dwdown')
    dact = _mm(dy, w['down'], 'nt', F32, tag + '_dact')
    dgate, dval, g['conv'], g['conv_b'] = _dwconv_bwd(up, 0, dff, w['conv'], 0, w['conv_b'], 'glu', dff, dact,
                                                      tag + '_conv_bwd')
    dup = jnp.concatenate([dgate, dval], axis=1)
    g['up'] = _mm(h, dup, 'tn', F32, tag + '_dwup')
    dh, exchanged = _mm_carry(dup, w['up'], 'nt', F32, tag + '_dh', reduce_pairs, tag, g)
    return [dh], g, exchanged


def _local_step(x, target, mods, norm_g, wf, wg, wffn, late=None, reduce_pairs=None):
    tape = []
    for i in range(2):
        for sub in range(2):
            if sub == 0:
                fwd, bwd, w, tag = [(_fox_fwd, _fox_bwd, wf, 'fox'), (_gdn_fwd, _gdn_bwd, wg, 'gdn')][i]
            else:
                fwd, bwd, w, tag = _ffn_fwd, _ffn_bwd, wffn[i], 'ffn%d' % i
            shift, scale, gate = _split_mod(mods[i, sub])
            g_pre, g_post = norm_g[i, 2 * sub][None], norm_g[i, 2 * sub + 1][None]
            h = _pre_norm(x, g_pre, scale, shift, tag + '_prenorm')
            if late is not None and (i, sub) == (0, 0):
                y, sv, (wg, wffn) = fwd(h, w, tag, late)
            else:
                y, sv = fwd(h, w, tag)
            x_out = _post_res(x, y, gate, g_post, tag + '_postnorm')
            tape.append((bwd, w, tag, sv, x, y, g_pre, g_post, scale, gate))
            x = x_out
    dx, lsum = _loss_head(x, target, 'loss_head')
    loss = lsum[0, 0]
    dmods = [[None, None], [None, None]]
    dnorm = [[None] * 4, [None] * 4]
    wgrads = {}
    exchanged = {}
    for idx in reversed(range(4)):
        i, sub = divmod(idx, 2)
        bwd, w, tag, sv, x_in, y, g_pre, g_post, scale, gate = tape[idx]
        dy, dgate, dgpost = _post_res_bwd(dx, y, gate, g_post, tag + '_postnorm_bwd')
        dh, wgrads[tag], ex = bwd(dy, w, sv, tag, reduce_pairs)
        exchanged.update(ex)
        dx, dshift, dscale, dgpre = _pre_norm_bwd(dh, x_in, g_pre, scale, dx, tag + '_prenorm_bwd')
        dmods[i][sub] = jnp.concatenate([dshift[0], dscale[0], dgate[0]])
        dnorm[i][2 * sub], dnorm[i][2 * sub + 1] = dgpre[0], dgpost[0]
    dmods = jnp.stack([jnp.stack(r) for r in dmods])
    dnorm = jnp.stack([jnp.stack(r) for r in dnorm])
    return loss, dx, dmods, dnorm, wgrads, exchanged


def _unpack_lanes_dev(packed, shapes):
    n = packed.shape[0]
    out, r = [], 0
    for shp in shapes:
        k = math.prod(shp)
        nr = -(-k // LANES)
        out.append(packed[:, r:r + nr].reshape(n, -1)[:, :k].reshape((n,) + tuple(shp)))
        r += nr
    return out


def _gather_lanes(arrs, name):
    packed = _pack_lanes(arrs)
    got = _allgather8(packed, name).reshape(N_DEV, packed.shape[0], LANES)
    return _unpack_lanes_dev(got, [a.shape for a in arrs]), got


def kernel(x, c, ada_w, ada_b, norm_g, fox_w_in, fox_f_bias, fox_q_norm, fox_k_norm, fox_w_o, gdn_w_in, gdn_conv_w, gdn_a_log, gdn_dt_bias, gdn_out_norm, gdn_w_o, ffn_w_up, ffn_conv_w, ffn_conv_b, ffn_w_down, loss_target, m_ada_w, m_ada_b, m_norm_g, m_fox_w_in, m_fox_f_bias, m_fox_q_norm, m_fox_k_norm, m_fox_w_o, m_gdn_w_in, m_gdn_conv_w, m_gdn_a_log, m_gdn_dt_bias, m_gdn_out_norm, m_gdn_w_o, m_ffn_w_up, m_ffn_conv_w, m_ffn_conv_b, m_ffn_w_down, v_ada_w, v_ada_b, v_norm_g, v_fox_w_in, v_fox_f_bias, v_fox_q_norm, v_fox_k_norm, v_fox_w_o, v_gdn_w_in, v_gdn_conv_w, v_gdn_a_log, v_gdn_dt_bias, v_gdn_out_norm, v_gdn_w_o, v_ffn_w_up, v_ffn_conv_w, v_ffn_conv_b, v_ffn_w_down):
    args = locals()
    w = {n: args[n] for n in WEIGHTS}
    mom = {n: args['m_' + n] for n in WEIGHTS}
    var = {n: args['v_' + n] for n in WEIGHTS}
    _, T, D = x.shape
    H = D // HD
    Hv = V_PER_K * H
    xi, yi, ci = _place()
    s_idx = 2 * xi + yi
    b_idx = 4 * xi + 2 * yi + ci
    sc_arr = jnp.stack([s_idx, ci]).astype(jnp.int32)
    c_arr = jnp.reshape(ci, (1,)).astype(jnp.int32)

    (c_all, ab_all, ng_all, gcw_all, fcw_all), _ = _gather_lanes(
        [jnp.tile(c, (SUBLANES, 1)), ada_b, norm_g, gdn_conv_w, ffn_conv_w], 'gather_small')
    c_all = c_all[:, 0, :]
    chips = lambda a: jnp.concatenate([a[2 * s] for s in range(N_CHIPS)], axis=-1)
    norm_g_full, gdn_conv_full, ffn_conv_full = chips(ng_all), chips(gcw_all), chips(fcw_all)

    Ns = ada_w.shape[-1]
    ada_w4 = ada_w.reshape(4, D, Ns)
    part = jnp.stack([_mm(c_all, ada_w4[i], 'nn', F32, 'ada_proj%d' % i, a_act='silu') for i in range(4)])
    part = part + ada_b.reshape(4, 1, Ns)
    (part_all,), _ = _gather_lanes([part], 'gather_mods')
    mine = lax.dynamic_index_in_dim(part_all[0::2], b_idx, axis=2, keepdims=False)
    mods = mine.transpose(1, 0, 2).reshape(2, 2, N_CHIPS * Ns)

    as2d = lambda a: a.reshape(-1, a.shape[-1])
    own = {n: as2d(w[n]).astype(BF16) for n in BIG}

    def whole(n, gathered):
        shards = lax.dynamic_update_index_in_dim(gathered, own[n], s_idx, 0)
        return _shards_to_full(shards.reshape((N_CHIPS,) + w[n].shape), BIG_SHARD_AXIS[n])

    fw = whole('fox_w_in', _gather_weights(own['fox_w_in'], 'gather_fox_w_in'))[0]
    wf = dict(cat=jnp.concatenate([fw[:, :3 * D], fw[:, 3 * D + H:]], axis=1), f=_pad_cols(fw[:, 3 * D:3 * D + H], LANES),
              f_bias=_pad_cols(fox_f_bias, LANES), q_norm=fox_q_norm, k_norm=fox_k_norm)
    later = [n for n in BIG if n != 'fox_w_in']

    def later_weights(gathered):
        full = {n: whole(n, a) for n, a in zip(later, gathered)}
        gw = full['gdn_w_in'][0]
        wg = dict(cat=gw[:, :6 * D], ab=_pad_cols(gw[:, 6 * D:], LANES), conv=gdn_conv_full[0],
                  a_log=_pad_cols(gdn_a_log, LANES), dt_bias=_pad_cols(gdn_dt_bias, LANES), out_norm=gdn_out_norm,
                  o=full['gdn_w_o'][0])
        wffn = [dict(up=full['ffn_w_up'][i], conv=ffn_conv_full[i], conv_b=ffn_conv_b[i][None],
                     down=full['ffn_w_down'][i]) for i in range(2)]
        return full['fox_w_o'][0], (wg, wffn)

    def reduce_pairs(tag, gr):
        if tag == 'fox':
            full = {('fox_w_in', 0): jnp.concatenate([gr['cat'][:, :3 * D], gr['f'][:, :H], gr['cat'][:, 3 * D:]], axis=1),
                    ('fox_w_o', 0): gr['o']}
        elif tag == 'gdn':
            full = {('gdn_w_in', 0): jnp.concatenate([gr['cat'], gr['ab'][:, :2 * Hv]], axis=1), ('gdn_w_o', 0): gr['o']}
        else:
            layer = int(tag[-1])
            full = {('ffn_w_up', layer): gr['up'], ('ffn_w_down', layer): gr['down']}
        out = []
        for (n, layer), a in full.items():
            shards = _full_to_shards(a, BIG_SHARD_AXIS[n] - 1)
            name = '%s%d' % (n, layer)
            got = _swap_halves(shards, name + '_to_sibling')
            out.append(((n, layer), _add_halves(shards, got, c_arr, name + '_add_sibling')))
        return out

    loss, dx, dmods, dnorm, g, exchanged = _local_step(
        x[0], loss_target[0], mods, norm_g_full, wf, None, None,
        late=([own[n] for n in later], later_weights), reduce_pairs=reduce_pairs)
    loss = lax.psum(loss, ('x', 'y', 'c'))

    gf, gg = g['fox'], g['gdn']
    big_out = {}
    for n in BIG:
        parts = []
        for layer in range(w[n].shape[0]):
            tag = '%s%d' % (n, layer)
            pair_sum, received = exchanged[(n, layer)]
            half_sum = _add_chips(pair_sum, received, sc_arr, tag + '_add_chips')
            parts.append(_join_halves(half_sum, tag + '_join'))
        g_shard = parts[0] if len(parts) == 1 else jnp.concatenate(parts, axis=0)
        big_out[n] = [o.reshape(w[n].shape)
                      for o in _adamw(as2d(w[n]), as2d(mom[n]), as2d(var[n]), [g_shard], 'adamw_' + n)]

    small_part = [dmods, dnorm, gf['f_bias'][:, :H], gf['q_norm'], gf['k_norm'], gg['conv'][None],
                  gg['a_log'][:, :Hv], gg['dt_bias'][:, :Hv], gg['out_norm'],
                  jnp.stack([g['ffn0']['conv'], g['ffn1']['conv']]),
                  jnp.concatenate([g['ffn0']['conv_b'], g['ffn1']['conv_b']], axis=0)]
    (dmods_all, *_), got = _gather_lanes(small_part, 'gather_small_grads')
    tot = _unpack_lanes(_sum_devices(got, 'sum_small_grads'), [a.shape for a in small_part])
    small_full = dict(zip(SMALL, tot))
    small_g = {n: (lax.dynamic_slice_in_dim(small_full[n], s_idx * w[n].shape[-1], w[n].shape[-1], axis=-1)
                   if n in SMALL_SHARDED else small_full[n]) for n in SMALL}
    packs = lambda d: _pack_lanes([d[n] for n in SMALL])
    small_shapes = [w[n].shape for n in SMALL]
    small_out = [_unpack_lanes(o, small_shapes)
                 for o in _adamw(packs(w), packs(mom), packs(var), [packs(small_g)], 'adamw_small')]

    dm = lax.dynamic_slice_in_dim(dmods_all.reshape(N_DEV, 4, N_CHIPS * Ns), s_idx * Ns, Ns, axis=-1).transpose(1, 0, 2)
    ada_out = [o.reshape(ada_w.shape) for o in
               _ada_w_update(c_all, dm, ada_w4, m_ada_w.reshape(4, D, Ns), v_ada_w.reshape(4, D, Ns), 'adamw_ada_w')]

    outs = []
    for k in range(4):
        by_name = {'ada_w': ada_out[k]}
        by_name.update({n: big_out[n][k] for n in BIG})
        by_name.update(zip(SMALL, small_out[k]))
        outs += [by_name[n] for n in WEIGHTS]
    return (loss, dx[None], *outs)
```
